```python
import math
import jax, jax.numpy as jnp
from jax import lax
import numpy as np

D_MODEL = 1024
BATCH = 4
SEQ = 8192
DEPTH = 4

MIX_DIM = D_MODEL
HEAD_DIM = 64
POOL_DIM = MIX_DIM // 4
POOL_GROUPS = 4
POOL_WINDOWS = (2, 4, 8, 16)
CONV_DIM = MIX_DIM // 4
CONV_WIDTH = 31
NSA_DIM = MIX_DIM // 2
NSA_HEADS = NSA_DIM // HEAD_DIM
NSA_KV_GROUPS = 2
NSA_HPG = NSA_HEADS // NSA_KV_GROUPS
KV_DIM = NSA_KV_GROUPS * HEAD_DIM
CMP_LEN = 32
CMP_STRIDE = 16
CMP_HIDDEN = 128
SEL_BLOCK = 64
SEL_TOPN = 16
WINDOW = 512
Q_BLOCK = 128
N_BUCKETS = 32
MAX_DISTANCE = 1024
N_EXPERTS = 32
TOP_K = 4
D_FF = D_MODEL
SWIGLU_ALPHA = 1.702
SWIGLU_LIMIT = 7.0
EXPERT_BLOCK = 512
EPS = 1e-5
NEG_INF = -1e30
FORCE_SCORE = 1e4
IN_WIDTH = POOL_DIM + 2 * CONV_DIM + NSA_DIM + 6 * KV_DIM + 3 * NSA_HEADS

kernel_name = 'hybrid_pool_conv_nsa_moe_adaln'


def rmsnorm(x, g):
    xf = x.astype(jnp.float32)
    y = xf * lax.rsqrt(jnp.mean(xf * xf, axis=-1, keepdims=True) + EPS)
    return (y * g.astype(jnp.float32)).astype(x.dtype)


def t5_bucket(n):
    n = jnp.maximum(n, 0)
    max_exact = N_BUCKETS // 2
    nf = jnp.maximum(n, 1).astype(jnp.float32)
    large = max_exact + (jnp.log(nf / max_exact) / math.log(MAX_DISTANCE / max_exact)
                         * (N_BUCKETS - max_exact)).astype(jnp.int32)
    large = jnp.minimum(large, N_BUCKETS - 1)
    return jnp.where(n < max_exact, n, large)


def head_bias(rel_bias, n):
    b = rel_bias[t5_bucket(n)]
    b = b.reshape(n.shape + (NSA_KV_GROUPS, NSA_HPG))
    return b.transpose(2, 3, 0, 1).astype(jnp.float32)


def masked_softmax(logits, mask):
    p = jax.nn.softmax(jnp.where(mask, logits, NEG_INF), axis=-1)
    return jnp.where(mask, p, 0.0)


def pool_mixer(u, pool_w, pool_scale):
    B, S, C = u.shape
    cg = C // POOL_GROUPS
    t1 = jnp.arange(1, S + 1, dtype=jnp.float32)[None, :, None]
    outs = []
    for g, w in enumerate(POOL_WINDOWS):
        ug = u[..., g * cg:(g + 1) * cg].astype(jnp.float32)
        cs = jnp.cumsum(ug, axis=1)
        lower = jnp.pad(cs, ((0, 0), (w, 0), (0, 0)))[:, :S]
        cnt = jnp.minimum(t1, float(w))
        outs.append((cs - lower) / cnt - ug)
    pooled = jnp.stack(outs, axis=2).astype(u.dtype)
    y = jnp.einsum('bsgc,gcd->bsgd', pooled, pool_w).reshape(B, S, C)
    return y * pool_scale


def conv_mixer(u_val, u_gate, conv_w, conv_b, ln_g, ln_b):
    v = u_val * jax.nn.sigmoid(u_gate)
    y = lax.conv_general_dilated(v, conv_w[:, None, :], window_strides=(1,),
                                 padding=[(CONV_WIDTH - 1, 0)],
                                 dimension_numbers=('NWC', 'WIO', 'NWC'),
                                 feature_group_count=CONV_DIM) + conv_b
    yf = y.astype(jnp.float32)
    mu = jnp.mean(yf, axis=-1, keepdims=True)
    var = jnp.mean(jnp.square(yf - mu), axis=-1, keepdims=True)
    yn = (yf - mu) * lax.rsqrt(var + EPS) * ln_g.astype(jnp.float32) + ln_b.astype(jnp.float32)
    return jax.nn.silu(yn).astype(u_val.dtype)


def compress(kv, pe, w1, w2):
    B, S, G, HD = kv.shape
    nc = (S - CMP_LEN) // CMP_STRIDE + 1
    idx = jnp.arange(nc)[:, None] * CMP_STRIDE + jnp.arange(CMP_LEN)[None, :]
    blk = kv[:, idx] + pe[None, None, :, None, :]
    blk = blk.transpose(0, 1, 3, 2, 4).reshape(B, nc, G, CMP_LEN * HD)
    return jax.nn.gelu(blk @ w1) @ w2


def nsa_mixer(q, k_cmp, v_cmp, k_sel, v_sel, k_win, v_win, gates, rel_bias):
    B, S = q.shape[0], q.shape[1]
    G, HPG, HD = NSA_KV_GROUPS, NSA_HPG, HEAD_DIM
    nc = k_cmp.shape[1]
    n_sel = S // SEL_BLOCK
    topn = min(SEL_TOPN, n_sel)
    scale = HD ** -0.5
    f32 = jnp.float32
    cmp_end = jnp.arange(nc) * CMP_STRIDE + CMP_LEN - 1
    cmp_start = cmp_end - (CMP_LEN - 1)
    sel_ids = jnp.arange(n_sel)
    overlap = ((cmp_end[:, None] >= sel_ids[None, :] * SEL_BLOCK)
               & (cmp_start[:, None] < (sel_ids[None, :] + 1) * SEL_BLOCK)).astype(f32)
    k_sel_t = k_sel.transpose(0, 2, 1, 3)
    v_sel_t = v_sel.transpose(0, 2, 1, 3)
    pad = ((0, 0), (WINDOW, 0), (0, 0), (0, 0))
    k_win_p = jnp.pad(k_win, pad)
    v_win_p = jnp.pad(v_win, pad)
    tbl_sel = rel_bias.reshape(N_BUCKETS, G, HPG).transpose(2, 1, 0).reshape(HPG, G * N_BUCKETS)
    g_off = (jnp.arange(G) * N_BUCKETS)[None, :, None, None]

    def block(bi):
        q0 = bi * Q_BLOCK
        t = q0 + jnp.arange(Q_BLOCK)
        qb = lax.dynamic_slice_in_dim(q, q0, Q_BLOCK, axis=1)
        gb = lax.dynamic_slice_in_dim(gates, q0, Q_BLOCK, axis=1).astype(f32)
        s_c = jnp.einsum('bqghd,bngd->bghqn', qb, k_cmp, preferred_element_type=f32) * scale \
            + head_bias(rel_bias, t[:, None] - cmp_end[None, :])
        p_c = masked_softmax(s_c, cmp_end[None, :] <= t[:, None])
        o_c = jnp.einsum('bghqn,bngd->bqghd', p_c.astype(v_cmp.dtype), v_cmp)
        imp = jnp.einsum('bghqn,nj->bgqj', p_c, overlap)
        cur = (t // SEL_BLOCK)[:, None]
        j = sel_ids[None, :]
        forced = (j == 0) | (j == cur) | (j == cur - 1)
        score = jnp.where(j <= cur, jnp.where(forced, FORCE_SCORE, imp), NEG_INF)
        _, sel = lax.top_k(score, topn)
        kpos = (sel[..., None] * SEL_BLOCK + jnp.arange(SEL_BLOCK)).reshape(B, G, Q_BLOCK, topn * SEL_BLOCK)
        flat = kpos.reshape(B, G, -1, 1)
        ks = jnp.take_along_axis(k_sel_t, flat, axis=2).reshape(B, G, Q_BLOCK, -1, HD)
        vs = jnp.take_along_axis(v_sel_t, flat, axis=2).reshape(B, G, Q_BLOCK, -1, HD)
        rel_s = t[:, None] - kpos
        bias_s = tbl_sel[:, g_off + t5_bucket(rel_s)].transpose(1, 2, 0, 3, 4).astype(f32)
        s_s = jnp.einsum('bqghd,bgqkd->bghqk', qb, ks, preferred_element_type=f32) * scale + bias_s
        p_s = masked_softmax(s_s, (rel_s >= 0)[:, :, None])
        o_s = jnp.einsum('bghqk,bgqkd->bqghd', p_s.astype(vs.dtype), vs)
        kw = lax.dynamic_slice_in_dim(k_win_p, q0, Q_BLOCK + WINDOW, axis=1)
        vw = lax.dynamic_slice_in_dim(v_win_p, q0, Q_BLOCK + WINDOW, axis=1)
        kpos_w = q0 - WINDOW + jnp.arange(Q_BLOCK + WINDOW)
        rel_w = t[:, None] - kpos_w[None, :]
        valid_w = (kpos_w[None, :] >= 0) & (rel_w >= 0) & (rel_w < WINDOW)
        s_w = jnp.einsum('bqghd,bkgd->bghqk', qb, kw, preferred_element_type=f32) * scale \
            + head_bias(rel_bias, rel_w)
        p_w = masked_softmax(s_w, valid_w)
        o_w = jnp.einsum('bghqk,bkgd->bqghd', p_w.astype(vw.dtype), vw)
        o = gb[:, :, 0, :, :, None] * o_c + gb[:, :, 1, :, :, None] * o_s + gb[:, :, 2, :, :, None] * o_w
        return o.reshape(B, Q_BLOCK, NSA_DIM).astype(q.dtype)

    out = lax.map(block, jnp.arange(S // Q_BLOCK))
    return out.transpose(1, 0, 2, 3).reshape(B, S, NSA_DIM)


def moe_ffn(h, router_w, router_b, w_gu, b_gu, w_down, b_down):
    B, S, D = h.shape
    F = w_down.shape[1]
    N = B * S
    M = N * TOP_K
    n_slots = -(-(M + N_EXPERTS * EXPERT_BLOCK) // EXPERT_BLOCK) * EXPERT_BLOCK
    n_blocks = n_slots // EXPERT_BLOCK
    hf = h.reshape(N, D)
    logits = jnp.matmul(hf, router_w, preferred_element_type=jnp.float32) + router_b.astype(jnp.float32)
    top_logit, top_e = lax.top_k(logits, TOP_K)
    top_w = jax.nn.softmax(top_logit, axis=-1)
    flat_e = top_e.reshape(M)
    flat_tok = jnp.repeat(jnp.arange(N, dtype=jnp.int32), TOP_K)
    order = jnp.argsort(flat_e)
    sorted_e = flat_e[order]
    counts = jnp.bincount(flat_e, length=N_EXPERTS)
    padded = (counts + EXPERT_BLOCK - 1) // EXPERT_BLOCK * EXPERT_BLOCK
    start = jnp.cumsum(counts) - counts
    pstart = jnp.cumsum(padded) - padded
    dest = pstart[sorted_e] + jnp.arange(M, dtype=jnp.int32) - start[sorted_e]
    slot_tok = jnp.zeros((n_slots,), jnp.int32).at[dest].set(flat_tok[order])
    slot_w = jnp.zeros((n_slots,), jnp.float32).at[dest].set(top_w.reshape(M)[order])
    blk_e = jnp.minimum(jnp.searchsorted(jnp.cumsum(padded), jnp.arange(n_blocks) * EXPERT_BLOCK, side='right'),
                        N_EXPERTS - 1)

    def expert_block(args):
        tok, e = args
        xb = hf[tok]
        gu = xb @ w_gu[e] + b_gu[e]
        gate = jnp.minimum(gu[:, :F], SWIGLU_LIMIT)
        up = jnp.clip(gu[:, F:], -SWIGLU_LIMIT, SWIGLU_LIMIT)
        act = (up + 1) * gate * jax.nn.sigmoid(SWIGLU_ALPHA * gate)
        return act @ w_down[e] + b_down[e]

    y = lax.map(expert_block, (slot_tok.reshape(n_blocks, EXPERT_BLOCK), blk_e))
    y = y.reshape(n_slots, D).astype(jnp.float32) * slot_w[:, None]
    out = jnp.zeros((N, D), jnp.float32).at[slot_tok].add(y)
    return out.reshape(B, S, D).astype(h.dtype)


def setup_inputs(seed: int = 0) -> dict:
    key = jax.random.key(seed)
    ks = jax.random.split(key, 32)
    nrm = jax.random.normal
    L, D, E, F = DEPTH, D_MODEL, N_EXPERTS, D_FF
    return {
        'x': nrm(ks[0], (BATCH, SEQ, D), jnp.float32),
        'c': nrm(ks[1], (BATCH, D), jnp.float32),
        'w_mod': nrm(ks[2], (L, D, 6 * D)) * (0.5 * D ** -0.5),
        'b_mod': nrm(ks[3], (L, 6 * D)) * 0.02,
        'norm1_g': 1.0 + 0.05 * nrm(ks[4], (L, D)),
        'norm2_g': 1.0 + 0.05 * nrm(ks[5], (L, D)),
        'w_in': nrm(ks[6], (L, D, IN_WIDTH)) * D ** -0.5,
        'w_out': nrm(ks[7], (L, MIX_DIM, D)) * MIX_DIM ** -0.5,
        'pool_w': nrm(ks[8], (L, POOL_GROUPS, POOL_DIM // POOL_GROUPS, POOL_DIM // POOL_GROUPS)) * (POOL_DIM // POOL_GROUPS) ** -0.5,
        'pool_scale': 1.0 + 0.1 * nrm(ks[9], (L, POOL_DIM)),
        'conv_w': nrm(ks[10], (L, CONV_WIDTH, CONV_DIM)) * CONV_WIDTH ** -0.5,
        'conv_b': nrm(ks[11], (L, CONV_DIM)) * 0.02,
        'conv_ln_g': 1.0 + 0.05 * nrm(ks[12], (L, CONV_DIM)),
        'conv_ln_b': nrm(ks[13], (L, CONV_DIM)) * 0.02,
        'cmp_pe_k': nrm(ks[14], (L, CMP_LEN, HEAD_DIM)) * 0.1,
        'cmp_pe_v': nrm(ks[15], (L, CMP_LEN, HEAD_DIM)) * 0.1,
        'cmp_w1_k': nrm(ks[16], (L, CMP_LEN * HEAD_DIM, CMP_HIDDEN)) * (CMP_LEN * HEAD_DIM) ** -0.5,
        'cmp_w2_k': nrm(ks[17], (L, CMP_HIDDEN, HEAD_DIM)) * CMP_HIDDEN ** -0.5,
        'cmp_w1_v': nrm(ks[18], (L, CMP_LEN * HEAD_DIM, CMP_HIDDEN)) * (CMP_LEN * HEAD_DIM) ** -0.5,
        'cmp_w2_v': nrm(ks[19], (L, CMP_HIDDEN, HEAD_DIM)) * CMP_HIDDEN ** -0.5,
        'rel_bias': nrm(ks[20], (N_BUCKETS, NSA_HEADS)) * 0.5,
        'router_w': nrm(ks[21], (L, D, E)) * D ** -0.5,
        'router_b': nrm(ks[22], (L, E)) * 0.01,
        'expert_w_gu': nrm(ks[23], (L, E, D, 2 * F)) * D ** -0.5,
        'expert_b_gu': nrm(ks[24], (L, E, 2 * F)) * 0.02,
        'expert_w_down': nrm(ks[25], (L, E, F, D)) * F ** -0.5,
        'expert_b_down': nrm(ks[26], (L, E, D)) * 0.02,
        'final_g': 1.0 + 0.05 * nrm(ks[27], (D,)),
    }


def reference(x, c, w_mod, b_mod, norm1_g, norm2_g, w_in, w_out, pool_w, pool_scale,
              conv_w, conv_b, conv_ln_g, conv_ln_b, cmp_pe_k, cmp_pe_v, cmp_w1_k, cmp_w2_k,
              cmp_w1_v, cmp_w2_v, rel_bias, router_w, router_b, expert_w_gu, expert_b_gu,
              expert_w_down, expert_b_down, final_g):
    B, S, D = x.shape
    G, HPG, HD = NSA_KV_GROUPS, NSA_HPG, HEAD_DIM
    widths = [POOL_DIM, CONV_DIM, CONV_DIM, NSA_DIM] + [KV_DIM] * 6 + [3 * NSA_HEADS]
    splits = [int(v) for v in np.cumsum(widths)[:-1]]
    cond = jax.nn.silu(c)
    for l in range(DEPTH):
        mod = (cond @ w_mod[l] + b_mod[l])[:, None, :]
        sh1, sc1, g1, sh2, sc2, g2 = jnp.split(mod, 6, axis=-1)
        h = rmsnorm(x, norm1_g[l]) * (1 + sc1) + sh1
        z = h @ w_in[l]
        u_pool, u_val, u_gate, q, kc, vc, ksl, vsl, kwn, vwn, gl = jnp.split(z, splits, axis=-1)
        kv = lambda a: a.reshape(B, S, G, HD)
        y_pool = pool_mixer(u_pool, pool_w[l], pool_scale[l])
        y_conv = conv_mixer(u_val, u_gate, conv_w[l], conv_b[l], conv_ln_g[l], conv_ln_b[l])
        k_cmp = compress(kv(kc), cmp_pe_k[l], cmp_w1_k[l], cmp_w2_k[l])
        v_cmp = compress(kv(vc), cmp_pe_v[l], cmp_w1_v[l], cmp_w2_v[l])
        gates = jax.nn.sigmoid(gl.reshape(B, S, 3, G, HPG))
        y_nsa = nsa_mixer(q.reshape(B, S, G, HPG, HD), k_cmp, v_cmp, kv(ksl), kv(vsl),
                          kv(kwn), kv(vwn), gates, rel_bias)
        mixed = jnp.concatenate([y_pool, y_conv, y_nsa], axis=-1)
        x = x + g1 * (mixed @ w_out[l])
        h2 = rmsnorm(x, norm2_g[l]) * (1 + sc2) + sh2
        x = x + g2 * moe_ffn(h2, router_w[l], router_b[l], expert_w_gu[l], expert_b_gu[l],
                             expert_w_down[l], expert_b_down[l])
    return rmsnorm(x, final_g)
```

```python
import functools
import math

import jax
import jax.numpy as jnp
import numpy as np
from jax import lax
from jax.experimental import pallas as pl
from jax.experimental.pallas import tpu as pltpu

F32 = jnp.float32
BF16 = jnp.bfloat16
I32 = jnp.int32

HEAD_DIM = 64
POOL_DIM = 256
POOL_GROUPS = 4
POOL_WINDOWS = (2, 4, 8, 16)
CONV_DIM = 256
CONV_WIDTH = 31
NSA_DIM = 512
NSA_HEADS = 8
KV_GROUPS = 2
HPG = 4
CMP_LEN = 32
CMP_STRIDE = 16
CMP_HIDDEN = 128
SEL_BLOCK = 64
SEL_TOPN = 16
WINDOW = 512
Q_TILE = 128
N_BUCKETS = 32
MAX_DISTANCE = 1024
N_EXPERTS = 32
TOP_K = 4
SWIGLU_ALPHA = 1.702
SWIGLU_LIMIT = 7.0
EPS = 1e-5
NEG_INF = -1e30
FORCE_SCORE = 1e4

LANES = 128
ROWS = HPG * Q_TILE
NEAR_TILES = 10
MASK_BIAS = -32768.0
TOK_TILE = 512
EXPERT_BLOCK = 512
VMEM_LIMIT = 56 * 1024 * 1024


def _cparams(sem, vmem=VMEM_LIMIT):
    return pltpu.CompilerParams(dimension_semantics=sem, vmem_limit_bytes=vmem)


def _t5_bucket(n):
    n = jnp.maximum(n, 0)
    max_exact = N_BUCKETS // 2
    nf = jnp.maximum(n, 1).astype(F32)
    large = max_exact + (jnp.log(nf / max_exact) / math.log(MAX_DISTANCE / max_exact)
                         * (N_BUCKETS - max_exact)).astype(I32)
    large = jnp.minimum(large, N_BUCKETS - 1)
    return jnp.where(n < max_exact, n, large)


def _mod_kernel(c_ref, w_ref, b_ref, o_ref):
    c = c_ref[...]
    cond = c * jax.nn.sigmoid(c)
    o_ref[0] = jnp.dot(cond.astype(BF16), w_ref[0].astype(BF16),
                       preferred_element_type=F32) + b_ref[0]


def _modulation(c, w_mod, b_mod):
    L, D, W = w_mod.shape
    B = c.shape[0]
    tn = 1536
    return pl.pallas_call(
        _mod_kernel,
        out_shape=jax.ShapeDtypeStruct((L, B, W), F32),
        grid=(L, W // tn),
        in_specs=[pl.BlockSpec((B, D), lambda l, j: (0, 0)),
                  pl.BlockSpec((1, D, tn), lambda l, j: (l, 0, j)),
                  pl.BlockSpec((1, 1, tn), lambda l, j: (l, 0, j))],
        out_specs=pl.BlockSpec((1, B, tn), lambda l, j: (l, 0, j)),
        compiler_params=_cparams(("arbitrary", "arbitrary")),
        name="modulation",
    )(c, w_mod, b_mod.reshape(L, 1, W))


C_UPC, C_Q, C_KVC, C_KS, C_VS, C_KW, C_VW, C_GT, C_END = 0, 768, 1280, 1536, 2048, 2304, 2560, 2816, 3072


def _in_weight(w_in):
    D = w_in.shape[0]
    src = np.full((C_END,), -1, np.int64)
    src[C_UPC:C_UPC + 768] = np.arange(768)
    src[C_Q:C_Q + 512] = 768 + np.arange(512)
    src[C_KVC:C_KVC + 256] = 1280 + np.arange(256)
    for g in range(KV_GROUPS):
        src[C_KS + g * 256:C_KS + g * 256 + 64] = 1536 + g * 64 + np.arange(64)
        src[C_VS + g * 128:C_VS + g * 128 + 64] = 1664 + g * 64 + np.arange(64)
        src[C_KW + g * 128:C_KW + g * 128 + 64] = 1792 + g * 64 + np.arange(64)
        src[C_VW + g * 128:C_VW + g * 128 + 64] = 1920 + g * 64 + np.arange(64)
        for br in range(3):
            for h in range(HPG):
                src[C_GT + g * 128 + br * HPG + h] = 2048 + br * NSA_HEADS + g * HPG + h
    scale = np.ones((C_END,), np.float32)
    scale[C_Q:C_Q + 512] = HEAD_DIM ** -0.5
    scale[src < 0] = 0.0
    w = jnp.take(w_in, jnp.asarray(np.maximum(src, 0)), axis=1) * jnp.asarray(scale)
    return w.astype(BF16)


def _inproj_kernel(x_ref, g_ref, sc_ref, sh_ref, w_ref,
                   upc_ref, q_ref, kvc_ref, ks_ref, vs_ref, kw_ref, vw_ref, gt_ref):
    ti = pl.program_id(1)
    x = x_ref[0]
    tm = x.shape[0]
    y = x * lax.rsqrt(jnp.mean(x * x, axis=-1, keepdims=True) + EPS) * g_ref[...]
    h = y * (1.0 + sc_ref[0]) + sh_ref[0]
    z = jnp.dot(h.astype(BF16), w_ref[...], preferred_element_type=F32)
    upc_ref[0] = z[:, C_UPC:C_Q]
    q_ref[0] = z[:, C_Q:C_KVC].astype(BF16)
    kvc_ref[0] = z[:, C_KVC:C_KS].astype(BF16)
    lane = lax.broadcasted_iota(I32, (tm, 256), 1)
    row = lax.broadcasted_iota(I32, (tm, 256), 0)
    blk = (ti * tm + row) // SEL_BLOCK
    onehot = (lane - LANES == blk).astype(F32)
    for g in range(KV_GROUPS):
        ks_ref[0, :, g * 256:(g + 1) * 256] = (z[:, C_KS + g * 256:C_KS + (g + 1) * 256] + onehot).astype(BF16)
    lane = lax.broadcasted_iota(I32, (tm, 256), 1)
    ones_col = ((lane % LANES) == HEAD_DIM).astype(F32)
    vs_ref[0] = (z[:, C_VS:C_KW] + ones_col).astype(BF16)
    kw_ref[0] = z[:, C_KW:C_VW].astype(BF16)
    vw_ref[0] = (z[:, C_VW:C_GT] + ones_col).astype(BF16)
    gt_ref[0] = jax.nn.sigmoid(z[:, C_GT:C_END])


def _in_proj(x, g1, sc, sh, w_big):
    B, S, D = x.shape
    tm = TOK_TILE
    tok = lambda w: pl.BlockSpec((1, tm, w), lambda b, i: (b, i, 0))
    outs = [(768, F32), (512, BF16), (256, BF16), (512, BF16), (256, BF16), (256, BF16), (256, BF16), (256, F32)]
    return pl.pallas_call(
        _inproj_kernel,
        out_shape=[jax.ShapeDtypeStruct((B, S, w), dt) for w, dt in outs],
        grid=(B, S // tm),
        in_specs=[tok(D),
                  pl.BlockSpec((1, D), lambda b, i: (0, 0)),
                  pl.BlockSpec((1, 1, D), lambda b, i: (b, 0, 0)),
                  pl.BlockSpec((1, 1, D), lambda b, i: (b, 0, 0)),
                  pl.BlockSpec((D, C_END), lambda b, i: (0, 0))],
        out_specs=[tok(w) for w, _ in outs],
        compiler_params=_cparams(("arbitrary", "arbitrary")),
        name="in_proj",
    )(x, g1.reshape(1, D), sc, sh, w_big)


HALO = 32


def _poolconv_kernel(cur_ref, halo_ref, pw_ref, ps_ref, cw_ref, cb_ref, lg_ref, lb_ref, o_ref, ext_ref, v_ref):
    ti = pl.program_id(1)
    ts = cur_ref.shape[1]
    halo = halo_ref[0] * (ti > 0).astype(F32)
    ext_ref[0:HALO, :] = halo
    ext_ref[HALO:HALO + ts, :] = cur_ref[0]
    u = ext_ref[HALO:HALO + ts, 0:POOL_DIM]
    lane = lax.broadcasted_iota(I32, (ts, POOL_DIM), 1)
    grp = lane // (POOL_DIM // POOL_GROUPS)
    run = u
    pooled = jnp.zeros_like(u)
    for k in range(1, POOL_WINDOWS[-1]):
        run = run + ext_ref[HALO - k:HALO - k + ts, 0:POOL_DIM]
        if (k + 1) in POOL_WINDOWS:
            pooled = jnp.where(grp == POOL_WINDOWS.index(k + 1), run, pooled)
    wlane = jnp.where(grp == 0, 2.0, jnp.where(grp == 1, 4.0, jnp.where(grp == 2, 8.0, 16.0)))
    t1 = (ti * ts + lax.broadcasted_iota(I32, (ts, POOL_DIM), 0) + 1).astype(F32)
    cnt = jnp.minimum(t1, wlane)
    pooled = pooled / cnt - u
    y_pool = jnp.dot(pooled.astype(BF16), pw_ref[...], preferred_element_type=F32) * ps_ref[...]
    o_ref[0, :, 0:POOL_DIM] = y_pool.astype(BF16)
    uv = ext_ref[:, POOL_DIM:POOL_DIM + CONV_DIM]
    ug = ext_ref[:, POOL_DIM + CONV_DIM:POOL_DIM + 2 * CONV_DIM]
    v_ref[...] = uv * jax.nn.sigmoid(ug)
    acc = jnp.zeros((ts, CONV_DIM), F32) + cb_ref[...]
    for k in range(CONV_WIDTH):
        o = HALO - (CONV_WIDTH - 1) + k
        acc = acc + v_ref[o:o + ts, :] * cw_ref[k:k + 1, :]
    mu = jnp.mean(acc, axis=-1, keepdims=True)
    d = acc - mu
    var = jnp.mean(d * d, axis=-1, keepdims=True)
    yn = d * lax.rsqrt(var + EPS) * lg_ref[...] + lb_ref[...]
    o_ref[0, :, POOL_DIM:POOL_DIM + CONV_DIM] = (yn * jax.nn.sigmoid(yn)).astype(BF16)


def _pool_conv(upc, pool_w_bd, pool_scale, conv_w, conv_b, ln_g, ln_b):
    B, S, W = upc.shape
    ts = TOK_TILE
    r = ts // HALO
    vec = lambda n: pl.BlockSpec((1, n), lambda b, i: (0, 0))
    return pl.pallas_call(
        _poolconv_kernel,
        out_shape=jax.ShapeDtypeStruct((B, S, POOL_DIM + CONV_DIM), BF16),
        grid=(B, S // ts),
        in_specs=[pl.BlockSpec((1, ts, W), lambda b, i: (b, i, 0)),
                  pl.BlockSpec((1, HALO, W), lambda b, i: (b, jnp.maximum(i * r - 1, 0), 0)),
                  pl.BlockSpec((POOL_DIM, POOL_DIM), lambda b, i: (0, 0)),
                  vec(POOL_DIM),
                  pl.BlockSpec((CONV_WIDTH + 1, CONV_DIM), lambda b, i: (0, 0)),
                  vec(CONV_DIM), vec(CONV_DIM), vec(CONV_DIM)],
        out_specs=pl.BlockSpec((1, ts, POOL_DIM + CONV_DIM), lambda b, i: (b, i, 0)),
        scratch_shapes=[pltpu.VMEM((HALO + ts, W), F32), pltpu.VMEM((HALO + ts, CONV_DIM), F32)],
        compiler_params=_cparams(("arbitrary", "arbitrary")),
        name="pool_conv",
    )(upc, upc, pool_w_bd, pool_scale.reshape(1, -1), conv_w, conv_b.reshape(1, -1),
      ln_g.reshape(1, -1), ln_b.reshape(1, -1))


N_STREAM = 2 * KV_GROUPS
CHUNK_W = CMP_STRIDE * 2 * KV_GROUPS * HEAD_DIM


def _cmp_weights(w1_k, w1_v):
    half = CMP_STRIDE * HEAD_DIM
    cols = []
    for s in range(N_STREAM):
        w1 = w1_k if s < KV_GROUPS else w1_v
        for part in range(2):
            blk = w1[part * half:(part + 1) * half].reshape(CMP_STRIDE, 1, HEAD_DIM, CMP_HIDDEN)
            z = jnp.zeros((CMP_STRIDE, N_STREAM, HEAD_DIM, CMP_HIDDEN), w1.dtype)
            z = lax.dynamic_update_slice(z, blk, (0, s, 0, 0))
            cols.append(z.reshape(CHUNK_W, CMP_HIDDEN))
    return jnp.concatenate(cols, axis=1).astype(BF16)


def _gelu_tanh(x):
    return 0.5 * x * (1.0 + jnp.tanh(math.sqrt(2.0 / math.pi) * (x + 0.044715 * (x * x * x))))


def _compress_kernel(c_ref, w_ref, pek_ref, pev_ref, w1k_ref, w1v_ref, w2k_ref, w2vt_ref, kc_ref, vct_ref):
    r = jnp.dot(c_ref[0], w_ref[...], preferred_element_type=F32)
    ncp = r.shape[0]
    pe_k = jnp.dot(pek_ref[...], w1k_ref[...], preferred_element_type=F32)[0:1]
    pe_v = jnp.dot(pev_ref[...], w1v_ref[...], preferred_element_type=F32)[0:1]
    for s in range(N_STREAM):
        a = r[:, s * 256:s * 256 + CMP_HIDDEN]
        b = r[:, s * 256 + CMP_HIDDEN:(s + 1) * 256]
        hid = a + pltpu.roll(b, ncp - 1, 0) + (pe_k if s < KV_GROUPS else pe_v)
        act = _gelu_tanh(hid).astype(BF16)
        if s < KV_GROUPS:
            kc_ref[0, s] = jnp.dot(act, w2k_ref[...], preferred_element_type=F32).astype(BF16)
        else:
            vct_ref[0, s - KV_GROUPS] = lax.dot_general(
                w2vt_ref[...], act, (((1,), (1,)), ((), ())), preferred_element_type=F32).astype(BF16)


def _compress(kvc, wcmp, pe_k, pe_v, w1_k, w1_v, w2_k, w2_v):
    B, S, _ = kvc.shape
    ncp = S // CMP_STRIDE
    chunks = kvc.reshape(B, ncp, CHUNK_W)
    pe8 = lambda pe: jnp.broadcast_to(pe.reshape(1, -1), (8, CMP_LEN * HEAD_DIM)).astype(BF16)
    w2k = jnp.pad(w2_k, ((0, 0), (0, LANES - HEAD_DIM))).astype(BF16)
    w2vt = jnp.pad(w2_v.T, ((0, LANES - HEAD_DIM), (0, 0))).astype(BF16)
    full = lambda a: pl.BlockSpec(a.shape, lambda b: (0,) * a.ndim)
    args = (wcmp, pe8(pe_k), pe8(pe_v), w1_k.astype(BF16), w1_v.astype(BF16), w2k, w2vt)
    return pl.pallas_call(
        _compress_kernel,
        out_shape=[jax.ShapeDtypeStruct((B, KV_GROUPS, ncp, LANES), BF16),
                   jax.ShapeDtypeStruct((B, KV_GROUPS, LANES, ncp), BF16)],
        grid=(B,),
        in_specs=[pl.BlockSpec((1, ncp, CHUNK_W), lambda b: (b, 0, 0))] + [full(a) for a in args],
        out_specs=[pl.BlockSpec((1, KV_GROUPS, ncp, LANES), lambda b: (b, 0, 0, 0)),
                   pl.BlockSpec((1, KV_GROUPS, LANES, ncp), lambda b: (b, 0, 0, 0))],
        compiler_params=_cparams(("arbitrary",)),
        name="compress",
    )(chunks, *args)


def _bias_tables(rel_bias, S):
    nq = S // Q_TILE
    ncp = S // CMP_STRIDE
    rb = rel_bias.reshape(N_BUCKETS, KV_GROUPS, HPG).transpose(1, 2, 0)
    i = np.arange(Q_TILE)[:, None]
    j = np.arange(LANES)[None, :]
    tabs = []
    for off in range(NEAR_TILES + 1):
        if off < NEAR_TILES:
            d = off * LANES + i - j
            valid = d >= 0
        else:
            d = (WINDOW // LANES) * LANES + i - j
            valid = d < WINDOW
        t = rb[:, :, _t5_bucket(jnp.asarray(d, I32))]
        tabs.append(jnp.where(jnp.asarray(valid)[None, None], t, NEG_INF))
    tn = jnp.stack(tabs, axis=1).reshape(KV_GROUPS, NEAR_TILES + 1, ROWS, LANES)
    c0 = (Q_TILE // CMP_STRIDE) * (nq - 1)
    r = np.arange(c0 + ncp)[:, None]
    d = np.arange(Q_TILE)[None, :] - CMP_STRIDE * (r - c0) - (CMP_LEN - 1)
    t = rb[:, :, _t5_bucket(jnp.asarray(d, I32))]
    t = jnp.where(jnp.asarray(d >= 0)[None, None], t, NEG_INF)
    tct = t.transpose(0, 2, 1, 3).reshape(KV_GROUPS, c0 + ncp, ROWS)
    return tn.astype(F32), tct.astype(F32)


def _overlap_t(S):
    ncp = S // CMP_STRIDE
    n = np.arange(ncp)[None, :]
    jb = np.arange(LANES)[:, None]
    end = n * CMP_STRIDE + CMP_LEN - 1
    start = n * CMP_STRIDE
    ov = (end >= jb * SEL_BLOCK) & (start < (jb + 1) * SEL_BLOCK) & (n < ncp - 1)
    return jnp.asarray(ov.astype(np.float32), BF16)


def _attn_kernel(q_ref, kc_ref, vct_ref, ks_ref, vs_ref, kw_ref, vw_ref, gt_ref, tn_ref, tct_ref, ovt_ref,
                 o_ref, qaug_ref, *, c0):
    qi = pl.program_id(2)
    nt = (((1,), (1,)), ((), ()))
    q = q_ref[0, 0].reshape(ROWS, HEAD_DIM)
    ncp = kc_ref.shape[2]

    st = lax.dot_general(kc_ref[0, 0][:, 0:HEAD_DIM], q, nt, preferred_element_type=F32)
    r0 = pl.multiple_of(c0 - (Q_TILE // CMP_STRIDE) * qi, 8)
    st = st + tct_ref[0, pl.ds(r0, ncp), :]
    mc = jnp.maximum(jnp.max(st, axis=0, keepdims=True), -1e20)
    pc = jnp.exp(st - mc)
    lc = jnp.sum(pc, axis=0, keepdims=True)
    pc = pc * (1.0 / jnp.maximum(lc, 1e-30))
    oct_ = jnp.dot(vct_ref[0, 0], pc.astype(BF16), preferred_element_type=F32)
    o_c = oct_.T[:, 0:HEAD_DIM]

    ps = pc[:, 0:Q_TILE]
    for h in range(1, HPG):
        ps = ps + pc[:, h * Q_TILE:(h + 1) * Q_TILE]
    ps_hi = ps.astype(BF16)
    ps_lo = (ps - ps_hi.astype(F32)).astype(BF16)
    imp = (jnp.dot(ovt_ref[...], ps_hi, preferred_element_type=F32)
           + jnp.dot(ovt_ref[...], ps_lo, preferred_element_type=F32))
    jb = lax.broadcasted_iota(I32, (LANES, Q_TILE), 0)
    ii = lax.broadcasted_iota(I32, (LANES, Q_TILE), 1)
    cur = (Q_TILE // SEL_BLOCK) * qi + (ii >= SEL_BLOCK).astype(I32)
    forced = (jb == 0) | (jb == cur) | (jb == cur - 1)
    score = jnp.where(jb <= cur, jnp.where(forced, FORCE_SCORE, imp), NEG_INF)
    sel = jnp.zeros((LANES, Q_TILE), jnp.bool_)
    for _ in range(SEL_TOPN):
        mx = jnp.max(score, axis=0, keepdims=True)
        first = jnp.min(jnp.where(score == mx, jb, LANES), axis=0, keepdims=True)
        pick = jb == first
        sel = sel | pick
        score = jnp.where(pick, -jnp.inf, score)
    mb = jnp.where(sel, 0.0, MASK_BIAS).T.astype(BF16)

    qaug_ref[:, 0:HEAD_DIM] = q
    qaug_ref[:, HEAD_DIM:LANES] = jnp.zeros((ROWS, LANES - HEAD_DIM), BF16)
    for h in range(HPG):
        qaug_ref[h * Q_TILE:(h + 1) * Q_TILE, LANES:2 * LANES] = mb
    qa = qaug_ref[...]
    qw = qaug_ref[:, 0:LANES]

    def flash_step(carry, s, v):
        m, acc = carry
        m_new = jnp.maximum(m, jnp.max(s, axis=1, keepdims=True))
        alpha = jnp.exp(m - m_new)
        p = jnp.exp(s - m_new)
        acc = acc * alpha + jnp.dot(p.astype(BF16), v, preferred_element_type=F32)
        return m_new, acc

    init = (jnp.full((ROWS, 1), NEG_INF, F32), jnp.zeros((ROWS, LANES), F32))

    nfar = jnp.maximum(qi - (NEAR_TILES - 2), 0) // 2
    far_bias = tn_ref[0, NEAR_TILES - 1][:, 0:1]

    def far_body(kt, carry):
        k0 = pl.multiple_of(kt * 2 * LANES, 2 * LANES)
        s = lax.dot_general(qa, ks_ref[0, pl.ds(k0, 2 * LANES), :], nt, preferred_element_type=F32) + far_bias
        return flash_step(carry, s, vs_ref[0, pl.ds(k0, 2 * LANES), :])

    def near_body(kt, carry):
        k0 = pl.multiple_of(kt * LANES, LANES)
        s = lax.dot_general(qa, ks_ref[0, pl.ds(k0, LANES), :], nt, preferred_element_type=F32)
        return flash_step(carry, s + tn_ref[0, qi - kt], vs_ref[0, pl.ds(k0, LANES), :])

    carry = lax.fori_loop(0, nfar, far_body, init)
    _, acc_s = lax.fori_loop(2 * nfar, qi + 1, near_body, carry)
    o_s = acc_s[:, 0:HEAD_DIM] * (1.0 / acc_s[:, HEAD_DIM:HEAD_DIM + 1])

    wt = WINDOW // LANES

    def win_body(kt, carry):
        k0 = pl.multiple_of(kt * LANES, LANES)
        off = qi - kt
        tab = jnp.where(off == wt, NEAR_TILES, off)
        s = lax.dot_general(qw, kw_ref[0, pl.ds(k0, LANES), :], nt, preferred_element_type=F32)
        return flash_step(carry, s + tn_ref[0, tab], vw_ref[0, pl.ds(k0, LANES), :])

    _, acc_w = lax.fori_loop(jnp.maximum(qi - wt, 0), qi + 1, win_body, init)
    o_w = acc_w[:, 0:HEAD_DIM] * (1.0 / acc_w[:, HEAD_DIM:HEAD_DIM + 1])

    gt = gt_ref[0]
    for h in range(HPG):
        sl = slice(h * Q_TILE, (h + 1) * Q_TILE)
        o = (gt[:, h:h + 1] * o_c[sl] + gt[:, HPG + h:HPG + h + 1] * o_s[sl]
             + gt[:, 2 * HPG + h:2 * HPG + h + 1] * o_w[sl])
        o_ref[0, 0, h] = o.astype(BF16)


def _attention(q5, kc, vct, ks, vs, kw, vw, gt, tn, tct, ovt):
    B, G, _, S, _ = q5.shape
    nq = S // Q_TILE
    ncp = S // CMP_STRIDE
    c0 = (Q_TILE // CMP_STRIDE) * (nq - 1)
    return pl.pallas_call(
        functools.partial(_attn_kernel, c0=c0),
        out_shape=jax.ShapeDtypeStruct((B, G, HPG, S, HEAD_DIM), BF16),
        grid=(B, G, nq),
        in_specs=[pl.BlockSpec((1, 1, HPG, Q_TILE, HEAD_DIM), lambda b, g, i: (b, g, 0, i, 0)),
                  pl.BlockSpec((1, 1, ncp, LANES), lambda b, g, i: (b, g, 0, 0)),
                  pl.BlockSpec((1, 1, LANES, ncp), lambda b, g, i: (b, g, 0, 0)),
                  pl.BlockSpec((1, S, 2 * LANES), lambda b, g, i: (b, 0, g)),
                  pl.BlockSpec((1, S, LANES), lambda b, g, i: (b, 0, g)),
                  pl.BlockSpec((1, S, LANES), lambda b, g, i: (b, 0, g)),
                  pl.BlockSpec((1, S, LANES), lambda b, g, i: (b, 0, g)),
                  pl.BlockSpec((1, Q_TILE, LANES), lambda b, g, i: (b, i, g)),
                  pl.BlockSpec((1, NEAR_TILES + 1, ROWS, LANES), lambda b, g, i: (g, 0, 0, 0)),
                  pl.BlockSpec((1, c0 + ncp, ROWS), lambda b, g, i: (g, 0, 0)),
                  pl.BlockSpec((LANES, ncp), lambda b, g, i: (0, 0))],
        out_specs=pl.BlockSpec((1, 1, HPG, Q_TILE, HEAD_DIM), lambda b, g, i: (b, g, 0, i, 0)),
        scratch_shapes=[pltpu.VMEM((ROWS, 2 * LANES), BF16)],
        compiler_params=_cparams(("arbitrary", "arbitrary", "arbitrary")),
        name="nsa_attention",
    )(q5, kc, vct, ks, vs, kw, vw, gt, tn, tct, ovt)


def _post_attn_kernel(x_ref, ypc_ref, yn_ref, wo_ref, g1_ref, n2_ref, sc_ref, sh_ref, rwh_ref, rwl_ref, rb_ref,
                      xo_ref, h2_ref, ei_ref, tw_ref, cnt_ref, run_ref):
    first = (pl.program_id(0) == 0) & (pl.program_id(1) == 0)

    @pl.when(first)
    def _():
        run_ref[...] = jnp.zeros_like(run_ref)

    tm = x_ref.shape[1]
    half = wo_ref.shape[0] // 2
    mixed = (jnp.dot(ypc_ref[0], wo_ref[0:half, :], preferred_element_type=F32)
             + jnp.dot(yn_ref[0], wo_ref[half:, :], preferred_element_type=F32))
    x = x_ref[0] + g1_ref[0] * mixed
    xo_ref[0] = x
    y = x * lax.rsqrt(jnp.mean(x * x, axis=-1, keepdims=True) + EPS) * n2_ref[...]
    h2 = y * (1.0 + sc_ref[0]) + sh_ref[0]
    h2_ref[0] = h2
    hh = h2.astype(BF16)
    hl = (h2 - hh.astype(F32)).astype(BF16)
    logit = (jnp.dot(hh, rwh_ref[...], preferred_element_type=F32)
             + jnp.dot(hl, rwh_ref[...], preferred_element_type=F32)
             + jnp.dot(hh, rwl_ref[...], preferred_element_type=F32)) + rb_ref[...]
    lane = lax.broadcasted_iota(I32, (tm, LANES), 1)
    vals, hots, idxs = [], [], []
    for _ in range(TOP_K):
        mx = jnp.max(logit, axis=1, keepdims=True)
        idx = jnp.min(jnp.where(logit == mx, lane, LANES), axis=1, keepdims=True)
        hot = lane == idx
        vals.append(mx)
        hots.append(hot)
        idxs.append(idx)
        logit = jnp.where(hot, -jnp.inf, logit)
    ex = [jnp.exp(v - vals[0]) for v in vals]
    inv = 1.0 / (ex[0] + ex[1] + ex[2] + ex[3])
    assign = (hots[0] | hots[1] | hots[2] | hots[3]).astype(BF16)
    r = lax.broadcasted_iota(I32, (tm, tm), 0)
    c = lax.broadcasted_iota(I32, (tm, tm), 1)
    before = jnp.dot((c < r).astype(BF16), assign, preferred_element_type=F32) + run_ref[...]
    ei = jnp.zeros((tm, LANES), I32)
    tw = jnp.zeros((tm, LANES), F32)
    for k in range(TOP_K):
        e_k = idxs[k]
        r_k = jnp.sum(jnp.where(hots[k], before, 0.0), axis=1, keepdims=True).astype(I32)
        ei = jnp.where(lane == k, e_k, jnp.where(lane == TOP_K + k, r_k, ei))
        tw = jnp.where(lane == k, ex[k] * inv, tw)
    ei_ref[0] = ei
    tw_ref[0] = tw
    run_ref[...] = run_ref[...] + jnp.sum(assign.astype(F32), axis=0, keepdims=True)
    cnt_ref[...] = run_ref[...]


def _post_attn(x, ypc, ynsa, w_out, g1, n2g, sc2, sh2, rw_hi, rw_lo, rb):
    B, S, D = x.shape
    tm = TOK_TILE
    tok = lambda w: pl.BlockSpec((1, tm, w), lambda b, i: (b, i, 0))
    per_b = pl.BlockSpec((1, 1, D), lambda b, i: (b, 0, 0))
    full = lambda a: pl.BlockSpec(a.shape, lambda b, i: (0,) * a.ndim)
    return pl.pallas_call(
        _post_attn_kernel,
        out_shape=[jax.ShapeDtypeStruct((B, S, D), F32), jax.ShapeDtypeStruct((B, S, D), F32),
                   jax.ShapeDtypeStruct((B, S, LANES), I32), jax.ShapeDtypeStruct((B, S, LANES), F32),
                   jax.ShapeDtypeStruct((1, LANES), F32)],
        grid=(B, S // tm),
        in_specs=[tok(D), tok(ypc.shape[-1]), tok(ynsa.shape[-1]), full(w_out), per_b,
                  pl.BlockSpec((1, D), lambda b, i: (0, 0)), per_b, per_b,
                  full(rw_hi), full(rw_lo), full(rb)],
        out_specs=[tok(D), tok(D), tok(LANES), tok(LANES), pl.BlockSpec((1, LANES), lambda b, i: (0, 0))],
        scratch_shapes=[pltpu.VMEM((1, LANES), F32)],
        compiler_params=_cparams(("arbitrary", "arbitrary")),
        name="post_attn_router",
    )(x, ypc, ynsa, w_out, g1, n2g.reshape(1, D), sc2, sh2, rw_hi, rw_lo, rb)


def _row_copies(n, make):
    def start(i, c):
        make(i).start()
        return c

    def wait(i, c):
        make(i).wait()
        return c

    lax.fori_loop(0, n, start, 0)
    lax.fori_loop(0, n, wait, 0)


def _dispatch_kernel(dest_ref, h_ref, xs_in_ref, xs_ref, sem):
    del xs_in_ref
    n = h_ref.shape[0] * TOP_K

    def make(a):
        return pltpu.make_async_copy(h_ref.at[pl.ds(a // TOP_K, 1), :],
                                     xs_ref.at[pl.ds(dest_ref[0, 0, a], 1), :], sem)

    _row_copies(n, make)


def _dispatch(dest, h2, xs_zero):
    N, D = h2.shape
    tm = TOK_TILE
    nt = N // tm
    return pl.pallas_call(
        _dispatch_kernel,
        out_shape=jax.ShapeDtypeStruct(xs_zero.shape, xs_zero.dtype),
        grid=(nt,),
        in_specs=[pl.BlockSpec((1, 1, tm * TOP_K), lambda i: (i, 0, 0), memory_space=pltpu.SMEM),
                  pl.BlockSpec((tm, D), lambda i: (i, 0)),
                  pl.BlockSpec(memory_space=pl.ANY)],
        out_specs=pl.BlockSpec(memory_space=pl.ANY),
        scratch_shapes=[pltpu.SemaphoreType.DMA(())],
        input_output_aliases={2: 0},
        compiler_params=_cparams(("arbitrary",)),
        name="moe_dispatch",
    )(dest.reshape(nt, 1, tm * TOP_K), h2, xs_zero)


def _expert_kernel(be_ref, nu_ref, x_ref, wgu_ref, bgu_ref, wd_ref, bd_ref, y_ref):
    @pl.when(pl.program_id(0) < nu_ref[0])
    def _():
        F = wd_ref.shape[1]
        gu = jnp.dot(x_ref[...].astype(BF16), wgu_ref[0], preferred_element_type=F32) + bgu_ref[0]
        gate = jnp.minimum(gu[:, :F], SWIGLU_LIMIT)
        up = jnp.clip(gu[:, F:], -SWIGLU_LIMIT, SWIGLU_LIMIT)
        act = (up + 1.0) * gate * jax.nn.sigmoid(SWIGLU_ALPHA * gate)
        y_ref[...] = jnp.dot(act.astype(BF16), wd_ref[0], preferred_element_type=F32) + bd_ref[0]


def _experts(blk_e, n_used, xs, w_gu, b_gu, w_down, b_down):
    n_slots, D = xs.shape
    E, _, F2 = w_gu.shape
    F = F2 // 2
    nb = n_slots // EXPERT_BLOCK
    blk = lambda i, be, nu: (jnp.minimum(i, nu[0] - 1), 0)
    return pl.pallas_call(
        _expert_kernel,
        out_shape=jax.ShapeDtypeStruct((n_slots, D), F32),
        grid_spec=pltpu.PrefetchScalarGridSpec(
            num_scalar_prefetch=2,
            grid=(nb,),
            in_specs=[pl.BlockSpec((EXPERT_BLOCK, D), blk),
                      pl.BlockSpec((1, D, F2), lambda i, be, nu: (be[i], 0, 0)),
                      pl.BlockSpec((1, 1, F2), lambda i, be, nu: (be[i], 0, 0)),
                      pl.BlockSpec((1, F, D), lambda i, be, nu: (be[i], 0, 0)),
                      pl.BlockSpec((1, 1, D), lambda i, be, nu: (be[i], 0, 0))],
            out_specs=pl.BlockSpec((EXPERT_BLOCK, D), blk)),
        compiler_params=_cparams(("arbitrary",)),
        name="moe_experts",
    )(blk_e, n_used, xs, w_gu, b_gu.reshape(E, 1, F2), w_down, b_down.reshape(E, 1, D))


def _combine_kernel(dest_ref, y_ref, x_ref, tw_ref, g2_ref, fg_ref, o_ref, rows_ref, sem, *, final):
    tm = x_ref.shape[1]

    def make(a):
        return pltpu.make_async_copy(y_ref.at[pl.ds(dest_ref[0, 0, a], 1), :],
                                     rows_ref.at[a % TOP_K, pl.ds(a // TOP_K, 1), :], sem)

    _row_copies(tm * TOP_K, make)
    tw = tw_ref[0]
    moe = tw[:, 0:1] * rows_ref[0]
    for k in range(1, TOP_K):
        moe = moe + tw[:, k:k + 1] * rows_ref[k]
    x = x_ref[0] + g2_ref[0] * moe
    if final:
        x = x * lax.rsqrt(jnp.mean(x * x, axis=-1, keepdims=True) + EPS) * fg_ref[...]
    o_ref[0] = x


def _combine(dest, y, x, tw, g2, final_g, final):
    B, S, D = x.shape
    tm = TOK_TILE
    nt = S // tm
    tok = lambda w: pl.BlockSpec((1, tm, w), lambda b, i: (b, i, 0))
    return pl.pallas_call(
        functools.partial(_combine_kernel, final=final),
        out_shape=jax.ShapeDtypeStruct((B, S, D), F32),
        grid=(B, nt),
        in_specs=[pl.BlockSpec((1, 1, tm * TOP_K), lambda b, i: (b * nt + i, 0, 0), memory_space=pltpu.SMEM),
                  pl.BlockSpec(memory_space=pl.ANY),
                  tok(D), tok(LANES),
                  pl.BlockSpec((1, 1, D), lambda b, i: (b, 0, 0)),
                  pl.BlockSpec((1, D), lambda b, i: (0, 0))],
        out_specs=tok(D),
        scratch_shapes=[pltpu.VMEM((TOP_K, tm, D), F32), pltpu.SemaphoreType.DMA(())],
        compiler_params=_cparams(("arbitrary", "arbitrary")),
        name="moe_combine",
    )(dest.reshape(B * nt, 1, tm * TOP_K), y, x, tw, g2, final_g.reshape(1, D))


def _moe(x, h2, ei, tw, counts, g2, w_gu, b_gu, w_down, b_down, final_g, final):
    B, S, D = x.shape
    N = B * S
    n_slots = -(-(N * TOP_K + N_EXPERTS * EXPERT_BLOCK) // EXPERT_BLOCK) * EXPERT_BLOCK
    nb = n_slots // EXPERT_BLOCK
    cnt = counts[0, :N_EXPERTS].astype(I32)
    padded = (cnt + EXPERT_BLOCK - 1) // EXPERT_BLOCK * EXPERT_BLOCK
    pend = jnp.cumsum(padded)
    pstart = pend - padded
    ei2 = ei.reshape(N, LANES)
    dest = (pstart[ei2[:, 0:TOP_K]] + ei2[:, TOP_K:2 * TOP_K]).reshape(N * TOP_K)
    blk_e = jnp.minimum(jnp.searchsorted(pend, jnp.arange(nb, dtype=I32) * EXPERT_BLOCK, side='right'),
                        N_EXPERTS - 1).astype(I32)
    n_used = (pend[-1:] // EXPERT_BLOCK).astype(I32)
    xs = _dispatch(dest, h2.reshape(N, D), jnp.zeros((n_slots, D), F32))
    y = _experts(blk_e, n_used, xs, w_gu, b_gu, w_down, b_down)
    return _combine(dest, y, x, tw, g2, final_g, final)


def kernel(x, c, w_mod, b_mod, norm1_g, norm2_g, w_in, w_out, pool_w, pool_scale, conv_w, conv_b, conv_ln_g,
           conv_ln_b, cmp_pe_k, cmp_pe_v, cmp_w1_k, cmp_w2_k, cmp_w1_v, cmp_w2_v, rel_bias, router_w, router_b,
           expert_w_gu, expert_b_gu, expert_w_down, expert_b_down, final_g):
    B, S, D = x.shape
    L = w_mod.shape[0]
    assert S % TOK_TILE == 0 and S // SEL_BLOCK <= LANES and S // Q_TILE >= 1
    mod = _modulation(c, w_mod, b_mod)
    tn, tct = _bias_tables(rel_bias, S)
    ovt = _overlap_t(S)
    cg = POOL_DIM // POOL_GROUPS
    for l in range(L):
        m6 = mod[l].reshape(B, 6, 1, D)
        sh1, sc1, g1, sh2, sc2, g2 = (m6[:, k] for k in range(6))
        upc, q, kvc, ks, vs, kw, vw, gt = _in_proj(x, norm1_g[l], sc1, sh1, _in_weight(w_in[l]))
        pw_bd = jnp.zeros((POOL_DIM, POOL_DIM), F32)
        for g in range(POOL_GROUPS):
            pw_bd = lax.dynamic_update_slice(pw_bd, pool_w[l, g], (g * cg, g * cg))
        cw = jnp.pad(conv_w[l], ((0, 1), (0, 0)))
        ypc = _pool_conv(upc, pw_bd.astype(BF16), pool_scale[l], cw, conv_b[l], conv_ln_g[l], conv_ln_b[l])
        kc, vct = _compress(kvc, _cmp_weights(cmp_w1_k[l], cmp_w1_v[l]), cmp_pe_k[l], cmp_pe_v[l],
                            cmp_w1_k[l], cmp_w1_v[l], cmp_w2_k[l], cmp_w2_v[l])
        q5 = q.reshape(B, S, KV_GROUPS, HPG, HEAD_DIM).transpose(0, 2, 3, 1, 4)
        o5 = _attention(q5, kc, vct, ks, vs, kw, vw, gt, tn, tct, ovt)
        ynsa = o5.transpose(0, 3, 1, 2, 4).reshape(B, S, NSA_DIM)
        rw = jnp.pad(router_w[l], ((0, 0), (0, LANES - N_EXPERTS)))
        rw_hi = rw.astype(BF16)
        rw_lo = (rw - rw_hi.astype(F32)).astype(BF16)
        rb = jnp.pad(router_b[l].reshape(1, -1), ((0, 0), (0, LANES - N_EXPERTS)), constant_values=NEG_INF)
        x, h2, ei, tw, counts = _post_attn(x, ypc, ynsa, w_out[l].astype(BF16), g1, norm2_g[l], sc2, sh2,
                                           rw_hi, rw_lo, rb)
        x = _moe(x, h2, ei, tw, counts, g2, expert_w_gu[l].astype(BF16), expert_b_gu[l],
                 expert_w_down[l].astype(BF16), expert_b_down[l], final_g, final=(l == L - 1))
    return x
```

```python
import functools
import math

import jax
import jax.numpy as jnp
import numpy as np
from jax import lax
from jax.experimental import pallas as pl
from jax.experimental.pallas import tpu as pltpu

F32 = jnp.float32
BF16 = jnp.bfloat16
I32 = jnp.int32

HEAD_DIM = 64
POOL_DIM = 256
POOL_GROUPS = 4
POOL_WINDOWS = (2, 4, 8, 16)
CONV_DIM = 256
CONV_WIDTH = 31
NSA_DIM = 512
NSA_HEADS = 8
KV_GROUPS = 2
HPG = 4
CMP_LEN = 32
CMP_STRIDE = 16
CMP_HIDDEN = 128
SEL_BLOCK = 64
SEL_TOPN = 16
WINDOW = 512
Q_TILE = 128
N_BUCKETS = 32
MAX_DISTANCE = 1024
N_EXPERTS = 32
TOP_K = 4
SWIGLU_ALPHA = 1.702
SWIGLU_LIMIT = 7.0
EPS = 1e-5
NEG_INF = -1e30
FORCE_SCORE = 1e4

LANES = 128
ROWS = HPG * Q_TILE
NEAR_TILES = 10
MASK_BIAS = -32768.0
TOK_TILE = 512
EXPERT_BLOCK = 512
VMEM_LIMIT = 56 * 1024 * 1024


def _cparams(sem, vmem=VMEM_LIMIT):
    return pltpu.CompilerParams(dimension_semantics=sem, vmem_limit_bytes=vmem)


def _t5_bucket(n):
    n = jnp.maximum(n, 0)
    max_exact = N_BUCKETS // 2
    nf = jnp.maximum(n, 1).astype(F32)
    large = max_exact + (jnp.log(nf / max_exact) / math.log(MAX_DISTANCE / max_exact)
                         * (N_BUCKETS - max_exact)).astype(I32)
    large = jnp.minimum(large, N_BUCKETS - 1)
    return jnp.where(n < max_exact, n, large)


def _mod_kernel(c_ref, w_ref, b_ref, o_ref):
    c = c_ref[...]
    cond = c * jax.nn.sigmoid(c)
    o_ref[0] = jnp.dot(cond.astype(BF16), w_ref[0].astype(BF16),
                       preferred_element_type=F32) + b_ref[0]


def _modulation(c, w_mod, b_mod):
    L, D, W = w_mod.shape
    B = c.shape[0]
    tn = 1536
    return pl.pallas_call(
        _mod_kernel,
        out_shape=jax.ShapeDtypeStruct((L, B, W), F32),
        grid=(L, W // tn),
        in_specs=[pl.BlockSpec((B, D), lambda l, j: (0, 0)),
                  pl.BlockSpec((1, D, tn), lambda l, j: (l, 0, j)),
                  pl.BlockSpec((1, 1, tn), lambda l, j: (l, 0, j))],
        out_specs=pl.BlockSpec((1, B, tn), lambda l, j: (l, 0, j)),
        compiler_params=_cparams(("arbitrary", "arbitrary")),
        name="modulation",
    )(c, w_mod, b_mod.reshape(L, 1, W))


C_UPC, C_Q, C_KVC, C_KS, C_VS, C_KW, C_VW, C_GT, C_END = 0, 768, 1280, 1536, 2048, 2304, 2560, 2816, 3072


def _in_weight(w_in):
    D = w_in.shape[0]
    src = np.full((C_END,), -1, np.int64)
    src[C_UPC:C_UPC + 768] = np.arange(768)
    src[C_Q:C_Q + 512] = 768 + np.arange(512)
    src[C_KVC:C_KVC + 256] = 1280 + np.arange(256)
    for g in range(KV_GROUPS):
        src[C_KS + g * 256:C_KS + g * 256 + 64] = 1536 + g * 64 + np.arange(64)
        src[C_VS + g * 128:C_VS + g * 128 + 64] = 1664 + g * 64 + np.arange(64)
        src[C_KW + g * 128:C_KW + g * 128 + 64] = 1792 + g * 64 + np.arange(64)
        src[C_VW + g * 128:C_VW + g * 128 + 64] = 1920 + g * 64 + np.arange(64)
        for br in range(3):
            for h in range(HPG):
                src[C_GT + g * 128 + br * HPG + h] = 2048 + br * NSA_HEADS + g * HPG + h
    scale = np.ones((C_END,), np.float32)
    scale[C_Q:C_Q + 512] = HEAD_DIM ** -0.5
    scale[src < 0] = 0.0
    w = jnp.take(w_in, jnp.asarray(np.maximum(src, 0)), axis=1) * jnp.asarray(scale)
    return w.astype(BF16)


def _inproj_kernel(x_ref, g_ref, sc_ref, sh_ref, w_ref,
                   upc_ref, q_ref, kvc_ref, ks_ref, vs_ref, kw_ref, vw_ref, gt_ref):
    ti = pl.program_id(1)
    x = x_ref[0]
    tm = x.shape[0]
    y = x * lax.rsqrt(jnp.mean(x * x, axis=-1, keepdims=True) + EPS) * g_ref[...]
    h = y * (1.0 + sc_ref[0]) + sh_ref[0]
    z = jnp.dot(h.astype(BF16), w_ref[...], preferred_element_type=F32)
    upc_ref[0] = z[:, C_UPC:C_Q]
    q_ref[0] = z[:, C_Q:C_KVC].astype(BF16)
    kvc_ref[0] = z[:, C_KVC:C_KS].astype(BF16)
    lane = lax.broadcasted_iota(I32, (tm, 256), 1)
    row = lax.broadcasted_iota(I32, (tm, 256), 0)
    blk = (ti * tm + row) // SEL_BLOCK
    onehot = (lane - LANES == blk).astype(F32)
    for g in range(KV_GROUPS):
        ks_ref[0, :, g * 256:(g + 1) * 256] = (z[:, C_KS + g * 256:C_KS + (g + 1) * 256] + onehot).astype(BF16)
    lane = lax.broadcasted_iota(I32, (tm, 256), 1)
    ones_col = ((lane % LANES) == HEAD_DIM).astype(F32)
    vs_ref[0] = (z[:, C_VS:C_KW] + ones_col).astype(BF16)
    kw_ref[0] = z[:, C_KW:C_VW].astype(BF16)
    vw_ref[0] = (z[:, C_VW:C_GT] + ones_col).astype(BF16)
    gt_ref[0] = jax.nn.sigmoid(z[:, C_GT:C_END])


def _in_proj(x, g1, sc, sh, w_big):
    B, S, D = x.shape
    tm = TOK_TILE
    tok = lambda w: pl.BlockSpec((1, tm, w), lambda b, i: (b, i, 0))
    outs = [(768, F32), (512, BF16), (256, BF16), (512, BF16), (256, BF16), (256, BF16), (256, BF16), (256, F32)]
    return pl.pallas_call(
        _inproj_kernel,
        out_shape=[jax.ShapeDtypeStruct((B, S, w), dt) for w, dt in outs],
        grid=(B, S // tm),
        in_specs=[tok(D),
                  pl.BlockSpec((1, D), lambda b, i: (0, 0)),
                  pl.BlockSpec((1, 1, D), lambda b, i: (b, 0, 0)),
                  pl.BlockSpec((1, 1, D), lambda b, i: (b, 0, 0)),
                  pl.BlockSpec((D, C_END), lambda b, i: (0, 0))],
        out_specs=[tok(w) for w, _ in outs],
        compiler_params=_cparams(("arbitrary", "arbitrary")),
        name="in_proj",
    )(x, g1.reshape(1, D), sc, sh, w_big)


HALO = 32


def _poolconv_kernel(cur_ref, halo_ref, pw_ref, ps_ref, cw_ref, cb_ref, lg_ref, lb_ref, o_ref, ext_ref, v_ref):
    ti = pl.program_id(1)
    ts = cur_ref.shape[1]
    halo = halo_ref[0] * (ti > 0).astype(F32)
    ext_ref[0:HALO, :] = halo
    ext_ref[HALO:HALO + ts, :] = cur_ref[0]
    u = ext_ref[HALO:HALO + ts, 0:POOL_DIM]
    lane = lax.broadcasted_iota(I32, (ts, POOL_DIM), 1)
    grp = lane // (POOL_DIM // POOL_GROUPS)
    run = u
    pooled = jnp.zeros_like(u)
    for k in range(1, POOL_WINDOWS[-1]):
        run = run + ext_ref[HALO - k:HALO - k + ts, 0:POOL_DIM]
        if (k + 1) in POOL_WINDOWS:
            pooled = jnp.where(grp == POOL_WINDOWS.index(k + 1), run, pooled)
    wlane = jnp.where(grp == 0, 2.0, jnp.where(grp == 1, 4.0, jnp.where(grp == 2, 8.0, 16.0)))
    t1 = (ti * ts + lax.broadcasted_iota(I32, (ts, POOL_DIM), 0) + 1).astype(F32)
    cnt = jnp.minimum(t1, wlane)
    pooled = pooled / cnt - u
    y_pool = jnp.dot(pooled.astype(BF16), pw_ref[...], preferred_element_type=F32) * ps_ref[...]
    o_ref[0, :, 0:POOL_DIM] = y_pool.astype(BF16)
    uv = ext_ref[:, POOL_DIM:POOL_DIM + CONV_DIM]
    ug = ext_ref[:, POOL_DIM + CONV_DIM:POOL_DIM + 2 * CONV_DIM]
    v_ref[...] = uv * jax.nn.sigmoid(ug)
    acc = jnp.zeros((ts, CONV_DIM), F32) + cb_ref[...]
    for k in range(CONV_WIDTH):
        o = HALO - (CONV_WIDTH - 1) + k
        acc = acc + v_ref[o:o + ts, :] * cw_ref[k:k + 1, :]
    mu = jnp.mean(acc, axis=-1, keepdims=True)
    d = acc - mu
    var = jnp.mean(d * d, axis=-1, keepdims=True)
    yn = d * lax.rsqrt(var + EPS) * lg_ref[...] + lb_ref[...]
    o_ref[0, :, POOL_DIM:POOL_DIM + CONV_DIM] = (yn * jax.nn.sigmoid(yn)).astype(BF16)


def _pool_conv(upc, pool_w_bd, pool_scale, conv_w, conv_b, ln_g, ln_b):
    B, S, W = upc.shape
    ts = TOK_TILE
    r = ts // HALO
    vec = lambda n: pl.BlockSpec((1, n), lambda b, i: (0, 0))
    return pl.pallas_call(
        _poolconv_kernel,
        out_shape=jax.ShapeDtypeStruct((B, S, POOL_DIM + CONV_DIM), BF16),
        grid=(B, S // ts),
        in_specs=[pl.BlockSpec((1, ts, W), lambda b, i: (b, i, 0)),
                  pl.BlockSpec((1, HALO, W), lambda b, i: (b, jnp.maximum(i * r - 1, 0), 0)),
                  pl.BlockSpec((POOL_DIM, POOL_DIM), lambda b, i: (0, 0)),
                  vec(POOL_DIM),
                  pl.BlockSpec((CONV_WIDTH + 1, CONV_DIM), lambda b, i: (0, 0)),
                  vec(CONV_DIM), vec(CONV_DIM), vec(CONV_DIM)],
        out_specs=pl.BlockSpec((1, ts, POOL_DIM + CONV_DIM), lambda b, i: (b, i, 0)),
        scratch_shapes=[pltpu.VMEM((HALO + ts, W), F32), pltpu.VMEM((HALO + ts, CONV_DIM), F32)],
        compiler_params=_cparams(("arbitrary", "arbitrary")),
        name="pool_conv",
    )(upc, upc, pool_w_bd, pool_scale.reshape(1, -1), conv_w, conv_b.reshape(1, -1),
      ln_g.reshape(1, -1), ln_b.reshape(1, -1))


N_STREAM = 2 * KV_GROUPS
CHUNK_W = CMP_STRIDE * 2 * KV_GROUPS * HEAD_DIM


def _cmp_weights(w1_k, w1_v):
    half = CMP_STRIDE * HEAD_DIM
    cols = []
    for s in range(N_STREAM):
        w1 = w1_k if s < KV_GROUPS else w1_v
        for part in range(2):
            blk = w1[part * half:(part + 1) * half].reshape(CMP_STRIDE, 1, HEAD_DIM, CMP_HIDDEN)
            z = jnp.zeros((CMP_STRIDE, N_STREAM, HEAD_DIM, CMP_HIDDEN), w1.dtype)
            z = lax.dynamic_update_slice(z, blk, (0, s, 0, 0))
            cols.append(z.reshape(CHUNK_W, CMP_HIDDEN))
    return jnp.concatenate(cols, axis=1).astype(BF16)


def _gelu_tanh(x):
    return 0.5 * x * (1.0 + jnp.tanh(math.sqrt(2.0 / math.pi) * (x + 0.044715 * (x * x * x))))


def _compress_kernel(c_ref, w_ref, pek_ref, pev_ref, w1k_ref, w1v_ref, w2k_ref, w2vt_ref, kc_ref, vct_ref):
    r = jnp.dot(c_ref[0], w_ref[...], preferred_element_type=F32)
    ncp = r.shape[0]
    pe_k = jnp.dot(pek_ref[...], w1k_ref[...], preferred_element_type=F32)[0:1]
    pe_v = jnp.dot(pev_ref[...], w1v_ref[...], preferred_element_type=F32)[0:1]
    for s in range(N_STREAM):
        a = r[:, s * 256:s * 256 + CMP_HIDDEN]
        b = r[:, s * 256 + CMP_HIDDEN:(s + 1) * 256]
        hid = a + pltpu.roll(b, ncp - 1, 0) + (pe_k if s < KV_GROUPS else pe_v)
        act = _gelu_tanh(hid).astype(BF16)
        if s < KV_GROUPS:
            kc_ref[0, s] = jnp.dot(act, w2k_ref[...], preferred_element_type=F32).astype(BF16)
        else:
            vct_ref[0, s - KV_GROUPS] = lax.dot_general(
                w2vt_ref[...], act, (((1,), (1,)), ((), ())), preferred_element_type=F32).astype(BF16)


def _compress(kvc, wcmp, pe_k, pe_v, w1_k, w1_v, w2_k, w2_v):
    B, S, _ = kvc.shape
    ncp = S // CMP_STRIDE
    chunks = kvc.reshape(B, ncp, CHUNK_W)
    pe8 = lambda pe: jnp.broadcast_to(pe.reshape(1, -1), (8, CMP_LEN * HEAD_DIM)).astype(BF16)
    w2k = jnp.pad(w2_k, ((0, 0), (0, LANES - HEAD_DIM))).astype(BF16)
    w2vt = jnp.pad(w2_v.T, ((0, LANES - HEAD_DIM), (0, 0))).astype(BF16)
    full = lambda a: pl.BlockSpec(a.shape, lambda b: (0,) * a.ndim)
    args = (wcmp, pe8(pe_k), pe8(pe_v), w1_k.astype(BF16), w1_v.astype(BF16), w2k, w2vt)
    return pl.pallas_call(
        _compress_kernel,
        out_shape=[jax.ShapeDtypeStruct((B, KV_GROUPS, ncp, LANES), BF16),
                   jax.ShapeDtypeStruct((B, KV_GROUPS, LANES, ncp), BF16)],
        grid=(B,),
        in_specs=[pl.BlockSpec((1, ncp, CHUNK_W), lambda b: (b, 0, 0))] + [full(a) for a in args],
        out_specs=[pl.BlockSpec((1, KV_GROUPS, ncp, LANES), lambda b: (b, 0, 0, 0)),
                   pl.BlockSpec((1, KV_GROUPS, LANES, ncp), lambda b: (b, 0, 0, 0))],
        compiler_params=_cparams(("arbitrary",)),
        name="compress",
    )(chunks, *args)


def _bias_tables(rel_bias, S):
    nq = S // Q_TILE
    ncp = S // CMP_STRIDE
    rb = rel_bias.reshape(N_BUCKETS, KV_GROUPS, HPG).transpose(1, 2, 0)
    i = np.arange(Q_TILE)[:, None]
    j = np.arange(LANES)[None, :]
    tabs = []
    for off in range(NEAR_TILES + 1):
        if off < NEAR_TILES:
            d = off * LANES + i - j
            valid = d >= 0
        else:
            d = (WINDOW // LANES) * LANES + i - j
            valid = d < WINDOW
        t = rb[:, :, _t5_bucket(jnp.asarray(d, I32))]
        tabs.append(jnp.where(jnp.asarray(valid)[None, None], t, NEG_INF))
    tn = jnp.stack(tabs, axis=1).reshape(KV_GROUPS, NEAR_TILES + 1, ROWS, LANES)
    c0 = (Q_TILE // CMP_STRIDE) * (nq - 1)
    r = np.arange(c0 + ncp)[:, None]
    d = np.arange(Q_TILE)[None, :] - CMP_STRIDE * (r - c0) - (CMP_LEN - 1)
    t = rb[:, :, _t5_bucket(jnp.asarray(d, I32))]
    t = jnp.where(jnp.asarray(d >= 0)[None, None], t, NEG_INF)
    tct = t.transpose(0, 2, 1, 3).reshape(KV_GROUPS, c0 + ncp, ROWS)
    return tn.astype(F32), tct.astype(F32)


def _overlap_t(S):
    ncp = S // CMP_STRIDE
    n = np.arange(ncp)[None, :]
    jb = np.arange(LANES)[:, None]
    end = n * CMP_STRIDE + CMP_LEN - 1
    start = n * CMP_STRIDE
    ov = (end >= jb * SEL_BLOCK) & (start < (jb + 1) * SEL_BLOCK) & (n < ncp - 1)
    return jnp.asarray(ov.astype(np.float32), BF16)


def _attn_kernel(q_ref, kc_ref, vct_ref, ks_ref, vs_ref, kw_ref, vw_ref, gt_ref, tn_ref, tct_ref, ovt_ref,
                 o_ref, qaug_ref, *, c0):
    qi = pl.program_id(2)
    nt = (((1,), (1,)), ((), ()))
    q = q_ref[0, 0].reshape(ROWS, HEAD_DIM)
    ncp = kc_ref.shape[2]

    st = lax.dot_general(kc_ref[0, 0][:, 0:HEAD_DIM], q, nt, preferred_element_type=F32)
    r0 = pl.multiple_of(c0 - (Q_TILE // CMP_STRIDE) * qi, 8)
    st = st + tct_ref[0, pl.ds(r0, ncp), :]
    mc = jnp.maximum(jnp.max(st, axis=0, keepdims=True), -1e20)
    pc = jnp.exp(st - mc)
    lc = jnp.sum(pc, axis=0, keepdims=True)
    pc = pc * (1.0 / jnp.maximum(lc, 1e-30))
    oct_ = jnp.dot(vct_ref[0, 0], pc.astype(BF16), preferred_element_type=F32)
    o_c = oct_.T[:, 0:HEAD_DIM]

    ps = pc[:, 0:Q_TILE]
    for h in range(1, HPG):
        ps = ps + pc[:, h * Q_TILE:(h + 1) * Q_TILE]
    ps_hi = ps.astype(BF16)
    ps_lo = (ps - ps_hi.astype(F32)).astype(BF16)
    imp = (jnp.dot(ovt_ref[...], ps_hi, preferred_element_type=F32)
           + jnp.dot(ovt_ref[...], ps_lo, preferred_element_type=F32))
    jb = lax.broadcasted_iota(I32, (LANES, Q_TILE), 0)
    ii = lax.broadcasted_iota(I32, (LANES, Q_TILE), 1)
    cur = (Q_TILE // SEL_BLOCK) * qi + (ii >= SEL_BLOCK).astype(I32)
    forced = (jb == 0) | (jb == cur) | (jb == cur - 1)
    score = jnp.where(jb <= cur, jnp.where(forced, FORCE_SCORE, imp), NEG_INF)
    sel = jnp.zeros((LANES, Q_TILE), jnp.bool_)
    for _ in range(SEL_TOPN):
        mx = jnp.max(score, axis=0, keepdims=True)
        first = jnp.min(jnp.where(score == mx, jb, LANES), axis=0, keepdims=True)
        pick = jb == first
        sel = sel | pick
        score = jnp.where(pick, -jnp.inf, score)
    mb = jnp.where(sel, 0.0, MASK_BIAS).T.astype(BF16)

    qaug_ref[:, 0:HEAD_DIM] = q
    qaug_ref[:, HEAD_DIM:LANES] = jnp.zeros((ROWS, LANES - HEAD_DIM), BF16)
    for h in range(HPG):
        qaug_ref[h * Q_TILE:(h + 1) * Q_TILE, LANES:2 * LANES] = mb
    qa = qaug_ref[...]
    qw = qaug_ref[:, 0:LANES]

    def flash_step(carry, s, v):
        m, acc = carry
        m_new = jnp.maximum(m, jnp.max(s, axis=1, keepdims=True))
        alpha = jnp.exp(m - m_new)
        p = jnp.exp(s - m_new)
        acc = acc * alpha + jnp.dot(p.astype(BF16), v, preferred_element_type=F32)
        return m_new, acc

    init = (jnp.full((ROWS, 1), NEG_INF, F32), jnp.zeros((ROWS, LANES), F32))

    nfar = jnp.maximum(qi - (NEAR_TILES - 2), 0) // 2
    far_bias = tn_ref[0, NEAR_TILES - 1][:, 0:1]

    def far_body(kt, carry):
        k0 = pl.multiple_of(kt * 2 * LANES, 2 * LANES)
        s = lax.dot_general(qa, ks_ref[0, pl.ds(k0, 2 * LANES), :], nt, preferred_element_type=F32) + far_bias
        return flash_step(carry, s, vs_ref[0, pl.ds(k0, 2 * LANES), :])

    def near_body(kt, carry):
        k0 = pl.multiple_of(kt * LANES, LANES)
        s = lax.dot_general(qa, ks_ref[0, pl.ds(k0, LANES), :], nt, preferred_element_type=F32)
        return flash_step(carry, s + tn_ref[0, qi - kt], vs_ref[0, pl.ds(k0, LANES), :])

    carry = lax.fori_loop(0, nfar, far_body, init)
    _, acc_s = lax.fori_loop(2 * nfar, qi + 1, near_body, carry)
    o_s = acc_s[:, 0:HEAD_DIM] * (1.0 / acc_s[:, HEAD_DIM:HEAD_DIM + 1])

    wt = WINDOW // LANES

    def win_body(kt, carry):
        k0 = pl.multiple_of(kt * LANES, LANES)
        off = qi - kt
        tab = jnp.where(off == wt, NEAR_TILES, off)
        s = lax.dot_general(qw, kw_ref[0, pl.ds(k0, LANES), :], nt, preferred_element_type=F32)
        return flash_step(carry, s + tn_ref[0, tab], vw_ref[0, pl.ds(k0, LANES), :])

    _, acc_w = lax.fori_loop(jnp.maximum(qi - wt, 0), qi + 1, win_body, init)
    o_w = acc_w[:, 0:HEAD_DIM] * (1.0 / acc_w[:, HEAD_DIM:HEAD_DIM + 1])

    gt = gt_ref[0]
    for h in range(HPG):
        sl = slice(h * Q_TILE, (h + 1) * Q_TILE)
        o = (gt[:, h:h + 1] * o_c[sl] + gt[:, HPG + h:HPG + h + 1] * o_s[sl]
             + gt[:, 2 * HPG + h:2 * HPG + h + 1] * o_w[sl])
        o_ref[0, 0, h] = o.astype(BF16)


def _attention(q5, kc, vct, ks, vs, kw, vw, gt, tn, tct, ovt):
    B, G, _, S, _ = q5.shape
    nq = S // Q_TILE
    ncp = S // CMP_STRIDE
    c0 = (Q_TILE // CMP_STRIDE) * (nq - 1)
    return pl.pallas_call(
        functools.partial(_attn_kernel, c0=c0),
        out_shape=jax.ShapeDtypeStruct((B, G, HPG, S, HEAD_DIM), BF16),
        grid=(B, G, nq),
        in_specs=[pl.BlockSpec((1, 1, HPG, Q_TILE, HEAD_DIM), lambda b, g, i: (b, g, 0, i, 0)),
                  pl.BlockSpec((1, 1, ncp, LANES), lambda b, g, i: (b, g, 0, 0)),
                  pl.BlockSpec((1, 1, LANES, ncp), lambda b, g, i: (b, g, 0, 0)),
                  pl.BlockSpec((1, S, 2 * LANES), lambda b, g, i: (b, 0, g)),
                  pl.BlockSpec((1, S, LANES), lambda b, g, i: (b, 0, g)),
                  pl.BlockSpec((1, S, LANES), lambda b, g, i: (b, 0, g)),
                  pl.BlockSpec((1, S, LANES), lambda b, g, i: (b, 0, g)),
                  pl.BlockSpec((1, Q_TILE, LANES), lambda b, g, i: (b, i, g)),
                  pl.BlockSpec((1, NEAR_TILES + 1, ROWS, LANES), lambda b, g, i: (g, 0, 0, 0)),
                  pl.BlockSpec((1, c0 + ncp, ROWS), lambda b, g, i: (g, 0, 0)),
                  pl.BlockSpec((LANES, ncp), lambda b, g, i: (0, 0))],
        out_specs=pl.BlockSpec((1, 1, HPG, Q_TILE, HEAD_DIM), lambda b, g, i: (b, g, 0, i, 0)),
        scratch_shapes=[pltpu.VMEM((ROWS, 2 * LANES), BF16)],
        compiler_params=_cparams(("arbitrary", "arbitrary", "arbitrary")),
        name="nsa_attention",
    )(q5, kc, vct, ks, vs, kw, vw, gt, tn, tct, ovt)


def _post_attn_kernel(x_ref, ypc_ref, yn_ref, wo_ref, g1_ref, n2_ref, sc_ref, sh_ref, rwh_ref, rwl_ref, rb_ref,
                      xo_ref, h2_ref, ei_ref, tw_ref, cnt_ref, run_ref):
    first = (pl.program_id(0) == 0) & (pl.program_id(1) == 0)

    @pl.when(first)
    def _():
        run_ref[...] = jnp.zeros_like(run_ref)

    tm = x_ref.shape[1]
    half = wo_ref.shape[0] // 2
    mixed = (jnp.dot(ypc_ref[0], wo_ref[0:half, :], preferred_element_type=F32)
             + jnp.dot(yn_ref[0], wo_ref[half:, :], preferred_element_type=F32))
    x = x_ref[0] + g1_ref[0] * mixed
    xo_ref[0] = x
    y = x * lax.rsqrt(jnp.mean(x * x, axis=-1, keepdims=True) + EPS) * n2_ref[...]
    h2 = y * (1.0 + sc_ref[0]) + sh_ref[0]
    _rows_to_tiles(h2_ref, h2)
    hh = h2.astype(BF16)
    hl = (h2 - hh.astype(F32)).astype(BF16)
    logit = (jnp.dot(hh, rwh_ref[...], preferred_element_type=F32)
             + jnp.dot(hl, rwh_ref[...], preferred_element_type=F32)
             + jnp.dot(hh, rwl_ref[...], preferred_element_type=F32)) + rb_ref[...]
    lane = lax.broadcasted_iota(I32, (tm, LANES), 1)
    vals, hots, idxs = [], [], []
    for _ in range(TOP_K):
        mx = jnp.max(logit, axis=1, keepdims=True)
        idx = jnp.min(jnp.where(logit == mx, lane, LANES), axis=1, keepdims=True)
        hot = lane == idx
        vals.append(mx)
        hots.append(hot)
        idxs.append(idx)
        logit = jnp.where(hot, -jnp.inf, logit)
    ex = [jnp.exp(v - vals[0]) for v in vals]
    inv = 1.0 / (ex[0] + ex[1] + ex[2] + ex[3])
    assign = (hots[0] | hots[1] | hots[2] | hots[3]).astype(BF16)
    r = lax.broadcasted_iota(I32, (tm, tm), 0)
    c = lax.broadcasted_iota(I32, (tm, tm), 1)
    before = jnp.dot((c < r).astype(BF16), assign, preferred_element_type=F32) + run_ref[...]
    ei = jnp.zeros((tm, LANES), I32)
    tw = jnp.zeros((tm, LANES), F32)
    for k in range(TOP_K):
        e_k = idxs[k]
        r_k = jnp.sum(jnp.where(hots[k], before, 0.0), axis=1, keepdims=True).astype(I32)
        ei = jnp.where(lane == k, e_k, jnp.where(lane == TOP_K + k, r_k, ei))
        tw = jnp.where(lane == k, ex[k] * inv, tw)
    ei_ref[0] = ei
    tw_ref[0] = tw
    run_ref[...] = run_ref[...] + jnp.sum(assign.astype(F32), axis=0, keepdims=True)
    cnt_ref[...] = run_ref[...]


def _post_attn(x, ypc, ynsa, w_out, g1, n2g, sc2, sh2, rw_hi, rw_lo, rb):
    B, S, D = x.shape
    tm = TOK_TILE
    tok = lambda w: pl.BlockSpec((1, tm, w), lambda b, i: (b, i, 0))
    per_b = pl.BlockSpec((1, 1, D), lambda b, i: (b, 0, 0))
    full = lambda a: pl.BlockSpec(a.shape, lambda b, i: (0,) * a.ndim)
    return pl.pallas_call(
        _post_attn_kernel,
        out_shape=[jax.ShapeDtypeStruct((B, S, D), F32), jax.ShapeDtypeStruct((B * S * ROW_TILES, LANES), F32),
                   jax.ShapeDtypeStruct((B, S, LANES), I32), jax.ShapeDtypeStruct((B, S, LANES), F32),
                   jax.ShapeDtypeStruct((1, LANES), F32)],
        grid=(B, S // tm),
        in_specs=[tok(D), tok(ypc.shape[-1]), tok(ynsa.shape[-1]), full(w_out), per_b,
                  pl.BlockSpec((1, D), lambda b, i: (0, 0)), per_b, per_b,
                  full(rw_hi), full(rw_lo), full(rb)],
        out_specs=[tok(D), pl.BlockSpec((tm * ROW_TILES, LANES), lambda b, i: (b * (S // tm) + i, 0)),
                   tok(LANES), tok(LANES), pl.BlockSpec((1, LANES), lambda b, i: (0, 0))],
        scratch_shapes=[pltpu.VMEM((1, LANES), F32)],
        compiler_params=_cparams(("arbitrary", "arbitrary")),
        name="post_attn_router",
    )(x, ypc, ynsa, w_out, g1, n2g.reshape(1, D), sc2, sh2, rw_hi, rw_lo, rb)


ROW_TILES = 8
DMA_UNROLL = 2


def _rows_to_tiles(ref, val):
    n = val.shape[0]
    for s in range(ROW_TILES):
        ref[pl.ds(s, n, stride=ROW_TILES), :] = val[:, s * LANES:(s + 1) * LANES]


def _tiles_to_rows(ref, n):
    return jnp.concatenate([ref[pl.ds(s, n, stride=ROW_TILES), :] for s in range(ROW_TILES)], axis=1)


def _tile_at(ref, i):
    return ref.at[pl.ds(pl.multiple_of(i * ROW_TILES, ROW_TILES), ROW_TILES), :]


def _dispatch_kernel(dest_ref, h_ref, xs_in_ref, xs_ref, sem):
    del xs_in_ref
    tm = h_ref.shape[0] // ROW_TILES

    def body(i, c):
        for u in range(DMA_UNROLL):
            t = i * DMA_UNROLL + u
            src = _tile_at(h_ref, t)
            for k in range(TOP_K):
                pltpu.make_async_copy(src, _tile_at(xs_ref, dest_ref[0, 0, t * TOP_K + k]), sem).start()
        return c

    lax.fori_loop(0, tm // DMA_UNROLL, body, 0)
    for _ in range(TOP_K):
        pltpu.make_async_copy(h_ref, xs_ref.at[pl.ds(0, tm * ROW_TILES), :], sem).wait()


def _dispatch(dest, h2t, xs_zero):
    tm = TOK_TILE
    nt = h2t.shape[0] // (tm * ROW_TILES)
    return pl.pallas_call(
        _dispatch_kernel,
        out_shape=jax.ShapeDtypeStruct(xs_zero.shape, xs_zero.dtype),
        grid=(nt,),
        in_specs=[pl.BlockSpec((1, 1, tm * TOP_K), lambda i: (i, 0, 0), memory_space=pltpu.SMEM),
                  pl.BlockSpec((tm * ROW_TILES, LANES), lambda i: (i, 0)),
                  pl.BlockSpec(memory_space=pl.ANY)],
        out_specs=pl.BlockSpec(memory_space=pl.ANY),
        scratch_shapes=[pltpu.SemaphoreType.DMA(())],
        input_output_aliases={2: 0},
        compiler_params=_cparams(("arbitrary",)),
        name="moe_dispatch",
    )(dest.reshape(nt, 1, tm * TOP_K), h2t, xs_zero)


def _expert_kernel(be_ref, nu_ref, x_ref, wgu_ref, bgu_ref, wd_ref, bd_ref, y_ref):
    @pl.when(pl.program_id(0) < nu_ref[0])
    def _():
        F = wd_ref.shape[1]
        x = _tiles_to_rows(x_ref, EXPERT_BLOCK).astype(BF16)
        gu = jnp.dot(x, wgu_ref[0], preferred_element_type=F32) + bgu_ref[0]
        gate = jnp.minimum(gu[:, :F], SWIGLU_LIMIT)
        up = jnp.clip(gu[:, F:], -SWIGLU_LIMIT, SWIGLU_LIMIT)
        act = (up + 1.0) * gate * jax.nn.sigmoid(SWIGLU_ALPHA * gate)
        _rows_to_tiles(y_ref, jnp.dot(act.astype(BF16), wd_ref[0], preferred_element_type=F32) + bd_ref[0])


def _experts(blk_e, n_used, xs, w_gu, b_gu, w_down, b_down):
    E, D, F2 = w_gu.shape
    F = F2 // 2
    rows = EXPERT_BLOCK * ROW_TILES
    nb = xs.shape[0] // rows
    blk = lambda i, be, nu: (jnp.minimum(i, nu[0] - 1), 0)
    return pl.pallas_call(
        _expert_kernel,
        out_shape=jax.ShapeDtypeStruct(xs.shape, F32),
        grid_spec=pltpu.PrefetchScalarGridSpec(
            num_scalar_prefetch=2,
            grid=(nb,),
            in_specs=[pl.BlockSpec((rows, LANES), blk),
                      pl.BlockSpec((1, D, F2), lambda i, be, nu: (be[i], 0, 0)),
                      pl.BlockSpec((1, 1, F2), lambda i, be, nu: (be[i], 0, 0)),
                      pl.BlockSpec((1, F, D), lambda i, be, nu: (be[i], 0, 0)),
                      pl.BlockSpec((1, 1, D), lambda i, be, nu: (be[i], 0, 0))],
            out_specs=pl.BlockSpec((rows, LANES), blk)),
        compiler_params=_cparams(("arbitrary",)),
        name="moe_experts",
    )(blk_e, n_used, xs, w_gu, b_gu.reshape(E, 1, F2), w_down, b_down.reshape(E, 1, D))


def _combine_kernel(dest_ref, y_ref, x_ref, tw_ref, g2_ref, fg_ref, o_ref, rows_ref, sem, *, final):
    tm = x_ref.shape[1]

    def body(i, c):
        for u in range(DMA_UNROLL):
            t = i * DMA_UNROLL + u
            for k in range(TOP_K):
                pltpu.make_async_copy(_tile_at(y_ref, dest_ref[0, 0, t * TOP_K + k]),
                                      _tile_at(rows_ref.at[k], t), sem).start()
        return c

    lax.fori_loop(0, tm // DMA_UNROLL, body, 0)
    for k in range(TOP_K):
        pltpu.make_async_copy(y_ref.at[pl.ds(0, tm * ROW_TILES), :], rows_ref.at[k], sem).wait()
    tw = tw_ref[0]
    moe = tw[:, 0:1] * _tiles_to_rows(rows_ref.at[0], tm)
    for k in range(1, TOP_K):
        moe = moe + tw[:, k:k + 1] * _tiles_to_rows(rows_ref.at[k], tm)
    x = x_ref[0] + g2_ref[0] * moe
    if final:
        x = x * lax.rsqrt(jnp.mean(x * x, axis=-1, keepdims=True) + EPS) * fg_ref[...]
    o_ref[0] = x


def _combine(dest, y, x, tw, g2, final_g, final):
    B, S, D = x.shape
    tm = TOK_TILE
    nt = S // tm
    tok = lambda w: pl.BlockSpec((1, tm, w), lambda b, i: (b, i, 0))
    return pl.pallas_call(
        functools.partial(_combine_kernel, final=final),
        out_shape=jax.ShapeDtypeStruct((B, S, D), F32),
        grid=(B, nt),
        in_specs=[pl.BlockSpec((1, 1, tm * TOP_K), lambda b, i: (b * nt + i, 0, 0), memory_space=pltpu.SMEM),
                  pl.BlockSpec(memory_space=pl.ANY),
                  tok(D), tok(LANES),
                  pl.BlockSpec((1, 1, D), lambda b, i: (b, 0, 0)),
                  pl.BlockSpec((1, D), lambda b, i: (0, 0))],
        out_specs=tok(D),
        scratch_shapes=[pltpu.VMEM((TOP_K, tm * ROW_TILES, LANES), F32), pltpu.SemaphoreType.DMA(())],
        compiler_params=_cparams(("arbitrary", "arbitrary")),
        name="moe_combine",
    )(dest.reshape(B * nt, 1, tm * TOP_K), y, x, tw, g2, final_g.reshape(1, D))


def _moe(x, h2, ei, tw, counts, g2, w_gu, b_gu, w_down, b_down, final_g, final):
    B, S, D = x.shape
    N = B * S
    n_slots = -(-(N * TOP_K + N_EXPERTS * EXPERT_BLOCK) // EXPERT_BLOCK) * EXPERT_BLOCK
    nb = n_slots // EXPERT_BLOCK
    cnt = counts[0, :N_EXPERTS].astype(I32)
    padded = (cnt + EXPERT_BLOCK - 1) // EXPERT_BLOCK * EXPERT_BLOCK
    pend = jnp.cumsum(padded)
    pstart = pend - padded
    ei2 = ei.reshape(N, LANES)
    dest = (pstart[ei2[:, 0:TOP_K]] + ei2[:, TOP_K:2 * TOP_K]).reshape(N * TOP_K)
    blk_e = jnp.minimum(jnp.searchsorted(pend, jnp.arange(nb, dtype=I32) * EXPERT_BLOCK, side='right'),
                        N_EXPERTS - 1).astype(I32)
    n_used = (pend[-1:] // EXPERT_BLOCK).astype(I32)
    xs = _dispatch(dest, h2, jnp.zeros((n_slots * ROW_TILES, LANES), F32))
    y = _experts(blk_e, n_used, xs, w_gu, b_gu, w_down, b_down)
    return _combine(dest, y, x, tw, g2, final_g, final)


def kernel(x, c, w_mod, b_mod, norm1_g, norm2_g, w_in, w_out, pool_w, pool_scale, conv_w, conv_b, conv_ln_g,
           conv_ln_b, cmp_pe_k, cmp_pe_v, cmp_w1_k, cmp_w2_k, cmp_w1_v, cmp_w2_v, rel_bias, router_w, router_b,
           expert_w_gu, expert_b_gu, expert_w_down, expert_b_down, final_g):
    B, S, D = x.shape
    L = w_mod.shape[0]
    assert S % TOK_TILE == 0 and S // SEL_BLOCK <= LANES and D == ROW_TILES * LANES
    mod = _modulation(c, w_mod, b_mod)
    tn, tct = _bias_tables(rel_bias, S)
    ovt = _overlap_t(S)
    cg = POOL_DIM // POOL_GROUPS
    for l in range(L):
        m6 = mod[l].reshape(B, 6, 1, D)
        sh1, sc1, g1, sh2, sc2, g2 = (m6[:, k] for k in range(6))
        upc, q, kvc, ks, vs, kw, vw, gt = _in_proj(x, norm1_g[l], sc1, sh1, _in_weight(w_in[l]))
        pw_bd = jnp.zeros((POOL_DIM, POOL_DIM), F32)
        for g in range(POOL_GROUPS):
            pw_bd = lax.dynamic_update_slice(pw_bd, pool_w[l, g], (g * cg, g * cg))
        cw = jnp.pad(conv_w[l], ((0, 1), (0, 0)))
        ypc = _pool_conv(upc, pw_bd.astype(BF16), pool_scale[l], cw, conv_b[l], conv_ln_g[l], conv_ln_b[l])
        kc, vct = _compress(kvc, _cmp_weights(cmp_w1_k[l], cmp_w1_v[l]), cmp_pe_k[l], cmp_pe_v[l],
                            cmp_w1_k[l], cmp_w1_v[l], cmp_w2_k[l], cmp_w2_v[l])
        q5 = q.reshape(B, S, KV_GROUPS, HPG, HEAD_DIM).transpose(0, 2, 3, 1, 4)
        o5 = _attention(q5, kc, vct, ks, vs, kw, vw, gt, tn, tct, ovt)
        ynsa = o5.transpose(0, 3, 1, 2, 4).reshape(B, S, NSA_DIM)
        rw = jnp.pad(router_w[l], ((0, 0), (0, LANES - N_EXPERTS)))
        rw_hi = rw.astype(BF16)
        rw_lo = (rw - rw_hi.astype(F32)).astype(BF16)
        rb = jnp.pad(router_b[l].reshape(1, -1), ((0, 0), (0, LANES - N_EXPERTS)), constant_values=NEG_INF)
        x, h2, ei, tw, counts = _post_attn(x, ypc, ynsa, w_out[l].astype(BF16), g1, norm2_g[l], sc2, sh2,
                                           rw_hi, rw_lo, rb)
        x = _moe(x, h2, ei, tw, counts, g2, expert_w_gu[l].astype(BF16), expert_b_gu[l],
                 expert_w_down[l].astype(BF16), expert_b_down[l], final_g, final=(l == L - 1))
    return x
```

```python
import functools
import math

import jax
import jax.numpy as jnp
import numpy as np
from jax import lax
from jax.experimental import pallas as pl
from jax.experimental.pallas import tpu as pltpu

F32 = jnp.float32
BF16 = jnp.bfloat16
I32 = jnp.int32

HEAD_DIM = 64
POOL_DIM = 256
POOL_GROUPS = 4
POOL_WINDOWS = (2, 4, 8, 16)
CONV_DIM = 256
CONV_WIDTH = 31
NSA_DIM = 512
NSA_HEADS = 8
KV_GROUPS = 2
HPG = 4
CMP_LEN = 32
CMP_STRIDE = 16
CMP_HIDDEN = 128
SEL_BLOCK = 64
SEL_TOPN = 16
WINDOW = 512
Q_TILE = 128
N_BUCKETS = 32
MAX_DISTANCE = 1024
N_EXPERTS = 32
TOP_K = 4
SWIGLU_ALPHA = 1.702
SWIGLU_LIMIT = 7.0
EPS = 1e-5
NEG_INF = -1e30
FORCE_SCORE = 1e4

LANES = 128
ROWS = HPG * Q_TILE
NEAR_KEYS = 1280
WIN_KEYS = WINDOW + Q_TILE
FAR_TILE = 512
KEY_PAD = NEAR_KEYS - Q_TILE
MASK_BIAS = -32768.0
TOK_TILE = 512
EXPERT_BLOCK = 512
VMEM_LIMIT = 56 * 1024 * 1024


def _cparams(sem, vmem=VMEM_LIMIT):
    return pltpu.CompilerParams(dimension_semantics=sem, vmem_limit_bytes=vmem)


def _t5_bucket(n):
    n = jnp.maximum(n, 0)
    max_exact = N_BUCKETS // 2
    nf = jnp.maximum(n, 1).astype(F32)
    large = max_exact + (jnp.log(nf / max_exact) / math.log(MAX_DISTANCE / max_exact)
                         * (N_BUCKETS - max_exact)).astype(I32)
    large = jnp.minimum(large, N_BUCKETS - 1)
    return jnp.where(n < max_exact, n, large)


def _mod_kernel(c_ref, w_ref, b_ref, o_ref):
    c = c_ref[...]
    cond = c * jax.nn.sigmoid(c)
    o_ref[0] = jnp.dot(cond.astype(BF16), w_ref[0].astype(BF16),
                       preferred_element_type=F32) + b_ref[0]


def _modulation(c, w_mod, b_mod):
    L, D, W = w_mod.shape
    B = c.shape[0]
    tn = 1536
    return pl.pallas_call(
        _mod_kernel,
        out_shape=jax.ShapeDtypeStruct((L, B, W), F32),
        grid=(L, W // tn),
        in_specs=[pl.BlockSpec((B, D), lambda l, j: (0, 0)),
                  pl.BlockSpec((1, D, tn), lambda l, j: (l, 0, j)),
                  pl.BlockSpec((1, 1, tn), lambda l, j: (l, 0, j))],
        out_specs=pl.BlockSpec((1, B, tn), lambda l, j: (l, 0, j)),
        compiler_params=_cparams(("arbitrary", "arbitrary")),
        name="modulation",
    )(c, w_mod, b_mod.reshape(L, 1, W))


C_UPC, C_Q, C_KVC, C_KS, C_VS, C_KW, C_VW, C_GT, C_END = 0, 768, 1280, 1536, 2048, 2304, 2560, 2816, 3072


def _in_weight(w_in):
    D = w_in.shape[0]
    src = np.full((C_END,), -1, np.int64)
    src[C_UPC:C_UPC + 768] = np.arange(768)
    src[C_Q:C_Q + 512] = 768 + np.arange(512)
    src[C_KVC:C_KVC + 256] = 1280 + np.arange(256)
    for g in range(KV_GROUPS):
        src[C_KS + g * 256:C_KS + g * 256 + 64] = 1536 + g * 64 + np.arange(64)
        src[C_VS + g * 128:C_VS + g * 128 + 64] = 1664 + g * 64 + np.arange(64)
        src[C_KW + g * 128:C_KW + g * 128 + 64] = 1792 + g * 64 + np.arange(64)
        src[C_VW + g * 128:C_VW + g * 128 + 64] = 1920 + g * 64 + np.arange(64)
        for br in range(3):
            for h in range(HPG):
                src[C_GT + g * 128 + br * HPG + h] = 2048 + br * NSA_HEADS + g * HPG + h
    scale = np.ones((C_END,), np.float32)
    scale[C_Q:C_Q + 512] = HEAD_DIM ** -0.5
    scale[src < 0] = 0.0
    w = jnp.take(w_in, jnp.asarray(np.maximum(src, 0)), axis=1) * jnp.asarray(scale)
    return w.astype(BF16)


def _inproj_kernel(x_ref, g_ref, sc_ref, sh_ref, w_ref,
                   upc_ref, q_ref, kvc_ref, ks_ref, vs_ref, kw_ref, vw_ref, gt_ref):
    ti = pl.program_id(1)
    x = x_ref[0]
    tm = x.shape[0]
    y = x * lax.rsqrt(jnp.mean(x * x, axis=-1, keepdims=True) + EPS) * g_ref[...]
    h = y * (1.0 + sc_ref[0]) + sh_ref[0]
    z = jnp.dot(h.astype(BF16), w_ref[...], preferred_element_type=F32)
    upc_ref[0] = z[:, C_UPC:C_Q]
    q_ref[0] = z[:, C_Q:C_KVC].astype(BF16)
    kvc_ref[0] = z[:, C_KVC:C_KS].astype(BF16)
    lane = lax.broadcasted_iota(I32, (tm, 256), 1)
    row = lax.broadcasted_iota(I32, (tm, 256), 0)
    blk = (ti * tm + row) // SEL_BLOCK
    onehot = ((lane - LANES == blk) | (lane == HEAD_DIM + 1) | (lane == HEAD_DIM + 2)).astype(F32)
    for g in range(KV_GROUPS):
        ks_ref[0, :, g * 256:(g + 1) * 256] = (z[:, C_KS + g * 256:C_KS + (g + 1) * 256] + onehot).astype(BF16)
    lane = lax.broadcasted_iota(I32, (tm, 256), 1)
    ones_col = ((lane % LANES) == HEAD_DIM).astype(F32)
    vs_ref[0] = (z[:, C_VS:C_KW] + ones_col).astype(BF16)
    kw_ref[0] = z[:, C_KW:C_VW].astype(BF16)
    vw_ref[0] = (z[:, C_VW:C_GT] + ones_col).astype(BF16)
    gt_ref[0] = jax.nn.sigmoid(z[:, C_GT:C_END])


def _in_proj(x, g1, sc, sh, w_big):
    B, S, D = x.shape
    tm = TOK_TILE
    tok = lambda w: pl.BlockSpec((1, tm, w), lambda b, i: (b, i, 0))
    outs = [(768, F32), (512, BF16), (256, BF16), (512, BF16), (256, BF16), (256, BF16), (256, BF16), (256, F32)]
    return pl.pallas_call(
        _inproj_kernel,
        out_shape=[jax.ShapeDtypeStruct((B, S, w), dt) for w, dt in outs],
        grid=(B, S // tm),
        in_specs=[tok(D),
                  pl.BlockSpec((1, D), lambda b, i: (0, 0)),
                  pl.BlockSpec((1, 1, D), lambda b, i: (b, 0, 0)),
                  pl.BlockSpec((1, 1, D), lambda b, i: (b, 0, 0)),
                  pl.BlockSpec((D, C_END), lambda b, i: (0, 0))],
        out_specs=[tok(w) for w, _ in outs],
        compiler_params=_cparams(("arbitrary", "arbitrary")),
        name="in_proj",
    )(x, g1.reshape(1, D), sc, sh, w_big)


HALO = 32


def _poolconv_kernel(cur_ref, halo_ref, pw_ref, ps_ref, cw_ref, cb_ref, lg_ref, lb_ref, o_ref, ext_ref, v_ref):
    ti = pl.program_id(1)
    ts = cur_ref.shape[1]
    halo = halo_ref[0] * (ti > 0).astype(F32)
    ext_ref[0:HALO, :] = halo
    ext_ref[HALO:HALO + ts, :] = cur_ref[0]
    u = ext_ref[HALO:HALO + ts, 0:POOL_DIM]
    lane = lax.broadcasted_iota(I32, (ts, POOL_DIM), 1)
    grp = lane // (POOL_DIM // POOL_GROUPS)
    run = u
    pooled = jnp.zeros_like(u)
    for k in range(1, POOL_WINDOWS[-1]):
        run = run + ext_ref[HALO - k:HALO - k + ts, 0:POOL_DIM]
        if (k + 1) in POOL_WINDOWS:
            pooled = jnp.where(grp == POOL_WINDOWS.index(k + 1), run, pooled)
    wlane = jnp.where(grp == 0, 2.0, jnp.where(grp == 1, 4.0, jnp.where(grp == 2, 8.0, 16.0)))
    t1 = (ti * ts + lax.broadcasted_iota(I32, (ts, POOL_DIM), 0) + 1).astype(F32)
    cnt = jnp.minimum(t1, wlane)
    pooled = pooled / cnt - u
    y_pool = jnp.dot(pooled.astype(BF16), pw_ref[...], preferred_element_type=F32) * ps_ref[...]
    o_ref[0, :, 0:POOL_DIM] = y_pool.astype(BF16)
    uv = ext_ref[:, POOL_DIM:POOL_DIM + CONV_DIM]
    ug = ext_ref[:, POOL_DIM + CONV_DIM:POOL_DIM + 2 * CONV_DIM]
    v_ref[...] = uv * jax.nn.sigmoid(ug)
    acc = jnp.zeros((ts, CONV_DIM), F32) + cb_ref[...]
    for k in range(CONV_WIDTH):
        o = HALO - (CONV_WIDTH - 1) + k
        acc = acc + v_ref[o:o + ts, :] * cw_ref[k:k + 1, :]
    mu = jnp.mean(acc, axis=-1, keepdims=True)
    d = acc - mu
    var = jnp.mean(d * d, axis=-1, keepdims=True)
    yn = d * lax.rsqrt(var + EPS) * lg_ref[...] + lb_ref[...]
    o_ref[0, :, POOL_DIM:POOL_DIM + CONV_DIM] = (yn * jax.nn.sigmoid(yn)).astype(BF16)


def _pool_conv(upc, pool_w_bd, pool_scale, conv_w, conv_b, ln_g, ln_b):
    B, S, W = upc.shape
    ts = TOK_TILE
    r = ts // HALO
    vec = lambda n: pl.BlockSpec((1, n), lambda b, i: (0, 0))
    return pl.pallas_call(
        _poolconv_kernel,
        out_shape=jax.ShapeDtypeStruct((B, S, POOL_DIM + CONV_DIM), BF16),
        grid=(B, S // ts),
        in_specs=[pl.BlockSpec((1, ts, W), lambda b, i: (b, i, 0)),
                  pl.BlockSpec((1, HALO, W), lambda b, i: (b, jnp.maximum(i * r - 1, 0), 0)),
                  pl.BlockSpec((POOL_DIM, POOL_DIM), lambda b, i: (0, 0)),
                  vec(POOL_DIM),
                  pl.BlockSpec((CONV_WIDTH + 1, CONV_DIM), lambda b, i: (0, 0)),
                  vec(CONV_DIM), vec(CONV_DIM), vec(CONV_DIM)],
        out_specs=pl.BlockSpec((1, ts, POOL_DIM + CONV_DIM), lambda b, i: (b, i, 0)),
        scratch_shapes=[pltpu.VMEM((HALO + ts, W), F32), pltpu.VMEM((HALO + ts, CONV_DIM), F32)],
        compiler_params=_cparams(("arbitrary", "arbitrary")),
        name="pool_conv",
    )(upc, upc, pool_w_bd, pool_scale.reshape(1, -1), conv_w, conv_b.reshape(1, -1),
      ln_g.reshape(1, -1), ln_b.reshape(1, -1))


N_STREAM = 2 * KV_GROUPS
CHUNK_W = CMP_STRIDE * 2 * KV_GROUPS * HEAD_DIM


def _cmp_weights(w1_k, w1_v):
    half = CMP_STRIDE * HEAD_DIM
    cols = []
    for s in range(N_STREAM):
        w1 = w1_k if s < KV_GROUPS else w1_v
        for part in range(2):
            blk = w1[part * half:(part + 1) * half].reshape(CMP_STRIDE, 1, HEAD_DIM, CMP_HIDDEN)
            z = jnp.zeros((CMP_STRIDE, N_STREAM, HEAD_DIM, CMP_HIDDEN), w1.dtype)
            z = lax.dynamic_update_slice(z, blk, (0, s, 0, 0))
            cols.append(z.reshape(CHUNK_W, CMP_HIDDEN))
    return jnp.concatenate(cols, axis=1).astype(BF16)


def _gelu_tanh(x):
    return 0.5 * x * (1.0 + jnp.tanh(math.sqrt(2.0 / math.pi) * (x + 0.044715 * (x * x * x))))


def _compress_kernel(c_ref, w_ref, pek_ref, pev_ref, w1k_ref, w1v_ref, w2k_ref, w2vt_ref, kc_ref, vct_ref):
    r = jnp.dot(c_ref[0], w_ref[...], preferred_element_type=F32)
    ncp = r.shape[0]
    pe_k = jnp.dot(pek_ref[...], w1k_ref[...], preferred_element_type=F32)[0:1]
    pe_v = jnp.dot(pev_ref[...], w1v_ref[...], preferred_element_type=F32)[0:1]
    for s in range(N_STREAM):
        a = r[:, s * 256:s * 256 + CMP_HIDDEN]
        b = r[:, s * 256 + CMP_HIDDEN:(s + 1) * 256]
        hid = a + pltpu.roll(b, ncp - 1, 0) + (pe_k if s < KV_GROUPS else pe_v)
        act = _gelu_tanh(hid).astype(BF16)
        if s < KV_GROUPS:
            kc_ref[0, s] = jnp.dot(act, w2k_ref[...], preferred_element_type=F32).astype(BF16)
        else:
            vct_ref[0, s - KV_GROUPS] = lax.dot_general(
                w2vt_ref[...], act, (((1,), (1,)), ((), ())), preferred_element_type=F32).astype(BF16)


def _compress(kvc, wcmp, pe_k, pe_v, w1_k, w1_v, w2_k, w2_v):
    B, S, _ = kvc.shape
    ncp = S // CMP_STRIDE
    chunks = kvc.reshape(B, ncp, CHUNK_W)
    pe8 = lambda pe: jnp.broadcast_to(pe.reshape(1, -1), (8, CMP_LEN * HEAD_DIM)).astype(BF16)
    w2k = jnp.pad(w2_k, ((0, 0), (0, LANES - HEAD_DIM))).astype(BF16)
    w2vt = jnp.pad(w2_v.T, ((0, LANES - HEAD_DIM), (0, 0))).astype(BF16)
    full = lambda a: pl.BlockSpec(a.shape, lambda b: (0,) * a.ndim)
    args = (wcmp, pe8(pe_k), pe8(pe_v), w1_k.astype(BF16), w1_v.astype(BF16), w2k, w2vt)
    return pl.pallas_call(
        _compress_kernel,
        out_shape=[jax.ShapeDtypeStruct((B, KV_GROUPS, ncp, LANES), BF16),
                   jax.ShapeDtypeStruct((B, KV_GROUPS, LANES, ncp), BF16)],
        grid=(B,),
        in_specs=[pl.BlockSpec((1, ncp, CHUNK_W), lambda b: (b, 0, 0))] + [full(a) for a in args],
        out_specs=[pl.BlockSpec((1, KV_GROUPS, ncp, LANES), lambda b: (b, 0, 0, 0)),
                   pl.BlockSpec((1, KV_GROUPS, LANES, ncp), lambda b: (b, 0, 0, 0))],
        compiler_params=_cparams(("arbitrary",)),
        name="compress",
    )(chunks, *args)


def _bias_tables(rel_bias, S):
    nq = S // Q_TILE
    ncp = S // CMP_STRIDE
    rb = rel_bias.reshape(N_BUCKETS, KV_GROUPS, HPG).transpose(1, 2, 0)
    far = rb[:, :, N_BUCKETS - 1]
    far_hi = far.astype(BF16)
    far_lo = (far - far_hi.astype(F32)).astype(BF16)
    far_sum = far_hi.astype(F32) + far_lo.astype(F32)
    i = np.arange(Q_TILE)[None, :]

    def table(d, valid, sub):
        t = rb[:, :, _t5_bucket(jnp.asarray(d, I32))] - sub[:, :, None, None]
        t = jnp.where(jnp.asarray(valid)[None, None], t, NEG_INF)
        return t.transpose(0, 2, 1, 3).reshape(KV_GROUPS, d.shape[0], ROWS).astype(F32)

    d = i - np.arange(NEAR_KEYS)[:, None] + (NEAR_KEYS - Q_TILE)
    t_near = table(d, d >= 0, far_sum)
    d = i - np.arange(WIN_KEYS)[:, None] + (WIN_KEYS - Q_TILE)
    t_win = table(d, (d >= 0) & (d < WINDOW), jnp.zeros_like(far_sum))
    c0 = (Q_TILE // CMP_STRIDE) * (nq - 1)
    d = i - CMP_STRIDE * (np.arange(c0 + ncp)[:, None] - c0) - (CMP_LEN - 1)
    t_cmp = table(d, d >= 0, jnp.zeros_like(far_sum))
    rows = jnp.zeros((KV_GROUPS, HEAD_DIM, HPG, Q_TILE), F32)
    rows = rows.at[:, 0].set(1.0)
    rows = rows.at[:, 1].set(jnp.broadcast_to(far_hi.astype(F32)[:, :, None], (KV_GROUPS, HPG, Q_TILE)))
    rows = rows.at[:, 2].set(jnp.broadcast_to(far_lo.astype(F32)[:, :, None], (KV_GROUPS, HPG, Q_TILE)))
    return t_near, t_win, t_cmp, rows.reshape(KV_GROUPS, HEAD_DIM, ROWS).astype(BF16)


def _overlap_t(S):
    ncp = S // CMP_STRIDE
    n = np.arange(ncp)[None, :]
    jb = np.arange(LANES)[:, None]
    end = n * CMP_STRIDE + CMP_LEN - 1
    start = n * CMP_STRIDE
    ov = (end >= jb * SEL_BLOCK) & (start < (jb + 1) * SEL_BLOCK) & (n < ncp - 1)
    return jnp.asarray(ov.astype(np.float32), BF16)


def _pad_keys(k, group_w):
    B, _, W = k.shape
    marker = (np.arange(W) % group_w == HEAD_DIM).astype(np.float32) * MASK_BIAS
    pad = jnp.broadcast_to(jnp.asarray(marker, k.dtype), (B, KEY_PAD, W))
    return jnp.concatenate([pad, k], axis=1)


def _value_tiles(v):
    B, S, _ = v.shape
    v = jnp.pad(v, ((0, 0), (KEY_PAD, 0), (0, 0)))
    return v.reshape(B, (S + KEY_PAD) // LANES, LANES, KV_GROUPS, LANES).transpose(0, 3, 1, 4, 2)


def _key_tiles(ref, k0, n):
    t0 = k0 // LANES
    return jnp.concatenate([ref[0, 0, t0 + u] for u in range(n)], axis=1)


def _attn_kernel(qt_ref, qc_ref, kc_ref, vct_ref, ks_ref, vs_ref, kw_ref, vw_ref, gt_ref, tn_ref, tw_ref, tct_ref,
                 ovt_ref, o_ref, qa_ref, *, c0):
    qi = pl.program_id(2)
    qt = qt_ref[0, 0, 0]
    ncp = kc_ref.shape[2]

    st = jnp.dot(kc_ref[0, 0][:, 0:HEAD_DIM], qt, preferred_element_type=F32)
    r0 = pl.multiple_of(c0 - (Q_TILE // CMP_STRIDE) * qi, 8)
    st = st + tct_ref[0, pl.ds(r0, ncp), :]
    mc = jnp.maximum(jnp.max(st, axis=0, keepdims=True), -1e20)
    pc = jnp.exp(st - mc)
    lc = jnp.sum(pc, axis=0, keepdims=True)
    pc = pc * (1.0 / jnp.maximum(lc, 1e-30))
    o_c = jnp.dot(vct_ref[0, 0], pc.astype(BF16), preferred_element_type=F32)[0:HEAD_DIM]

    ps = pc[:, 0:Q_TILE]
    for h in range(1, HPG):
        ps = ps + pc[:, h * Q_TILE:(h + 1) * Q_TILE]
    ps_hi = ps.astype(BF16)
    ps_lo = (ps - ps_hi.astype(F32)).astype(BF16)
    imp = (jnp.dot(ovt_ref[...], ps_hi, preferred_element_type=F32)
           + jnp.dot(ovt_ref[...], ps_lo, preferred_element_type=F32))
    jb = lax.broadcasted_iota(I32, (LANES, Q_TILE), 0)
    ii = lax.broadcasted_iota(I32, (LANES, Q_TILE), 1)
    cur = (Q_TILE // SEL_BLOCK) * qi + (ii >= SEL_BLOCK).astype(I32)
    forced = (jb == 0) | (jb == cur) | (jb == cur - 1)
    score = jnp.where(jb <= cur, jnp.where(forced, FORCE_SCORE, imp), NEG_INF)
    sel = jnp.zeros((LANES, Q_TILE), jnp.bool_)
    for _ in range(SEL_TOPN):
        mx = jnp.max(score, axis=0, keepdims=True)
        first = jnp.min(jnp.where(score == mx, jb, LANES), axis=0, keepdims=True)
        pick = jb == first
        sel = sel | pick
        score = jnp.where(pick, -jnp.inf, score)
    mbt = jnp.where(sel, 0.0, MASK_BIAS).astype(BF16)

    qa_ref[0:HEAD_DIM, :] = qt
    qa_ref[HEAD_DIM:LANES, :] = qc_ref[0]
    for h in range(HPG):
        qa_ref[LANES:2 * LANES, h * Q_TILE:(h + 1) * Q_TILE] = mbt
    qa = qa_ref[...]

    def flash_step(carry, s, vt):
        m, acc = carry
        m_new = jnp.maximum(m, jnp.max(s, axis=0, keepdims=True))
        alpha = jnp.exp(m - m_new)
        p = jnp.exp(s - m_new)
        acc = acc * alpha + jnp.dot(vt, p.astype(BF16), preferred_element_type=F32)
        return m_new, acc

    def normalised(acc):
        return acc[0:HEAD_DIM] * (1.0 / acc[HEAD_DIM:HEAD_DIM + 1])

    init = (jnp.full((1, ROWS), NEG_INF, F32), jnp.zeros((LANES, ROWS), F32))
    k1 = Q_TILE * qi + (KEY_PAD + Q_TILE)

    n0 = pl.multiple_of(k1 - NEAR_KEYS, LANES)
    s = jnp.dot(ks_ref[0, pl.ds(n0, NEAR_KEYS), :], qa, preferred_element_type=F32) + tn_ref[0]
    carry = flash_step(init, s, _key_tiles(vs_ref, n0, NEAR_KEYS // LANES))
    far_keys = jnp.maximum(n0 - KEY_PAD, 0)

    def far_body(j, carry):
        k0 = pl.multiple_of(n0 - FAR_TILE * (j + 1), LANES)
        s = jnp.dot(ks_ref[0, pl.ds(k0, FAR_TILE), :], qa, preferred_element_type=F32)
        return flash_step(carry, s, _key_tiles(vs_ref, k0, FAR_TILE // LANES))

    _, acc_s = lax.fori_loop(0, (far_keys + FAR_TILE - 1) // FAR_TILE, far_body, carry)
    o_s = normalised(acc_s)

    w0 = pl.multiple_of(k1 - WIN_KEYS, LANES)
    s = jnp.dot(kw_ref[0, pl.ds(w0, WIN_KEYS), :], qa[0:LANES], preferred_element_type=F32) + tw_ref[0]
    _, acc_w = flash_step(init, s, _key_tiles(vw_ref, w0, WIN_KEYS // LANES))
    o_w = normalised(acc_w)

    gtt = gt_ref[0].T
    gate = lambda br: jnp.concatenate([gtt[br * HPG + h:br * HPG + h + 1, :] for h in range(HPG)], axis=1)
    o_ref[0, 0, 0] = (gate(0) * o_c + gate(1) * o_s + gate(2) * o_w).astype(BF16)


def _attention(qt, qc, kc, vct, ks, vs, kw, vw, gt, tn, tw, tct, ovt):
    B, G, nq, _, _ = qt.shape
    S = nq * Q_TILE
    sp = S + KEY_PAD
    ncp = S // CMP_STRIDE
    c0 = (Q_TILE // CMP_STRIDE) * (nq - 1)
    per_g = lambda a: pl.BlockSpec((1,) + a.shape[1:], lambda b, g, i: (g,) + (0,) * (a.ndim - 1))
    val = pl.BlockSpec((1, 1, sp // LANES, LANES, LANES), lambda b, g, i: (b, g, 0, 0, 0))
    return pl.pallas_call(
        functools.partial(_attn_kernel, c0=c0),
        out_shape=jax.ShapeDtypeStruct((B, G, nq, HEAD_DIM, ROWS), BF16),
        grid=(B, G, nq),
        in_specs=[pl.BlockSpec((1, 1, 1, HEAD_DIM, ROWS), lambda b, g, i: (b, g, i, 0, 0)),
                  per_g(qc),
                  pl.BlockSpec((1, 1, ncp, LANES), lambda b, g, i: (b, g, 0, 0)),
                  pl.BlockSpec((1, 1, LANES, ncp), lambda b, g, i: (b, g, 0, 0)),
                  pl.BlockSpec((1, sp, 2 * LANES), lambda b, g, i: (b, 0, g)),
                  val,
                  pl.BlockSpec((1, sp, LANES), lambda b, g, i: (b, 0, g)),
                  val,
                  pl.BlockSpec((1, Q_TILE, LANES), lambda b, g, i: (b, i, g)),
                  per_g(tn), per_g(tw), per_g(tct),
                  pl.BlockSpec((LANES, ncp), lambda b, g, i: (0, 0))],
        out_specs=pl.BlockSpec((1, 1, 1, HEAD_DIM, ROWS), lambda b, g, i: (b, g, i, 0, 0)),
        scratch_shapes=[pltpu.VMEM((2 * LANES, ROWS), BF16)],
        compiler_params=_cparams(("arbitrary", "arbitrary", "arbitrary")),
        name="nsa_attention",
    )(qt, qc, kc, vct, ks, vs, kw, vw, gt, tn, tw, tct, ovt)


def _post_attn_kernel(x_ref, ypc_ref, yn_ref, wo_ref, g1_ref, n2_ref, sc_ref, sh_ref, rwh_ref, rwl_ref, rb_ref,
                      xo_ref, h2_ref, ei_ref, tw_ref, cnt_ref, run_ref):
    first = (pl.program_id(0) == 0) & (pl.program_id(1) == 0)

    @pl.when(first)
    def _():
        run_ref[...] = jnp.zeros_like(run_ref)

    tm = x_ref.shape[1]
    half = wo_ref.shape[0] // 2
    mixed = (jnp.dot(ypc_ref[0], wo_ref[0:half, :], preferred_element_type=F32)
             + jnp.dot(yn_ref[0], wo_ref[half:, :], preferred_element_type=F32))
    x = x_ref[0] + g1_ref[0] * mixed
    xo_ref[0] = x
    y = x * lax.rsqrt(jnp.mean(x * x, axis=-1, keepdims=True) + EPS) * n2_ref[...]
    h2 = y * (1.0 + sc_ref[0]) + sh_ref[0]
    _rows_to_tiles(h2_ref, h2)
    hh = h2.astype(BF16)
    hl = (h2 - hh.astype(F32)).astype(BF16)
    logit = (jnp.dot(hh, rwh_ref[...], preferred_element_type=F32)
             + jnp.dot(hl, rwh_ref[...], preferred_element_type=F32)
             + jnp.dot(hh, rwl_ref[...], preferred_element_type=F32)) + rb_ref[...]
    lane = lax.broadcasted_iota(I32, (tm, LANES), 1)
    vals, hots, idxs = [], [], []
    for _ in range(TOP_K):
        mx = jnp.max(logit, axis=1, keepdims=True)
        idx = jnp.min(jnp.where(logit == mx, lane, LANES), axis=1, keepdims=True)
        hot = lane == idx
        vals.append(mx)
        hots.append(hot)
        idxs.append(idx)
        logit = jnp.where(hot, -jnp.inf, logit)
    ex = [jnp.exp(v - vals[0]) for v in vals]
    inv = 1.0 / (ex[0] + ex[1] + ex[2] + ex[3])
    assign = (hots[0] | hots[1] | hots[2] | hots[3]).astype(BF16)
    r = lax.broadcasted_iota(I32, (tm, tm), 0)
    c = lax.broadcasted_iota(I32, (tm, tm), 1)
    before = jnp.dot((c < r).astype(BF16), assign, preferred_element_type=F32) + run_ref[...]
    ei = jnp.zeros((tm, LANES), I32)
    tw = jnp.zeros((tm, LANES), F32)
    for k in range(TOP_K):
        e_k = idxs[k]
        r_k = jnp.sum(jnp.where(hots[k], before, 0.0), axis=1, keepdims=True).astype(I32)
        ei = jnp.where(lane == k, e_k, jnp.where(lane == TOP_K + k, r_k, ei))
        tw = jnp.where(lane == k, ex[k] * inv, tw)
    ei_ref[0] = ei
    tw_ref[0] = tw
    run_ref[...] = run_ref[...] + jnp.sum(assign.astype(F32), axis=0, keepdims=True)
    cnt_ref[...] = run_ref[...]


def _post_attn(x, ypc, ynsa, w_out, g1, n2g, sc2, sh2, rw_hi, rw_lo, rb):
    B, S, D = x.shape
    tm = TOK_TILE
    tok = lambda w: pl.BlockSpec((1, tm, w), lambda b, i: (b, i, 0))
    per_b = pl.BlockSpec((1, 1, D), lambda b, i: (b, 0, 0))
    full = lambda a: pl.BlockSpec(a.shape, lambda b, i: (0,) * a.ndim)
    return pl.pallas_call(
        _post_attn_kernel,
        out_shape=[jax.ShapeDtypeStruct((B, S, D), F32), jax.ShapeDtypeStruct((B * S * ROW_TILES, LANES), F32),
                   jax.ShapeDtypeStruct((B, S, LANES), I32), jax.ShapeDtypeStruct((B, S, LANES), F32),
                   jax.ShapeDtypeStruct((1, LANES), F32)],
        grid=(B, S // tm),
        in_specs=[tok(D), tok(ypc.shape[-1]), tok(ynsa.shape[-1]), full(w_out), per_b,
                  pl.BlockSpec((1, D), lambda b, i: (0, 0)), per_b, per_b,
                  full(rw_hi), full(rw_lo), full(rb)],
        out_specs=[tok(D), pl.BlockSpec((tm * ROW_TILES, LANES), lambda b, i: (b * (S // tm) + i, 0)),
                   tok(LANES), tok(LANES), pl.BlockSpec((1, LANES), lambda b, i: (0, 0))],
        scratch_shapes=[pltpu.VMEM((1, LANES), F32)],
        compiler_params=_cparams(("arbitrary", "arbitrary")),
        name="post_attn_router",
    )(x, ypc, ynsa, w_out, g1, n2g.reshape(1, D), sc2, sh2, rw_hi, rw_lo, rb)


ROW_TILES = 8
DMA_UNROLL = 2


def _rows_to_tiles(ref, val):
    n = val.shape[0]
    for s in range(ROW_TILES):
        ref[pl.ds(s, n, stride=ROW_TILES), :] = val[:, s * LANES:(s + 1) * LANES]


def _tiles_to_rows(ref, n):
    return jnp.concatenate([ref[pl.ds(s, n, stride=ROW_TILES), :] for s in range(ROW_TILES)], axis=1)


def _tile_at(ref, i):
    return ref.at[pl.ds(pl.multiple_of(i * ROW_TILES, ROW_TILES), ROW_TILES), :]


def _dispatch_kernel(dest_ref, h_ref, xs_in_ref, xs_ref, sem):
    del xs_in_ref
    tm = h_ref.shape[0] // ROW_TILES

    def body(i, c):
        for u in range(DMA_UNROLL):
            t = i * DMA_UNROLL + u
            src = _tile_at(h_ref, t)
            for k in range(TOP_K):
                pltpu.make_async_copy(src, _tile_at(xs_ref, dest_ref[0, 0, t * TOP_K + k]), sem).start()
        return c

    lax.fori_loop(0, tm // DMA_UNROLL, body, 0)
    for _ in range(TOP_K):
        pltpu.make_async_copy(h_ref, xs_ref.at[pl.ds(0, tm * ROW_TILES), :], sem).wait()


def _dispatch(dest, h2t, xs_zero):
    tm = TOK_TILE
    nt = h2t.shape[0] // (tm * ROW_TILES)
    return pl.pallas_call(
        _dispatch_kernel,
        out_shape=jax.ShapeDtypeStruct(xs_zero.shape, xs_zero.dtype),
        grid=(nt,),
        in_specs=[pl.BlockSpec((1, 1, tm * TOP_K), lambda i: (i, 0, 0), memory_space=pltpu.SMEM),
                  pl.BlockSpec((tm * ROW_TILES, LANES), lambda i: (i, 0)),
                  pl.BlockSpec(memory_space=pl.ANY)],
        out_specs=pl.BlockSpec(memory_space=pl.ANY),
        scratch_shapes=[pltpu.SemaphoreType.DMA(())],
        input_output_aliases={2: 0},
        compiler_params=_cparams(("arbitrary",)),
        name="moe_dispatch",
    )(dest.reshape(nt, 1, tm * TOP_K), h2t, xs_zero)


def _expert_kernel(be_ref, nu_ref, x_ref, wgu_ref, bgu_ref, wd_ref, bd_ref, y_ref):
    @pl.when(pl.program_id(0) < nu_ref[0])
    def _():
        F = wd_ref.shape[1]
        x = _tiles_to_rows(x_ref, EXPERT_BLOCK).astype(BF16)
        gu = jnp.dot(x, wgu_ref[0], preferred_element_type=F32) + bgu_ref[0]
        gate = jnp.minimum(gu[:, :F], SWIGLU_LIMIT)
        up = jnp.clip(gu[:, F:], -SWIGLU_LIMIT, SWIGLU_LIMIT)
        act = (up + 1.0) * gate * jax.nn.sigmoid(SWIGLU_ALPHA * gate)
        _rows_to_tiles(y_ref, jnp.dot(act.astype(BF16), wd_ref[0], preferred_element_type=F32) + bd_ref[0])


def _experts(blk_e, n_used, xs, w_gu, b_gu, w_down, b_down):
    E, D, F2 = w_gu.shape
    F = F2 // 2
    rows = EXPERT_BLOCK * ROW_TILES
    nb = xs.shape[0] // rows
    blk = lambda i, be, nu: (jnp.minimum(i, nu[0] - 1), 0)
    return pl.pallas_call(
        _expert_kernel,
        out_shape=jax.ShapeDtypeStruct(xs.shape, F32),
        grid_spec=pltpu.PrefetchScalarGridSpec(
            num_scalar_prefetch=2,
            grid=(nb,),
            in_specs=[pl.BlockSpec((rows, LANES), blk),
                      pl.BlockSpec((1, D, F2), lambda i, be, nu: (be[i], 0, 0)),
                      pl.BlockSpec((1, 1, F2), lambda i, be, nu: (be[i], 0, 0)),
                      pl.BlockSpec((1, F, D), lambda i, be, nu: (be[i], 0, 0)),
                      pl.BlockSpec((1, 1, D), lambda i, be, nu: (be[i], 0, 0))],
            out_specs=pl.BlockSpec((rows, LANES), blk)),
        compiler_params=_cparams(("arbitrary",)),
        name="moe_experts",
    )(blk_e, n_used, xs, w_gu, b_gu.reshape(E, 1, F2), w_down, b_down.reshape(E, 1, D))


def _combine_kernel(dest_ref, y_ref, x_ref, tw_ref, g2_ref, fg_ref, o_ref, rows_ref, sem, *, final):
    tm = x_ref.shape[1]

    def body(i, c):
        for u in range(DMA_UNROLL):
            t = i * DMA_UNROLL + u
            for k in range(TOP_K):
                pltpu.make_async_copy(_tile_at(y_ref, dest_ref[0, 0, t * TOP_K + k]),
                                      _tile_at(rows_ref.at[k], t), sem).start()
        return c

    lax.fori_loop(0, tm // DMA_UNROLL, body, 0)
    for k in range(TOP_K):
        pltpu.make_async_copy(y_ref.at[pl.ds(0, tm * ROW_TILES), :], rows_ref.at[k], sem).wait()
    tw = tw_ref[0]
    moe = tw[:, 0:1] * _tiles_to_rows(rows_ref.at[0], tm)
    for k in range(1, TOP_K):
        moe = moe + tw[:, k:k + 1] * _tiles_to_rows(rows_ref.at[k], tm)
    x = x_ref[0] + g2_ref[0] * moe
    if final:
        x = x * lax.rsqrt(jnp.mean(x * x, axis=-1, keepdims=True) + EPS) * fg_ref[...]
    o_ref[0] = x


def _combine(dest, y, x, tw, g2, final_g, final):
    B, S, D = x.shape
    tm = TOK_TILE
    nt = S // tm
    tok = lambda w: pl.BlockSpec((1, tm, w), lambda b, i: (b, i, 0))
    return pl.pallas_call(
        functools.partial(_combine_kernel, final=final),
        out_shape=jax.ShapeDtypeStruct((B, S, D), F32),
        grid=(B, nt),
        in_specs=[pl.BlockSpec((1, 1, tm * TOP_K), lambda b, i: (b * nt + i, 0, 0), memory_space=pltpu.SMEM),
                  pl.BlockSpec(memory_space=pl.ANY),
                  tok(D), tok(LANES),
                  pl.BlockSpec((1, 1, D), lambda b, i: (b, 0, 0)),
                  pl.BlockSpec((1, D), lambda b, i: (0, 0))],
        out_specs=tok(D),
        scratch_shapes=[pltpu.VMEM((TOP_K, tm * ROW_TILES, LANES), F32), pltpu.SemaphoreType.DMA(())],
        compiler_params=_cparams(("arbitrary", "arbitrary")),
        name="moe_combine",
    )(dest.reshape(B * nt, 1, tm * TOP_K), y, x, tw, g2, final_g.reshape(1, D))


def _moe(x, h2, ei, tw, counts, g2, w_gu, b_gu, w_down, b_down, final_g, final):
    B, S, D = x.shape
    N = B * S
    n_slots = -(-(N * TOP_K + N_EXPERTS * EXPERT_BLOCK) // EXPERT_BLOCK) * EXPERT_BLOCK
    nb = n_slots // EXPERT_BLOCK
    cnt = counts[0, :N_EXPERTS].astype(I32)
    padded = (cnt + EXPERT_BLOCK - 1) // EXPERT_BLOCK * EXPERT_BLOCK
    pend = jnp.cumsum(padded)
    pstart = pend - padded
    ei2 = ei.reshape(N, LANES)
    dest = (pstart[ei2[:, 0:TOP_K]] + ei2[:, TOP_K:2 * TOP_K]).reshape(N * TOP_K)
    blk_e = jnp.minimum(jnp.searchsorted(pend, jnp.arange(nb, dtype=I32) * EXPERT_BLOCK, side='right'),
                        N_EXPERTS - 1).astype(I32)
    n_used = (pend[-1:] // EXPERT_BLOCK).astype(I32)
    xs = _dispatch(dest, h2, jnp.zeros((n_slots * ROW_TILES, LANES), F32))
    y = _experts(blk_e, n_used, xs, w_gu, b_gu, w_down, b_down)
    return _combine(dest, y, x, tw, g2, final_g, final)


def kernel(x, c, w_mod, b_mod, norm1_g, norm2_g, w_in, w_out, pool_w, pool_scale, conv_w, conv_b, conv_ln_g,
           conv_ln_b, cmp_pe_k, cmp_pe_v, cmp_w1_k, cmp_w2_k, cmp_w1_v, cmp_w2_v, rel_bias, router_w, router_b,
           expert_w_gu, expert_b_gu, expert_w_down, expert_b_down, final_g):
    B, S, D = x.shape
    L = w_mod.shape[0]
    assert S % TOK_TILE == 0 and S // SEL_BLOCK <= LANES and D == ROW_TILES * LANES
    mod = _modulation(c, w_mod, b_mod)
    nq = S // Q_TILE
    t_near, t_win, t_cmp, qc = _bias_tables(rel_bias, S)
    ovt = _overlap_t(S)
    cg = POOL_DIM // POOL_GROUPS
    for l in range(L):
        m6 = mod[l].reshape(B, 6, 1, D)
        sh1, sc1, g1, sh2, sc2, g2 = (m6[:, k] for k in range(6))
        upc, q, kvc, ks, vs, kw, vw, gt = _in_proj(x, norm1_g[l], sc1, sh1, _in_weight(w_in[l]))
        pw_bd = jnp.zeros((POOL_DIM, POOL_DIM), F32)
        for g in range(POOL_GROUPS):
            pw_bd = lax.dynamic_update_slice(pw_bd, pool_w[l, g], (g * cg, g * cg))
        cw = jnp.pad(conv_w[l], ((0, 1), (0, 0)))
        ypc = _pool_conv(upc, pw_bd.astype(BF16), pool_scale[l], cw, conv_b[l], conv_ln_g[l], conv_ln_b[l])
        kc, vct = _compress(kvc, _cmp_weights(cmp_w1_k[l], cmp_w1_v[l]), cmp_pe_k[l], cmp_pe_v[l],
                            cmp_w1_k[l], cmp_w1_v[l], cmp_w2_k[l], cmp_w2_v[l])
        qt = (q.reshape(B, nq, Q_TILE, KV_GROUPS, HPG, HEAD_DIM).transpose(0, 3, 1, 5, 4, 2)
              .reshape(B, KV_GROUPS, nq, HEAD_DIM, ROWS))
        ot = _attention(qt, qc, kc, vct, _pad_keys(ks, 2 * LANES), _value_tiles(vs), _pad_keys(kw, LANES),
                        _value_tiles(vw), gt, t_near, t_win, t_cmp, ovt)
        ynsa = (ot.reshape(B, KV_GROUPS, nq, HEAD_DIM, HPG, Q_TILE).transpose(0, 2, 5, 1, 4, 3)
                .reshape(B, S, NSA_DIM))
        rw = jnp.pad(router_w[l], ((0, 0), (0, LANES - N_EXPERTS)))
        rw_hi = rw.astype(BF16)
        rw_lo = (rw - rw_hi.astype(F32)).astype(BF16)
        rb = jnp.pad(router_b[l].reshape(1, -1), ((0, 0), (0, LANES - N_EXPERTS)), constant_values=NEG_INF)
        x, h2, ei, tw, counts = _post_attn(x, ypc, ynsa, w_out[l].astype(BF16), g1, norm2_g[l], sc2, sh2,
                                           rw_hi, rw_lo, rb)
        x = _moe(x, h2, ei, tw, counts, g2, expert_w_gu[l].astype(BF16), expert_b_gu[l],
                 expert_w_down[l].astype(BF16), expert_b_down[l], final_g, final=(l == L - 1))
    return x
```

```python
import functools
import math

import jax
import jax.numpy as jnp
import numpy as np
from jax import lax
from jax.experimental import pallas as pl
from jax.experimental.pallas import tpu as pltpu

F32 = jnp.float32
BF16 = jnp.bfloat16
I32 = jnp.int32

HEAD_DIM = 64
POOL_DIM = 256
POOL_GROUPS = 4
POOL_WINDOWS = (2, 4, 8, 16)
CONV_DIM = 256
CONV_WIDTH = 31
NSA_DIM = 512
NSA_HEADS = 8
KV_GROUPS = 2
HPG = 4
CMP_LEN = 32
CMP_STRIDE = 16
CMP_HIDDEN = 128
SEL_BLOCK = 64
SEL_TOPN = 16
WINDOW = 512
Q_TILE = 128
N_BUCKETS = 32
MAX_DISTANCE = 1024
N_EXPERTS = 32
TOP_K = 4
SWIGLU_ALPHA = 1.702
SWIGLU_LIMIT = 7.0
EPS = 1e-5
NEG_INF = -1e30
FORCE_SCORE = 1e4

LANES = 128
ROWS = HPG * Q_TILE
NEAR_KEYS = 1280
WIN_KEYS = WINDOW + Q_TILE
FAR_TILE = 512
KEY_PAD = NEAR_KEYS - Q_TILE
MASK_BIAS = -32768.0
TOK_TILE = 512
EXPERT_BLOCK = 512
VMEM_LIMIT = 56 * 1024 * 1024


def _cparams(sem, vmem=VMEM_LIMIT):
    return pltpu.CompilerParams(dimension_semantics=sem, vmem_limit_bytes=vmem)


def _t5_bucket(n):
    n = jnp.maximum(n, 0)
    max_exact = N_BUCKETS // 2
    nf = jnp.maximum(n, 1).astype(F32)
    large = max_exact + (jnp.log(nf / max_exact) / math.log(MAX_DISTANCE / max_exact)
                         * (N_BUCKETS - max_exact)).astype(I32)
    large = jnp.minimum(large, N_BUCKETS - 1)
    return jnp.where(n < max_exact, n, large)


def _mod_kernel(c_ref, w_ref, b_ref, o_ref):
    c = c_ref[...]
    cond = c * jax.nn.sigmoid(c)
    o_ref[0] = jnp.dot(cond.astype(BF16), w_ref[0].astype(BF16),
                       preferred_element_type=F32) + b_ref[0]


def _modulation(c, w_mod, b_mod):
    L, D, W = w_mod.shape
    B = c.shape[0]
    tn = 1536
    return pl.pallas_call(
        _mod_kernel,
        out_shape=jax.ShapeDtypeStruct((L, B, W), F32),
        grid=(L, W // tn),
        in_specs=[pl.BlockSpec((B, D), lambda l, j: (0, 0)),
                  pl.BlockSpec((1, D, tn), lambda l, j: (l, 0, j)),
                  pl.BlockSpec((1, 1, tn), lambda l, j: (l, 0, j))],
        out_specs=pl.BlockSpec((1, B, tn), lambda l, j: (l, 0, j)),
        compiler_params=_cparams(("arbitrary", "arbitrary")),
        name="modulation",
    )(c, w_mod, b_mod.reshape(L, 1, W))


C_UPC, C_Q, C_KVC, C_KS, C_VS, C_KW, C_VW, C_GT, C_END = 0, 768, 1280, 1536, 2048, 2304, 2560, 2816, 3072


def _in_weight(w_in):
    col = lambda a, n: w_in[:, :, a:a + n]
    zero = lambda n: jnp.zeros(w_in.shape[:2] + (n,), w_in.dtype)
    parts = [col(0, 768), col(768, 512) * HEAD_DIM ** -0.5, col(1280, 256)]
    for g in range(KV_GROUPS):
        parts += [col(1536 + g * HEAD_DIM, HEAD_DIM), zero(2 * LANES - HEAD_DIM)]
    for base in (1664, 1792, 1920):
        for g in range(KV_GROUPS):
            parts += [col(base + g * HEAD_DIM, HEAD_DIM), zero(LANES - HEAD_DIM)]
    for g in range(KV_GROUPS):
        parts += [col(2048 + br * NSA_HEADS + g * HPG, HPG) for br in range(3)] + [zero(LANES - 3 * HPG)]
    return jnp.concatenate(parts, axis=2).astype(BF16)


def _inproj_kernel(x_ref, g_ref, sc_ref, sh_ref, w_ref,
                   upc_ref, q_ref, kvc_ref, ks_ref, vs_ref, kw_ref, vw_ref, gt_ref):
    ti = pl.program_id(1)
    x = x_ref[0]
    tm = x.shape[0]
    y = x * lax.rsqrt(jnp.mean(x * x, axis=-1, keepdims=True) + EPS) * g_ref[...]
    h = y * (1.0 + sc_ref[0]) + sh_ref[0]
    z = jnp.dot(h.astype(BF16), w_ref[...], preferred_element_type=F32)
    upc_ref[0] = z[:, C_UPC:C_Q]
    q_ref[0] = z[:, C_Q:C_KVC].astype(BF16)
    kvc_ref[0] = z[:, C_KVC:C_KS].astype(BF16)
    lane = lax.broadcasted_iota(I32, (tm, 256), 1)
    row = lax.broadcasted_iota(I32, (tm, 256), 0)
    blk = (ti * tm + row) // SEL_BLOCK
    onehot = ((lane - LANES == blk) | (lane == HEAD_DIM + 1) | (lane == HEAD_DIM + 2)).astype(F32)
    for g in range(KV_GROUPS):
        ks_ref[0, :, g * 256:(g + 1) * 256] = (z[:, C_KS + g * 256:C_KS + (g + 1) * 256] + onehot).astype(BF16)
    lane = lax.broadcasted_iota(I32, (tm, 256), 1)
    ones_col = ((lane % LANES) == HEAD_DIM).astype(F32)
    vs_ref[0] = (z[:, C_VS:C_KW] + ones_col).astype(BF16)
    kw_ref[0] = z[:, C_KW:C_VW].astype(BF16)
    vw_ref[0] = (z[:, C_VW:C_GT] + ones_col).astype(BF16)
    gt_ref[0] = jax.nn.sigmoid(z[:, C_GT:C_END])


def _in_proj(x, g1, sc, sh, w_big):
    B, S, D = x.shape
    tm = TOK_TILE
    tok = lambda w: pl.BlockSpec((1, tm, w), lambda b, i: (b, i, 0))
    outs = [(768, F32), (512, BF16), (256, BF16), (512, BF16), (256, BF16), (256, BF16), (256, BF16), (256, F32)]
    return pl.pallas_call(
        _inproj_kernel,
        out_shape=[jax.ShapeDtypeStruct((B, S, w), dt) for w, dt in outs],
        grid=(B, S // tm),
        in_specs=[tok(D),
                  pl.BlockSpec((1, D), lambda b, i: (0, 0)),
                  pl.BlockSpec((1, 1, D), lambda b, i: (b, 0, 0)),
                  pl.BlockSpec((1, 1, D), lambda b, i: (b, 0, 0)),
                  pl.BlockSpec((D, C_END), lambda b, i: (0, 0))],
        out_specs=[tok(w) for w, _ in outs],
        compiler_params=_cparams(("arbitrary", "arbitrary")),
        name="in_proj",
    )(x, g1.reshape(1, D), sc, sh, w_big)


HALO = 32


def _poolconv_kernel(cur_ref, halo_ref, pw_ref, ps_ref, cw_ref, cb_ref, lg_ref, lb_ref, o_ref, ext_ref, v_ref):
    ti = pl.program_id(1)
    ts = cur_ref.shape[1]
    halo = halo_ref[0] * (ti > 0).astype(F32)
    ext_ref[0:HALO, :] = halo
    ext_ref[HALO:HALO + ts, :] = cur_ref[0]
    u = ext_ref[HALO:HALO + ts, 0:POOL_DIM]
    lane = lax.broadcasted_iota(I32, (ts, POOL_DIM), 1)
    grp = lane // (POOL_DIM // POOL_GROUPS)
    run = u
    pooled = jnp.zeros_like(u)
    for k in range(1, POOL_WINDOWS[-1]):
        run = run + ext_ref[HALO - k:HALO - k + ts, 0:POOL_DIM]
        if (k + 1) in POOL_WINDOWS:
            pooled = jnp.where(grp == POOL_WINDOWS.index(k + 1), run, pooled)
    wlane = jnp.where(grp == 0, 2.0, jnp.where(grp == 1, 4.0, jnp.where(grp == 2, 8.0, 16.0)))
    t1 = (ti * ts + lax.broadcasted_iota(I32, (ts, POOL_DIM), 0) + 1).astype(F32)
    cnt = jnp.minimum(t1, wlane)
    pooled = pooled / cnt - u
    y_pool = jnp.dot(pooled.astype(BF16), pw_ref[...], preferred_element_type=F32) * ps_ref[...]
    o_ref[0, :, 0:POOL_DIM] = y_pool.astype(BF16)
    uv = ext_ref[:, POOL_DIM:POOL_DIM + CONV_DIM]
    ug = ext_ref[:, POOL_DIM + CONV_DIM:POOL_DIM + 2 * CONV_DIM]
    v_ref[...] = uv * jax.nn.sigmoid(ug)
    acc = jnp.zeros((ts, CONV_DIM), F32) + cb_ref[...]
    for k in range(CONV_WIDTH):
        o = HALO - (CONV_WIDTH - 1) + k
        acc = acc + v_ref[o:o + ts, :] * cw_ref[k:k + 1, :]
    mu = jnp.mean(acc, axis=-1, keepdims=True)
    d = acc - mu
    var = jnp.mean(d * d, axis=-1, keepdims=True)
    yn = d * lax.rsqrt(var + EPS) * lg_ref[...] + lb_ref[...]
    o_ref[0, :, POOL_DIM:POOL_DIM + CONV_DIM] = (yn * jax.nn.sigmoid(yn)).astype(BF16)


def _pool_conv(upc, pool_w_bd, pool_scale, conv_w, conv_b, ln_g, ln_b):
    B, S, W = upc.shape
    ts = TOK_TILE
    r = ts // HALO
    vec = lambda n: pl.BlockSpec((1, n), lambda b, i: (0, 0))
    return pl.pallas_call(
        _poolconv_kernel,
        out_shape=jax.ShapeDtypeStruct((B, S, POOL_DIM + CONV_DIM), BF16),
        grid=(B, S // ts),
        in_specs=[pl.BlockSpec((1, ts, W), lambda b, i: (b, i, 0)),
                  pl.BlockSpec((1, HALO, W), lambda b, i: (b, jnp.maximum(i * r - 1, 0), 0)),
                  pl.BlockSpec((POOL_DIM, POOL_DIM), lambda b, i: (0, 0)),
                  vec(POOL_DIM),
                  pl.BlockSpec((CONV_WIDTH + 1, CONV_DIM), lambda b, i: (0, 0)),
                  vec(CONV_DIM), vec(CONV_DIM), vec(CONV_DIM)],
        out_specs=pl.BlockSpec((1, ts, POOL_DIM + CONV_DIM), lambda b, i: (b, i, 0)),
        scratch_shapes=[pltpu.VMEM((HALO + ts, W), F32), pltpu.VMEM((HALO + ts, CONV_DIM), F32)],
        compiler_params=_cparams(("arbitrary", "arbitrary")),
        name="pool_conv",
    )(upc, upc, pool_w_bd, pool_scale.reshape(1, -1), conv_w, conv_b.reshape(1, -1),
      ln_g.reshape(1, -1), ln_b.reshape(1, -1))


N_STREAM = 2 * KV_GROUPS
CHUNK_W = CMP_STRIDE * 2 * KV_GROUPS * HEAD_DIM


def _cmp_weights(w1_k, w1_v):
    half = CMP_STRIDE * HEAD_DIM
    cols = []
    for s in range(N_STREAM):
        w1 = w1_k if s < KV_GROUPS else w1_v
        for part in range(2):
            blk = w1[part * half:(part + 1) * half].reshape(CMP_STRIDE, 1, HEAD_DIM, CMP_HIDDEN)
            z = jnp.zeros((CMP_STRIDE, N_STREAM, HEAD_DIM, CMP_HIDDEN), w1.dtype)
            z = lax.dynamic_update_slice(z, blk, (0, s, 0, 0))
            cols.append(z.reshape(CHUNK_W, CMP_HIDDEN))
    return jnp.concatenate(cols, axis=1).astype(BF16)


def _gelu_tanh(x):
    return 0.5 * x * (1.0 + jnp.tanh(math.sqrt(2.0 / math.pi) * (x + 0.044715 * (x * x * x))))


def _compress_kernel(c_ref, w_ref, pek_ref, pev_ref, w1k_ref, w1v_ref, w2k_ref, w2vt_ref, kc_ref, vct_ref):
    r = jnp.dot(c_ref[0], w_ref[...], preferred_element_type=F32)
    ncp = r.shape[0]
    pe_k = jnp.dot(pek_ref[...], w1k_ref[...], preferred_element_type=F32)[0:1]
    pe_v = jnp.dot(pev_ref[...], w1v_ref[...], preferred_element_type=F32)[0:1]
    for s in range(N_STREAM):
        a = r[:, s * 256:s * 256 + CMP_HIDDEN]
        b = r[:, s * 256 + CMP_HIDDEN:(s + 1) * 256]
        hid = a + pltpu.roll(b, ncp - 1, 0) + (pe_k if s < KV_GROUPS else pe_v)
        act = _gelu_tanh(hid).astype(BF16)
        if s < KV_GROUPS:
            kc_ref[0, s] = jnp.dot(act, w2k_ref[...], preferred_element_type=F32).astype(BF16)
        else:
            vct_ref[0, s - KV_GROUPS] = lax.dot_general(
                w2vt_ref[...], act, (((1,), (1,)), ((), ())), preferred_element_type=F32).astype(BF16)


def _compress(kvc, wcmp, pe_k, pe_v, w1_k, w1_v, w2_k, w2_v):
    B, S, _ = kvc.shape
    ncp = S // CMP_STRIDE
    chunks = kvc.reshape(B, ncp, CHUNK_W)
    pe8 = lambda pe: jnp.broadcast_to(pe.reshape(1, -1), (8, CMP_LEN * HEAD_DIM)).astype(BF16)
    w2k = jnp.pad(w2_k, ((0, 0), (0, LANES - HEAD_DIM))).astype(BF16)
    w2vt = jnp.pad(w2_v.T, ((0, LANES - HEAD_DIM), (0, 0))).astype(BF16)
    full = lambda a: pl.BlockSpec(a.shape, lambda b: (0,) * a.ndim)
    args = (wcmp, pe8(pe_k), pe8(pe_v), w1_k.astype(BF16), w1_v.astype(BF16), w2k, w2vt)
    return pl.pallas_call(
        _compress_kernel,
        out_shape=[jax.ShapeDtypeStruct((B, KV_GROUPS, ncp, LANES), BF16),
                   jax.ShapeDtypeStruct((B, KV_GROUPS, LANES, ncp), BF16)],
        grid=(B,),
        in_specs=[pl.BlockSpec((1, ncp, CHUNK_W), lambda b: (b, 0, 0))] + [full(a) for a in args],
        out_specs=[pl.BlockSpec((1, KV_GROUPS, ncp, LANES), lambda b: (b, 0, 0, 0)),
                   pl.BlockSpec((1, KV_GROUPS, LANES, ncp), lambda b: (b, 0, 0, 0))],
        compiler_params=_cparams(("arbitrary",)),
        name="compress",
    )(chunks, *args)


def _bias_tables(rel_bias, S):
    nq = S // Q_TILE
    ncp = S // CMP_STRIDE
    rb = rel_bias.reshape(N_BUCKETS, KV_GROUPS, HPG).transpose(1, 2, 0)
    far = rb[:, :, N_BUCKETS - 1]
    far_hi = far.astype(BF16)
    far_lo = (far - far_hi.astype(F32)).astype(BF16)
    far_sum = far_hi.astype(F32) + far_lo.astype(F32)
    i = np.arange(Q_TILE)[None, :]

    def table(d, valid, sub):
        onehot = jax.nn.one_hot(_t5_bucket(jnp.asarray(d, I32)), N_BUCKETS, dtype=F32)
        t = jnp.einsum('rib,ghb->grhi', onehot, rb, precision=lax.Precision.HIGHEST)
        t = jnp.where(jnp.asarray(valid)[None, :, None, :], t - sub[:, None, :, None], NEG_INF)
        return t.reshape(KV_GROUPS, d.shape[0], ROWS)

    d = i - np.arange(NEAR_KEYS)[:, None] + (NEAR_KEYS - Q_TILE)
    t_near = table(d, d >= 0, far_sum)
    d = i - np.arange(WIN_KEYS)[:, None] + (WIN_KEYS - Q_TILE)
    t_win = table(d, (d >= 0) & (d < WINDOW), jnp.zeros_like(far_sum))
    c0 = (Q_TILE // CMP_STRIDE) * (nq - 1)
    d = i - CMP_STRIDE * (np.arange(c0 + ncp)[:, None] - c0) - (CMP_LEN - 1)
    t_cmp = table(d, d >= 0, jnp.zeros_like(far_sum))
    rows = jnp.zeros((KV_GROUPS, HEAD_DIM, HPG, Q_TILE), F32)
    rows = rows.at[:, 0].set(1.0)
    rows = rows.at[:, 1].set(jnp.broadcast_to(far_hi.astype(F32)[:, :, None], (KV_GROUPS, HPG, Q_TILE)))
    rows = rows.at[:, 2].set(jnp.broadcast_to(far_lo.astype(F32)[:, :, None], (KV_GROUPS, HPG, Q_TILE)))
    return t_near, t_win, t_cmp, rows.reshape(KV_GROUPS, HEAD_DIM, ROWS).astype(BF16)


def _overlap_t(S):
    ncp = S // CMP_STRIDE
    n = np.arange(ncp)[None, :]
    jb = np.arange(LANES)[:, None]
    end = n * CMP_STRIDE + CMP_LEN - 1
    start = n * CMP_STRIDE
    ov = (end >= jb * SEL_BLOCK) & (start < (jb + 1) * SEL_BLOCK) & (n < ncp - 1)
    return jnp.asarray(ov.astype(np.float32), BF16)


def _pad_keys(k, group_w):
    B, _, W = k.shape
    marker = (np.arange(W) % group_w == HEAD_DIM).astype(np.float32) * MASK_BIAS
    pad = jnp.broadcast_to(jnp.asarray(marker, k.dtype), (B, KEY_PAD, W))
    return jnp.concatenate([pad, k], axis=1)


def _value_tiles(v):
    B, S, _ = v.shape
    v = jnp.pad(v, ((0, 0), (KEY_PAD, 0), (0, 0)))
    return v.reshape(B, (S + KEY_PAD) // LANES, LANES, KV_GROUPS, LANES).transpose(0, 3, 1, 4, 2)


def _key_tiles(ref, k0, n):
    t0 = k0 // LANES
    return jnp.concatenate([ref[0, 0, t0 + u] for u in range(n)], axis=1)


def _attn_kernel(qt_ref, qc_ref, kc_ref, vct_ref, ks_ref, vs_ref, kw_ref, vw_ref, gt_ref, tn_ref, tw_ref, tct_ref,
                 ovt_ref, o_ref, qa_ref, *, c0):
    qi = pl.program_id(2)
    qt = qt_ref[0, 0, 0]
    ncp = kc_ref.shape[2]

    st = jnp.dot(kc_ref[0, 0][:, 0:HEAD_DIM], qt, preferred_element_type=F32)
    r0 = pl.multiple_of(c0 - (Q_TILE // CMP_STRIDE) * qi, 8)
    st = st + tct_ref[0, pl.ds(r0, ncp), :]
    mc = jnp.maximum(jnp.max(st, axis=0, keepdims=True), -1e20)
    pc = jnp.exp(st - mc)
    lc = jnp.sum(pc, axis=0, keepdims=True)
    pc = pc * (1.0 / jnp.maximum(lc, 1e-30))
    o_c = jnp.dot(vct_ref[0, 0], pc.astype(BF16), preferred_element_type=F32)[0:HEAD_DIM]

    ps = pc[:, 0:Q_TILE]
    for h in range(1, HPG):
        ps = ps + pc[:, h * Q_TILE:(h + 1) * Q_TILE]
    ps_hi = ps.astype(BF16)
    ps_lo = (ps - ps_hi.astype(F32)).astype(BF16)
    imp = (jnp.dot(ovt_ref[...], ps_hi, preferred_element_type=F32)
           + jnp.dot(ovt_ref[...], ps_lo, preferred_element_type=F32))
    jb = lax.broadcasted_iota(I32, (LANES, Q_TILE), 0)
    ii = lax.broadcasted_iota(I32, (LANES, Q_TILE), 1)
    cur = (Q_TILE // SEL_BLOCK) * qi + (ii >= SEL_BLOCK).astype(I32)
    forced = (jb == 0) | (jb == cur) | (jb == cur - 1)
    score = jnp.where(jb <= cur, jnp.where(forced, FORCE_SCORE, imp), NEG_INF)
    sel = jnp.zeros((LANES, Q_TILE), jnp.bool_)
    for _ in range(SEL_TOPN):
        mx = jnp.max(score, axis=0, keepdims=True)
        first = jnp.min(jnp.where(score == mx, jb, LANES), axis=0, keepdims=True)
        pick = jb == first
        sel = sel | pick
        score = jnp.where(pick, -jnp.inf, score)
    mbt = jnp.where(sel, 0.0, MASK_BIAS).astype(BF16)

    qa_ref[0:HEAD_DIM, :] = qt
    qa_ref[HEAD_DIM:LANES, :] = qc_ref[0]
    for h in range(HPG):
        qa_ref[LANES:2 * LANES, h * Q_TILE:(h + 1) * Q_TILE] = mbt
    qa = qa_ref[...]

    def flash_step(carry, s, vt):
        m, acc = carry
        m_new = jnp.maximum(m, jnp.max(s, axis=0, keepdims=True))
        alpha = jnp.exp(m - m_new)
        p = jnp.exp(s - m_new)
        acc = acc * alpha + jnp.dot(vt, p.astype(BF16), preferred_element_type=F32)
        return m_new, acc

    def normalised(acc):
        return acc[0:HEAD_DIM] * (1.0 / acc[HEAD_DIM:HEAD_DIM + 1])

    init = (jnp.full((1, ROWS), NEG_INF, F32), jnp.zeros((LANES, ROWS), F32))
    k1 = Q_TILE * qi + (KEY_PAD + Q_TILE)

    n0 = pl.multiple_of(k1 - NEAR_KEYS, LANES)
    s = jnp.dot(ks_ref[0, pl.ds(n0, NEAR_KEYS), :], qa, preferred_element_type=F32) + tn_ref[0]
    carry = flash_step(init, s, _key_tiles(vs_ref, n0, NEAR_KEYS // LANES))
    far_keys = jnp.maximum(n0 - KEY_PAD, 0)

    def far_body(j, carry):
        k0 = pl.multiple_of(n0 - FAR_TILE * (j + 1), LANES)
        s = jnp.dot(ks_ref[0, pl.ds(k0, FAR_TILE), :], qa, preferred_element_type=F32)
        return flash_step(carry, s, _key_tiles(vs_ref, k0, FAR_TILE // LANES))

    _, acc_s = lax.fori_loop(0, (far_keys + FAR_TILE - 1) // FAR_TILE, far_body, carry)
    o_s = normalised(acc_s)

    w0 = pl.multiple_of(k1 - WIN_KEYS, LANES)
    s = jnp.dot(kw_ref[0, pl.ds(w0, WIN_KEYS), :], qa[0:LANES], preferred_element_type=F32) + tw_ref[0]
    _, acc_w = flash_step(init, s, _key_tiles(vw_ref, w0, WIN_KEYS // LANES))
    o_w = normalised(acc_w)

    gtt = gt_ref[0].T
    gate = lambda br: jnp.concatenate([gtt[br * HPG + h:br * HPG + h + 1, :] for h in range(HPG)], axis=1)
    o_ref[0, 0, 0] = (gate(0) * o_c + gate(1) * o_s + gate(2) * o_w).astype(BF16)


def _attention(qt, qc, kc, vct, ks, vs, kw, vw, gt, tn, tw, tct, ovt):
    B, G, nq, _, _ = qt.shape
    S = nq * Q_TILE
    sp = S + KEY_PAD
    ncp = S // CMP_STRIDE
    c0 = (Q_TILE // CMP_STRIDE) * (nq - 1)
    per_g = lambda a: pl.BlockSpec((1,) + a.shape[1:], lambda b, g, i: (g,) + (0,) * (a.ndim - 1))
    val = pl.BlockSpec((1, 1, sp // LANES, LANES, LANES), lambda b, g, i: (b, g, 0, 0, 0))
    return pl.pallas_call(
        functools.partial(_attn_kernel, c0=c0),
        out_shape=jax.ShapeDtypeStruct((B, G, nq, HEAD_DIM, ROWS), BF16),
        grid=(B, G, nq),
        in_specs=[pl.BlockSpec((1, 1, 1, HEAD_DIM, ROWS), lambda b, g, i: (b, g, i, 0, 0)),
                  per_g(qc),
                  pl.BlockSpec((1, 1, ncp, LANES), lambda b, g, i: (b, g, 0, 0)),
                  pl.BlockSpec((1, 1, LANES, ncp), lambda b, g, i: (b, g, 0, 0)),
                  pl.BlockSpec((1, sp, 2 * LANES), lambda b, g, i: (b, 0, g)),
                  val,
                  pl.BlockSpec((1, sp, LANES), lambda b, g, i: (b, 0, g)),
                  val,
                  pl.BlockSpec((1, Q_TILE, LANES), lambda b, g, i: (b, i, g)),
                  per_g(tn), per_g(tw), per_g(tct),
                  pl.BlockSpec((LANES, ncp), lambda b, g, i: (0, 0))],
        out_specs=pl.BlockSpec((1, 1, 1, HEAD_DIM, ROWS), lambda b, g, i: (b, g, i, 0, 0)),
        scratch_shapes=[pltpu.VMEM((2 * LANES, ROWS), BF16)],
        compiler_params=_cparams(("arbitrary", "arbitrary", "arbitrary")),
        name="nsa_attention",
    )(qt, qc, kc, vct, ks, vs, kw, vw, gt, tn, tw, tct, ovt)


def _post_attn_kernel(x_ref, ypc_ref, yn_ref, wo_ref, g1_ref, n2_ref, sc_ref, sh_ref, rwh_ref, rwl_ref, rb_ref,
                      xo_ref, h2_ref, ei_ref, tw_ref, cnt_ref, run_ref):
    first = (pl.program_id(0) == 0) & (pl.program_id(1) == 0)

    @pl.when(first)
    def _():
        run_ref[...] = jnp.zeros_like(run_ref)

    tm = x_ref.shape[1]
    half = wo_ref.shape[0] // 2
    mixed = (jnp.dot(ypc_ref[0], wo_ref[0:half, :], preferred_element_type=F32)
             + jnp.dot(yn_ref[0], wo_ref[half:, :], preferred_element_type=F32))
    x = x_ref[0] + g1_ref[0] * mixed
    xo_ref[0] = x
    y = x * lax.rsqrt(jnp.mean(x * x, axis=-1, keepdims=True) + EPS) * n2_ref[...]
    h2 = y * (1.0 + sc_ref[0]) + sh_ref[0]
    _rows_to_tiles(h2_ref, h2)
    hh = h2.astype(BF16)
    hl = (h2 - hh.astype(F32)).astype(BF16)
    logit = (jnp.dot(hh, rwh_ref[...], preferred_element_type=F32)
             + jnp.dot(hl, rwh_ref[...], preferred_element_type=F32)
             + jnp.dot(hh, rwl_ref[...], preferred_element_type=F32)) + rb_ref[...]
    lane = lax.broadcasted_iota(I32, (tm, LANES), 1)
    vals, hots, idxs = [], [], []
    for _ in range(TOP_K):
        mx = jnp.max(logit, axis=1, keepdims=True)
        idx = jnp.min(jnp.where(logit == mx, lane, LANES), axis=1, keepdims=True)
        hot = lane == idx
        vals.append(mx)
        hots.append(hot)
        idxs.append(idx)
        logit = jnp.where(hot, -jnp.inf, logit)
    ex = [jnp.exp(v - vals[0]) for v in vals]
    inv = 1.0 / (ex[0] + ex[1] + ex[2] + ex[3])
    assign = (hots[0] | hots[1] | hots[2] | hots[3]).astype(BF16)
    r = lax.broadcasted_iota(I32, (tm, tm), 0)
    c = lax.broadcasted_iota(I32, (tm, tm), 1)
    before = jnp.dot((c < r).astype(BF16), assign, preferred_element_type=F32) + run_ref[...]
    ei = jnp.zeros((tm, LANES), I32)
    tw = jnp.zeros((tm, LANES), F32)
    for k in range(TOP_K):
        e_k = idxs[k]
        r_k = jnp.sum(jnp.where(hots[k], before, 0.0), axis=1, keepdims=True).astype(I32)
        ei = jnp.where(lane == k, e_k, jnp.where(lane == TOP_K + k, r_k, ei))
        tw = jnp.where(lane == k, ex[k] * inv, tw)
    ei_ref[0] = ei
    tw_ref[0] = tw
    run_ref[...] = run_ref[...] + jnp.sum(assign.astype(F32), axis=0, keepdims=True)
    cnt_ref[...] = run_ref[...]


def _post_attn(x, ypc, ynsa, w_out, g1, n2g, sc2, sh2, rw_hi, rw_lo, rb):
    B, S, D = x.shape
    tm = TOK_TILE
    tok = lambda w: pl.BlockSpec((1, tm, w), lambda b, i: (b, i, 0))
    per_b = pl.BlockSpec((1, 1, D), lambda b, i: (b, 0, 0))
    full = lambda a: pl.BlockSpec(a.shape, lambda b, i: (0,) * a.ndim)
    return pl.pallas_call(
        _post_attn_kernel,
        out_shape=[jax.ShapeDtypeStruct((B, S, D), F32), jax.ShapeDtypeStruct((B * S * ROW_TILES, LANES), F32),
                   jax.ShapeDtypeStruct((B, S, LANES), I32), jax.ShapeDtypeStruct((B, S, LANES), F32),
                   jax.ShapeDtypeStruct((1, LANES), F32)],
        grid=(B, S // tm),
        in_specs=[tok(D), tok(ypc.shape[-1]), tok(ynsa.shape[-1]), full(w_out), per_b,
                  pl.BlockSpec((1, D), lambda b, i: (0, 0)), per_b, per_b,
                  full(rw_hi), full(rw_lo), full(rb)],
        out_specs=[tok(D), pl.BlockSpec((tm * ROW_TILES, LANES), lambda b, i: (b * (S // tm) + i, 0)),
                   tok(LANES), tok(LANES), pl.BlockSpec((1, LANES), lambda b, i: (0, 0))],
        scratch_shapes=[pltpu.VMEM((1, LANES), F32)],
        compiler_params=_cparams(("arbitrary", "arbitrary")),
        name="post_attn_router",
    )(x, ypc, ynsa, w_out, g1, n2g.reshape(1, D), sc2, sh2, rw_hi, rw_lo, rb)


ROW_TILES = 8
DMA_UNROLL = 2


def _rows_to_tiles(ref, val):
    n = val.shape[0]
    for s in range(ROW_TILES):
        ref[pl.ds(s, n, stride=ROW_TILES), :] = val[:, s * LANES:(s + 1) * LANES]


def _tiles_to_rows(ref, n):
    return jnp.concatenate([ref[pl.ds(s, n, stride=ROW_TILES), :] for s in range(ROW_TILES)], axis=1)


def _tile_at(ref, i):
    return ref.at[pl.ds(pl.multiple_of(i * ROW_TILES, ROW_TILES), ROW_TILES), :]


def _dispatch_kernel(fill0_ref, filln_ref, dest_ref, h_ref, xs_ref, zero_ref, sem, zsem):
    tm = h_ref.shape[0] // ROW_TILES

    @pl.when(pl.program_id(0) == 0)
    def _():
        zero_ref[...] = jnp.zeros_like(zero_ref)

        def expert(e, c):
            def copy(s):
                return pltpu.make_async_copy(zero_ref, _tile_at(xs_ref, fill0_ref[e] + s), zsem)

            lax.fori_loop(0, filln_ref[e], lambda s, c2: (copy(s).start(), c2)[1], 0)
            lax.fori_loop(0, filln_ref[e], lambda s, c2: (copy(s).wait(), c2)[1], 0)
            return c

        lax.fori_loop(0, N_EXPERTS, expert, 0)

    def body(i, c):
        for u in range(DMA_UNROLL):
            t = i * DMA_UNROLL + u
            src = _tile_at(h_ref, t)
            for k in range(TOP_K):
                pltpu.make_async_copy(src, _tile_at(xs_ref, dest_ref[0, 0, t * TOP_K + k]), sem).start()
        return c

    lax.fori_loop(0, tm // DMA_UNROLL, body, 0)
    for _ in range(TOP_K):
        pltpu.make_async_copy(h_ref, xs_ref.at[pl.ds(0, tm * ROW_TILES), :], sem).wait()


def _dispatch(dest, h2t, fill0, filln, n_slots):
    tm = TOK_TILE
    nt = h2t.shape[0] // (tm * ROW_TILES)
    return pl.pallas_call(
        _dispatch_kernel,
        out_shape=jax.ShapeDtypeStruct((n_slots * ROW_TILES, LANES), F32),
        grid_spec=pltpu.PrefetchScalarGridSpec(
            num_scalar_prefetch=2,
            grid=(nt,),
            in_specs=[pl.BlockSpec((1, 1, tm * TOP_K), lambda i, f0, fn: (i, 0, 0), memory_space=pltpu.SMEM),
                      pl.BlockSpec((tm * ROW_TILES, LANES), lambda i, f0, fn: (i, 0))],
            out_specs=pl.BlockSpec(memory_space=pl.ANY),
            scratch_shapes=[pltpu.VMEM((ROW_TILES, LANES), F32), pltpu.SemaphoreType.DMA(()),
                            pltpu.SemaphoreType.DMA(())]),
        compiler_params=_cparams(("arbitrary",)),
        name="moe_dispatch",
    )(fill0, filln, dest.reshape(nt, 1, tm * TOP_K), h2t)


W_CHUNK = 512


def _expert_kernel(be_ref, nu_ref, x_ref, wgu_ref, bgu_ref, wd_ref, bd_ref, y_ref, wgu_s, wd_s):
    i = pl.program_id(0)

    @pl.when(i < nu_ref[0])
    def _():
        @pl.when((i == 0) | (be_ref[i] != be_ref[jnp.maximum(i - 1, 0)]))
        def _():
            for c in range(0, wgu_s.shape[1], W_CHUNK):
                wgu_s[:, c:c + W_CHUNK] = wgu_ref[0, :, c:c + W_CHUNK].astype(BF16)
            for c in range(0, wd_s.shape[1], W_CHUNK):
                wd_s[:, c:c + W_CHUNK] = wd_ref[0, :, c:c + W_CHUNK].astype(BF16)

        F = wd_s.shape[0]
        x = _tiles_to_rows(x_ref, EXPERT_BLOCK).astype(BF16)
        gu = jnp.dot(x, wgu_s[...], preferred_element_type=F32) + bgu_ref[0]
        gate = jnp.minimum(gu[:, :F], SWIGLU_LIMIT)
        up = jnp.clip(gu[:, F:], -SWIGLU_LIMIT, SWIGLU_LIMIT)
        act = (up + 1.0) * gate * jax.nn.sigmoid(SWIGLU_ALPHA * gate)
        _rows_to_tiles(y_ref, jnp.dot(act.astype(BF16), wd_s[...], preferred_element_type=F32) + bd_ref[0])


def _experts(blk_e, n_used, xs, layer, w_gu, b_gu, w_down, b_down):
    L, E, D, F2 = w_gu.shape
    F = F2 // 2
    rows = EXPERT_BLOCK * ROW_TILES
    nb = xs.shape[0] // rows
    blk = lambda i, be, nu: (jnp.minimum(i, nu[0] - 1), 0)
    per_e = lambda i, be, nu: (layer, be[i], 0, 0)
    return pl.pallas_call(
        _expert_kernel,
        out_shape=jax.ShapeDtypeStruct(xs.shape, F32),
        grid_spec=pltpu.PrefetchScalarGridSpec(
            num_scalar_prefetch=2,
            grid=(nb,),
            in_specs=[pl.BlockSpec((rows, LANES), blk),
                      pl.BlockSpec((None, 1, D, F2), per_e),
                      pl.BlockSpec((None, 1, 1, F2), per_e),
                      pl.BlockSpec((None, 1, F, D), per_e),
                      pl.BlockSpec((None, 1, 1, D), per_e)],
            out_specs=pl.BlockSpec((rows, LANES), blk),
            scratch_shapes=[pltpu.VMEM((D, F2), BF16), pltpu.VMEM((F, D), BF16)]),
        compiler_params=_cparams(("arbitrary",)),
        name="moe_experts",
    )(blk_e, n_used, xs, w_gu, b_gu.reshape(L, E, 1, F2), w_down, b_down.reshape(L, E, 1, D))


def _combine_kernel(dest_ref, y_ref, x_ref, tw_ref, g2_ref, fg_ref, o_ref, rows_ref, sem, *, final):
    tm = x_ref.shape[1]

    def body(i, c):
        for u in range(DMA_UNROLL):
            t = i * DMA_UNROLL + u
            for k in range(TOP_K):
                pltpu.make_async_copy(_tile_at(y_ref, dest_ref[0, 0, t * TOP_K + k]),
                                      _tile_at(rows_ref.at[k], t), sem).start()
        return c

    lax.fori_loop(0, tm // DMA_UNROLL, body, 0)
    for k in range(TOP_K):
        pltpu.make_async_copy(y_ref.at[pl.ds(0, tm * ROW_TILES), :], rows_ref.at[k], sem).wait()
    tw = tw_ref[0]
    moe = tw[:, 0:1] * _tiles_to_rows(rows_ref.at[0], tm)
    for k in range(1, TOP_K):
        moe = moe + tw[:, k:k + 1] * _tiles_to_rows(rows_ref.at[k], tm)
    x = x_ref[0] + g2_ref[0] * moe
    if final:
        x = x * lax.rsqrt(jnp.mean(x * x, axis=-1, keepdims=True) + EPS) * fg_ref[...]
    o_ref[0] = x


def _combine(dest, y, x, tw, g2, final_g, final):
    B, S, D = x.shape
    tm = TOK_TILE
    nt = S // tm
    tok = lambda w: pl.BlockSpec((1, tm, w), lambda b, i: (b, i, 0))
    return pl.pallas_call(
        functools.partial(_combine_kernel, final=final),
        out_shape=jax.ShapeDtypeStruct((B, S, D), F32),
        grid=(B, nt),
        in_specs=[pl.BlockSpec((1, 1, tm * TOP_K), lambda b, i: (b * nt + i, 0, 0), memory_space=pltpu.SMEM),
                  pl.BlockSpec(memory_space=pl.ANY),
                  tok(D), tok(LANES),
                  pl.BlockSpec((1, 1, D), lambda b, i: (b, 0, 0)),
                  pl.BlockSpec((1, D), lambda b, i: (0, 0))],
        out_specs=tok(D),
        scratch_shapes=[pltpu.VMEM((TOP_K, tm * ROW_TILES, LANES), F32), pltpu.SemaphoreType.DMA(())],
        compiler_params=_cparams(("arbitrary", "arbitrary")),
        name="moe_combine",
    )(dest.reshape(B * nt, 1, tm * TOP_K), y, x, tw, g2, final_g.reshape(1, D))


def _moe(x, h2, ei, tw, counts, g2, layer, w_gu, b_gu, w_down, b_down, final_g, final):
    B, S, D = x.shape
    N = B * S
    n_slots = -(-(N * TOP_K + N_EXPERTS * EXPERT_BLOCK) // EXPERT_BLOCK) * EXPERT_BLOCK
    nb = n_slots // EXPERT_BLOCK
    cnt = counts[0, :N_EXPERTS].astype(I32)
    padded = (cnt + EXPERT_BLOCK - 1) // EXPERT_BLOCK * EXPERT_BLOCK
    pend = jnp.cumsum(padded)
    pstart = pend - padded
    ei2 = ei.reshape(N, LANES)
    dest = (pstart[ei2[:, 0:TOP_K]] + ei2[:, TOP_K:2 * TOP_K]).reshape(N * TOP_K)
    blk_start = jnp.arange(nb, dtype=I32) * EXPERT_BLOCK
    blk_e = jnp.minimum(jnp.sum((pend[None, :] <= blk_start[:, None]).astype(I32), axis=1), N_EXPERTS - 1)
    n_used = (pend[-1:] // EXPERT_BLOCK).astype(I32)
    xs = _dispatch(dest, h2, pstart + cnt, padded - cnt, n_slots)
    y = _experts(blk_e, n_used, xs, layer, w_gu, b_gu, w_down, b_down)
    return _combine(dest, y, x, tw, g2, final_g, final)


def kernel(x, c, w_mod, b_mod, norm1_g, norm2_g, w_in, w_out, pool_w, pool_scale, conv_w, conv_b, conv_ln_g,
           conv_ln_b, cmp_pe_k, cmp_pe_v, cmp_w1_k, cmp_w2_k, cmp_w1_v, cmp_w2_v, rel_bias, router_w, router_b,
           expert_w_gu, expert_b_gu, expert_w_down, expert_b_down, final_g):
    B, S, D = x.shape
    L = w_mod.shape[0]
    assert S % TOK_TILE == 0 and S // SEL_BLOCK <= LANES and D == ROW_TILES * LANES
    mod = _modulation(c, w_mod, b_mod)
    nq = S // Q_TILE
    t_near, t_win, t_cmp, qc = _bias_tables(rel_bias, S)
    ovt = _overlap_t(S)
    w_big = _in_weight(w_in)
    cg = POOL_DIM // POOL_GROUPS
    for l in range(L):
        m6 = mod[l].reshape(B, 6, 1, D)
        sh1, sc1, g1, sh2, sc2, g2 = (m6[:, k] for k in range(6))
        upc, q, kvc, ks, vs, kw, vw, gt = _in_proj(x, norm1_g[l], sc1, sh1, w_big[l])
        pw_bd = jnp.zeros((POOL_DIM, POOL_DIM), F32)
        for g in range(POOL_GROUPS):
            pw_bd = lax.dynamic_update_slice(pw_bd, pool_w[l, g], (g * cg, g * cg))
        cw = jnp.pad(conv_w[l], ((0, 1), (0, 0)))
        ypc = _pool_conv(upc, pw_bd.astype(BF16), pool_scale[l], cw, conv_b[l], conv_ln_g[l], conv_ln_b[l])
        kc, vct = _compress(kvc, _cmp_weights(cmp_w1_k[l], cmp_w1_v[l]), cmp_pe_k[l], cmp_pe_v[l],
                            cmp_w1_k[l], cmp_w1_v[l], cmp_w2_k[l], cmp_w2_v[l])
        qt = (q.reshape(B, nq, Q_TILE, KV_GROUPS, HPG, HEAD_DIM).transpose(0, 3, 1, 5, 4, 2)
              .reshape(B, KV_GROUPS, nq, HEAD_DIM, ROWS))
        ot = _attention(qt, qc, kc, vct, _pad_keys(ks, 2 * LANES), _value_tiles(vs), _pad_keys(kw, LANES),
                        _value_tiles(vw), gt, t_near, t_win, t_cmp, ovt)
        ynsa = (ot.reshape(B, KV_GROUPS, nq, HEAD_DIM, HPG, Q_TILE).transpose(0, 2, 5, 1, 4, 3)
                .reshape(B, S, NSA_DIM))
        rw = jnp.pad(router_w[l], ((0, 0), (0, LANES - N_EXPERTS)))
        rw_hi = rw.astype(BF16)
        rw_lo = (rw - rw_hi.astype(F32)).astype(BF16)
        rb = jnp.pad(router_b[l].reshape(1, -1), ((0, 0), (0, LANES - N_EXPERTS)), constant_values=NEG_INF)
        x, h2, ei, tw, counts = _post_attn(x, ypc, ynsa, w_out[l].astype(BF16), g1, norm2_g[l], sc2, sh2,
                                           rw_hi, rw_lo, rb)
        x = _moe(x, h2, ei, tw, counts, g2, l, expert_w_gu, expert_b_gu, expert_w_down, expert_b_down,
                 final_g, final=(l == L - 1))
    return x
```

```python
import functools
import math

import jax
import jax.numpy as jnp
import numpy as np
from jax import lax
from jax.experimental import pallas as pl
from jax.experimental.pallas import tpu as pltpu

F32 = jnp.float32
BF16 = jnp.bfloat16
I32 = jnp.int32

HEAD_DIM = 64
POOL_DIM = 256
POOL_GROUPS = 4
POOL_WINDOWS = (2, 4, 8, 16)
CONV_DIM = 256
CONV_WIDTH = 31
NSA_DIM = 512
NSA_HEADS = 8
KV_GROUPS = 2
HPG = 4
CMP_LEN = 32
CMP_STRIDE = 16
CMP_HIDDEN = 128
SEL_BLOCK = 64
SEL_TOPN = 16
N_FORCED = 3
WINDOW = 512
Q_TILE = 128
N_BUCKETS = 32
MAX_DISTANCE = 1024
N_EXPERTS = 32
TOP_K = 4
SWIGLU_ALPHA = 1.702
SWIGLU_LIMIT = 7.0
EPS = 1e-5
NEG_INF = -1e30
FORCE_SCORE = 1e4

LANES = 128
ROWS = HPG * Q_TILE
LOG2E = 1.4426950408889634
FAR_HEADROOM = 100.0
NEAR_KEYS = 1024
WIN_KEYS = WINDOW + Q_TILE
FAR_TILE = 512
KEY_PAD = 2 * FAR_TILE
MASK_BIAS = -32768.0
TOK_TILE = 512
EXPERT_BLOCK = 512
VMEM_LIMIT = 56 * 1024 * 1024


def _cparams(sem, vmem=VMEM_LIMIT):
    return pltpu.CompilerParams(dimension_semantics=sem, vmem_limit_bytes=vmem)


def _t5_bucket(n):
    n = jnp.maximum(n, 0)
    max_exact = N_BUCKETS // 2
    nf = jnp.maximum(n, 1).astype(F32)
    large = max_exact + (jnp.log(nf / max_exact) / math.log(MAX_DISTANCE / max_exact)
                         * (N_BUCKETS - max_exact)).astype(I32)
    large = jnp.minimum(large, N_BUCKETS - 1)
    return jnp.where(n < max_exact, n, large)


def _mod_kernel(c_ref, w_ref, b_ref, o_ref):
    c = c_ref[...]
    cond = c * jax.nn.sigmoid(c)
    o_ref[0] = jnp.dot(cond.astype(BF16), w_ref[0].astype(BF16),
                       preferred_element_type=F32) + b_ref[0]


def _modulation(c, w_mod, b_mod):
    L, D, W = w_mod.shape
    B = c.shape[0]
    tn = 1536
    return pl.pallas_call(
        _mod_kernel,
        out_shape=jax.ShapeDtypeStruct((L, B, W), F32),
        grid=(L, W // tn),
        in_specs=[pl.BlockSpec((B, D), lambda l, j: (0, 0)),
                  pl.BlockSpec((1, D, tn), lambda l, j: (l, 0, j)),
                  pl.BlockSpec((1, 1, tn), lambda l, j: (l, 0, j))],
        out_specs=pl.BlockSpec((1, B, tn), lambda l, j: (l, 0, j)),
        compiler_params=_cparams(("arbitrary", "arbitrary")),
        name="modulation",
    )(c, w_mod, b_mod.reshape(L, 1, W))


C_UPC, C_Q, C_KVC, C_KS, C_VS, C_KW, C_VW, C_GT, C_END = 0, 768, 1280, 1536, 2048, 2304, 2560, 2816, 3072


def _in_weight(w_in):
    col = lambda a, n: w_in[:, :, a:a + n]
    zero = lambda n: jnp.zeros(w_in.shape[:2] + (n,), w_in.dtype)
    parts = [col(0, 768), col(768, 512) * (HEAD_DIM ** -0.5 * LOG2E), col(1280, 256)]
    for g in range(KV_GROUPS):
        parts += [col(1536 + g * HEAD_DIM, HEAD_DIM), zero(2 * LANES - HEAD_DIM)]
    for base in (1664, 1792, 1920):
        for g in range(KV_GROUPS):
            parts += [col(base + g * HEAD_DIM, HEAD_DIM), zero(LANES - HEAD_DIM)]
    for g in range(KV_GROUPS):
        parts += [col(2048 + br * NSA_HEADS + g * HPG, HPG) for br in range(3)] + [zero(LANES - 3 * HPG)]
    return jnp.concatenate(parts, axis=2).astype(BF16)


def _inproj_kernel(x_ref, g_ref, sc_ref, sh_ref, w_ref,
                   upc_ref, q_ref, kvc_ref, ks_ref, vs_ref, kw_ref, vw_ref, gt_ref):
    ti = pl.program_id(1)
    x = x_ref[0]
    tm = x.shape[0]
    y = x * lax.rsqrt(jnp.mean(x * x, axis=-1, keepdims=True) + EPS) * g_ref[...]
    h = y * (1.0 + sc_ref[0]) + sh_ref[0]
    z = jnp.dot(h.astype(BF16), w_ref[...], preferred_element_type=F32)
    upc_ref[0] = z[:, C_UPC:C_Q]
    q_ref[0] = z[:, C_Q:C_KVC].astype(BF16)
    kvc_ref[0] = z[:, C_KVC:C_KS].astype(BF16)
    lane = lax.broadcasted_iota(I32, (tm, 256), 1)
    row = lax.broadcasted_iota(I32, (tm, 256), 0)
    blk = (ti * tm + row) // SEL_BLOCK
    onehot = ((lane - LANES == blk) | (lane == HEAD_DIM + 1) | (lane == HEAD_DIM + 2)).astype(F32)
    for g in range(KV_GROUPS):
        ks_ref[0, :, g * 256:(g + 1) * 256] = (z[:, C_KS + g * 256:C_KS + (g + 1) * 256] + onehot).astype(BF16)
    lane = lax.broadcasted_iota(I32, (tm, 256), 1)
    ones_col = ((lane % LANES) == HEAD_DIM).astype(F32)
    vs_ref[0] = (z[:, C_VS:C_KW] + ones_col).astype(BF16)
    kw_ref[0] = z[:, C_KW:C_VW].astype(BF16)
    vw_ref[0] = (z[:, C_VW:C_GT] + ones_col).astype(BF16)
    gt_ref[0] = jax.nn.sigmoid(z[:, C_GT:C_END])


def _in_proj(x, g1, sc, sh, w_big):
    B, S, D = x.shape
    tm = TOK_TILE
    tok = lambda w: pl.BlockSpec((1, tm, w), lambda b, i: (b, i, 0))
    outs = [(768, F32), (512, BF16), (256, BF16), (512, BF16), (256, BF16), (256, BF16), (256, BF16), (256, F32)]
    return pl.pallas_call(
        _inproj_kernel,
        out_shape=[jax.ShapeDtypeStruct((B, S, w), dt) for w, dt in outs],
        grid=(B, S // tm),
        in_specs=[tok(D),
                  pl.BlockSpec((1, D), lambda b, i: (0, 0)),
                  pl.BlockSpec((1, 1, D), lambda b, i: (b, 0, 0)),
                  pl.BlockSpec((1, 1, D), lambda b, i: (b, 0, 0)),
                  pl.BlockSpec((D, C_END), lambda b, i: (0, 0))],
        out_specs=[tok(w) for w, _ in outs],
        compiler_params=_cparams(("arbitrary", "arbitrary")),
        name="in_proj",
    )(x, g1.reshape(1, D), sc, sh, w_big)


HALO = 32


def _poolconv_kernel(cur_ref, halo_ref, pw_ref, ps_ref, cw_ref, cb_ref, lg_ref, lb_ref, o_ref, ext_ref, v_ref):
    ti = pl.program_id(1)
    ts = cur_ref.shape[1]
    halo = halo_ref[0] * (ti > 0).astype(F32)
    ext_ref[0:HALO, :] = halo
    ext_ref[HALO:HALO + ts, :] = cur_ref[0]
    u = ext_ref[HALO:HALO + ts, 0:POOL_DIM]
    lane = lax.broadcasted_iota(I32, (ts, POOL_DIM), 1)
    grp = lane // (POOL_DIM // POOL_GROUPS)
    run = u
    pooled = jnp.zeros_like(u)
    for k in range(1, POOL_WINDOWS[-1]):
        run = run + ext_ref[HALO - k:HALO - k + ts, 0:POOL_DIM]
        if (k + 1) in POOL_WINDOWS:
            pooled = jnp.where(grp == POOL_WINDOWS.index(k + 1), run, pooled)
    wlane = jnp.where(grp == 0, 2.0, jnp.where(grp == 1, 4.0, jnp.where(grp == 2, 8.0, 16.0)))
    t1 = (ti * ts + lax.broadcasted_iota(I32, (ts, POOL_DIM), 0) + 1).astype(F32)
    cnt = jnp.minimum(t1, wlane)
    pooled = pooled / cnt - u
    y_pool = jnp.dot(pooled.astype(BF16), pw_ref[...], preferred_element_type=F32) * ps_ref[...]
    o_ref[0, :, 0:POOL_DIM] = y_pool.astype(BF16)
    uv = ext_ref[:, POOL_DIM:POOL_DIM + CONV_DIM]
    ug = ext_ref[:, POOL_DIM + CONV_DIM:POOL_DIM + 2 * CONV_DIM]
    v_ref[...] = uv * jax.nn.sigmoid(ug)
    acc = jnp.zeros((ts, CONV_DIM), F32) + cb_ref[...]
    for k in range(CONV_WIDTH):
        o = HALO - (CONV_WIDTH - 1) + k
        acc = acc + v_ref[o:o + ts, :] * cw_ref[k:k + 1, :]
    mu = jnp.mean(acc, axis=-1, keepdims=True)
    d = acc - mu
    var = jnp.mean(d * d, axis=-1, keepdims=True)
    yn = d * lax.rsqrt(var + EPS) * lg_ref[...] + lb_ref[...]
    o_ref[0, :, POOL_DIM:POOL_DIM + CONV_DIM] = (yn * jax.nn.sigmoid(yn)).astype(BF16)


def _pool_conv(upc, pool_w_bd, pool_scale, conv_w, conv_b, ln_g, ln_b):
    B, S, W = upc.shape
    ts = TOK_TILE
    r = ts // HALO
    vec = lambda n: pl.BlockSpec((1, n), lambda b, i: (0, 0))
    return pl.pallas_call(
        _poolconv_kernel,
        out_shape=jax.ShapeDtypeStruct((B, S, POOL_DIM + CONV_DIM), BF16),
        grid=(B, S // ts),
        in_specs=[pl.BlockSpec((1, ts, W), lambda b, i: (b, i, 0)),
                  pl.BlockSpec((1, HALO, W), lambda b, i: (b, jnp.maximum(i * r - 1, 0), 0)),
                  pl.BlockSpec((POOL_DIM, POOL_DIM), lambda b, i: (0, 0)),
                  vec(POOL_DIM),
                  pl.BlockSpec((CONV_WIDTH + 1, CONV_DIM), lambda b, i: (0, 0)),
                  vec(CONV_DIM), vec(CONV_DIM), vec(CONV_DIM)],
        out_specs=pl.BlockSpec((1, ts, POOL_DIM + CONV_DIM), lambda b, i: (b, i, 0)),
        scratch_shapes=[pltpu.VMEM((HALO + ts, W), F32), pltpu.VMEM((HALO + ts, CONV_DIM), F32)],
        compiler_params=_cparams(("arbitrary", "arbitrary")),
        name="pool_conv",
    )(upc, upc, pool_w_bd, pool_scale.reshape(1, -1), conv_w, conv_b.reshape(1, -1),
      ln_g.reshape(1, -1), ln_b.reshape(1, -1))


N_STREAM = 2 * KV_GROUPS
CHUNK_W = CMP_STRIDE * 2 * KV_GROUPS * HEAD_DIM


def _cmp_weights(w1_k, w1_v):
    half = CMP_STRIDE * HEAD_DIM
    cols = []
    for s in range(N_STREAM):
        w1 = w1_k if s < KV_GROUPS else w1_v
        for part in range(2):
            blk = w1[part * half:(part + 1) * half].reshape(CMP_STRIDE, 1, HEAD_DIM, CMP_HIDDEN)
            z = jnp.zeros((CMP_STRIDE, N_STREAM, HEAD_DIM, CMP_HIDDEN), w1.dtype)
            z = lax.dynamic_update_slice(z, blk, (0, s, 0, 0))
            cols.append(z.reshape(CHUNK_W, CMP_HIDDEN))
    return jnp.concatenate(cols, axis=1).astype(BF16)


def _gelu_tanh(x):
    return 0.5 * x * (1.0 + jnp.tanh(math.sqrt(2.0 / math.pi) * (x + 0.044715 * (x * x * x))))


def _compress_kernel(c_ref, w_ref, pek_ref, pev_ref, w1k_ref, w1v_ref, w2k_ref, w2vt_ref, kc_ref, vct_ref):
    r = jnp.dot(c_ref[0], w_ref[...], preferred_element_type=F32)
    ncp = r.shape[0]
    pe_k = jnp.dot(pek_ref[...], w1k_ref[...], preferred_element_type=F32)[0:1]
    pe_v = jnp.dot(pev_ref[...], w1v_ref[...], preferred_element_type=F32)[0:1]
    for s in range(N_STREAM):
        a = r[:, s * 256:s * 256 + CMP_HIDDEN]
        b = r[:, s * 256 + CMP_HIDDEN:(s + 1) * 256]
        hid = a + pltpu.roll(b, ncp - 1, 0) + (pe_k if s < KV_GROUPS else pe_v)
        act = _gelu_tanh(hid).astype(BF16)
        if s < KV_GROUPS:
            kc_ref[0, s] = jnp.dot(act, w2k_ref[...], preferred_element_type=F32).astype(BF16)
        else:
            vct_ref[0, s - KV_GROUPS] = lax.dot_general(
                w2vt_ref[...], act, (((1,), (1,)), ((), ())), preferred_element_type=F32).astype(BF16)


def _compress(kvc, wcmp, pe_k, pe_v, w1_k, w1_v, w2_k, w2_v):
    B, S, _ = kvc.shape
    ncp = S // CMP_STRIDE
    chunks = kvc.reshape(B, ncp, CHUNK_W)
    pe8 = lambda pe: jnp.broadcast_to(pe.reshape(1, -1), (8, CMP_LEN * HEAD_DIM)).astype(BF16)
    w2k = jnp.pad(w2_k, ((0, 0), (0, LANES - HEAD_DIM))).astype(BF16)
    w2vt = jnp.pad(w2_v.T, ((0, LANES - HEAD_DIM), (0, 0))).astype(BF16)
    full = lambda a: pl.BlockSpec(a.shape, lambda b: (0,) * a.ndim)
    args = (wcmp, pe8(pe_k), pe8(pe_v), w1_k.astype(BF16), w1_v.astype(BF16), w2k, w2vt)
    return pl.pallas_call(
        _compress_kernel,
        out_shape=[jax.ShapeDtypeStruct((B, KV_GROUPS, ncp, LANES), BF16),
                   jax.ShapeDtypeStruct((B, KV_GROUPS, LANES, ncp), BF16)],
        grid=(B,),
        in_specs=[pl.BlockSpec((1, ncp, CHUNK_W), lambda b: (b, 0, 0))] + [full(a) for a in args],
        out_specs=[pl.BlockSpec((1, KV_GROUPS, ncp, LANES), lambda b: (b, 0, 0, 0)),
                   pl.BlockSpec((1, KV_GROUPS, LANES, ncp), lambda b: (b, 0, 0, 0))],
        compiler_params=_cparams(("arbitrary",)),
        name="compress",
    )(chunks, *args)


def _bias_tables(rel_bias, S):
    nq = S // Q_TILE
    ncp = S // CMP_STRIDE
    rb = rel_bias.reshape(N_BUCKETS, KV_GROUPS, HPG).transpose(1, 2, 0) * LOG2E
    far = rb[:, :, N_BUCKETS - 1]
    far_hi = far.astype(BF16)
    far_lo = (far - far_hi.astype(F32)).astype(BF16)
    far_sum = far_hi.astype(F32) + far_lo.astype(F32)
    i = np.arange(Q_TILE)[None, :]

    def table(d, valid, sub):
        onehot = jax.nn.one_hot(_t5_bucket(jnp.asarray(d, I32)), N_BUCKETS, dtype=F32)
        t = jnp.einsum('rib,ghb->grhi', onehot, rb, precision=lax.Precision.HIGHEST)
        t = jnp.where(jnp.asarray(valid)[None, :, None, :], t - sub[:, None, :, None], NEG_INF)
        return t.reshape(KV_GROUPS, d.shape[0], ROWS)

    d = i - np.arange(NEAR_KEYS)[:, None] + (NEAR_KEYS - Q_TILE)
    t_near = table(d, d >= 0, far_sum)
    d = i - np.arange(WIN_KEYS)[:, None] + (WIN_KEYS - Q_TILE)
    t_win = table(d, (d >= 0) & (d < WINDOW), jnp.zeros_like(far_sum))
    c0 = (Q_TILE // CMP_STRIDE) * (nq - 1)
    d = i - CMP_STRIDE * (np.arange(c0 + ncp)[:, None] - c0) - (CMP_LEN - 1)
    t_cmp = table(d, d >= 0, jnp.zeros_like(far_sum))
    rows = jnp.zeros((KV_GROUPS, HEAD_DIM, HPG, Q_TILE), F32)
    rows = rows.at[:, 0].set(1.0)
    rows = rows.at[:, 1].set(jnp.broadcast_to(far_hi.astype(F32)[:, :, None], (KV_GROUPS, HPG, Q_TILE)))
    rows = rows.at[:, 2].set(jnp.broadcast_to(far_lo.astype(F32)[:, :, None], (KV_GROUPS, HPG, Q_TILE)))
    return t_near, t_win, t_cmp, rows.reshape(KV_GROUPS, HEAD_DIM, ROWS).astype(BF16)


def _overlap_t(S):
    ncp = S // CMP_STRIDE
    n = np.arange(ncp)[None, :]
    jb = np.arange(LANES)[:, None]
    end = n * CMP_STRIDE + CMP_LEN - 1
    start = n * CMP_STRIDE
    ov = (end >= jb * SEL_BLOCK) & (start < (jb + 1) * SEL_BLOCK) & (n < ncp - 1)
    return jnp.asarray(ov.astype(np.float32), BF16)


def _pad_keys(k, group_w):
    B, _, W = k.shape
    marker = (np.arange(W) % group_w == HEAD_DIM).astype(np.float32) * MASK_BIAS
    pad = jnp.broadcast_to(jnp.asarray(marker, k.dtype), (B, KEY_PAD, W))
    return jnp.concatenate([pad, k], axis=1)


def _value_tiles(v):
    B, S, _ = v.shape
    v = jnp.pad(v, ((0, 0), (KEY_PAD, 0), (0, 0)))
    return v.reshape(B, (S + KEY_PAD) // LANES, LANES, KV_GROUPS, LANES).transpose(0, 3, 1, 4, 2)


def _key_tiles(ref, k0, n):
    t0 = k0 // LANES
    return jnp.concatenate([ref[0, 0, t0 + u] for u in range(n)], axis=1)


def _attn_kernel(qt_ref, qc_ref, kc_ref, vct_ref, ks_ref, vs_ref, kw_ref, vw_ref, gt_ref, tn_ref, tw_ref, tct_ref,
                 ovt_ref, o_ref, qa_ref, acc_ref, *, c0):
    qi = pl.program_id(2)
    qt = qt_ref[0, 0, 0]
    ncp = kc_ref.shape[2]

    r0 = pl.multiple_of(c0 - (Q_TILE // CMP_STRIDE) * qi, 8)
    st = (jnp.dot(kc_ref[0, 0][:, 0:HEAD_DIM], qt, preferred_element_type=F32)
          + tct_ref[0, pl.ds(r0, ncp), :])
    mc = jnp.maximum(jnp.max(st, axis=0, keepdims=True), -1e20)
    pc = jnp.exp2(st - mc)
    pc = pc * (1.0 / jnp.maximum(jnp.sum(pc, axis=0, keepdims=True), 1e-30))
    o_c = jnp.dot(vct_ref[0, 0], pc.astype(BF16), preferred_element_type=F32)[0:HEAD_DIM]

    ps = pc[:, 0:Q_TILE]
    for h in range(1, HPG):
        ps = ps + pc[:, h * Q_TILE:(h + 1) * Q_TILE]
    ps_hi = ps.astype(BF16)
    ps_lo = (ps - ps_hi.astype(F32)).astype(BF16)
    imp = (jnp.dot(ovt_ref[...], ps_hi, preferred_element_type=F32)
           + jnp.dot(ovt_ref[...], ps_lo, preferred_element_type=F32))

    def flash_step(carry, s, vt):
        m, acc = carry
        m_new = jnp.maximum(m, jnp.max(s, axis=0, keepdims=True))
        alpha = jnp.exp2(m - m_new)
        p = jnp.exp2(s - m_new)
        acc = acc * alpha + jnp.dot(vt, p.astype(BF16), preferred_element_type=F32)
        return m_new, acc

    def normalised(acc):
        return acc[0:HEAD_DIM] * (1.0 / acc[HEAD_DIM:HEAD_DIM + 1])

    init = (jnp.full((1, ROWS), NEG_INF, F32), jnp.zeros((LANES, ROWS), F32))
    k1 = Q_TILE * qi + (KEY_PAD + Q_TILE)

    qa_ref[0:HEAD_DIM, :] = qt
    qa_ref[HEAD_DIM:LANES, :] = qc_ref[0]

    w0 = pl.multiple_of(k1 - WIN_KEYS, LANES)
    s = jnp.dot(kw_ref[0, pl.ds(w0, WIN_KEYS), :], qa_ref[0:LANES, :], preferred_element_type=F32) + tw_ref[0]
    o_w = normalised(flash_step(init, s, _key_tiles(vw_ref, w0, WIN_KEYS // LANES))[1])

    jb = lax.broadcasted_iota(I32, (LANES, Q_TILE), 0)
    ii = lax.broadcasted_iota(I32, (LANES, Q_TILE), 1)
    cur = (Q_TILE // SEL_BLOCK) * qi + (ii >= SEL_BLOCK).astype(I32)
    sel = (jb == 0) | (jb == cur) | (jb == cur - 1)
    score = jnp.where(jb <= cur, jnp.where(sel, -jnp.inf, imp), NEG_INF)
    for _ in range(SEL_TOPN - N_FORCED):
        mx = jnp.max(score, axis=0, keepdims=True)
        first = jnp.min(jnp.where(score == mx, jb, LANES), axis=0, keepdims=True)
        pick = jb == first
        sel = sel | pick
        score = jnp.where(pick, -jnp.inf, score)
    mbt = jnp.where(sel, 0.0, MASK_BIAS).astype(BF16)
    for h in range(HPG):
        qa_ref[LANES:2 * LANES, h * Q_TILE:(h + 1) * Q_TILE] = mbt

    n0 = pl.multiple_of(k1 - NEAR_KEYS, LANES)
    half = NEAR_KEYS // 2
    sa = (jnp.dot(ks_ref[0, pl.ds(n0, half), :], qa_ref[...], preferred_element_type=F32)
          + tn_ref[0, 0:half, :])
    sb = (jnp.dot(ks_ref[0, pl.ds(n0 + half, half), :], qa_ref[...], preferred_element_type=F32)
          + tn_ref[0, half:NEAR_KEYS, :])
    m_near = jnp.maximum(jnp.max(sa, axis=0, keepdims=True), jnp.max(sb, axis=0, keepdims=True))
    pa = jnp.exp2(sa - m_near).astype(BF16)
    da = jnp.dot(_key_tiles(vs_ref, n0, half // LANES), pa, preferred_element_type=F32)
    pb = jnp.exp2(sb - m_near).astype(BF16)
    acc_near = da + jnp.dot(_key_tiles(vs_ref, n0 + half, half // LANES), pb, preferred_element_type=F32)
    n_pairs = (jnp.maximum(n0 - KEY_PAD, 0) + 2 * FAR_TILE - 1) // (2 * FAR_TILE)

    def far_scores(j):
        k0 = pl.multiple_of(n0 - FAR_TILE * (j + 1), LANES)
        return jnp.dot(ks_ref[0, pl.ds(k0, FAR_TILE), :], qa_ref[...], preferred_element_type=F32)

    def far_values(j):
        return _key_tiles(vs_ref, pl.multiple_of(n0 - FAR_TILE * (j + 1), LANES), FAR_TILE // LANES)

    def fast_body(jj, top):
        sa = far_scores(2 * jj)
        sb = far_scores(2 * jj + 1)
        pa = jnp.exp2(sa - m_near).astype(BF16)
        da = jnp.dot(far_values(2 * jj), pa, preferred_element_type=F32)
        pb = jnp.exp2(sb - m_near).astype(BF16)
        acc_ref[...] += da + jnp.dot(far_values(2 * jj + 1), pb, preferred_element_type=F32)
        return jnp.maximum(top, jnp.maximum(jnp.max(sa, axis=0, keepdims=True),
                                            jnp.max(sb, axis=0, keepdims=True)))

    acc_ref[...] = acc_near
    top = lax.fori_loop(0, n_pairs, fast_body, m_near)

    @pl.when(jnp.max(top - m_near) > FAR_HEADROOM)
    def _():
        def safe_body(j, carry):
            return flash_step(carry, far_scores(j), far_values(j))

        acc_ref[...] = lax.fori_loop(0, 2 * n_pairs, safe_body, (m_near, acc_near))[1]

    o_s = normalised(acc_ref[...])

    gtt = gt_ref[0].T
    gate = lambda br: jnp.concatenate([gtt[br * HPG + h:br * HPG + h + 1, :] for h in range(HPG)], axis=1)
    o_ref[0, 0, 0] = (gate(0) * o_c + gate(1) * o_s + gate(2) * o_w).astype(BF16)


def _attention(qt, qc, kc, vct, ks, vs, kw, vw, gt, tn, tw, tct, ovt):
    B, G, nq, _, _ = qt.shape
    S = nq * Q_TILE
    sp = S + KEY_PAD
    ncp = S // CMP_STRIDE
    c0 = (Q_TILE // CMP_STRIDE) * (nq - 1)
    per_g = lambda a: pl.BlockSpec((1,) + a.shape[1:], lambda b, g, i: (g,) + (0,) * (a.ndim - 1))
    val = pl.BlockSpec((1, 1, sp // LANES, LANES, LANES), lambda b, g, i: (b, g, 0, 0, 0))
    return pl.pallas_call(
        functools.partial(_attn_kernel, c0=c0),
        out_shape=jax.ShapeDtypeStruct((B, G, nq, HEAD_DIM, ROWS), BF16),
        grid=(B, G, nq),
        in_specs=[pl.BlockSpec((1, 1, 1, HEAD_DIM, ROWS), lambda b, g, i: (b, g, i, 0, 0)),
                  per_g(qc),
                  pl.BlockSpec((1, 1, ncp, LANES), lambda b, g, i: (b, g, 0, 0)),
                  pl.BlockSpec((1, 1, LANES, ncp), lambda b, g, i: (b, g, 0, 0)),
                  pl.BlockSpec((1, sp, 2 * LANES), lambda b, g, i: (b, 0, g)),
                  val,
                  pl.BlockSpec((1, sp, LANES), lambda b, g, i: (b, 0, g)),
                  val,
                  pl.BlockSpec((1, Q_TILE, LANES), lambda b, g, i: (b, i, g)),
                  per_g(tn), per_g(tw), per_g(tct),
                  pl.BlockSpec((LANES, ncp), lambda b, g, i: (0, 0))],
        out_specs=pl.BlockSpec((1, 1, 1, HEAD_DIM, ROWS), lambda b, g, i: (b, g, i, 0, 0)),
        scratch_shapes=[pltpu.VMEM((2 * LANES, ROWS), BF16), pltpu.VMEM((LANES, ROWS), F32)],
        compiler_params=_cparams(("arbitrary", "arbitrary", "arbitrary")),
        name="nsa_attention",
    )(qt, qc, kc, vct, ks, vs, kw, vw, gt, tn, tw, tct, ovt)


def _post_attn_kernel(x_ref, ypc_ref, yn_ref, wo_ref, g1_ref, n2_ref, sc_ref, sh_ref, rwh_ref, rwl_ref, rb_ref,
                      xo_ref, h2_ref, ei_ref, tw_ref, cnt_ref, run_ref):
    first = (pl.program_id(0) == 0) & (pl.program_id(1) == 0)

    @pl.when(first)
    def _():
        run_ref[...] = jnp.zeros_like(run_ref)

    tm = x_ref.shape[1]
    half = wo_ref.shape[0] // 2
    mixed = (jnp.dot(ypc_ref[0], wo_ref[0:half, :], preferred_element_type=F32)
             + jnp.dot(yn_ref[0], wo_ref[half:, :], preferred_element_type=F32))
    x = x_ref[0] + g1_ref[0] * mixed
    xo_ref[0] = x
    y = x * lax.rsqrt(jnp.mean(x * x, axis=-1, keepdims=True) + EPS) * n2_ref[...]
    h2 = y * (1.0 + sc_ref[0]) + sh_ref[0]
    _rows_to_tiles(h2_ref, h2)
    hh = h2.astype(BF16)
    hl = (h2 - hh.astype(F32)).astype(BF16)
    logit = (jnp.dot(hh, rwh_ref[...], preferred_element_type=F32)
             + jnp.dot(hl, rwh_ref[...], preferred_element_type=F32)
             + jnp.dot(hh, rwl_ref[...], preferred_element_type=F32)) + rb_ref[...]
    lane = lax.broadcasted_iota(I32, (tm, LANES), 1)
    vals, hots, idxs = [], [], []
    for _ in range(TOP_K):
        mx = jnp.max(logit, axis=1, keepdims=True)
        idx = jnp.min(jnp.where(logit == mx, lane, LANES), axis=1, keepdims=True)
        hot = lane == idx
        vals.append(mx)
        hots.append(hot)
        idxs.append(idx)
        logit = jnp.where(hot, -jnp.inf, logit)
    ex = [jnp.exp(v - vals[0]) for v in vals]
    inv = 1.0 / (ex[0] + ex[1] + ex[2] + ex[3])
    assign = (hots[0] | hots[1] | hots[2] | hots[3]).astype(BF16)
    r = lax.broadcasted_iota(I32, (tm, tm), 0)
    c = lax.broadcasted_iota(I32, (tm, tm), 1)
    before = jnp.dot((c < r).astype(BF16), assign, preferred_element_type=F32) + run_ref[...]
    ei = jnp.zeros((tm, LANES), I32)
    tw = jnp.zeros((tm, LANES), F32)
    for k in range(TOP_K):
        e_k = idxs[k]
        r_k = jnp.sum(jnp.where(hots[k], before, 0.0), axis=1, keepdims=True).astype(I32)
        ei = jnp.where(lane == k, e_k, jnp.where(lane == TOP_K + k, r_k, ei))
        tw = jnp.where(lane == k, ex[k] * inv, tw)
    ei_ref[0] = ei
    tw_ref[0] = tw
    run_ref[...] = run_ref[...] + jnp.sum(assign.astype(F32), axis=0, keepdims=True)
    cnt_ref[...] = run_ref[...]


def _post_attn(x, ypc, ynsa, w_out, g1, n2g, sc2, sh2, rw_hi, rw_lo, rb):
    B, S, D = x.shape
    tm = TOK_TILE
    tok = lambda w: pl.BlockSpec((1, tm, w), lambda b, i: (b, i, 0))
    per_b = pl.BlockSpec((1, 1, D), lambda b, i: (b, 0, 0))
    full = lambda a: pl.BlockSpec(a.shape, lambda b, i: (0,) * a.ndim)
    return pl.pallas_call(
        _post_attn_kernel,
        out_shape=[jax.ShapeDtypeStruct((B, S, D), F32), jax.ShapeDtypeStruct((B * S * ROW_TILES, LANES), F32),
                   jax.ShapeDtypeStruct((B, S, LANES), I32), jax.ShapeDtypeStruct((B, S, LANES), F32),
                   jax.ShapeDtypeStruct((1, LANES), F32)],
        grid=(B, S // tm),
        in_specs=[tok(D), tok(ypc.shape[-1]), tok(ynsa.shape[-1]), full(w_out), per_b,
                  pl.BlockSpec((1, D), lambda b, i: (0, 0)), per_b, per_b,
                  full(rw_hi), full(rw_lo), full(rb)],
        out_specs=[tok(D), pl.BlockSpec((tm * ROW_TILES, LANES), lambda b, i: (b * (S // tm) + i, 0)),
                   tok(LANES), tok(LANES), pl.BlockSpec((1, LANES), lambda b, i: (0, 0))],
        scratch_shapes=[pltpu.VMEM((1, LANES), F32)],
        compiler_params=_cparams(("arbitrary", "arbitrary")),
        name="post_attn_router",
    )(x, ypc, ynsa, w_out, g1, n2g.reshape(1, D), sc2, sh2, rw_hi, rw_lo, rb)


ROW_TILES = 8
DMA_UNROLL = 2


def _rows_to_tiles(ref, val):
    n = val.shape[0]
    for s in range(ROW_TILES):
        ref[pl.ds(s, n, stride=ROW_TILES), :] = val[:, s * LANES:(s + 1) * LANES]


def _tiles_to_rows(ref, n):
    return jnp.concatenate([ref[pl.ds(s, n, stride=ROW_TILES), :] for s in range(ROW_TILES)], axis=1)


def _tile_at(ref, i):
    return ref.at[pl.ds(pl.multiple_of(i * ROW_TILES, ROW_TILES), ROW_TILES), :]


def _dispatch_kernel(fill0_ref, filln_ref, dest_ref, h_ref, xs_ref, zero_ref, sem, zsem):
    tm = h_ref.shape[0] // ROW_TILES

    @pl.when(pl.program_id(0) == 0)
    def _():
        zero_ref[...] = jnp.zeros_like(zero_ref)

        def expert(e, c):
            def copy(s):
                return pltpu.make_async_copy(zero_ref, _tile_at(xs_ref, fill0_ref[e] + s), zsem)

            lax.fori_loop(0, filln_ref[e], lambda s, c2: (copy(s).start(), c2)[1], 0)
            lax.fori_loop(0, filln_ref[e], lambda s, c2: (copy(s).wait(), c2)[1], 0)
            return c

        lax.fori_loop(0, N_EXPERTS, expert, 0)

    def body(i, c):
        for u in range(DMA_UNROLL):
            t = i * DMA_UNROLL + u
            src = _tile_at(h_ref, t)
            for k in range(TOP_K):
                pltpu.make_async_copy(src, _tile_at(xs_ref, dest_ref[0, 0, t * TOP_K + k]), sem).start()
        return c

    lax.fori_loop(0, tm // DMA_UNROLL, body, 0)
    for _ in range(TOP_K):
        pltpu.make_async_copy(h_ref, xs_ref.at[pl.ds(0, tm * ROW_TILES), :], sem).wait()


def _dispatch(dest, h2t, fill0, filln, n_slots):
    tm = TOK_TILE
    nt = h2t.shape[0] // (tm * ROW_TILES)
    return pl.pallas_call(
        _dispatch_kernel,
        out_shape=jax.ShapeDtypeStruct((n_slots * ROW_TILES, LANES), F32),
        grid_spec=pltpu.PrefetchScalarGridSpec(
            num_scalar_prefetch=2,
            grid=(nt,),
            in_specs=[pl.BlockSpec((1, 1, tm * TOP_K), lambda i, f0, fn: (i, 0, 0), memory_space=pltpu.SMEM),
                      pl.BlockSpec((tm * ROW_TILES, LANES), lambda i, f0, fn: (i, 0))],
            out_specs=pl.BlockSpec(memory_space=pl.ANY),
            scratch_shapes=[pltpu.VMEM((ROW_TILES, LANES), F32), pltpu.SemaphoreType.DMA(()),
                            pltpu.SemaphoreType.DMA(())]),
        compiler_params=_cparams(("arbitrary",)),
        name="moe_dispatch",
    )(fill0, filln, dest.reshape(nt, 1, tm * TOP_K), h2t)


W_CHUNK = 512


def _expert_kernel(be_ref, nu_ref, x_ref, wgu_ref, bgu_ref, wd_ref, bd_ref, y_ref, wgu_s, wd_s):
    i = pl.program_id(0)

    @pl.when(i < nu_ref[0])
    def _():
        @pl.when((i == 0) | (be_ref[i] != be_ref[jnp.maximum(i - 1, 0)]))
        def _():
            for c in range(0, wgu_s.shape[1], W_CHUNK):
                wgu_s[:, c:c + W_CHUNK] = wgu_ref[0, :, c:c + W_CHUNK].astype(BF16)
            for c in range(0, wd_s.shape[1], W_CHUNK):
                wd_s[:, c:c + W_CHUNK] = wd_ref[0, :, c:c + W_CHUNK].astype(BF16)

        F = wd_s.shape[0]
        x = _tiles_to_rows(x_ref, EXPERT_BLOCK).astype(BF16)
        gu = jnp.dot(x, wgu_s[...], preferred_element_type=F32) + bgu_ref[0]
        gate = jnp.minimum(gu[:, :F], SWIGLU_LIMIT)
        up = jnp.clip(gu[:, F:], -SWIGLU_LIMIT, SWIGLU_LIMIT)
        act = (up + 1.0) * gate * jax.nn.sigmoid(SWIGLU_ALPHA * gate)
        _rows_to_tiles(y_ref, jnp.dot(act.astype(BF16), wd_s[...], preferred_element_type=F32) + bd_ref[0])


def _experts(blk_e, n_used, xs, layer, w_gu, b_gu, w_down, b_down):
    L, E, D, F2 = w_gu.shape
    F = F2 // 2
    rows = EXPERT_BLOCK * ROW_TILES
    nb = xs.shape[0] // rows
    blk = lambda i, be, nu: (jnp.minimum(i, nu[0] - 1), 0)
    per_e = lambda i, be, nu: (layer, be[i], 0, 0)
    return pl.pallas_call(
        _expert_kernel,
        out_shape=jax.ShapeDtypeStruct(xs.shape, F32),
        grid_spec=pltpu.PrefetchScalarGridSpec(
            num_scalar_prefetch=2,
            grid=(nb,),
            in_specs=[pl.BlockSpec((rows, LANES), blk),
                      pl.BlockSpec((None, 1, D, F2), per_e),
                      pl.BlockSpec((None, 1, 1, F2), per_e),
                      pl.BlockSpec((None, 1, F, D), per_e),
                      pl.BlockSpec((None, 1, 1, D), per_e)],
            out_specs=pl.BlockSpec((rows, LANES), blk),
            scratch_shapes=[pltpu.VMEM((D, F2), BF16), pltpu.VMEM((F, D), BF16)]),
        compiler_params=_cparams(("arbitrary",)),
        name="moe_experts",
    )(blk_e, n_used, xs, w_gu, b_gu.reshape(L, E, 1, F2), w_down, b_down.reshape(L, E, 1, D))


def _combine_kernel(dest_ref, y_ref, x_ref, tw_ref, g2_ref, fg_ref, o_ref, rows_ref, sem, *, final):
    tm = x_ref.shape[1]

    def body(i, c):
        for u in range(DMA_UNROLL):
            t = i * DMA_UNROLL + u
            for k in range(TOP_K):
                pltpu.make_async_copy(_tile_at(y_ref, dest_ref[0, 0, t * TOP_K + k]),
                                      _tile_at(rows_ref.at[k], t), sem).start()
        return c

    lax.fori_loop(0, tm // DMA_UNROLL, body, 0)
    for k in range(TOP_K):
        pltpu.make_async_copy(y_ref.at[pl.ds(0, tm * ROW_TILES), :], rows_ref.at[k], sem).wait()
    tw = tw_ref[0]
    moe = tw[:, 0:1] * _tiles_to_rows(rows_ref.at[0], tm)
    for k in range(1, TOP_K):
        moe = moe + tw[:, k:k + 1] * _tiles_to_rows(rows_ref.at[k], tm)
    x = x_ref[0] + g2_ref[0] * moe
    if final:
        x = x * lax.rsqrt(jnp.mean(x * x, axis=-1, keepdims=True) + EPS) * fg_ref[...]
    o_ref[0] = x


def _combine(dest, y, x, tw, g2, final_g, final):
    B, S, D = x.shape
    tm = TOK_TILE
    nt = S // tm
    tok = lambda w: pl.BlockSpec((1, tm, w), lambda b, i: (b, i, 0))
    return pl.pallas_call(
        functools.partial(_combine_kernel, final=final),
        out_shape=jax.ShapeDtypeStruct((B, S, D), F32),
        grid=(B, nt),
        in_specs=[pl.BlockSpec((1, 1, tm * TOP_K), lambda b, i: (b * nt + i, 0, 0), memory_space=pltpu.SMEM),
                  pl.BlockSpec(memory_space=pl.ANY),
                  tok(D), tok(LANES),
                  pl.BlockSpec((1, 1, D), lambda b, i: (b, 0, 0)),
                  pl.BlockSpec((1, D), lambda b, i: (0, 0))],
        out_specs=tok(D),
        scratch_shapes=[pltpu.VMEM((TOP_K, tm * ROW_TILES, LANES), F32), pltpu.SemaphoreType.DMA(())],
        compiler_params=_cparams(("arbitrary", "arbitrary")),
        name="moe_combine",
    )(dest.reshape(B * nt, 1, tm * TOP_K), y, x, tw, g2, final_g.reshape(1, D))


def _moe(x, h2, ei, tw, counts, g2, layer, w_gu, b_gu, w_down, b_down, final_g, final):
    B, S, D = x.shape
    N = B * S
    n_slots = -(-(N * TOP_K + N_EXPERTS * EXPERT_BLOCK) // EXPERT_BLOCK) * EXPERT_BLOCK
    nb = n_slots // EXPERT_BLOCK
    cnt = counts[0, :N_EXPERTS].astype(I32)
    padded = (cnt + EXPERT_BLOCK - 1) // EXPERT_BLOCK * EXPERT_BLOCK
    pend = jnp.cumsum(padded)
    pstart = pend - padded
    ei2 = ei.reshape(N, LANES)
    dest = (pstart[ei2[:, 0:TOP_K]] + ei2[:, TOP_K:2 * TOP_K]).reshape(N * TOP_K)
    blk_start = jnp.arange(nb, dtype=I32) * EXPERT_BLOCK
    blk_e = jnp.minimum(jnp.sum((pend[None, :] <= blk_start[:, None]).astype(I32), axis=1), N_EXPERTS - 1)
    n_used = (pend[-1:] // EXPERT_BLOCK).astype(I32)
    xs = _dispatch(dest, h2, pstart + cnt, padded - cnt, n_slots)
    y = _experts(blk_e, n_used, xs, layer, w_gu, b_gu, w_down, b_down)
    return _combine(dest, y, x, tw, g2, final_g, final)


def kernel(x, c, w_mod, b_mod, norm1_g, norm2_g, w_in, w_out, pool_w, pool_scale, conv_w, conv_b, conv_ln_g,
           conv_ln_b, cmp_pe_k, cmp_pe_v, cmp_w1_k, cmp_w2_k, cmp_w1_v, cmp_w2_v, rel_bias, router_w, router_b,
           expert_w_gu, expert_b_gu, expert_w_down, expert_b_down, final_g):
    B, S, D = x.shape
    L = w_mod.shape[0]
    assert S % TOK_TILE == 0 and S // SEL_BLOCK <= LANES and D == ROW_TILES * LANES
    mod = _modulation(c, w_mod, b_mod)
    nq = S // Q_TILE
    t_near, t_win, t_cmp, qc = _bias_tables(rel_bias, S)
    ovt = _overlap_t(S)
    w_big = _in_weight(w_in)
    cg = POOL_DIM // POOL_GROUPS
    for l in range(L):
        m6 = mod[l].reshape(B, 6, 1, D)
        sh1, sc1, g1, sh2, sc2, g2 = (m6[:, k] for k in range(6))
        upc, q, kvc, ks, vs, kw, vw, gt = _in_proj(x, norm1_g[l], sc1, sh1, w_big[l])
        pw_bd = jnp.zeros((POOL_DIM, POOL_DIM), F32)
        for g in range(POOL_GROUPS):
            pw_bd = lax.dynamic_update_slice(pw_bd, pool_w[l, g], (g * cg, g * cg))
        cw = jnp.pad(conv_w[l], ((0, 1), (0, 0)))
        ypc = _pool_conv(upc, pw_bd.astype(BF16), pool_scale[l], cw, conv_b[l], conv_ln_g[l], conv_ln_b[l])
        kc, vct = _compress(kvc, _cmp_weights(cmp_w1_k[l], cmp_w1_v[l]), cmp_pe_k[l], cmp_pe_v[l],
                            cmp_w1_k[l], cmp_w1_v[l], cmp_w2_k[l], cmp_w2_v[l])
        qt = (q.reshape(B, nq, Q_TILE, KV_GROUPS, HPG, HEAD_DIM).transpose(0, 3, 1, 5, 4, 2)
              .reshape(B, KV_GROUPS, nq, HEAD_DIM, ROWS))
        ot = _attention(qt, qc, kc, vct, _pad_keys(ks, 2 * LANES), _value_tiles(vs), _pad_keys(kw, LANES),
                        _value_tiles(vw), gt, t_near, t_win, t_cmp, ovt)
        ynsa = (ot.reshape(B, KV_GROUPS, nq, HEAD_DIM, HPG, Q_TILE).transpose(0, 2, 5, 1, 4, 3)
                .reshape(B, S, NSA_DIM))
        rw = jnp.pad(router_w[l], ((0, 0), (0, LANES - N_EXPERTS)))
        rw_hi = rw.astype(BF16)
        rw_lo = (rw - rw_hi.astype(F32)).astype(BF16)
        rb = jnp.pad(router_b[l].reshape(1, -1), ((0, 0), (0, LANES - N_EXPERTS)), constant_values=NEG_INF)
        x, h2, ei, tw, counts = _post_attn(x, ypc, ynsa, w_out[l].astype(BF16), g1, norm2_g[l], sc2, sh2,
                                           rw_hi, rw_lo, rb)
        x = _moe(x, h2, ei, tw, counts, g2, l, expert_w_gu, expert_b_gu, expert_w_down, expert_b_down,
                 final_g, final=(l == L - 1))
    return x
```

```python
import functools
import math

import jax
import jax.numpy as jnp
import numpy as np
from jax import lax
from jax.experimental import pallas as pl
from jax.experimental.pallas import tpu as pltpu

F32 = jnp.float32
BF16 = jnp.bfloat16
I32 = jnp.int32

HEAD_DIM = 64
POOL_DIM = 256
POOL_GROUPS = 4
POOL_WINDOWS = (2, 4, 8, 16)
CONV_DIM = 256
CONV_WIDTH = 31
NSA_DIM = 512
NSA_HEADS = 8
KV_GROUPS = 2
HPG = 4
CMP_LEN = 32
CMP_STRIDE = 16
CMP_HIDDEN = 128
SEL_BLOCK = 64
SEL_TOPN = 16
N_FORCED = 3
WINDOW = 512
Q_TILE = 128
N_BUCKETS = 32
MAX_DISTANCE = 1024
N_EXPERTS = 32
TOP_K = 4
SWIGLU_ALPHA = 1.702
SWIGLU_LIMIT = 7.0
EPS = 1e-5
NEG_INF = -1e30
FORCE_SCORE = 1e4

LANES = 128
ROWS = HPG * Q_TILE
LOG2E = 1.4426950408889634
FAR_HEADROOM = 100.0
NEAR_KEYS = 1024
WIN_KEYS = WINDOW + Q_TILE
FAR_TILE = 512
FAR_GROUP = 4
KEY_PAD = FAR_GROUP * FAR_TILE
MASK_BIAS = -32768.0
TOK_TILE = 512
EXPERT_BLOCK = 512
VMEM_LIMIT = 56 * 1024 * 1024


def _cparams(sem, vmem=VMEM_LIMIT):
    return pltpu.CompilerParams(dimension_semantics=sem, vmem_limit_bytes=vmem)


def _t5_bucket(n):
    n = jnp.maximum(n, 0)
    max_exact = N_BUCKETS // 2
    nf = jnp.maximum(n, 1).astype(F32)
    large = max_exact + (jnp.log(nf / max_exact) / math.log(MAX_DISTANCE / max_exact)
                         * (N_BUCKETS - max_exact)).astype(I32)
    large = jnp.minimum(large, N_BUCKETS - 1)
    return jnp.where(n < max_exact, n, large)


def _mod_kernel(c_ref, w_ref, b_ref, o_ref):
    c = c_ref[...]
    cond = c * jax.nn.sigmoid(c)
    o_ref[0] = jnp.dot(cond.astype(BF16), w_ref[0].astype(BF16),
                       preferred_element_type=F32) + b_ref[0]


def _modulation(c, w_mod, b_mod):
    L, D, W = w_mod.shape
    B = c.shape[0]
    tn = 1536
    return pl.pallas_call(
        _mod_kernel,
        out_shape=jax.ShapeDtypeStruct((L, B, W), F32),
        grid=(L, W // tn),
        in_specs=[pl.BlockSpec((B, D), lambda l, j: (0, 0)),
                  pl.BlockSpec((1, D, tn), lambda l, j: (l, 0, j)),
                  pl.BlockSpec((1, 1, tn), lambda l, j: (l, 0, j))],
        out_specs=pl.BlockSpec((1, B, tn), lambda l, j: (l, 0, j)),
        compiler_params=_cparams(("arbitrary", "arbitrary")),
        name="modulation",
    )(c, w_mod, b_mod.reshape(L, 1, W))


C_UPC, C_Q, C_KVC, C_KS, C_VS, C_KW, C_VW, C_GT, C_END = 0, 768, 1280, 1536, 2048, 2304, 2560, 2816, 3072


def _in_weight(w_in):
    col = lambda a, n: w_in[:, :, a:a + n]
    zero = lambda n: jnp.zeros(w_in.shape[:2] + (n,), w_in.dtype)
    parts = [col(0, 768), col(768, 512) * (HEAD_DIM ** -0.5 * LOG2E), col(1280, 256)]
    for g in range(KV_GROUPS):
        parts += [col(1536 + g * HEAD_DIM, HEAD_DIM), zero(2 * LANES - HEAD_DIM)]
    for base in (1664, 1792, 1920):
        for g in range(KV_GROUPS):
            parts += [col(base + g * HEAD_DIM, HEAD_DIM), zero(LANES - HEAD_DIM)]
    for g in range(KV_GROUPS):
        parts += [col(2048 + br * NSA_HEADS + g * HPG, HPG) for br in range(3)] + [zero(LANES - 3 * HPG)]
    return jnp.concatenate(parts, axis=2).astype(BF16)


def _inproj_kernel(x_ref, g_ref, sc_ref, sh_ref, w_ref,
                   upc_ref, q_ref, kvc_ref, ks_ref, vs_ref, kw_ref, vw_ref, gt_ref):
    ti = pl.program_id(1)
    x = x_ref[0]
    tm = x.shape[0]
    y = x * lax.rsqrt(jnp.mean(x * x, axis=-1, keepdims=True) + EPS) * g_ref[...]
    h = y * (1.0 + sc_ref[0]) + sh_ref[0]
    z = jnp.dot(h.astype(BF16), w_ref[...], preferred_element_type=F32)
    upc_ref[0] = z[:, C_UPC:C_Q]
    q_ref[0] = z[:, C_Q:C_KVC].astype(BF16)
    kvc_ref[0] = z[:, C_KVC:C_KS].astype(BF16)
    lane = lax.broadcasted_iota(I32, (tm, 256), 1)
    row = lax.broadcasted_iota(I32, (tm, 256), 0)
    blk = (ti * tm + row) // SEL_BLOCK
    onehot = ((lane - LANES == blk) | (lane == HEAD_DIM + 1) | (lane == HEAD_DIM + 2)).astype(F32)
    for g in range(KV_GROUPS):
        ks_ref[0, :, g * 256:(g + 1) * 256] = (z[:, C_KS + g * 256:C_KS + (g + 1) * 256] + onehot).astype(BF16)
    lane = lax.broadcasted_iota(I32, (tm, 256), 1)
    ones_col = ((lane % LANES) == HEAD_DIM).astype(F32)
    vs_ref[0] = (z[:, C_VS:C_KW] + ones_col).astype(BF16)
    kw_ref[0] = z[:, C_KW:C_VW].astype(BF16)
    vw_ref[0] = (z[:, C_VW:C_GT] + ones_col).astype(BF16)
    gt_ref[0] = jax.nn.sigmoid(z[:, C_GT:C_END])


def _in_proj(x, g1, sc, sh, w_big):
    B, S, D = x.shape
    tm = TOK_TILE
    tok = lambda w: pl.BlockSpec((1, tm, w), lambda b, i: (b, i, 0))
    outs = [(768, F32), (512, BF16), (256, BF16), (512, BF16), (256, BF16), (256, BF16), (256, BF16), (256, F32)]
    return pl.pallas_call(
        _inproj_kernel,
        out_shape=[jax.ShapeDtypeStruct((B, S, w), dt) for w, dt in outs],
        grid=(B, S // tm),
        in_specs=[tok(D),
                  pl.BlockSpec((1, D), lambda b, i: (0, 0)),
                  pl.BlockSpec((1, 1, D), lambda b, i: (b, 0, 0)),
                  pl.BlockSpec((1, 1, D), lambda b, i: (b, 0, 0)),
                  pl.BlockSpec((D, C_END), lambda b, i: (0, 0))],
        out_specs=[tok(w) for w, _ in outs],
        compiler_params=_cparams(("arbitrary", "arbitrary")),
        name="in_proj",
    )(x, g1.reshape(1, D), sc, sh, w_big)


HALO = 32


def _poolconv_kernel(cur_ref, halo_ref, pw_ref, ps_ref, cw_ref, cb_ref, lg_ref, lb_ref, o_ref, ext_ref, v_ref):
    ti = pl.program_id(1)
    ts = cur_ref.shape[1]
    halo = halo_ref[0] * (ti > 0).astype(F32)
    ext_ref[0:HALO, :] = halo
    ext_ref[HALO:HALO + ts, :] = cur_ref[0]
    u = ext_ref[HALO:HALO + ts, 0:POOL_DIM]
    lane = lax.broadcasted_iota(I32, (ts, POOL_DIM), 1)
    grp = lane // (POOL_DIM // POOL_GROUPS)
    run = u
    pooled = jnp.zeros_like(u)
    for k in range(1, POOL_WINDOWS[-1]):
        run = run + ext_ref[HALO - k:HALO - k + ts, 0:POOL_DIM]
        if (k + 1) in POOL_WINDOWS:
            pooled = jnp.where(grp == POOL_WINDOWS.index(k + 1), run, pooled)
    wlane = jnp.where(grp == 0, 2.0, jnp.where(grp == 1, 4.0, jnp.where(grp == 2, 8.0, 16.0)))
    t1 = (ti * ts + lax.broadcasted_iota(I32, (ts, POOL_DIM), 0) + 1).astype(F32)
    cnt = jnp.minimum(t1, wlane)
    pooled = pooled / cnt - u
    y_pool = jnp.dot(pooled.astype(BF16), pw_ref[...], preferred_element_type=F32) * ps_ref[...]
    o_ref[0, :, 0:POOL_DIM] = y_pool.astype(BF16)
    uv = ext_ref[:, POOL_DIM:POOL_DIM + CONV_DIM]
    ug = ext_ref[:, POOL_DIM + CONV_DIM:POOL_DIM + 2 * CONV_DIM]
    v_ref[...] = uv * jax.nn.sigmoid(ug)
    acc = jnp.zeros((ts, CONV_DIM), F32) + cb_ref[...]
    for k in range(CONV_WIDTH):
        o = HALO - (CONV_WIDTH - 1) + k
        acc = acc + v_ref[o:o + ts, :] * cw_ref[k:k + 1, :]
    mu = jnp.mean(acc, axis=-1, keepdims=True)
    d = acc - mu
    var = jnp.mean(d * d, axis=-1, keepdims=True)
    yn = d * lax.rsqrt(var + EPS) * lg_ref[...] + lb_ref[...]
    o_ref[0, :, POOL_DIM:POOL_DIM + CONV_DIM] = (yn * jax.nn.sigmoid(yn)).astype(BF16)


def _pool_conv(upc, pool_w_bd, pool_scale, conv_w, conv_b, ln_g, ln_b):
    B, S, W = upc.shape
    ts = TOK_TILE
    r = ts // HALO
    vec = lambda n: pl.BlockSpec((1, n), lambda b, i: (0, 0))
    return pl.pallas_call(
        _poolconv_kernel,
        out_shape=jax.ShapeDtypeStruct((B, S, POOL_DIM + CONV_DIM), BF16),
        grid=(B, S // ts),
        in_specs=[pl.BlockSpec((1, ts, W), lambda b, i: (b, i, 0)),
                  pl.BlockSpec((1, HALO, W), lambda b, i: (b, jnp.maximum(i * r - 1, 0), 0)),
                  pl.BlockSpec((POOL_DIM, POOL_DIM), lambda b, i: (0, 0)),
                  vec(POOL_DIM),
                  pl.BlockSpec((CONV_WIDTH + 1, CONV_DIM), lambda b, i: (0, 0)),
                  vec(CONV_DIM), vec(CONV_DIM), vec(CONV_DIM)],
        out_specs=pl.BlockSpec((1, ts, POOL_DIM + CONV_DIM), lambda b, i: (b, i, 0)),
        scratch_shapes=[pltpu.VMEM((HALO + ts, W), F32), pltpu.VMEM((HALO + ts, CONV_DIM), F32)],
        compiler_params=_cparams(("arbitrary", "arbitrary")),
        name="pool_conv",
    )(upc, upc, pool_w_bd, pool_scale.reshape(1, -1), conv_w, conv_b.reshape(1, -1),
      ln_g.reshape(1, -1), ln_b.reshape(1, -1))


N_STREAM = 2 * KV_GROUPS
CHUNK_W = CMP_STRIDE * 2 * KV_GROUPS * HEAD_DIM


def _cmp_weights(w1_k, w1_v):
    half = CMP_STRIDE * HEAD_DIM
    cols = []
    for s in range(N_STREAM):
        w1 = w1_k if s < KV_GROUPS else w1_v
        for part in range(2):
            blk = w1[part * half:(part + 1) * half].reshape(CMP_STRIDE, 1, HEAD_DIM, CMP_HIDDEN)
            z = jnp.zeros((CMP_STRIDE, N_STREAM, HEAD_DIM, CMP_HIDDEN), w1.dtype)
            z = lax.dynamic_update_slice(z, blk, (0, s, 0, 0))
            cols.append(z.reshape(CHUNK_W, CMP_HIDDEN))
    return jnp.concatenate(cols, axis=1).astype(BF16)


def _gelu_tanh(x):
    return 0.5 * x * (1.0 + jnp.tanh(math.sqrt(2.0 / math.pi) * (x + 0.044715 * (x * x * x))))


def _compress_kernel(c_ref, w_ref, pek_ref, pev_ref, w1k_ref, w1v_ref, w2k_ref, w2vt_ref, kc_ref, vct_ref):
    r = jnp.dot(c_ref[0], w_ref[...], preferred_element_type=F32)
    ncp = r.shape[0]
    pe_k = jnp.dot(pek_ref[...], w1k_ref[...], preferred_element_type=F32)[0:1]
    pe_v = jnp.dot(pev_ref[...], w1v_ref[...], preferred_element_type=F32)[0:1]
    for s in range(N_STREAM):
        a = r[:, s * 256:s * 256 + CMP_HIDDEN]
        b = r[:, s * 256 + CMP_HIDDEN:(s + 1) * 256]
        hid = a + pltpu.roll(b, ncp - 1, 0) + (pe_k if s < KV_GROUPS else pe_v)
        act = _gelu_tanh(hid).astype(BF16)
        if s < KV_GROUPS:
            kc_ref[0, s] = jnp.dot(act, w2k_ref[...], preferred_element_type=F32).astype(BF16)
        else:
            vct_ref[0, s - KV_GROUPS] = lax.dot_general(
                w2vt_ref[...], act, (((1,), (1,)), ((), ())), preferred_element_type=F32).astype(BF16)


def _compress(kvc, wcmp, pe_k, pe_v, w1_k, w1_v, w2_k, w2_v):
    B, S, _ = kvc.shape
    ncp = S // CMP_STRIDE
    chunks = kvc.reshape(B, ncp, CHUNK_W)
    pe8 = lambda pe: jnp.broadcast_to(pe.reshape(1, -1), (8, CMP_LEN * HEAD_DIM)).astype(BF16)
    w2k = jnp.pad(w2_k, ((0, 0), (0, LANES - HEAD_DIM))).astype(BF16)
    w2vt = jnp.pad(w2_v.T, ((0, LANES - HEAD_DIM), (0, 0))).astype(BF16)
    full = lambda a: pl.BlockSpec(a.shape, lambda b: (0,) * a.ndim)
    args = (wcmp, pe8(pe_k), pe8(pe_v), w1_k.astype(BF16), w1_v.astype(BF16), w2k, w2vt)
    return pl.pallas_call(
        _compress_kernel,
        out_shape=[jax.ShapeDtypeStruct((B, KV_GROUPS, ncp, LANES), BF16),
                   jax.ShapeDtypeStruct((B, KV_GROUPS, LANES, ncp), BF16)],
        grid=(B,),
        in_specs=[pl.BlockSpec((1, ncp, CHUNK_W), lambda b: (b, 0, 0))] + [full(a) for a in args],
        out_specs=[pl.BlockSpec((1, KV_GROUPS, ncp, LANES), lambda b: (b, 0, 0, 0)),
                   pl.BlockSpec((1, KV_GROUPS, LANES, ncp), lambda b: (b, 0, 0, 0))],
        compiler_params=_cparams(("arbitrary",)),
        name="compress",
    )(chunks, *args)


def _bias_tables(rel_bias, S):
    nq = S // Q_TILE
    ncp = S // CMP_STRIDE
    rb = rel_bias.reshape(N_BUCKETS, KV_GROUPS, HPG).transpose(1, 2, 0) * LOG2E
    far = rb[:, :, N_BUCKETS - 1]
    far_hi = far.astype(BF16)
    far_lo = (far - far_hi.astype(F32)).astype(BF16)
    far_sum = far_hi.astype(F32) + far_lo.astype(F32)
    i = np.arange(Q_TILE)[None, :]

    def table(d, valid, sub):
        onehot = jax.nn.one_hot(_t5_bucket(jnp.asarray(d, I32)), N_BUCKETS, dtype=F32)
        t = jnp.einsum('rib,ghb->grhi', onehot, rb, precision=lax.Precision.HIGHEST)
        t = jnp.where(jnp.asarray(valid)[None, :, None, :], t - sub[:, None, :, None], NEG_INF)
        return t.reshape(KV_GROUPS, d.shape[0], ROWS)

    d = i - np.arange(NEAR_KEYS)[:, None] + (NEAR_KEYS - Q_TILE)
    t_near = table(d, d >= 0, far_sum)
    d = i - np.arange(WIN_KEYS)[:, None] + (WIN_KEYS - Q_TILE)
    t_win = table(d, (d >= 0) & (d < WINDOW), jnp.zeros_like(far_sum))
    c0 = (Q_TILE // CMP_STRIDE) * (nq - 1)
    d = i - CMP_STRIDE * (np.arange(c0 + ncp)[:, None] - c0) - (CMP_LEN - 1)
    t_cmp = table(d, d >= 0, jnp.zeros_like(far_sum))
    rows = jnp.zeros((KV_GROUPS, HEAD_DIM, HPG, Q_TILE), F32)
    rows = rows.at[:, 0].set(1.0)
    rows = rows.at[:, 1].set(jnp.broadcast_to(far_hi.astype(F32)[:, :, None], (KV_GROUPS, HPG, Q_TILE)))
    rows = rows.at[:, 2].set(jnp.broadcast_to(far_lo.astype(F32)[:, :, None], (KV_GROUPS, HPG, Q_TILE)))
    return t_near, t_win, t_cmp, rows.reshape(KV_GROUPS, HEAD_DIM, ROWS).astype(BF16)


def _overlap_t(S):
    ncp = S // CMP_STRIDE
    n = np.arange(ncp)[None, :]
    jb = np.arange(LANES)[:, None]
    end = n * CMP_STRIDE + CMP_LEN - 1
    start = n * CMP_STRIDE
    ov = (end >= jb * SEL_BLOCK) & (start < (jb + 1) * SEL_BLOCK) & (n < ncp - 1)
    return jnp.asarray(ov.astype(np.float32), BF16)


def _pad_keys(k, group_w):
    B, _, W = k.shape
    marker = (np.arange(W) % group_w == HEAD_DIM).astype(np.float32) * MASK_BIAS
    pad = jnp.broadcast_to(jnp.asarray(marker, k.dtype), (B, KEY_PAD, W))
    return jnp.concatenate([pad, k], axis=1)


def _value_tiles(v):
    B, S, _ = v.shape
    v = jnp.pad(v, ((0, 0), (KEY_PAD, 0), (0, 0)))
    return v.reshape(B, (S + KEY_PAD) // LANES, LANES, KV_GROUPS, LANES).transpose(0, 3, 1, 4, 2)


def _key_tiles(ref, k0, n):
    t0 = k0 // LANES
    return jnp.concatenate([ref[0, 0, t0 + u] for u in range(n)], axis=1)


def _attn_kernel(qt_ref, qc_ref, kc_ref, vct_ref, ks_ref, vs_ref, kw_ref, vw_ref, gt_ref, tn_ref, tw_ref, tct_ref,
                 ovt_ref, o_ref, qa_ref, acc_ref, *, c0):
    qi = pl.program_id(2)
    qt = qt_ref[0, 0, 0]
    ncp = kc_ref.shape[2]

    r0 = pl.multiple_of(c0 - (Q_TILE // CMP_STRIDE) * qi, 8)
    st = (jnp.dot(kc_ref[0, 0][:, 0:HEAD_DIM], qt, preferred_element_type=F32)
          + tct_ref[0, pl.ds(r0, ncp), :])
    mc = jnp.maximum(jnp.max(st, axis=0, keepdims=True), -1e20)
    pc = jnp.exp2(st - mc)
    pc = pc * (1.0 / jnp.maximum(jnp.sum(pc, axis=0, keepdims=True), 1e-30))
    o_c = jnp.dot(vct_ref[0, 0], pc.astype(BF16), preferred_element_type=F32)[0:HEAD_DIM]

    ps = pc[:, 0:Q_TILE]
    for h in range(1, HPG):
        ps = ps + pc[:, h * Q_TILE:(h + 1) * Q_TILE]
    ps_hi = ps.astype(BF16)
    ps_lo = (ps - ps_hi.astype(F32)).astype(BF16)
    imp = (jnp.dot(ovt_ref[...], ps_hi, preferred_element_type=F32)
           + jnp.dot(ovt_ref[...], ps_lo, preferred_element_type=F32))

    def flash_step(carry, s, vt):
        m, acc = carry
        m_new = jnp.maximum(m, jnp.max(s, axis=0, keepdims=True))
        alpha = jnp.exp2(m - m_new)
        p = jnp.exp2(s - m_new)
        acc = acc * alpha + jnp.dot(vt, p.astype(BF16), preferred_element_type=F32)
        return m_new, acc

    def normalised(acc):
        return acc[0:HEAD_DIM] * (1.0 / acc[HEAD_DIM:HEAD_DIM + 1])

    k1 =Q_TILE * qi + (KEY_PAD + Q_TILE)

    qa_ref[0:HEAD_DIM, :] = qt
    qa_ref[HEAD_DIM:LANES, :] = qc_ref[0]

    w0 = pl.multiple_of(k1 - WIN_KEYS, LANES)
    s = jnp.dot(kw_ref[0, pl.ds(w0, WIN_KEYS), :], qa_ref[0:LANES, :], preferred_element_type=F32) + tw_ref[0]
    p = jnp.exp2(s - jnp.max(s, axis=0, keepdims=True)).astype(BF16)
    o_w = normalised(jnp.dot(_key_tiles(vw_ref, w0, WIN_KEYS // LANES), p, preferred_element_type=F32))

    jb = lax.broadcasted_iota(I32, (LANES, Q_TILE), 0)
    ii = lax.broadcasted_iota(I32, (LANES, Q_TILE), 1)
    cur = (Q_TILE // SEL_BLOCK) * qi + (ii >= SEL_BLOCK).astype(I32)
    sel = (jb == 0) | (jb == cur) | (jb == cur - 1)
    score = jnp.where(jb <= cur, jnp.where(sel, -jnp.inf, imp), NEG_INF)
    for _ in range(SEL_TOPN - N_FORCED):
        mx = jnp.max(score, axis=0, keepdims=True)
        first = jnp.min(jnp.where(score == mx, jb, LANES), axis=0, keepdims=True)
        pick = jb == first
        sel = sel | pick
        score = jnp.where(pick, -jnp.inf, score)
    mbt = jnp.where(sel, 0.0, MASK_BIAS).astype(BF16)
    for h in range(HPG):
        qa_ref[LANES:2 * LANES, h * Q_TILE:(h + 1) * Q_TILE] = mbt

    n0 = pl.multiple_of(k1 - NEAR_KEYS, LANES)
    half = NEAR_KEYS // 2
    sa = (jnp.dot(ks_ref[0, pl.ds(n0, half), :], qa_ref[...], preferred_element_type=F32)
          + tn_ref[0, 0:half, :])
    sb = (jnp.dot(ks_ref[0, pl.ds(n0 + half, half), :], qa_ref[...], preferred_element_type=F32)
          + tn_ref[0, half:NEAR_KEYS, :])
    m_near = jnp.maximum(jnp.max(sa, axis=0, keepdims=True), jnp.max(sb, axis=0, keepdims=True))
    pa = jnp.exp2(sa - m_near).astype(BF16)
    da = jnp.dot(_key_tiles(vs_ref, n0, half // LANES), pa, preferred_element_type=F32)
    pb = jnp.exp2(sb - m_near).astype(BF16)
    acc_near = da + jnp.dot(_key_tiles(vs_ref, n0 + half, half // LANES), pb, preferred_element_type=F32)
    n_pairs = (jnp.maximum(n0 - KEY_PAD, 0) + KEY_PAD - 1) // KEY_PAD

    def far_scores(j):
        k0 = pl.multiple_of(n0 - FAR_TILE * (j + 1), LANES)
        return jnp.dot(ks_ref[0, pl.ds(k0, FAR_TILE), :], qa_ref[...], preferred_element_type=F32)

    def far_values(j):
        return _key_tiles(vs_ref, pl.multiple_of(n0 - FAR_TILE * (j + 1), LANES), FAR_TILE // LANES)

    def fast_body(jj, top):
        tiles = [FAR_GROUP * jj + u for u in range(FAR_GROUP)]
        scores = [far_scores(tiles[0]), far_scores(tiles[1])]
        acc = None
        for u, j in enumerate(tiles):
            if u + 2 < FAR_GROUP:
                scores.append(far_scores(tiles[u + 2]))
            p = jnp.exp2(scores[u] - m_near).astype(BF16)
            d = jnp.dot(far_values(j), p, preferred_element_type=F32)
            acc = d if acc is None else acc + d
            top = jnp.maximum(top, jnp.max(scores[u], axis=0, keepdims=True))
        acc_ref[...] += acc
        return top

    acc_ref[...] = acc_near
    top = lax.fori_loop(0, n_pairs, fast_body, m_near)

    @pl.when(jnp.max(top - m_near) > FAR_HEADROOM)
    def _():
        def safe_body(j, carry):
            return flash_step(carry, far_scores(j), far_values(j))

        acc_ref[...] = lax.fori_loop(0, FAR_GROUP * n_pairs, safe_body, (m_near, acc_near))[1]

    o_s = normalised(acc_ref[...])

    gtt = gt_ref[0].T
    gate = lambda br: jnp.concatenate([gtt[br * HPG + h:br * HPG + h + 1, :] for h in range(HPG)], axis=1)
    o_ref[0, 0, 0] = (gate(0) * o_c + gate(1) * o_s + gate(2) * o_w).astype(BF16)


def _attention(qt, qc, kc, vct, ks, vs, kw, vw, gt, tn, tw, tct, ovt):
    B, G, nq, _, _ = qt.shape
    S = nq * Q_TILE
    sp = S + KEY_PAD
    ncp = S // CMP_STRIDE
    c0 = (Q_TILE // CMP_STRIDE) * (nq - 1)
    per_g = lambda a: pl.BlockSpec((1,) + a.shape[1:], lambda b, g, i: (g,) + (0,) * (a.ndim - 1))
    val = pl.BlockSpec((1, 1, sp // LANES, LANES, LANES), lambda b, g, i: (b, g, 0, 0, 0))
    return pl.pallas_call(
        functools.partial(_attn_kernel, c0=c0),
        out_shape=jax.ShapeDtypeStruct((B, G, nq, HEAD_DIM, ROWS), BF16),
        grid=(B, G, nq),
        in_specs=[pl.BlockSpec((1, 1, 1, HEAD_DIM, ROWS), lambda b, g, i: (b, g, i, 0, 0)),
                  per_g(qc),
                  pl.BlockSpec((1, 1, ncp, LANES), lambda b, g, i: (b, g, 0, 0)),
                  pl.BlockSpec((1, 1, LANES, ncp), lambda b, g, i: (b, g, 0, 0)),
                  pl.BlockSpec((1, sp, 2 * LANES), lambda b, g, i: (b, 0, g)),
                  val,
                  pl.BlockSpec((1, sp, LANES), lambda b, g, i: (b, 0, g)),
                  val,
                  pl.BlockSpec((1, Q_TILE, LANES), lambda b, g, i: (b, i, g)),
                  per_g(tn), per_g(tw), per_g(tct),
                  pl.BlockSpec((LANES, ncp), lambda b, g, i: (0, 0))],
        out_specs=pl.BlockSpec((1, 1, 1, HEAD_DIM, ROWS), lambda b, g, i: (b, g, i, 0, 0)),
        scratch_shapes=[pltpu.VMEM((2 * LANES, ROWS), BF16), pltpu.VMEM((LANES, ROWS), F32)],
        compiler_params=_cparams(("arbitrary", "arbitrary", "arbitrary")),
        name="nsa_attention",
    )(qt, qc, kc, vct, ks, vs, kw, vw, gt, tn, tw, tct, ovt)


def _post_attn_kernel(x_ref, ypc_ref, yn_ref, wo_ref, g1_ref, n2_ref, sc_ref, sh_ref, rwh_ref, rwl_ref, rb_ref,
                      xo_ref, h2_ref, ei_ref, tw_ref, cnt_ref, run_ref):
    first = (pl.program_id(0) == 0) & (pl.program_id(1) == 0)

    @pl.when(first)
    def _():
        run_ref[...] = jnp.zeros_like(run_ref)

    tm = x_ref.shape[1]
    half = wo_ref.shape[0] // 2
    mixed = (jnp.dot(ypc_ref[0], wo_ref[0:half, :], preferred_element_type=F32)
             + jnp.dot(yn_ref[0], wo_ref[half:, :], preferred_element_type=F32))
    x = x_ref[0] + g1_ref[0] * mixed
    xo_ref[0] = x
    y = x * lax.rsqrt(jnp.mean(x * x, axis=-1, keepdims=True) + EPS) * n2_ref[...]
    h2 = y * (1.0 + sc_ref[0]) + sh_ref[0]
    _rows_to_tiles(h2_ref, h2)
    hh = h2.astype(BF16)
    hl = (h2 - hh.astype(F32)).astype(BF16)
    logit = (jnp.dot(hh, rwh_ref[...], preferred_element_type=F32)
             + jnp.dot(hl, rwh_ref[...], preferred_element_type=F32)
             + jnp.dot(hh, rwl_ref[...], preferred_element_type=F32)) + rb_ref[...]
    lane = lax.broadcasted_iota(I32, (tm, LANES), 1)
    vals, hots, idxs = [], [], []
    for _ in range(TOP_K):
        mx = jnp.max(logit, axis=1, keepdims=True)
        idx = jnp.min(jnp.where(logit == mx, lane, LANES), axis=1, keepdims=True)
        hot = lane == idx
        vals.append(mx)
        hots.append(hot)
        idxs.append(idx)
        logit = jnp.where(hot, -jnp.inf, logit)
    ex = [jnp.exp(v - vals[0]) for v in vals]
    inv = 1.0 / (ex[0] + ex[1] + ex[2] + ex[3])
    assign = (hots[0] | hots[1] | hots[2] | hots[3]).astype(BF16)
    r = lax.broadcasted_iota(I32, (tm, tm), 0)
    c = lax.broadcasted_iota(I32, (tm, tm), 1)
    before = jnp.dot((c < r).astype(BF16), assign, preferred_element_type=F32) + run_ref[...]
    ei = jnp.zeros((tm, LANES), I32)
    tw = jnp.zeros((tm, LANES), F32)
    for k in range(TOP_K):
        e_k = idxs[k]
        r_k = jnp.sum(jnp.where(hots[k], before, 0.0), axis=1, keepdims=True).astype(I32)
        ei = jnp.where(lane == k, e_k, jnp.where(lane == TOP_K + k, r_k, ei))
        tw = jnp.where(lane == k, ex[k] * inv, tw)
    ei_ref[0] = ei
    tw_ref[0] = tw
    run_ref[...] = run_ref[...] + jnp.sum(assign.astype(F32), axis=0, keepdims=True)
    cnt_ref[...] = run_ref[...]


def _post_attn(x, ypc, ynsa, w_out, g1, n2g, sc2, sh2, rw_hi, rw_lo, rb):
    B, S, D = x.shape
    tm = TOK_TILE
    tok = lambda w: pl.BlockSpec((1, tm, w), lambda b, i: (b, i, 0))
    per_b = pl.BlockSpec((1, 1, D), lambda b, i: (b, 0, 0))
    full = lambda a: pl.BlockSpec(a.shape, lambda b, i: (0,) * a.ndim)
    return pl.pallas_call(
        _post_attn_kernel,
        out_shape=[jax.ShapeDtypeStruct((B, S, D), F32), jax.ShapeDtypeStruct((B * S * ROW_TILES, LANES), F32),
                   jax.ShapeDtypeStruct((B, S, LANES), I32), jax.ShapeDtypeStruct((B, S, LANES), F32),
                   jax.ShapeDtypeStruct((1, LANES), F32)],
        grid=(B, S // tm),
        in_specs=[tok(D), tok(ypc.shape[-1]), tok(ynsa.shape[-1]), full(w_out), per_b,
                  pl.BlockSpec((1, D), lambda b, i: (0, 0)), per_b, per_b,
                  full(rw_hi), full(rw_lo), full(rb)],
        out_specs=[tok(D), pl.BlockSpec((tm * ROW_TILES, LANES), lambda b, i: (b * (S // tm) + i, 0)),
                   tok(LANES), tok(LANES), pl.BlockSpec((1, LANES), lambda b, i: (0, 0))],
        scratch_shapes=[pltpu.VMEM((1, LANES), F32)],
        compiler_params=_cparams(("arbitrary", "arbitrary")),
        name="post_attn_router",
    )(x, ypc, ynsa, w_out, g1, n2g.reshape(1, D), sc2, sh2, rw_hi, rw_lo, rb)


ROW_TILES = 8
DMA_UNROLL = 2


def _rows_to_tiles(ref, val):
    n = val.shape[0]
    for s in range(ROW_TILES):
        ref[pl.ds(s, n, stride=ROW_TILES), :] = val[:, s * LANES:(s + 1) * LANES]


def _tiles_to_rows(ref, n):
    return jnp.concatenate([ref[pl.ds(s, n, stride=ROW_TILES), :] for s in range(ROW_TILES)], axis=1)


def _tile_at(ref, i):
    return ref.at[pl.ds(pl.multiple_of(i * ROW_TILES, ROW_TILES), ROW_TILES), :]


def _dispatch_kernel(fill0_ref, filln_ref, dest_ref, h_ref, xs_ref, zero_ref, sem, zsem):
    tm = h_ref.shape[0] // ROW_TILES

    @pl.when(pl.program_id(0) == 0)
    def _():
        zero_ref[...] = jnp.zeros_like(zero_ref)

        def expert(e, c):
            def copy(s):
                return pltpu.make_async_copy(zero_ref, _tile_at(xs_ref, fill0_ref[e] + s), zsem)

            lax.fori_loop(0, filln_ref[e], lambda s, c2: (copy(s).start(), c2)[1], 0)
            lax.fori_loop(0, filln_ref[e], lambda s, c2: (copy(s).wait(), c2)[1], 0)
            return c

        lax.fori_loop(0, N_EXPERTS, expert, 0)

    def body(i, c):
        for u in range(DMA_UNROLL):
            t = i * DMA_UNROLL + u
            src = _tile_at(h_ref, t)
            for k in range(TOP_K):
                pltpu.make_async_copy(src, _tile_at(xs_ref, dest_ref[0, 0, t * TOP_K + k]), sem).start(
                    priority=k % 2)
        return c

    lax.fori_loop(0, tm // DMA_UNROLL, body, 0)
    for _ in range(TOP_K):
        pltpu.make_async_copy(h_ref, xs_ref.at[pl.ds(0, tm * ROW_TILES), :], sem).wait()


def _dispatch(dest, h2t, fill0, filln, n_slots):
    tm = TOK_TILE
    nt = h2t.shape[0] // (tm * ROW_TILES)
    return pl.pallas_call(
        _dispatch_kernel,
        out_shape=jax.ShapeDtypeStruct((n_slots * ROW_TILES, LANES), F32),
        grid_spec=pltpu.PrefetchScalarGridSpec(
            num_scalar_prefetch=2,
            grid=(nt,),
            in_specs=[pl.BlockSpec((1, 1, tm * TOP_K), lambda i, f0, fn: (i, 0, 0), memory_space=pltpu.SMEM),
                      pl.BlockSpec((tm * ROW_TILES, LANES), lambda i, f0, fn: (i, 0))],
            out_specs=pl.BlockSpec(memory_space=pl.ANY),
            scratch_shapes=[pltpu.VMEM((ROW_TILES, LANES), F32), pltpu.SemaphoreType.DMA(()),
                            pltpu.SemaphoreType.DMA(())]),
        compiler_params=_cparams(("arbitrary",)),
        name="moe_dispatch",
    )(fill0, filln, dest.reshape(nt, 1, tm * TOP_K), h2t)


W_CHUNK = 512


def _expert_kernel(be_ref, nu_ref, x_ref, wgu_ref, bgu_ref, wd_ref, bd_ref, y_ref, wgu_s, wd_s):
    i = pl.program_id(0)

    @pl.when(i < nu_ref[0])
    def _():
        @pl.when((i == 0) | (be_ref[i] != be_ref[jnp.maximum(i - 1, 0)]))
        def _():
            for c in range(0, wgu_s.shape[1], W_CHUNK):
                wgu_s[:, c:c + W_CHUNK] = wgu_ref[0, :, c:c + W_CHUNK].astype(BF16)
            for c in range(0, wd_s.shape[1], W_CHUNK):
                wd_s[:, c:c + W_CHUNK] = wd_ref[0, :, c:c + W_CHUNK].astype(BF16)

        F = wd_s.shape[0]
        x = _tiles_to_rows(x_ref, EXPERT_BLOCK).astype(BF16)
        gu = jnp.dot(x, wgu_s[...], preferred_element_type=F32) + bgu_ref[0]
        gate = jnp.minimum(gu[:, :F], SWIGLU_LIMIT)
        up = jnp.clip(gu[:, F:], -SWIGLU_LIMIT, SWIGLU_LIMIT)
        act = (up + 1.0) * gate * jax.nn.sigmoid(SWIGLU_ALPHA * gate)
        _rows_to_tiles(y_ref, jnp.dot(act.astype(BF16), wd_s[...], preferred_element_type=F32) + bd_ref[0])


def _experts(blk_e, n_used, xs, layer, w_gu, b_gu, w_down, b_down):
    L, E, D, F2 = w_gu.shape
    F = F2 // 2
    rows = EXPERT_BLOCK * ROW_TILES
    nb = xs.shape[0] // rows
    blk = lambda i, be, nu: (jnp.minimum(i, nu[0] - 1), 0)
    per_e = lambda i, be, nu: (layer, be[i], 0, 0)
    return pl.pallas_call(
        _expert_kernel,
        out_shape=jax.ShapeDtypeStruct(xs.shape, F32),
        grid_spec=pltpu.PrefetchScalarGridSpec(
            num_scalar_prefetch=2,
            grid=(nb,),
            in_specs=[pl.BlockSpec((rows, LANES), blk),
                      pl.BlockSpec((None, 1, D, F2), per_e),
                      pl.BlockSpec((None, 1, 1, F2), per_e),
                      pl.BlockSpec((None, 1, F, D), per_e),
                      pl.BlockSpec((None, 1, 1, D), per_e)],
            out_specs=pl.BlockSpec((rows, LANES), blk),
            scratch_shapes=[pltpu.VMEM((D, F2), BF16), pltpu.VMEM((F, D), BF16)]),
        compiler_params=_cparams(("arbitrary",)),
        name="moe_experts",
    )(blk_e, n_used, xs, w_gu, b_gu.reshape(L, E, 1, F2), w_down, b_down.reshape(L, E, 1, D))


def _combine_kernel(dest_ref, y_ref, x_ref, tw_ref, g2_ref, fg_ref, o_ref, rows_ref, sem, *, final):
    tm = x_ref.shape[1]

    def body(i, c):
        for u in range(DMA_UNROLL):
            t = i * DMA_UNROLL + u
            for k in range(TOP_K):
                pltpu.make_async_copy(_tile_at(y_ref, dest_ref[0, 0, t * TOP_K + k]),
                                      _tile_at(rows_ref.at[k], t), sem).start(priority=k % 2)
        return c

    lax.fori_loop(0, tm // DMA_UNROLL, body, 0)
    for k in range(TOP_K):
        pltpu.make_async_copy(y_ref.at[pl.ds(0, tm * ROW_TILES), :], rows_ref.at[k], sem).wait()
    tw = tw_ref[0]
    moe = tw[:, 0:1] * _tiles_to_rows(rows_ref.at[0], tm)
    for k in range(1, TOP_K):
        moe = moe + tw[:, k:k + 1] * _tiles_to_rows(rows_ref.at[k], tm)
    x = x_ref[0] + g2_ref[0] * moe
    if final:
        x = x * lax.rsqrt(jnp.mean(x * x, axis=-1, keepdims=True) + EPS) * fg_ref[...]
    o_ref[0] = x


def _combine(dest, y, x, tw, g2, final_g, final):
    B, S, D = x.shape
    tm = TOK_TILE
    nt = S // tm
    tok = lambda w: pl.BlockSpec((1, tm, w), lambda b, i: (b, i, 0))
    return pl.pallas_call(
        functools.partial(_combine_kernel, final=final),
        out_shape=jax.ShapeDtypeStruct((B, S, D), F32),
        grid=(B, nt),
        in_specs=[pl.BlockSpec((1, 1, tm * TOP_K), lambda b, i: (b * nt + i, 0, 0), memory_space=pltpu.SMEM),
                  pl.BlockSpec(memory_space=pl.ANY),
                  tok(D), tok(LANES),
                  pl.BlockSpec((1, 1, D), lambda b, i: (b, 0, 0)),
                  pl.BlockSpec((1, D), lambda b, i: (0, 0))],
        out_specs=tok(D),
        scratch_shapes=[pltpu.VMEM((TOP_K, tm * ROW_TILES, LANES), F32), pltpu.SemaphoreType.DMA(())],
        compiler_params=_cparams(("arbitrary", "arbitrary")),
        name="moe_combine",
    )(dest.reshape(B * nt, 1, tm * TOP_K), y, x, tw, g2, final_g.reshape(1, D))


def _moe(x, h2, ei, tw, counts, g2, layer, w_gu, b_gu, w_down, b_down, final_g, final):
    B, S, D = x.shape
    N = B * S
    n_slots = -(-(N * TOP_K + N_EXPERTS * EXPERT_BLOCK) // EXPERT_BLOCK) * EXPERT_BLOCK
    nb = n_slots // EXPERT_BLOCK
    cnt = counts[0, :N_EXPERTS].astype(I32)
    padded = (cnt + EXPERT_BLOCK - 1) // EXPERT_BLOCK * EXPERT_BLOCK
    pend = jnp.cumsum(padded)
    pstart = pend - padded
    ei2 = ei.reshape(N, LANES)
    dest = (pstart[ei2[:, 0:TOP_K]] + ei2[:, TOP_K:2 * TOP_K]).reshape(N * TOP_K)
    blk_start = jnp.arange(nb, dtype=I32) * EXPERT_BLOCK
    blk_e = jnp.minimum(jnp.sum((pend[None, :] <= blk_start[:, None]).astype(I32), axis=1), N_EXPERTS - 1)
    n_used = (pend[-1:] // EXPERT_BLOCK).astype(I32)
    xs = _dispatch(dest, h2, pstart + cnt, padded - cnt, n_slots)
    y = _experts(blk_e, n_used, xs, layer, w_gu, b_gu, w_down, b_down)
    return _combine(dest, y, x, tw, g2, final_g, final)


def kernel(x, c, w_mod, b_mod, norm1_g, norm2_g, w_in, w_out, pool_w, pool_scale, conv_w, conv_b, conv_ln_g,
           conv_ln_b, cmp_pe_k, cmp_pe_v, cmp_w1_k, cmp_w2_k, cmp_w1_v, cmp_w2_v, rel_bias, router_w, router_b,
           expert_w_gu, expert_b_gu, expert_w_down, expert_b_down, final_g):
    B, S, D = x.shape
    L = w_mod.shape[0]
    assert S % TOK_TILE == 0 and S // SEL_BLOCK <= LANES and D == ROW_TILES * LANES
    mod = _modulation(c, w_mod, b_mod)
    nq = S // Q_TILE
    t_near, t_win, t_cmp, qc = _bias_tables(rel_bias, S)
    ovt = _overlap_t(S)
    w_big = _in_weight(w_in)
    cg = POOL_DIM // POOL_GROUPS
    for l in range(L):
        m6 = mod[l].reshape(B, 6, 1, D)
        sh1, sc1, g1, sh2, sc2, g2 = (m6[:, k] for k in range(6))
        upc, q, kvc, ks, vs, kw, vw, gt = _in_proj(x, norm1_g[l], sc1, sh1, w_big[l])
        pw_bd = jnp.zeros((POOL_DIM, POOL_DIM), F32)
        for g in range(POOL_GROUPS):
            pw_bd = lax.dynamic_update_slice(pw_bd, pool_w[l, g], (g * cg, g * cg))
        cw = jnp.pad(conv_w[l], ((0, 1), (0, 0)))
        ypc = _pool_conv(upc, pw_bd.astype(BF16), pool_scale[l], cw, conv_b[l], conv_ln_g[l], conv_ln_b[l])
        kc, vct = _compress(kvc, _cmp_weights(cmp_w1_k[l], cmp_w1_v[l]), cmp_pe_k[l], cmp_pe_v[l],
                            cmp_w1_k[l], cmp_w1_v[l], cmp_w2_k[l], cmp_w2_v[l])
        qt = (q.reshape(B, nq, Q_TILE, KV_GROUPS, HPG, HEAD_DIM).transpose(0, 3, 1, 5, 4, 2)
              .reshape(B, KV_GROUPS, nq, HEAD_DIM, ROWS))
        ot = _attention(qt, qc, kc, vct, _pad_keys(ks, 2 * LANES), _value_tiles(vs), _pad_keys(kw, LANES),
                        _value_tiles(vw), gt, t_near, t_win, t_cmp, ovt)
        ynsa = (ot.reshape(B, KV_GROUPS, nq, HEAD_DIM, HPG, Q_TILE).transpose(0, 2, 5, 1, 4, 3)
                .reshape(B, S, NSA_DIM))
        rw = jnp.pad(router_w[l], ((0, 0), (0, LANES - N_EXPERTS)))
        rw_hi = rw.astype(BF16)
        rw_lo = (rw - rw_hi.astype(F32)).astype(BF16)
        rb = jnp.pad(router_b[l].reshape(1, -1), ((0, 0), (0, LANES - N_EXPERTS)), constant_values=NEG_INF)
        x, h2, ei, tw, counts = _post_attn(x, ypc, ynsa, w_out[l].astype(BF16), g1, norm2_g[l], sc2, sh2,
                                           rw_hi, rw_lo, rb)
        x = _moe(x, h2, ei, tw, counts, g2, l, expert_w_gu, expert_b_gu, expert_w_down, expert_b_down,
                 final_g, final=(l == L - 1))
    return x
```

```python
import functools
import math

import jax
import jax.numpy as jnp
import numpy as np
from jax import lax
from jax.experimental import pallas as pl
from jax.experimental.pallas import tpu as pltpu

F32 = jnp.float32
BF16 = jnp.bfloat16
I32 = jnp.int32

HEAD_DIM = 64
POOL_DIM = 256
POOL_GROUPS = 4
POOL_WINDOWS = (2, 4, 8, 16)
CONV_DIM = 256
CONV_WIDTH = 31
NSA_DIM = 512
NSA_HEADS = 8
KV_GROUPS = 2
HPG = 4
CMP_LEN = 32
CMP_STRIDE = 16
CMP_HIDDEN = 128
SEL_BLOCK = 64
SEL_TOPN = 16
N_FORCED = 3
WINDOW = 512
Q_TILE = 128
N_BUCKETS = 32
MAX_DISTANCE = 1024
N_EXPERTS = 32
TOP_K = 4
SWIGLU_ALPHA = 1.702
SWIGLU_LIMIT = 7.0
EPS = 1e-5
NEG_INF = -1e30
FORCE_SCORE = 1e4

LANES = 128
ROWS = HPG * Q_TILE
LOG2E = 1.4426950408889634
FAR_HEADROOM = 100.0
NEAR_KEYS = 1024
WIN_KEYS = WINDOW + Q_TILE
FAR_TILE = 512
FAR_GROUP = 4
KEY_PAD = FAR_GROUP * FAR_TILE
MASK_BIAS = -32768.0
TOK_TILE = 512
EXPERT_BLOCK = 512
VMEM_LIMIT = 56 * 1024 * 1024


def _cparams(sem, vmem=VMEM_LIMIT):
    return pltpu.CompilerParams(dimension_semantics=sem, vmem_limit_bytes=vmem)


def _t5_bucket(n):
    n = jnp.maximum(n, 0)
    max_exact = N_BUCKETS // 2
    nf = jnp.maximum(n, 1).astype(F32)
    large = max_exact + (jnp.log(nf / max_exact) / math.log(MAX_DISTANCE / max_exact)
                         * (N_BUCKETS - max_exact)).astype(I32)
    large = jnp.minimum(large, N_BUCKETS - 1)
    return jnp.where(n < max_exact, n, large)


def _mod_kernel(c_ref, w_ref, b_ref, o_ref):
    c = c_ref[...]
    cond = c * jax.nn.sigmoid(c)
    o_ref[0] = jnp.dot(cond.astype(BF16), w_ref[0].astype(BF16),
                       preferred_element_type=F32) + b_ref[0]


def _modulation(c, w_mod, b_mod):
    L, D, W = w_mod.shape
    B = c.shape[0]
    tn = 1536
    return pl.pallas_call(
        _mod_kernel,
        out_shape=jax.ShapeDtypeStruct((L, B, W), F32),
        grid=(L, W // tn),
        in_specs=[pl.BlockSpec((B, D), lambda l, j: (0, 0)),
                  pl.BlockSpec((1, D, tn), lambda l, j: (l, 0, j)),
                  pl.BlockSpec((1, 1, tn), lambda l, j: (l, 0, j))],
        out_specs=pl.BlockSpec((1, B, tn), lambda l, j: (l, 0, j)),
        compiler_params=_cparams(("arbitrary", "arbitrary")),
        name="modulation",
    )(c, w_mod, b_mod.reshape(L, 1, W))


C_UPC, C_Q, C_KVC, C_KS, C_VS, C_KW, C_VW, C_GT, C_END = 0, 768, 1280, 1536, 2048, 2304, 2560, 2816, 3072


def _in_weight(w_in):
    col = lambda a, n: w_in[:, :, a:a + n]
    zero = lambda n: jnp.zeros(w_in.shape[:2] + (n,), w_in.dtype)
    parts = [col(0, 768), col(768, 512) * (HEAD_DIM ** -0.5 * LOG2E), col(1280, 256)]
    for g in range(KV_GROUPS):
        parts += [col(1536 + g * HEAD_DIM, HEAD_DIM), zero(2 * LANES - HEAD_DIM)]
    for base in (1664, 1792, 1920):
        for g in range(KV_GROUPS):
            parts += [col(base + g * HEAD_DIM, HEAD_DIM), zero(LANES - HEAD_DIM)]
    for g in range(KV_GROUPS):
        parts += [col(2048 + br * NSA_HEADS + g * HPG, HPG) for br in range(3)] + [zero(LANES - 3 * HPG)]
    return jnp.concatenate(parts, axis=2).astype(BF16)


PAD_STEPS = KEY_PAD // TOK_TILE
TILES = TOK_TILE // Q_TILE


def _inproj_kernel(x_ref, g_ref, sc_ref, sh_ref, w_ref,
                   upc_ref, qt_ref, kvc_ref, ks_ref, vs_ref, kw_ref, vw_ref, gt_ref):
    step = pl.program_id(1)
    tm = x_ref.shape[1]

    @pl.when(step < PAD_STEPS)
    def _():
        lane = lax.broadcasted_iota(I32, (tm, 4 * LANES), 1)
        ks_ref[0] = jnp.where(lane % (2 * LANES) == HEAD_DIM, MASK_BIAS, 0.0).astype(BF16)
        lane = lax.broadcasted_iota(I32, (tm, 2 * LANES), 1)
        kw_ref[0] = jnp.where(lane % LANES == HEAD_DIM, MASK_BIAS, 0.0).astype(BF16)
        for g in range(KV_GROUPS):
            for t in range(TILES):
                vs_ref[0, g, t] = jnp.zeros((LANES, LANES), BF16)
                vw_ref[0, g, t] = jnp.zeros((LANES, LANES), BF16)

    @pl.when(step >= PAD_STEPS)
    def _():
        ti = step - PAD_STEPS
        x = x_ref[0]
        y = x * lax.rsqrt(jnp.mean(x * x, axis=-1, keepdims=True) + EPS) * g_ref[...]
        h = y * (1.0 + sc_ref[0]) + sh_ref[0]
        z = jnp.dot(h.astype(BF16), w_ref[...], preferred_element_type=F32)
        upc_ref[0] = z[:, C_UPC:C_Q]
        kvc_ref[0] = z[:, C_KVC:C_KS].astype(BF16)
        gt_ref[0] = jax.nn.sigmoid(z[:, C_GT:C_END])
        zq = z[:, C_Q:C_KVC].T
        for g in range(KV_GROUPS):
            for t in range(TILES):
                qt_ref[0, g, t] = jnp.concatenate(
                    [zq[(g * HPG + hh) * HEAD_DIM:(g * HPG + hh + 1) * HEAD_DIM, t * Q_TILE:(t + 1) * Q_TILE]
                     for hh in range(HPG)], axis=1).astype(BF16)
        lane = lax.broadcasted_iota(I32, (tm, 2 * LANES), 1)
        row = lax.broadcasted_iota(I32, (tm, 2 * LANES), 0)
        blk = (ti * tm + row) // SEL_BLOCK
        onehot = ((lane - LANES == blk) | (lane == HEAD_DIM + 1) | (lane == HEAD_DIM + 2)).astype(F32)
        for g in range(KV_GROUPS):
            ks_ref[0, :, g * 256:(g + 1) * 256] = (z[:, C_KS + g * 256:C_KS + (g + 1) * 256] + onehot).astype(BF16)
        kw_ref[0] = z[:, C_KW:C_VW].astype(BF16)
        ones_col = ((lane % LANES) == HEAD_DIM).astype(F32)
        for v_ref, c in ((vs_ref, C_VS), (vw_ref, C_VW)):
            zv = (z[:, c:c + 2 * LANES] + ones_col).T
            for g in range(KV_GROUPS):
                for t in range(TILES):
                    v_ref[0, g, t] = zv[g * LANES:(g + 1) * LANES, t * Q_TILE:(t + 1) * Q_TILE].astype(BF16)


def _in_proj(x, g1, sc, sh, w_big):
    B, S, D = x.shape
    tm = TOK_TILE
    sp = S + KEY_PAD
    nq = S // Q_TILE
    cur = lambda i: jnp.maximum(i - PAD_STEPS, 0)
    tok = lambda w: pl.BlockSpec((1, tm, w), lambda b, i: (b, cur(i), 0))
    key = lambda w: pl.BlockSpec((1, tm, w), lambda b, i: (b, i, 0))
    val = pl.BlockSpec((1, KV_GROUPS, TILES, LANES, LANES), lambda b, i: (b, 0, i, 0, 0))
    val_shape = jax.ShapeDtypeStruct((B, KV_GROUPS, sp // LANES, LANES, LANES), BF16)
    return pl.pallas_call(
        _inproj_kernel,
        out_shape=[jax.ShapeDtypeStruct((B, S, 768), F32),
                   jax.ShapeDtypeStruct((B, KV_GROUPS, nq, HEAD_DIM, ROWS), BF16),
                   jax.ShapeDtypeStruct((B, S, 256), BF16),
                   jax.ShapeDtypeStruct((B, sp, 4 * LANES), BF16), val_shape,
                   jax.ShapeDtypeStruct((B, sp, 2 * LANES), BF16), val_shape,
                   jax.ShapeDtypeStruct((B, S, 256), F32)],
        grid=(B, S // tm + PAD_STEPS),
        in_specs=[tok(D),
                  pl.BlockSpec((1, D), lambda b, i: (0, 0)),
                  pl.BlockSpec((1, 1, D), lambda b, i: (b, 0, 0)),
                  pl.BlockSpec((1, 1, D), lambda b, i: (b, 0, 0)),
                  pl.BlockSpec((D, C_END), lambda b, i: (0, 0))],
        out_specs=[tok(768),
                   pl.BlockSpec((1, KV_GROUPS, TILES, HEAD_DIM, ROWS), lambda b, i: (b, 0, cur(i), 0, 0)),
                   tok(256), key(4 * LANES), val, key(2 * LANES), val, tok(256)],
        compiler_params=_cparams(("arbitrary", "arbitrary")),
        name="in_proj",
    )(x, g1.reshape(1, D), sc, sh, w_big)


HALO = 32


def _poolconv_kernel(cur_ref, halo_ref, pw_ref, ps_ref, cw_ref, cb_ref, lg_ref, lb_ref, o_ref, ext_ref, v_ref):
    ti = pl.program_id(1)
    ts = cur_ref.shape[1]
    halo = halo_ref[0] * (ti > 0).astype(F32)
    ext_ref[0:HALO, :] = halo
    ext_ref[HALO:HALO + ts, :] = cur_ref[0]
    u = ext_ref[HALO:HALO + ts, 0:POOL_DIM]
    lane = lax.broadcasted_iota(I32, (ts, POOL_DIM), 1)
    grp = lane // (POOL_DIM // POOL_GROUPS)
    run = u
    pooled = jnp.zeros_like(u)
    for k in range(1, POOL_WINDOWS[-1]):
        run = run + ext_ref[HALO - k:HALO - k + ts, 0:POOL_DIM]
        if (k + 1) in POOL_WINDOWS:
            pooled = jnp.where(grp == POOL_WINDOWS.index(k + 1), run, pooled)
    wlane = jnp.where(grp == 0, 2.0, jnp.where(grp == 1, 4.0, jnp.where(grp == 2, 8.0, 16.0)))
    t1 = (ti * ts + lax.broadcasted_iota(I32, (ts, POOL_DIM), 0) + 1).astype(F32)
    cnt = jnp.minimum(t1, wlane)
    pooled = pooled / cnt - u
    y_pool = jnp.dot(pooled.astype(BF16), pw_ref[...], preferred_element_type=F32) * ps_ref[...]
    o_ref[0, :, 0:POOL_DIM] = y_pool.astype(BF16)
    uv = ext_ref[:, POOL_DIM:POOL_DIM + CONV_DIM]
    ug = ext_ref[:, POOL_DIM + CONV_DIM:POOL_DIM + 2 * CONV_DIM]
    v_ref[...] = uv * jax.nn.sigmoid(ug)
    acc = jnp.zeros((ts, CONV_DIM), F32) + cb_ref[...]
    for k in range(CONV_WIDTH):
        o = HALO - (CONV_WIDTH - 1) + k
        acc = acc + v_ref[o:o + ts, :] * cw_ref[k:k + 1, :]
    mu = jnp.mean(acc, axis=-1, keepdims=True)
    d = acc - mu
    var = jnp.mean(d * d, axis=-1, keepdims=True)
    yn = d * lax.rsqrt(var + EPS) * lg_ref[...] + lb_ref[...]
    o_ref[0, :, POOL_DIM:POOL_DIM + CONV_DIM] = (yn * jax.nn.sigmoid(yn)).astype(BF16)


def _pool_conv(upc, pool_w_bd, pool_scale, conv_w, conv_b, ln_g, ln_b):
    B, S, W = upc.shape
    ts = TOK_TILE
    r = ts // HALO
    vec = lambda n: pl.BlockSpec((1, n), lambda b, i: (0, 0))
    return pl.pallas_call(
        _poolconv_kernel,
        out_shape=jax.ShapeDtypeStruct((B, S, POOL_DIM + CONV_DIM), BF16),
        grid=(B, S // ts),
        in_specs=[pl.BlockSpec((1, ts, W), lambda b, i: (b, i, 0)),
                  pl.BlockSpec((1, HALO, W), lambda b, i: (b, jnp.maximum(i * r - 1, 0), 0)),
                  pl.BlockSpec((POOL_DIM, POOL_DIM), lambda b, i: (0, 0)),
                  vec(POOL_DIM),
                  pl.BlockSpec((CONV_WIDTH + 1, CONV_DIM), lambda b, i: (0, 0)),
                  vec(CONV_DIM), vec(CONV_DIM), vec(CONV_DIM)],
        out_specs=pl.BlockSpec((1, ts, POOL_DIM + CONV_DIM), lambda b, i: (b, i, 0)),
        scratch_shapes=[pltpu.VMEM((HALO + ts, W), F32), pltpu.VMEM((HALO + ts, CONV_DIM), F32)],
        compiler_params=_cparams(("arbitrary", "arbitrary")),
        name="pool_conv",
    )(upc, upc, pool_w_bd, pool_scale.reshape(1, -1), conv_w, conv_b.reshape(1, -1),
      ln_g.reshape(1, -1), ln_b.reshape(1, -1))


N_STREAM = 2 * KV_GROUPS
CHUNK_W = CMP_STRIDE * 2 * KV_GROUPS * HEAD_DIM


def _cmp_weights(w1_k, w1_v):
    half = CMP_STRIDE * HEAD_DIM
    cols = []
    for s in range(N_STREAM):
        w1 = w1_k if s < KV_GROUPS else w1_v
        for part in range(2):
            blk = w1[part * half:(part + 1) * half].reshape(CMP_STRIDE, 1, HEAD_DIM, CMP_HIDDEN)
            z = jnp.zeros((CMP_STRIDE, N_STREAM, HEAD_DIM, CMP_HIDDEN), w1.dtype)
            z = lax.dynamic_update_slice(z, blk, (0, s, 0, 0))
            cols.append(z.reshape(CHUNK_W, CMP_HIDDEN))
    return jnp.concatenate(cols, axis=1).astype(BF16)


def _gelu_tanh(x):
    return 0.5 * x * (1.0 + jnp.tanh(math.sqrt(2.0 / math.pi) * (x + 0.044715 * (x * x * x))))


def _compress_kernel(c_ref, w_ref, pek_ref, pev_ref, w1k_ref, w1v_ref, w2k_ref, w2vt_ref, kc_ref, vct_ref):
    r = jnp.dot(c_ref[0], w_ref[...], preferred_element_type=F32)
    ncp = r.shape[0]
    pe_k = jnp.dot(pek_ref[...], w1k_ref[...], preferred_element_type=F32)[0:1]
    pe_v = jnp.dot(pev_ref[...], w1v_ref[...], preferred_element_type=F32)[0:1]
    for s in range(N_STREAM):
        a = r[:, s * 256:s * 256 + CMP_HIDDEN]
        b = r[:, s * 256 + CMP_HIDDEN:(s + 1) * 256]
        hid = a + pltpu.roll(b, ncp - 1, 0) + (pe_k if s < KV_GROUPS else pe_v)
        act = _gelu_tanh(hid).astype(BF16)
        if s < KV_GROUPS:
            kc_ref[0, s] = jnp.dot(act, w2k_ref[...], preferred_element_type=F32).astype(BF16)
        else:
            vct_ref[0, s - KV_GROUPS] = lax.dot_general(
                w2vt_ref[...], act, (((1,), (1,)), ((), ())), preferred_element_type=F32).astype(BF16)


def _compress(kvc, wcmp, pe_k, pe_v, w1_k, w1_v, w2_k, w2_v):
    B, S, _ = kvc.shape
    ncp = S // CMP_STRIDE
    chunks = kvc.reshape(B, ncp, CHUNK_W)
    pe8 = lambda pe: jnp.broadcast_to(pe.reshape(1, -1), (8, CMP_LEN * HEAD_DIM)).astype(BF16)
    w2k = jnp.pad(w2_k, ((0, 0), (0, LANES - HEAD_DIM))).astype(BF16)
    w2vt = jnp.pad(w2_v.T, ((0, LANES - HEAD_DIM), (0, 0))).astype(BF16)
    full = lambda a: pl.BlockSpec(a.shape, lambda b: (0,) * a.ndim)
    args = (wcmp, pe8(pe_k), pe8(pe_v), w1_k.astype(BF16), w1_v.astype(BF16), w2k, w2vt)
    return pl.pallas_call(
        _compress_kernel,
        out_shape=[jax.ShapeDtypeStruct((B, KV_GROUPS, ncp, LANES), BF16),
                   jax.ShapeDtypeStruct((B, KV_GROUPS, LANES, ncp), BF16)],
        grid=(B,),
        in_specs=[pl.BlockSpec((1, ncp, CHUNK_W), lambda b: (b, 0, 0))] + [full(a) for a in args],
        out_specs=[pl.BlockSpec((1, KV_GROUPS, ncp, LANES), lambda b: (b, 0, 0, 0)),
                   pl.BlockSpec((1, KV_GROUPS, LANES, ncp), lambda b: (b, 0, 0, 0))],
        compiler_params=_cparams(("arbitrary",)),
        name="compress",
    )(chunks, *args)


def _bias_tables(rel_bias, S):
    nq = S // Q_TILE
    ncp = S // CMP_STRIDE
    rb = rel_bias.reshape(N_BUCKETS, KV_GROUPS, HPG).transpose(1, 2, 0) * LOG2E
    far = rb[:, :, N_BUCKETS - 1]
    far_hi = far.astype(BF16)
    far_lo = (far - far_hi.astype(F32)).astype(BF16)
    far_sum = far_hi.astype(F32) + far_lo.astype(F32)
    i = np.arange(Q_TILE)[None, :]

    def table(d, valid, sub):
        onehot = jax.nn.one_hot(_t5_bucket(jnp.asarray(d, I32)), N_BUCKETS, dtype=F32)
        t = jnp.einsum('rib,ghb->grhi', onehot, rb, precision=lax.Precision.HIGHEST)
        t = jnp.where(jnp.asarray(valid)[None, :, None, :], t - sub[:, None, :, None], NEG_INF)
        return t.reshape(KV_GROUPS, d.shape[0], ROWS)

    d = i - np.arange(NEAR_KEYS)[:, None] + (NEAR_KEYS - Q_TILE)
    t_near = table(d, d >= 0, far_sum)
    d = i - np.arange(WIN_KEYS)[:, None] + (WIN_KEYS - Q_TILE)
    t_win = table(d, (d >= 0) & (d < WINDOW), jnp.zeros_like(far_sum))
    c0 = (Q_TILE // CMP_STRIDE) * (nq - 1)
    d = i - CMP_STRIDE * (np.arange(c0 + ncp)[:, None] - c0) - (CMP_LEN - 1)
    t_cmp = table(d, d >= 0, jnp.zeros_like(far_sum))
    rows = jnp.zeros((KV_GROUPS, HEAD_DIM, HPG, Q_TILE), F32)
    rows = rows.at[:, 0].set(1.0)
    rows = rows.at[:, 1].set(jnp.broadcast_to(far_hi.astype(F32)[:, :, None], (KV_GROUPS, HPG, Q_TILE)))
    rows = rows.at[:, 2].set(jnp.broadcast_to(far_lo.astype(F32)[:, :, None], (KV_GROUPS, HPG, Q_TILE)))
    return t_near, t_win, t_cmp, rows.reshape(KV_GROUPS, HEAD_DIM, ROWS).astype(BF16)


def _overlap_t(S):
    ncp = S // CMP_STRIDE
    n = np.arange(ncp)[None, :]
    jb = np.arange(LANES)[:, None]
    end = n * CMP_STRIDE + CMP_LEN - 1
    start = n * CMP_STRIDE
    ov = (end >= jb * SEL_BLOCK) & (start < (jb + 1) * SEL_BLOCK) & (n < ncp - 1)
    return jnp.asarray(ov.astype(np.float32), BF16)


def _key_tiles(ref, k0, n):
    t0 = k0 // LANES
    return jnp.concatenate([ref[0, 0, t0 + u] for u in range(n)], axis=1)


def _attn_kernel(qt_ref, qc_ref, kc_ref, vct_ref, ks_ref, vs_ref, kw_ref, vw_ref, gt_ref, tn_ref, tw_ref, tct_ref,
                 ovt_ref, o_ref, qa_ref, acc_ref, *, c0):
    qi = pl.program_id(2)
    qt = qt_ref[0, 0, 0]
    ncp = kc_ref.shape[2]

    r0 = pl.multiple_of(c0 - (Q_TILE // CMP_STRIDE) * qi, 8)
    st = (jnp.dot(kc_ref[0, 0][:, 0:HEAD_DIM], qt, preferred_element_type=F32)
          + tct_ref[0, pl.ds(r0, ncp), :])
    mc = jnp.maximum(jnp.max(st, axis=0, keepdims=True), -1e20)
    pc = jnp.exp2(st - mc)
    pc = pc * (1.0 / jnp.maximum(jnp.sum(pc, axis=0, keepdims=True), 1e-30))
    o_c = jnp.dot(vct_ref[0, 0], pc.astype(BF16), preferred_element_type=F32)[0:HEAD_DIM]

    ps = pc[:, 0:Q_TILE]
    for h in range(1, HPG):
        ps = ps + pc[:, h * Q_TILE:(h + 1) * Q_TILE]
    ps_hi = ps.astype(BF16)
    ps_lo = (ps - ps_hi.astype(F32)).astype(BF16)
    imp = (jnp.dot(ovt_ref[...], ps_hi, preferred_element_type=F32)
           + jnp.dot(ovt_ref[...], ps_lo, preferred_element_type=F32))

    def flash_step(carry, s, vt):
        m, acc = carry
        m_new = jnp.maximum(m, jnp.max(s, axis=0, keepdims=True))
        alpha = jnp.exp2(m - m_new)
        p = jnp.exp2(s - m_new)
        acc = acc * alpha + jnp.dot(vt, p.astype(BF16), preferred_element_type=F32)
        return m_new, acc

    def normalised(acc):
        return acc[0:HEAD_DIM] * (1.0 / acc[HEAD_DIM:HEAD_DIM + 1])

    k1 =Q_TILE * qi + (KEY_PAD + Q_TILE)

    qa_ref[0:HEAD_DIM, :] = qt
    qa_ref[HEAD_DIM:LANES, :] = qc_ref[0]

    w0 = pl.multiple_of(k1 - WIN_KEYS, LANES)
    s = jnp.dot(kw_ref[0, pl.ds(w0, WIN_KEYS), :], qa_ref[0:LANES, :], preferred_element_type=F32) + tw_ref[0]
    p = jnp.exp2(s - jnp.max(s, axis=0, keepdims=True)).astype(BF16)
    o_w = normalised(jnp.dot(_key_tiles(vw_ref, w0, WIN_KEYS // LANES), p, preferred_element_type=F32))

    jb = lax.broadcasted_iota(I32, (LANES, Q_TILE), 0)
    ii = lax.broadcasted_iota(I32, (LANES, Q_TILE), 1)
    cur = (Q_TILE // SEL_BLOCK) * qi + (ii >= SEL_BLOCK).astype(I32)
    sel = (jb == 0) | (jb == cur) | (jb == cur - 1)
    score = jnp.where(jb <= cur, jnp.where(sel, -jnp.inf, imp), NEG_INF)
    for _ in range(SEL_TOPN - N_FORCED):
        mx = jnp.max(score, axis=0, keepdims=True)
        first = jnp.min(jnp.where(score == mx, jb, LANES), axis=0, keepdims=True)
        pick = jb == first
        sel = sel | pick
        score = jnp.where(pick, -jnp.inf, score)
    mbt = jnp.where(sel, 0.0, MASK_BIAS).astype(BF16)
    for h in range(HPG):
        qa_ref[LANES:2 * LANES, h * Q_TILE:(h + 1) * Q_TILE] = mbt

    n0 = pl.multiple_of(k1 - NEAR_KEYS, LANES)
    half = NEAR_KEYS // 2
    sa = (jnp.dot(ks_ref[0, pl.ds(n0, half), :], qa_ref[...], preferred_element_type=F32)
          + tn_ref[0, 0:half, :])
    sb = (jnp.dot(ks_ref[0, pl.ds(n0 + half, half), :], qa_ref[...], preferred_element_type=F32)
          + tn_ref[0, half:NEAR_KEYS, :])
    m_near = jnp.maximum(jnp.max(sa, axis=0, keepdims=True), jnp.max(sb, axis=0, keepdims=True))
    pa = jnp.exp2(sa - m_near).astype(BF16)
    da = jnp.dot(_key_tiles(vs_ref, n0, half // LANES), pa, preferred_element_type=F32)
    pb = jnp.exp2(sb - m_near).astype(BF16)
    acc_near = da + jnp.dot(_key_tiles(vs_ref, n0 + half, half // LANES), pb, preferred_element_type=F32)
    n_pairs = (jnp.maximum(n0 - KEY_PAD, 0) + KEY_PAD - 1) // KEY_PAD

    def far_scores(j):
        k0 = pl.multiple_of(n0 - FAR_TILE * (j + 1), LANES)
        return jnp.dot(ks_ref[0, pl.ds(k0, FAR_TILE), :], qa_ref[...], preferred_element_type=F32)

    def far_values(j):
        return _key_tiles(vs_ref, pl.multiple_of(n0 - FAR_TILE * (j + 1), LANES), FAR_TILE // LANES)

    def fast_body(jj, top):
        tiles = [FAR_GROUP * jj + u for u in range(FAR_GROUP)]
        scores = [far_scores(tiles[0]), far_scores(tiles[1])]
        acc = None
        for u, j in enumerate(tiles):
            if u + 2 < FAR_GROUP:
                scores.append(far_scores(tiles[u + 2]))
            p = jnp.exp2(scores[u] - m_near).astype(BF16)
            d = jnp.dot(far_values(j), p, preferred_element_type=F32)
            acc = d if acc is None else acc + d
            top = jnp.maximum(top, jnp.max(scores[u], axis=0, keepdims=True))
        acc_ref[...] += acc
        return top

    acc_ref[...] = acc_near
    top = lax.fori_loop(0, n_pairs, fast_body, m_near)

    @pl.when(jnp.max(top - m_near) > FAR_HEADROOM)
    def _():
        def safe_body(j, carry):
            return flash_step(carry, far_scores(j), far_values(j))

        acc_ref[...] = lax.fori_loop(0, FAR_GROUP * n_pairs, safe_body, (m_near, acc_near))[1]

    o_s = normalised(acc_ref[...])

    gtt = gt_ref[0].T
    gate = lambda br: jnp.concatenate([gtt[br * HPG + h:br * HPG + h + 1, :] for h in range(HPG)], axis=1)
    o_ref[0, 0, 0] = (gate(0) * o_c + gate(1) * o_s + gate(2) * o_w).astype(BF16)


def _attention(qt, qc, kc, vct, ks, vs, kw, vw, gt, tn, tw, tct, ovt):
    B, G, nq, _, _ = qt.shape
    S = nq * Q_TILE
    sp = S + KEY_PAD
    ncp = S // CMP_STRIDE
    c0 = (Q_TILE // CMP_STRIDE) * (nq - 1)
    per_g = lambda a: pl.BlockSpec((1,) + a.shape[1:], lambda b, g, i: (g,) + (0,) * (a.ndim - 1))
    val = pl.BlockSpec((1, 1, sp // LANES, LANES, LANES), lambda b, g, i: (b, g, 0, 0, 0))
    return pl.pallas_call(
        functools.partial(_attn_kernel, c0=c0),
        out_shape=jax.ShapeDtypeStruct((B, G, nq, HEAD_DIM, ROWS), BF16),
        grid=(B, G, nq),
        in_specs=[pl.BlockSpec((1, 1, 1, HEAD_DIM, ROWS), lambda b, g, i: (b, g, i, 0, 0)),
                  per_g(qc),
                  pl.BlockSpec((1, 1, ncp, LANES), lambda b, g, i: (b, g, 0, 0)),
                  pl.BlockSpec((1, 1, LANES, ncp), lambda b, g, i: (b, g, 0, 0)),
                  pl.BlockSpec((1, sp, 2 * LANES), lambda b, g, i: (b, 0, g)),
                  val,
                  pl.BlockSpec((1, sp, LANES), lambda b, g, i: (b, 0, g)),
                  val,
                  pl.BlockSpec((1, Q_TILE, LANES), lambda b, g, i: (b, i, g)),
                  per_g(tn), per_g(tw), per_g(tct),
                  pl.BlockSpec((LANES, ncp), lambda b, g, i: (0, 0))],
        out_specs=pl.BlockSpec((1, 1, 1, HEAD_DIM, ROWS), lambda b, g, i: (b, g, i, 0, 0)),
        scratch_shapes=[pltpu.VMEM((2 * LANES, ROWS), BF16), pltpu.VMEM((LANES, ROWS), F32)],
        compiler_params=_cparams(("arbitrary", "arbitrary", "arbitrary")),
        name="nsa_attention",
    )(qt, qc, kc, vct, ks, vs, kw, vw, gt, tn, tw, tct, ovt)


def _post_attn_kernel(x_ref, ypc_ref, yn_ref, wo_ref, g1_ref, n2_ref, sc_ref, sh_ref, rwh_ref, rwl_ref, rb_ref,
                      xo_ref, h2_ref, ei_ref, tw_ref, cnt_ref, run_ref, hd_ref):
    first = (pl.program_id(0) == 0) & (pl.program_id(1) == 0)

    @pl.when(first)
    def _():
        run_ref[...] = jnp.zeros_like(run_ref)

    tm = x_ref.shape[1]
    half = wo_ref.shape[0] // 2
    tiles = []
    for t in range(TILES):
        groups = []
        for g in range(KV_GROUPS):
            blk = yn_ref[0, g, t].astype(F32)
            for hh in range(HPG):
                hd_ref[hh * HEAD_DIM:(hh + 1) * HEAD_DIM, :] = blk[:, hh * Q_TILE:(hh + 1) * Q_TILE]
            groups.append(hd_ref[...].T)
        tiles.append(jnp.concatenate(groups, axis=1))
    ynsa = jnp.concatenate(tiles, axis=0).astype(BF16)
    mixed = (jnp.dot(ypc_ref[0], wo_ref[0:half, :], preferred_element_type=F32)
             + jnp.dot(ynsa, wo_ref[half:, :], preferred_element_type=F32))
    x = x_ref[0] + g1_ref[0] * mixed
    xo_ref[0] = x
    y = x * lax.rsqrt(jnp.mean(x * x, axis=-1, keepdims=True) + EPS) * n2_ref[...]
    h2 = y * (1.0 + sc_ref[0]) + sh_ref[0]
    _rows_to_tiles(h2_ref, h2)
    hh = h2.astype(BF16)
    hl = (h2 - hh.astype(F32)).astype(BF16)
    logit = (jnp.dot(hh, rwh_ref[...], preferred_element_type=F32)
             + jnp.dot(hl, rwh_ref[...], preferred_element_type=F32)
             + jnp.dot(hh, rwl_ref[...], preferred_element_type=F32)) + rb_ref[...]
    lane = lax.broadcasted_iota(I32, (tm, LANES), 1)
    vals, hots, idxs = [], [], []
    for _ in range(TOP_K):
        mx = jnp.max(logit, axis=1, keepdims=True)
        idx = jnp.min(jnp.where(logit == mx, lane, LANES), axis=1, keepdims=True)
        hot = lane == idx
        vals.append(mx)
        hots.append(hot)
        idxs.append(idx)
        logit = jnp.where(hot, -jnp.inf, logit)
    ex = [jnp.exp(v - vals[0]) for v in vals]
    inv = 1.0 / (ex[0] + ex[1] + ex[2] + ex[3])
    assign = (hots[0] | hots[1] | hots[2] | hots[3]).astype(BF16)
    r = lax.broadcasted_iota(I32, (tm, tm), 0)
    c = lax.broadcasted_iota(I32, (tm, tm), 1)
    before = jnp.dot((c < r).astype(BF16), assign, preferred_element_type=F32) + run_ref[...]
    ei = jnp.zeros((tm, LANES), I32)
    tw = jnp.zeros((tm, LANES), F32)
    for k in range(TOP_K):
        e_k = idxs[k]
        r_k = jnp.sum(jnp.where(hots[k], before, 0.0), axis=1, keepdims=True).astype(I32)
        ei = jnp.where(lane == k, e_k, jnp.where(lane == TOP_K + k, r_k, ei))
        tw = jnp.where(lane == k, ex[k] * inv, tw)
    ei_ref[0] = ei
    tw_ref[0] = tw
    run_ref[...] = run_ref[...] + jnp.sum(assign.astype(F32), axis=0, keepdims=True)
    cnt_ref[...] = run_ref[...]


def _post_attn(x, ypc, ynsa, w_out, g1, n2g, sc2, sh2, rw_hi, rw_lo, rb):
    B, S, D = x.shape
    tm = TOK_TILE
    tok = lambda w: pl.BlockSpec((1, tm, w), lambda b, i: (b, i, 0))
    per_b = pl.BlockSpec((1, 1, D), lambda b, i: (b, 0, 0))
    full = lambda a: pl.BlockSpec(a.shape, lambda b, i: (0,) * a.ndim)
    return pl.pallas_call(
        _post_attn_kernel,
        out_shape=[jax.ShapeDtypeStruct((B, S, D), F32), jax.ShapeDtypeStruct((B * S * ROW_TILES, LANES), F32),
                   jax.ShapeDtypeStruct((B, S, LANES), I32), jax.ShapeDtypeStruct((B, S, LANES), F32),
                   jax.ShapeDtypeStruct((1, LANES), F32)],
        grid=(B, S // tm),
        in_specs=[tok(D), tok(ypc.shape[-1]),
                  pl.BlockSpec((1, KV_GROUPS, TILES, HEAD_DIM, ROWS), lambda b, i: (b, 0, i, 0, 0)),
                  full(w_out), per_b,
                  pl.BlockSpec((1, D), lambda b, i: (0, 0)), per_b, per_b,
                  full(rw_hi), full(rw_lo), full(rb)],
        out_specs=[tok(D), pl.BlockSpec((tm * ROW_TILES, LANES), lambda b, i: (b * (S // tm) + i, 0)),
                   tok(LANES), tok(LANES), pl.BlockSpec((1, LANES), lambda b, i: (0, 0))],
        scratch_shapes=[pltpu.VMEM((1, LANES), F32), pltpu.VMEM((HPG * HEAD_DIM, Q_TILE), F32)],
        compiler_params=_cparams(("arbitrary", "arbitrary")),
        name="post_attn_router",
    )(x, ypc, ynsa, w_out, g1, n2g.reshape(1, D), sc2, sh2, rw_hi, rw_lo, rb)


ROW_TILES = 8
DMA_UNROLL = 2


def _rows_to_tiles(ref, val):
    n = val.shape[0]
    for s in range(ROW_TILES):
        ref[pl.ds(s, n, stride=ROW_TILES), :] = val[:, s * LANES:(s + 1) * LANES]


def _tiles_to_rows(ref, n):
    return jnp.concatenate([ref[pl.ds(s, n, stride=ROW_TILES), :] for s in range(ROW_TILES)], axis=1)


def _tile_at(ref, i):
    return ref.at[pl.ds(pl.multiple_of(i * ROW_TILES, ROW_TILES), ROW_TILES), :]


def _dispatch_kernel(fill0_ref, filln_ref, dest_ref, h_ref, xs_ref, zero_ref, sem, zsem):
    tm = h_ref.shape[0] // ROW_TILES

    @pl.when(pl.program_id(0) == 0)
    def _():
        zero_ref[...] = jnp.zeros_like(zero_ref)

        def expert(e, c):
            def copy(s):
                return pltpu.make_async_copy(zero_ref, _tile_at(xs_ref, fill0_ref[e] + s), zsem)

            lax.fori_loop(0, filln_ref[e], lambda s, c2: (copy(s).start(), c2)[1], 0)
            lax.fori_loop(0, filln_ref[e], lambda s, c2: (copy(s).wait(), c2)[1], 0)
            return c

        lax.fori_loop(0, N_EXPERTS, expert, 0)

    def body(i, c):
        for u in range(DMA_UNROLL):
            t = i * DMA_UNROLL + u
            src = _tile_at(h_ref, t)
            for k in range(TOP_K):
                pltpu.make_async_copy(src, _tile_at(xs_ref, dest_ref[0, 0, t * TOP_K + k]), sem).start(
                    priority=k % 2)
        return c

    lax.fori_loop(0, tm // DMA_UNROLL, body, 0)
    for _ in range(TOP_K):
        pltpu.make_async_copy(h_ref, xs_ref.at[pl.ds(0, tm * ROW_TILES), :], sem).wait()


def _dispatch(dest, h2t, fill0, filln, n_slots):
    tm = TOK_TILE
    nt = h2t.shape[0] // (tm * ROW_TILES)
    return pl.pallas_call(
        _dispatch_kernel,
        out_shape=jax.ShapeDtypeStruct((n_slots * ROW_TILES, LANES), F32),
        grid_spec=pltpu.PrefetchScalarGridSpec(
            num_scalar_prefetch=2,
            grid=(nt,),
            in_specs=[pl.BlockSpec((1, 1, tm * TOP_K), lambda i, f0, fn: (i, 0, 0), memory_space=pltpu.SMEM),
                      pl.BlockSpec((tm * ROW_TILES, LANES), lambda i, f0, fn: (i, 0))],
            out_specs=pl.BlockSpec(memory_space=pl.ANY),
            scratch_shapes=[pltpu.VMEM((ROW_TILES, LANES), F32), pltpu.SemaphoreType.DMA(()),
                            pltpu.SemaphoreType.DMA(())]),
        compiler_params=_cparams(("arbitrary",)),
        name="moe_dispatch",
    )(fill0, filln, dest.reshape(nt, 1, tm * TOP_K), h2t)


W_CHUNK = 512


def _expert_kernel(be_ref, nu_ref, x_ref, wgu_ref, bgu_ref, wd_ref, bd_ref, y_ref, wgu_s, wd_s):
    i = pl.program_id(0)

    @pl.when(i < nu_ref[0])
    def _():
        @pl.when((i == 0) | (be_ref[i] != be_ref[jnp.maximum(i - 1, 0)]))
        def _():
            for c in range(0, wgu_s.shape[1], W_CHUNK):
                wgu_s[:, c:c + W_CHUNK] = wgu_ref[0, :, c:c + W_CHUNK].astype(BF16)
            for c in range(0, wd_s.shape[1], W_CHUNK):
                wd_s[:, c:c + W_CHUNK] = wd_ref[0, :, c:c + W_CHUNK].astype(BF16)

        F = wd_s.shape[0]
        x = _tiles_to_rows(x_ref, EXPERT_BLOCK).astype(BF16)
        gu = jnp.dot(x, wgu_s[...], preferred_element_type=F32) + bgu_ref[0]
        gate = jnp.minimum(gu[:, :F], SWIGLU_LIMIT)
        up = jnp.clip(gu[:, F:], -SWIGLU_LIMIT, SWIGLU_LIMIT)
        act = (up + 1.0) * gate * jax.nn.sigmoid(SWIGLU_ALPHA * gate)
        _rows_to_tiles(y_ref, jnp.dot(act.astype(BF16), wd_s[...], preferred_element_type=F32) + bd_ref[0])


def _experts(blk_e, n_used, xs, layer, w_gu, b_gu, w_down, b_down):
    L, E, D, F2 = w_gu.shape
    F = F2 // 2
    rows = EXPERT_BLOCK * ROW_TILES
    nb = xs.shape[0] // rows
    blk = lambda i, be, nu: (jnp.minimum(i, nu[0] - 1), 0)
    per_e = lambda i, be, nu: (layer, be[i], 0, 0)
    return pl.pallas_call(
        _expert_kernel,
        out_shape=jax.ShapeDtypeStruct(xs.shape, F32),
        grid_spec=pltpu.PrefetchScalarGridSpec(
            num_scalar_prefetch=2,
            grid=(nb,),
            in_specs=[pl.BlockSpec((rows, LANES), blk),
                      pl.BlockSpec((None, 1, D, F2), per_e),
                      pl.BlockSpec((None, 1, 1, F2), per_e),
                      pl.BlockSpec((None, 1, F, D), per_e),
                      pl.BlockSpec((None, 1, 1, D), per_e)],
            out_specs=pl.BlockSpec((rows, LANES), blk),
            scratch_shapes=[pltpu.VMEM((D, F2), BF16), pltpu.VMEM((F, D), BF16)]),
        compiler_params=_cparams(("arbitrary",)),
        name="moe_experts",
    )(blk_e, n_used, xs, w_gu, b_gu.reshape(L, E, 1, F2), w_down, b_down.reshape(L, E, 1, D))


def _combine_kernel(dest_ref, y_ref, x_ref, tw_ref, g2_ref, fg_ref, o_ref, rows_ref, sem, *, final):
    tm = x_ref.shape[1]

    def body(i, c):
        for u in range(DMA_UNROLL):
            t = i * DMA_UNROLL + u
            for k in range(TOP_K):
                pltpu.make_async_copy(_tile_at(y_ref, dest_ref[0, 0, t * TOP_K + k]),
                                      _tile_at(rows_ref.at[k], t), sem).start(priority=k % 2)
        return c

    lax.fori_loop(0, tm // DMA_UNROLL, body, 0)
    for k in range(TOP_K):
        pltpu.make_async_copy(y_ref.at[pl.ds(0, tm * ROW_TILES), :], rows_ref.at[k], sem).wait()
    tw = tw_ref[0]
    moe = tw[:, 0:1] * _tiles_to_rows(rows_ref.at[0], tm)
    for k in range(1, TOP_K):
        moe = moe + tw[:, k:k + 1] * _tiles_to_rows(rows_ref.at[k], tm)
    x = x_ref[0] + g2_ref[0] * moe
    if final:
        x = x * lax.rsqrt(jnp.mean(x * x, axis=-1, keepdims=True) + EPS) * fg_ref[...]
    o_ref[0] = x


def _combine(dest, y, x, tw, g2, final_g, final):
    B, S, D = x.shape
    tm = TOK_TILE
    nt = S // tm
    tok = lambda w: pl.BlockSpec((1, tm, w), lambda b, i: (b, i, 0))
    return pl.pallas_call(
        functools.partial(_combine_kernel, final=final),
        out_shape=jax.ShapeDtypeStruct((B, S, D), F32),
        grid=(B, nt),
        in_specs=[pl.BlockSpec((1, 1, tm * TOP_K), lambda b, i: (b * nt + i, 0, 0), memory_space=pltpu.SMEM),
                  pl.BlockSpec(memory_space=pl.ANY),
                  tok(D), tok(LANES),
                  pl.BlockSpec((1, 1, D), lambda b, i: (b, 0, 0)),
                  pl.BlockSpec((1, D), lambda b, i: (0, 0))],
        out_specs=tok(D),
        scratch_shapes=[pltpu.VMEM((TOP_K, tm * ROW_TILES, LANES), F32), pltpu.SemaphoreType.DMA(())],
        compiler_params=_cparams(("arbitrary", "arbitrary")),
        name="moe_combine",
    )(dest.reshape(B * nt, 1, tm * TOP_K), y, x, tw, g2, final_g.reshape(1, D))


def _moe(x, h2, ei, tw, counts, g2, layer, w_gu, b_gu, w_down, b_down, final_g, final):
    B, S, D = x.shape
    N = B * S
    n_slots = -(-(N * TOP_K + N_EXPERTS * EXPERT_BLOCK) // EXPERT_BLOCK) * EXPERT_BLOCK
    nb = n_slots // EXPERT_BLOCK
    cnt = counts[0, :N_EXPERTS].astype(I32)
    padded = (cnt + EXPERT_BLOCK - 1) // EXPERT_BLOCK * EXPERT_BLOCK
    pend = jnp.cumsum(padded)
    pstart = pend - padded
    ei2 = ei.reshape(N, LANES)
    dest = (pstart[ei2[:, 0:TOP_K]] + ei2[:, TOP_K:2 * TOP_K]).reshape(N * TOP_K)
    blk_start = jnp.arange(nb, dtype=I32) * EXPERT_BLOCK
    blk_e = jnp.minimum(jnp.sum((pend[None, :] <= blk_start[:, None]).astype(I32), axis=1), N_EXPERTS - 1)
    n_used = (pend[-1:] // EXPERT_BLOCK).astype(I32)
    xs = _dispatch(dest, h2, pstart + cnt, padded - cnt, n_slots)
    y = _experts(blk_e, n_used, xs, layer, w_gu, b_gu, w_down, b_down)
    return _combine(dest, y, x, tw, g2, final_g, final)


def kernel(x, c, w_mod, b_mod, norm1_g, norm2_g, w_in, w_out, pool_w, pool_scale, conv_w, conv_b, conv_ln_g,
           conv_ln_b, cmp_pe_k, cmp_pe_v, cmp_w1_k, cmp_w2_k, cmp_w1_v, cmp_w2_v, rel_bias, router_w, router_b,
           expert_w_gu, expert_b_gu, expert_w_down, expert_b_down, final_g):
    B, S, D = x.shape
    L = w_mod.shape[0]
    assert S % TOK_TILE == 0 and S // SEL_BLOCK <= LANES and D == ROW_TILES * LANES
    mod = _modulation(c, w_mod, b_mod)
    t_near, t_win, t_cmp, qc = _bias_tables(rel_bias, S)
    ovt = _overlap_t(S)
    w_big = _in_weight(w_in)
    cg = POOL_DIM // POOL_GROUPS
    for l in range(L):
        m6 = mod[l].reshape(B, 6, 1, D)
        sh1, sc1, g1, sh2, sc2, g2 = (m6[:, k] for k in range(6))
        upc, qt, kvc, ks, vs, kw, vw, gt = _in_proj(x, norm1_g[l], sc1, sh1, w_big[l])
        pw_bd = jnp.zeros((POOL_DIM, POOL_DIM), F32)
        for g in range(POOL_GROUPS):
            pw_bd = lax.dynamic_update_slice(pw_bd, pool_w[l, g], (g * cg, g * cg))
        cw = jnp.pad(conv_w[l], ((0, 1), (0, 0)))
        ypc = _pool_conv(upc, pw_bd.astype(BF16), pool_scale[l], cw, conv_b[l], conv_ln_g[l], conv_ln_b[l])
        kc, vct = _compress(kvc, _cmp_weights(cmp_w1_k[l], cmp_w1_v[l]), cmp_pe_k[l], cmp_pe_v[l],
                            cmp_w1_k[l], cmp_w1_v[l], cmp_w2_k[l], cmp_w2_v[l])
        ynsa = _attention(qt, qc, kc, vct, ks, vs, kw, vw, gt, t_near, t_win, t_cmp, ovt)
        rw = jnp.pad(router_w[l], ((0, 0), (0, LANES - N_EXPERTS)))
        rw_hi = rw.astype(BF16)
        rw_lo = (rw - rw_hi.astype(F32)).astype(BF16)
        rb = jnp.pad(router_b[l].reshape(1, -1), ((0, 0), (0, LANES - N_EXPERTS)), constant_values=NEG_INF)
        x, h2, ei, tw, counts = _post_attn(x, ypc, ynsa, w_out[l].astype(BF16), g1, norm2_g[l], sc2, sh2,
                                           rw_hi, rw_lo, rb)
        x = _moe(x, h2, ei, tw, counts, g2, l, expert_w_gu, expert_b_gu, expert_w_down, expert_b_down,
                 final_g, final=(l == L - 1))
    return x
```

```python
import functools
import math

import jax
import jax.numpy as jnp
import numpy as np
from jax import lax
from jax.experimental import pallas as pl
from jax.experimental.pallas import tpu as pltpu

F32 = jnp.float32
BF16 = jnp.bfloat16
I32 = jnp.int32

HEAD_DIM = 64
POOL_DIM = 256
POOL_GROUPS = 4
POOL_WINDOWS = (2, 4, 8, 16)
CONV_DIM = 256
CONV_WIDTH = 31
NSA_DIM = 512
NSA_HEADS = 8
KV_GROUPS = 2
HPG = 4
CMP_LEN = 32
CMP_STRIDE = 16
CMP_HIDDEN = 128
SEL_BLOCK = 64
SEL_TOPN = 16
N_FORCED = 3
WINDOW = 512
Q_TILE = 128
N_BUCKETS = 32
MAX_DISTANCE = 1024
N_EXPERTS = 32
TOP_K = 4
SWIGLU_ALPHA = 1.702
SWIGLU_LIMIT = 7.0
EPS = 1e-5
NEG_INF = -1e30
FORCE_SCORE = 1e4

LANES = 128
ROWS = HPG * Q_TILE
LOG2E = 1.4426950408889634
FAR_HEADROOM = 100.0
NEAR_KEYS = 1024
WIN_KEYS = WINDOW + Q_TILE
FAR_TILE = 512
FAR_GROUP = 4
KEY_PAD = FAR_GROUP * FAR_TILE
MASK_BIAS = -32768.0
TOK_TILE = 512
EXPERT_BLOCK = 512
VMEM_LIMIT = 56 * 1024 * 1024


def _cparams(sem, vmem=VMEM_LIMIT):
    return pltpu.CompilerParams(dimension_semantics=sem, vmem_limit_bytes=vmem)


def _t5_bucket(n):
    n = jnp.maximum(n, 0)
    max_exact = N_BUCKETS // 2
    nf = jnp.maximum(n, 1).astype(F32)
    large = max_exact + (jnp.log(nf / max_exact) / math.log(MAX_DISTANCE / max_exact)
                         * (N_BUCKETS - max_exact)).astype(I32)
    large = jnp.minimum(large, N_BUCKETS - 1)
    return jnp.where(n < max_exact, n, large)


def _mod_kernel(c_ref, w_ref, b_ref, o_ref):
    c = c_ref[...]
    cond = c * jax.nn.sigmoid(c)
    o_ref[0] = jnp.dot(cond.astype(BF16), w_ref[0].astype(BF16),
                       preferred_element_type=F32) + b_ref[0]


def _modulation(c, w_mod, b_mod):
    L, D, W = w_mod.shape
    B = c.shape[0]
    tn = 1536
    return pl.pallas_call(
        _mod_kernel,
        out_shape=jax.ShapeDtypeStruct((L, B, W), F32),
        grid=(L, W // tn),
        in_specs=[pl.BlockSpec((B, D), lambda l, j: (0, 0)),
                  pl.BlockSpec((1, D, tn), lambda l, j: (l, 0, j)),
                  pl.BlockSpec((1, 1, tn), lambda l, j: (l, 0, j))],
        out_specs=pl.BlockSpec((1, B, tn), lambda l, j: (l, 0, j)),
        compiler_params=_cparams(("arbitrary", "arbitrary")),
        name="modulation",
    )(c, w_mod, b_mod.reshape(L, 1, W))


C_UPC, C_Q, C_KVC, C_KS, C_VS, C_KW, C_VW, C_GT, C_END = 0, 768, 1280, 1536, 2048, 2304, 2560, 2816, 3072


def _in_weight(w_in):
    col = lambda a, n: w_in[:, :, a:a + n]
    zero = lambda n: jnp.zeros(w_in.shape[:2] + (n,), w_in.dtype)
    parts = [col(0, 768), col(768, 512) * (HEAD_DIM ** -0.5 * LOG2E), col(1280, 256)]
    for g in range(KV_GROUPS):
        parts += [col(1536 + g * HEAD_DIM, HEAD_DIM), zero(2 * LANES - HEAD_DIM)]
    for base in (1664, 1792, 1920):
        for g in range(KV_GROUPS):
            parts += [col(base + g * HEAD_DIM, HEAD_DIM), zero(LANES - HEAD_DIM)]
    for g in range(KV_GROUPS):
        parts += [col(2048 + br * NSA_HEADS + g * HPG, HPG) for br in range(3)] + [zero(LANES - 3 * HPG)]
    return jnp.concatenate(parts, axis=2).astype(BF16)


PAD_STEPS = KEY_PAD // TOK_TILE
TILES = TOK_TILE // Q_TILE


def _inproj_kernel(x_ref, g_ref, sc_ref, sh_ref, w_ref,
                   upc_ref, qt_ref, kvc_ref, ks_ref, vs_ref, kw_ref, vw_ref, gt_ref):
    step = pl.program_id(1)
    tm = x_ref.shape[1]

    @pl.when(step < PAD_STEPS)
    def _():
        lane = lax.broadcasted_iota(I32, (tm, 4 * LANES), 1)
        ks_ref[0] = jnp.where(lane % (2 * LANES) == HEAD_DIM, MASK_BIAS, 0.0).astype(BF16)
        lane = lax.broadcasted_iota(I32, (tm, 2 * LANES), 1)
        kw_ref[0] = jnp.where(lane % LANES == HEAD_DIM, MASK_BIAS, 0.0).astype(BF16)
        for g in range(KV_GROUPS):
            for t in range(TILES):
                vs_ref[0, g, t] = jnp.zeros((LANES, LANES), BF16)
                vw_ref[0, g, t] = jnp.zeros((LANES, LANES), BF16)

    @pl.when(step >= PAD_STEPS)
    def _():
        ti = step - PAD_STEPS
        x = x_ref[0]
        y = x * lax.rsqrt(jnp.mean(x * x, axis=-1, keepdims=True) + EPS) * g_ref[...]
        h = y * (1.0 + sc_ref[0]) + sh_ref[0]
        z = jnp.dot(h.astype(BF16), w_ref[...], preferred_element_type=F32)
        upc_ref[0] = z[:, C_UPC:C_Q]
        kvc_ref[0] = z[:, C_KVC:C_KS].astype(BF16)
        gt_ref[0] = jax.nn.sigmoid(z[:, C_GT:C_END])
        zq = z[:, C_Q:C_KVC].T
        for g in range(KV_GROUPS):
            for t in range(TILES):
                qt_ref[0, g, t] = jnp.concatenate(
                    [zq[(g * HPG + hh) * HEAD_DIM:(g * HPG + hh + 1) * HEAD_DIM, t * Q_TILE:(t + 1) * Q_TILE]
                     for hh in range(HPG)], axis=1).astype(BF16)
        lane = lax.broadcasted_iota(I32, (tm, 2 * LANES), 1)
        row = lax.broadcasted_iota(I32, (tm, 2 * LANES), 0)
        blk = (ti * tm + row) // SEL_BLOCK
        onehot = ((lane - LANES == blk) | (lane == HEAD_DIM + 1) | (lane == HEAD_DIM + 2)).astype(F32)
        for g in range(KV_GROUPS):
            ks_ref[0, :, g * 256:(g + 1) * 256] = (z[:, C_KS + g * 256:C_KS + (g + 1) * 256] + onehot).astype(BF16)
        kw_ref[0] = z[:, C_KW:C_VW].astype(BF16)
        ones_col = ((lane % LANES) == HEAD_DIM).astype(F32)
        for v_ref, c in ((vs_ref, C_VS), (vw_ref, C_VW)):
            zv = (z[:, c:c + 2 * LANES] + ones_col).T
            for g in range(KV_GROUPS):
                for t in range(TILES):
                    v_ref[0, g, t] = zv[g * LANES:(g + 1) * LANES, t * Q_TILE:(t + 1) * Q_TILE].astype(BF16)


def _in_proj(x, g1, sc, sh, w_big):
    B, S, D = x.shape
    tm = TOK_TILE
    sp = S + KEY_PAD
    nq = S // Q_TILE
    cur = lambda i: jnp.maximum(i - PAD_STEPS, 0)
    tok = lambda w: pl.BlockSpec((1, tm, w), lambda b, i: (b, cur(i), 0))
    key = lambda w: pl.BlockSpec((1, tm, w), lambda b, i: (b, i, 0))
    val = pl.BlockSpec((1, KV_GROUPS, TILES, LANES, LANES), lambda b, i: (b, 0, i, 0, 0))
    val_shape = jax.ShapeDtypeStruct((B, KV_GROUPS, sp // LANES, LANES, LANES), BF16)
    return pl.pallas_call(
        _inproj_kernel,
        out_shape=[jax.ShapeDtypeStruct((B, S, 768), F32),
                   jax.ShapeDtypeStruct((B, KV_GROUPS, nq, HEAD_DIM, ROWS), BF16),
                   jax.ShapeDtypeStruct((B, S, 256), BF16),
                   jax.ShapeDtypeStruct((B, sp, 4 * LANES), BF16), val_shape,
                   jax.ShapeDtypeStruct((B, sp, 2 * LANES), BF16), val_shape,
                   jax.ShapeDtypeStruct((B, S, 256), F32)],
        grid=(B, S // tm + PAD_STEPS),
        in_specs=[tok(D),
                  pl.BlockSpec((1, D), lambda b, i: (0, 0)),
                  pl.BlockSpec((1, 1, D), lambda b, i: (b, 0, 0)),
                  pl.BlockSpec((1, 1, D), lambda b, i: (b, 0, 0)),
                  pl.BlockSpec((D, C_END), lambda b, i: (0, 0))],
        out_specs=[tok(768),
                   pl.BlockSpec((1, KV_GROUPS, TILES, HEAD_DIM, ROWS), lambda b, i: (b, 0, cur(i), 0, 0)),
                   tok(256), key(4 * LANES), val, key(2 * LANES), val, tok(256)],
        compiler_params=_cparams(("arbitrary", "arbitrary")),
        name="in_proj",
    )(x, g1.reshape(1, D), sc, sh, w_big)


HALO = 32


def _poolconv_kernel(cur_ref, halo_ref, pw_ref, ps_ref, cw_ref, cb_ref, lg_ref, lb_ref, o_ref,
                     ext_ref, v_ref, pa_ref, pb_ref, vsh_ref):
    ti = pl.program_id(1)
    ts = cur_ref.shape[1]
    rows = HALO + ts
    halo = halo_ref[0] * (ti > 0).astype(F32)
    ext_ref[0:HALO, :] = halo
    ext_ref[HALO:rows, :] = cur_ref[0]
    u = ext_ref[HALO:rows, 0:POOL_DIM]
    lane = lax.broadcasted_iota(I32, (ts, POOL_DIM), 1)
    grp = lane // (POOL_DIM // POOL_GROUPS)
    pa_ref[8:rows, :] = ext_ref[8:rows, 0:POOL_DIM] + ext_ref[7:rows - 1, 0:POOL_DIM]
    pooled = pa_ref[HALO:rows, :]
    pb_ref[16:rows, :] = pa_ref[16:rows, :] + pa_ref[14:rows - 2, :]
    pooled = jnp.where(grp >= 1, pb_ref[HALO:rows, :], pooled)
    pa_ref[24:rows, :] = pb_ref[24:rows, :] + pb_ref[20:rows - 4, :]
    pooled = jnp.where(grp >= 2, pa_ref[HALO:rows, :], pooled)
    pooled = jnp.where(grp == 3, pa_ref[HALO:rows, :] + pa_ref[HALO - 8:rows - 8, :], pooled)
    wlane = jnp.where(grp == 0, 2.0, jnp.where(grp == 1, 4.0, jnp.where(grp == 2, 8.0, 16.0)))
    t1 = (ti * ts + lax.broadcasted_iota(I32, (ts, POOL_DIM), 0) + 1).astype(F32)
    cnt = jnp.minimum(t1, wlane)
    pooled = pooled / cnt - u
    y_pool = jnp.dot(pooled.astype(BF16), pw_ref[...], preferred_element_type=F32) * ps_ref[...]
    o_ref[0, :, 0:POOL_DIM] = y_pool.astype(BF16)
    uv = ext_ref[:, POOL_DIM:POOL_DIM + CONV_DIM]
    ug = ext_ref[:, POOL_DIM + CONV_DIM:POOL_DIM + 2 * CONV_DIM]
    v_ref[...] = uv * jax.nn.sigmoid(ug)
    for b in range(1, 8):
        vsh_ref[b - 1] = v_ref[b:rows - 8 + b, :]
    acc = jnp.zeros((ts, CONV_DIM), F32) + cb_ref[...]
    for k in range(CONV_WIDTH):
        a, b = divmod(HALO - (CONV_WIDTH - 1) + k, 8)
        tap = v_ref[8 * a:8 * a + ts, :] if b == 0 else vsh_ref[b - 1, 8 * a:8 * a + ts, :]
        acc = acc + tap * cw_ref[k:k + 1, :]
    mu = jnp.mean(acc, axis=-1, keepdims=True)
    d = acc - mu
    var = jnp.mean(d * d, axis=-1, keepdims=True)
    yn = d * lax.rsqrt(var + EPS) * lg_ref[...] + lb_ref[...]
    o_ref[0, :, POOL_DIM:POOL_DIM + CONV_DIM] = (yn * jax.nn.sigmoid(yn)).astype(BF16)


def _pool_conv(upc, pool_w_bd, pool_scale, conv_w, conv_b, ln_g, ln_b):
    B, S, W = upc.shape
    ts = TOK_TILE
    r = ts // HALO
    vec = lambda n: pl.BlockSpec((1, n), lambda b, i: (0, 0))
    return pl.pallas_call(
        _poolconv_kernel,
        out_shape=jax.ShapeDtypeStruct((B, S, POOL_DIM + CONV_DIM), BF16),
        grid=(B, S // ts),
        in_specs=[pl.BlockSpec((1, ts, W), lambda b, i: (b, i, 0)),
                  pl.BlockSpec((1, HALO, W), lambda b, i: (b, jnp.maximum(i * r - 1, 0), 0)),
                  pl.BlockSpec((POOL_DIM, POOL_DIM), lambda b, i: (0, 0)),
                  vec(POOL_DIM),
                  pl.BlockSpec((CONV_WIDTH + 1, CONV_DIM), lambda b, i: (0, 0)),
                  vec(CONV_DIM), vec(CONV_DIM), vec(CONV_DIM)],
        out_specs=pl.BlockSpec((1, ts, POOL_DIM + CONV_DIM), lambda b, i: (b, i, 0)),
        scratch_shapes=[pltpu.VMEM((HALO + ts, W), F32), pltpu.VMEM((HALO + ts, CONV_DIM), F32),
                        pltpu.VMEM((HALO + ts, POOL_DIM), F32), pltpu.VMEM((HALO + ts, POOL_DIM), F32),
                        pltpu.VMEM((7, HALO + ts - 8, CONV_DIM), F32)],
        compiler_params=_cparams(("arbitrary", "arbitrary")),
        name="pool_conv",
    )(upc, upc, pool_w_bd, pool_scale.reshape(1, -1), conv_w, conv_b.reshape(1, -1),
      ln_g.reshape(1, -1), ln_b.reshape(1, -1))


N_STREAM = 2 * KV_GROUPS
CHUNK_W = CMP_STRIDE * 2 * KV_GROUPS * HEAD_DIM


def _cmp_weights(w1_k, w1_v):
    half = CMP_STRIDE * HEAD_DIM
    cols = []
    for s in range(N_STREAM):
        w1 = w1_k if s < KV_GROUPS else w1_v
        for part in range(2):
            blk = w1[part * half:(part + 1) * half].reshape(CMP_STRIDE, 1, HEAD_DIM, CMP_HIDDEN)
            z = jnp.zeros((CMP_STRIDE, N_STREAM, HEAD_DIM, CMP_HIDDEN), w1.dtype)
            z = lax.dynamic_update_slice(z, blk, (0, s, 0, 0))
            cols.append(z.reshape(CHUNK_W, CMP_HIDDEN))
    return jnp.concatenate(cols, axis=1).astype(BF16)


def _gelu_tanh(x):
    return 0.5 * x * (1.0 + jnp.tanh(math.sqrt(2.0 / math.pi) * (x + 0.044715 * (x * x * x))))


def _compress_kernel(c_ref, w_ref, pek_ref, pev_ref, w1k_ref, w1v_ref, w2k_ref, w2vt_ref, kc_ref, vct_ref):
    r = jnp.dot(c_ref[0], w_ref[...], preferred_element_type=F32)
    ncp = r.shape[0]
    pe_k = jnp.dot(pek_ref[...], w1k_ref[...], preferred_element_type=F32)[0:1]
    pe_v = jnp.dot(pev_ref[...], w1v_ref[...], preferred_element_type=F32)[0:1]
    for s in range(N_STREAM):
        a = r[:, s * 256:s * 256 + CMP_HIDDEN]
        b = r[:, s * 256 + CMP_HIDDEN:(s + 1) * 256]
        hid = a + pltpu.roll(b, ncp - 1, 0) + (pe_k if s < KV_GROUPS else pe_v)
        act = _gelu_tanh(hid).astype(BF16)
        if s < KV_GROUPS:
            kc_ref[0, s] = jnp.dot(act, w2k_ref[...], preferred_element_type=F32).astype(BF16)
        else:
            vct_ref[0, s - KV_GROUPS] = lax.dot_general(
                w2vt_ref[...], act, (((1,), (1,)), ((), ())), preferred_element_type=F32).astype(BF16)


def _compress(kvc, wcmp, pe_k, pe_v, w1_k, w1_v, w2_k, w2_v):
    B, S, _ = kvc.shape
    ncp = S // CMP_STRIDE
    chunks = kvc.reshape(B, ncp, CHUNK_W)
    pe8 = lambda pe: jnp.broadcast_to(pe.reshape(1, -1), (8, CMP_LEN * HEAD_DIM)).astype(BF16)
    w2k = jnp.pad(w2_k, ((0, 0), (0, LANES - HEAD_DIM))).astype(BF16)
    w2vt = jnp.pad(w2_v.T, ((0, LANES - HEAD_DIM), (0, 0))).astype(BF16)
    full = lambda a: pl.BlockSpec(a.shape, lambda b: (0,) * a.ndim)
    args = (wcmp, pe8(pe_k), pe8(pe_v), w1_k.astype(BF16), w1_v.astype(BF16), w2k, w2vt)
    return pl.pallas_call(
        _compress_kernel,
        out_shape=[jax.ShapeDtypeStruct((B, KV_GROUPS, ncp, LANES), BF16),
                   jax.ShapeDtypeStruct((B, KV_GROUPS, LANES, ncp), BF16)],
        grid=(B,),
        in_specs=[pl.BlockSpec((1, ncp, CHUNK_W), lambda b: (b, 0, 0))] + [full(a) for a in args],
        out_specs=[pl.BlockSpec((1, KV_GROUPS, ncp, LANES), lambda b: (b, 0, 0, 0)),
                   pl.BlockSpec((1, KV_GROUPS, LANES, ncp), lambda b: (b, 0, 0, 0))],
        compiler_params=_cparams(("arbitrary",)),
        name="compress",
    )(chunks, *args)


def _bias_tables(rel_bias, S):
    nq = S // Q_TILE
    ncp = S // CMP_STRIDE
    rb = rel_bias.reshape(N_BUCKETS, KV_GROUPS, HPG).transpose(1, 2, 0) * LOG2E
    far = rb[:, :, N_BUCKETS - 1]
    far_hi = far.astype(BF16)
    far_lo = (far - far_hi.astype(F32)).astype(BF16)
    far_sum = far_hi.astype(F32) + far_lo.astype(F32)
    i = np.arange(Q_TILE)[None, :]

    def table(d, valid, sub):
        onehot = jax.nn.one_hot(_t5_bucket(jnp.asarray(d, I32)), N_BUCKETS, dtype=F32)
        t = jnp.einsum('rib,ghb->grhi', onehot, rb, precision=lax.Precision.HIGHEST)
        t = jnp.where(jnp.asarray(valid)[None, :, None, :], t - sub[:, None, :, None], NEG_INF)
        return t.reshape(KV_GROUPS, d.shape[0], ROWS)

    d = i - np.arange(NEAR_KEYS)[:, None] + (NEAR_KEYS - Q_TILE)
    t_near = table(d, d >= 0, far_sum)
    d = i - np.arange(WIN_KEYS)[:, None] + (WIN_KEYS - Q_TILE)
    t_win = table(d, (d >= 0) & (d < WINDOW), jnp.zeros_like(far_sum))
    c0 = (Q_TILE // CMP_STRIDE) * (nq - 1)
    d = i - CMP_STRIDE * (np.arange(c0 + ncp)[:, None] - c0) - (CMP_LEN - 1)
    t_cmp = table(d, d >= 0, jnp.zeros_like(far_sum))
    rows = jnp.zeros((KV_GROUPS, HEAD_DIM, HPG, Q_TILE), F32)
    rows = rows.at[:, 0].set(1.0)
    rows = rows.at[:, 1].set(jnp.broadcast_to(far_hi.astype(F32)[:, :, None], (KV_GROUPS, HPG, Q_TILE)))
    rows = rows.at[:, 2].set(jnp.broadcast_to(far_lo.astype(F32)[:, :, None], (KV_GROUPS, HPG, Q_TILE)))
    return t_near, t_win, t_cmp, rows.reshape(KV_GROUPS, HEAD_DIM, ROWS).astype(BF16)


def _overlap_t(S):
    ncp = S // CMP_STRIDE
    n = np.arange(ncp)[None, :]
    jb = np.arange(LANES)[:, None]
    end = n * CMP_STRIDE + CMP_LEN - 1
    start = n * CMP_STRIDE
    ov = (end >= jb * SEL_BLOCK) & (start < (jb + 1) * SEL_BLOCK) & (n < ncp - 1)
    return jnp.asarray(ov.astype(np.float32), BF16)


def _key_tiles(ref, k0, n):
    t0 = k0 // LANES
    return jnp.concatenate([ref[0, 0, t0 + u] for u in range(n)], axis=1)


def _attn_kernel(qt_ref, qc_ref, kc_ref, vct_ref, ks_ref, vs_ref, kw_ref, vw_ref, gt_ref, tn_ref, tw_ref, tct_ref,
                 ovt_ref, o_ref, qa_ref, acc_ref, *, c0):
    qi = pl.program_id(2)
    qt = qt_ref[0, 0, 0]
    ncp = kc_ref.shape[2]

    r0 = pl.multiple_of(c0 - (Q_TILE // CMP_STRIDE) * qi, 8)
    st = (jnp.dot(kc_ref[0, 0][:, 0:HEAD_DIM], qt, preferred_element_type=F32)
          + tct_ref[0, pl.ds(r0, ncp), :])
    mc = jnp.maximum(jnp.max(st, axis=0, keepdims=True), -1e20)
    pc = jnp.exp2(st - mc)
    pc = pc * (1.0 / jnp.maximum(jnp.sum(pc, axis=0, keepdims=True), 1e-30))
    o_c = jnp.dot(vct_ref[0, 0], pc.astype(BF16), preferred_element_type=F32)[0:HEAD_DIM]

    ps = pc[:, 0:Q_TILE]
    for h in range(1, HPG):
        ps = ps + pc[:, h * Q_TILE:(h + 1) * Q_TILE]
    ps_hi = ps.astype(BF16)
    ps_lo = (ps - ps_hi.astype(F32)).astype(BF16)
    imp = (jnp.dot(ovt_ref[...], ps_hi, preferred_element_type=F32)
           + jnp.dot(ovt_ref[...], ps_lo, preferred_element_type=F32))

    def flash_step(carry, s, vt):
        m, acc = carry
        m_new = jnp.maximum(m, jnp.max(s, axis=0, keepdims=True))
        alpha = jnp.exp2(m - m_new)
        p = jnp.exp2(s - m_new)
        acc = acc * alpha + jnp.dot(vt, p.astype(BF16), preferred_element_type=F32)
        return m_new, acc

    def normalised(acc):
        return acc[0:HEAD_DIM] * (1.0 / acc[HEAD_DIM:HEAD_DIM + 1])

    k1 =Q_TILE * qi + (KEY_PAD + Q_TILE)

    qa_ref[0:HEAD_DIM, :] = qt
    qa_ref[HEAD_DIM:LANES, :] = qc_ref[0]

    w0 = pl.multiple_of(k1 - WIN_KEYS, LANES)
    s = jnp.dot(kw_ref[0, pl.ds(w0, WIN_KEYS), :], qa_ref[0:LANES, :], preferred_element_type=F32) + tw_ref[0]
    p = jnp.exp2(s - jnp.max(s, axis=0, keepdims=True)).astype(BF16)
    o_w = normalised(jnp.dot(_key_tiles(vw_ref, w0, WIN_KEYS // LANES), p, preferred_element_type=F32))

    jb = lax.broadcasted_iota(I32, (LANES, Q_TILE), 0)
    ii = lax.broadcasted_iota(I32, (LANES, Q_TILE), 1)
    cur = (Q_TILE // SEL_BLOCK) * qi + (ii >= SEL_BLOCK).astype(I32)
    sel = (jb == 0) | (jb == cur) | (jb == cur - 1)
    score = jnp.where(jb <= cur, jnp.where(sel, -jnp.inf, imp), NEG_INF)
    for _ in range(SEL_TOPN - N_FORCED):
        mx = jnp.max(score, axis=0, keepdims=True)
        first = jnp.min(jnp.where(score == mx, jb, LANES), axis=0, keepdims=True)
        pick = jb == first
        sel = sel | pick
        score = jnp.where(pick, -jnp.inf, score)
    mbt = jnp.where(sel, 0.0, MASK_BIAS).astype(BF16)
    for h in range(HPG):
        qa_ref[LANES:2 * LANES, h * Q_TILE:(h + 1) * Q_TILE] = mbt

    n0 = pl.multiple_of(k1 - NEAR_KEYS, LANES)
    half = NEAR_KEYS // 2
    sa = (jnp.dot(ks_ref[0, pl.ds(n0, half), :], qa_ref[...], preferred_element_type=F32)
          + tn_ref[0, 0:half, :])
    sb = (jnp.dot(ks_ref[0, pl.ds(n0 + half, half), :], qa_ref[...], preferred_element_type=F32)
          + tn_ref[0, half:NEAR_KEYS, :])
    m_near = jnp.maximum(jnp.max(sa, axis=0, keepdims=True), jnp.max(sb, axis=0, keepdims=True))
    pa = jnp.exp2(sa - m_near).astype(BF16)
    da = jnp.dot(_key_tiles(vs_ref, n0, half // LANES), pa, preferred_element_type=F32)
    pb = jnp.exp2(sb - m_near).astype(BF16)
    acc_near = da + jnp.dot(_key_tiles(vs_ref, n0 + half, half // LANES), pb, preferred_element_type=F32)
    n_pairs = (jnp.maximum(n0 - KEY_PAD, 0) + KEY_PAD - 1) // KEY_PAD

    def far_scores(j):
        k0 = pl.multiple_of(n0 - FAR_TILE * (j + 1), LANES)
        return jnp.dot(ks_ref[0, pl.ds(k0, FAR_TILE), :], qa_ref[...], preferred_element_type=F32)

    def far_values(j):
        return _key_tiles(vs_ref, pl.multiple_of(n0 - FAR_TILE * (j + 1), LANES), FAR_TILE // LANES)

    def fast_body(jj, top):
        tiles = [FAR_GROUP * jj + u for u in range(FAR_GROUP)]
        scores = [far_scores(tiles[0]), far_scores(tiles[1])]
        acc = None
        for u, j in enumerate(tiles):
            if u + 2 < FAR_GROUP:
                scores.append(far_scores(tiles[u + 2]))
            p = jnp.exp2(scores[u] - m_near).astype(BF16)
            d = jnp.dot(far_values(j), p, preferred_element_type=F32)
            acc = d if acc is None else acc + d
            top = jnp.maximum(top, jnp.max(scores[u], axis=0, keepdims=True))
        acc_ref[...] += acc
        return top

    acc_ref[...] = acc_near
    top = lax.fori_loop(0, n_pairs, fast_body, m_near)

    @pl.when(jnp.max(top - m_near) > FAR_HEADROOM)
    def _():
        def safe_body(j, carry):
            return flash_step(carry, far_scores(j), far_values(j))

        acc_ref[...] = lax.fori_loop(0, FAR_GROUP * n_pairs, safe_body, (m_near, acc_near))[1]

    o_s = normalised(acc_ref[...])

    gtt = gt_ref[0].T
    gate = lambda br: jnp.concatenate([gtt[br * HPG + h:br * HPG + h + 1, :] for h in range(HPG)], axis=1)
    o_ref[0, 0, 0] = (gate(0) * o_c + gate(1) * o_s + gate(2) * o_w).astype(BF16)


def _attention(qt, qc, kc, vct, ks, vs, kw, vw, gt, tn, tw, tct, ovt):
    B, G, nq, _, _ = qt.shape
    S = nq * Q_TILE
    sp = S + KEY_PAD
    ncp = S // CMP_STRIDE
    c0 = (Q_TILE // CMP_STRIDE) * (nq - 1)
    per_g = lambda a: pl.BlockSpec((1,) + a.shape[1:], lambda b, g, i: (g,) + (0,) * (a.ndim - 1))
    val = pl.BlockSpec((1, 1, sp // LANES, LANES, LANES), lambda b, g, i: (b, g, 0, 0, 0))
    return pl.pallas_call(
        functools.partial(_attn_kernel, c0=c0),
        out_shape=jax.ShapeDtypeStruct((B, G, nq, HEAD_DIM, ROWS), BF16),
        grid=(B, G, nq),
        in_specs=[pl.BlockSpec((1, 1, 1, HEAD_DIM, ROWS), lambda b, g, i: (b, g, i, 0, 0)),
                  per_g(qc),
                  pl.BlockSpec((1, 1, ncp, LANES), lambda b, g, i: (b, g, 0, 0)),
                  pl.BlockSpec((1, 1, LANES, ncp), lambda b, g, i: (b, g, 0, 0)),
                  pl.BlockSpec((1, sp, 2 * LANES), lambda b, g, i: (b, 0, g)),
                  val,
                  pl.BlockSpec((1, sp, LANES), lambda b, g, i: (b, 0, g)),
                  val,
                  pl.BlockSpec((1, Q_TILE, LANES), lambda b, g, i: (b, i, g)),
                  per_g(tn), per_g(tw), per_g(tct),
                  pl.BlockSpec((LANES, ncp), lambda b, g, i: (0, 0))],
        out_specs=pl.BlockSpec((1, 1, 1, HEAD_DIM, ROWS), lambda b, g, i: (b, g, i, 0, 0)),
        scratch_shapes=[pltpu.VMEM((2 * LANES, ROWS), BF16), pltpu.VMEM((LANES, ROWS), F32)],
        compiler_params=_cparams(("arbitrary", "arbitrary", "arbitrary")),
        name="nsa_attention",
    )(qt, qc, kc, vct, ks, vs, kw, vw, gt, tn, tw, tct, ovt)


def _post_attn_kernel(x_ref, ypc_ref, yn_ref, wo_ref, g1_ref, n2_ref, sc_ref, sh_ref, rwh_ref, rwl_ref, rb_ref,
                      xo_ref, h2_ref, ei_ref, tw_ref, cnt_ref, run_ref, hd_ref):
    first = (pl.program_id(0) == 0) & (pl.program_id(1) == 0)

    @pl.when(first)
    def _():
        run_ref[...] = jnp.zeros_like(run_ref)

    tm = x_ref.shape[1]
    half = wo_ref.shape[0] // 2
    tiles = []
    for t in range(TILES):
        groups = []
        for g in range(KV_GROUPS):
            blk = yn_ref[0, g, t].astype(F32)
            for hh in range(HPG):
                hd_ref[hh * HEAD_DIM:(hh + 1) * HEAD_DIM, :] = blk[:, hh * Q_TILE:(hh + 1) * Q_TILE]
            groups.append(hd_ref[...].T)
        tiles.append(jnp.concatenate(groups, axis=1))
    ynsa = jnp.concatenate(tiles, axis=0).astype(BF16)
    mixed = (jnp.dot(ypc_ref[0], wo_ref[0:half, :], preferred_element_type=F32)
             + jnp.dot(ynsa, wo_ref[half:, :], preferred_element_type=F32))
    x = x_ref[0] + g1_ref[0] * mixed
    xo_ref[0] = x
    y = x * lax.rsqrt(jnp.mean(x * x, axis=-1, keepdims=True) + EPS) * n2_ref[...]
    h2 = y * (1.0 + sc_ref[0]) + sh_ref[0]
    _rows_to_tiles(h2_ref, h2)
    hh = h2.astype(BF16)
    hl = (h2 - hh.astype(F32)).astype(BF16)
    logit = (jnp.dot(hh, rwh_ref[...], preferred_element_type=F32)
             + jnp.dot(hl, rwh_ref[...], preferred_element_type=F32)
             + jnp.dot(hh, rwl_ref[...], preferred_element_type=F32)) + rb_ref[...]
    lane = lax.broadcasted_iota(I32, (tm, LANES), 1)
    vals, hots, idxs = [], [], []
    for _ in range(TOP_K):
        mx = jnp.max(logit, axis=1, keepdims=True)
        idx = jnp.min(jnp.where(logit == mx, lane, LANES), axis=1, keepdims=True)
        hot = lane == idx
        vals.append(mx)
        hots.append(hot)
        idxs.append(idx)
        logit = jnp.where(hot, -jnp.inf, logit)
    ex = [jnp.exp(v - vals[0]) for v in vals]
    inv = 1.0 / (ex[0] + ex[1] + ex[2] + ex[3])
    assign = (hots[0] | hots[1] | hots[2] | hots[3]).astype(BF16)
    r = lax.broadcasted_iota(I32, (tm, tm), 0)
    c = lax.broadcasted_iota(I32, (tm, tm), 1)
    before = jnp.dot((c < r).astype(BF16), assign, preferred_element_type=F32) + run_ref[...]
    ei = jnp.zeros((tm, LANES), I32)
    tw = jnp.zeros((tm, LANES), F32)
    for k in range(TOP_K):
        e_k = idxs[k]
        r_k = jnp.sum(jnp.where(hots[k], before, 0.0), axis=1, keepdims=True).astype(I32)
        ei = jnp.where(lane == k, e_k, jnp.where(lane == TOP_K + k, r_k, ei))
        tw = jnp.where(lane == k, ex[k] * inv, tw)
    ei_ref[0] = ei
    tw_ref[0] = tw
    run_ref[...] = run_ref[...] + jnp.sum(assign.astype(F32), axis=0, keepdims=True)
    cnt_ref[...] = run_ref[...]


def _post_attn(x, ypc, ynsa, w_out, g1, n2g, sc2, sh2, rw_hi, rw_lo, rb):
    B, S, D = x.shape
    tm = TOK_TILE
    tok = lambda w: pl.BlockSpec((1, tm, w), lambda b, i: (b, i, 0))
    per_b = pl.BlockSpec((1, 1, D), lambda b, i: (b, 0, 0))
    full = lambda a: pl.BlockSpec(a.shape, lambda b, i: (0,) * a.ndim)
    return pl.pallas_call(
        _post_attn_kernel,
        out_shape=[jax.ShapeDtypeStruct((B, S, D), F32), jax.ShapeDtypeStruct((B * S * ROW_TILES, LANES), F32),
                   jax.ShapeDtypeStruct((B, S, LANES), I32), jax.ShapeDtypeStruct((B, S, LANES), F32),
                   jax.ShapeDtypeStruct((1, LANES), F32)],
        grid=(B, S // tm),
        in_specs=[tok(D), tok(ypc.shape[-1]),
                  pl.BlockSpec((1, KV_GROUPS, TILES, HEAD_DIM, ROWS), lambda b, i: (b, 0, i, 0, 0)),
                  full(w_out), per_b,
                  pl.BlockSpec((1, D), lambda b, i: (0, 0)), per_b, per_b,
                  full(rw_hi), full(rw_lo), full(rb)],
        out_specs=[tok(D), pl.BlockSpec((tm * ROW_TILES, LANES), lambda b, i: (b * (S // tm) + i, 0)),
                   tok(LANES), tok(LANES), pl.BlockSpec((1, LANES), lambda b, i: (0, 0))],
        scratch_shapes=[pltpu.VMEM((1, LANES), F32), pltpu.VMEM((HPG * HEAD_DIM, Q_TILE), F32)],
        compiler_params=_cparams(("arbitrary", "arbitrary")),
        name="post_attn_router",
    )(x, ypc, ynsa, w_out, g1, n2g.reshape(1, D), sc2, sh2, rw_hi, rw_lo, rb)


ROW_TILES = 8
DMA_UNROLL = 2
ZERO_BITS = tuple(1 << b for b in reversed(range((EXPERT_BLOCK - 1).bit_length())))


def _rows_to_tiles(ref, val):
    n = val.shape[0]
    for s in range(ROW_TILES):
        ref[pl.ds(s, n, stride=ROW_TILES), :] = val[:, s * LANES:(s + 1) * LANES]


def _tiles_to_rows(ref, n):
    return jnp.concatenate([ref[pl.ds(s, n, stride=ROW_TILES), :] for s in range(ROW_TILES)], axis=1)


def _tile_at(ref, i):
    return ref.at[pl.ds(pl.multiple_of(i * ROW_TILES, ROW_TILES), ROW_TILES), :]


def _dispatch_kernel(fill0_ref, filln_ref, dest_ref, h_ref, xs_ref, zero_ref, sem, zsem):
    tm = h_ref.shape[0] // ROW_TILES

    @pl.when(pl.program_id(0) == 0)
    def _():
        zero_ref[...] = jnp.zeros_like(zero_ref)

        def expert(e, c):
            n = filln_ref[e]
            for wait in (False, True):
                for bit in ZERO_BITS:
                    @pl.when((n & bit) != 0)
                    def _():
                        first = fill0_ref[e] + (n & ~(2 * bit - 1))
                        dst = xs_ref.at[pl.ds(pl.multiple_of(first * ROW_TILES, ROW_TILES), bit * ROW_TILES), :]
                        cp = pltpu.make_async_copy(zero_ref.at[pl.ds(0, bit * ROW_TILES), :], dst, zsem)
                        cp.wait() if wait else cp.start()
            return c

        lax.fori_loop(0, N_EXPERTS, expert, 0)

    def body(i, c):
        for u in range(DMA_UNROLL):
            t = i * DMA_UNROLL + u
            src = _tile_at(h_ref, t)
            for k in range(TOP_K):
                pltpu.make_async_copy(src, _tile_at(xs_ref, dest_ref[0, 0, t * TOP_K + k]), sem).start(
                    priority=k % 2)
        return c

    lax.fori_loop(0, tm // DMA_UNROLL, body, 0)
    for _ in range(TOP_K):
        pltpu.make_async_copy(h_ref, xs_ref.at[pl.ds(0, tm * ROW_TILES), :], sem).wait()


def _dispatch(dest, h2t, fill0, filln, n_slots):
    tm = TOK_TILE
    nt = h2t.shape[0] // (tm * ROW_TILES)
    return pl.pallas_call(
        _dispatch_kernel,
        out_shape=jax.ShapeDtypeStruct((n_slots * ROW_TILES, LANES), F32),
        grid_spec=pltpu.PrefetchScalarGridSpec(
            num_scalar_prefetch=2,
            grid=(nt,),
            in_specs=[pl.BlockSpec((1, 1, tm * TOP_K), lambda i, f0, fn: (i, 0, 0), memory_space=pltpu.SMEM),
                      pl.BlockSpec((tm * ROW_TILES, LANES), lambda i, f0, fn: (i, 0))],
            out_specs=pl.BlockSpec(memory_space=pl.ANY),
            scratch_shapes=[pltpu.VMEM((ZERO_BITS[0] * ROW_TILES, LANES), F32), pltpu.SemaphoreType.DMA(()),
                            pltpu.SemaphoreType.DMA(())]),
        compiler_params=_cparams(("arbitrary",)),
        name="moe_dispatch",
    )(fill0, filln, dest.reshape(nt, 1, tm * TOP_K), h2t)


W_CHUNK = 512


def _expert_kernel(be_ref, nu_ref, x_ref, wgu_ref, bgu_ref, wd_ref, bd_ref, y_ref, wgu_s, wd_s):
    i = pl.program_id(0)

    @pl.when(i < nu_ref[0])
    def _():
        @pl.when((i == 0) | (be_ref[i] != be_ref[jnp.maximum(i - 1, 0)]))
        def _():
            for c in range(0, wgu_s.shape[1], W_CHUNK):
                wgu_s[:, c:c + W_CHUNK] = wgu_ref[0, :, c:c + W_CHUNK].astype(BF16)
            for c in range(0, wd_s.shape[1], W_CHUNK):
                wd_s[:, c:c + W_CHUNK] = wd_ref[0, :, c:c + W_CHUNK].astype(BF16)

        F = wd_s.shape[0]
        x = _tiles_to_rows(x_ref, EXPERT_BLOCK).astype(BF16)
        gu = jnp.dot(x, wgu_s[...], preferred_element_type=F32) + bgu_ref[0]
        gate = jnp.minimum(gu[:, :F], SWIGLU_LIMIT)
        up = jnp.clip(gu[:, F:], -SWIGLU_LIMIT, SWIGLU_LIMIT)
        act = (up + 1.0) * gate * jax.nn.sigmoid(SWIGLU_ALPHA * gate)
        _rows_to_tiles(y_ref, jnp.dot(act.astype(BF16), wd_s[...], preferred_element_type=F32) + bd_ref[0])


def _experts(blk_e, n_used, xs, layer, w_gu, b_gu, w_down, b_down):
    L, E, D, F2 = w_gu.shape
    F = F2 // 2
    rows = EXPERT_BLOCK * ROW_TILES
    nb = xs.shape[0] // rows
    blk = lambda i, be, nu: (jnp.minimum(i, nu[0] - 1), 0)
    per_e = lambda i, be, nu: (layer, be[i], 0, 0)
    return pl.pallas_call(
        _expert_kernel,
        out_shape=jax.ShapeDtypeStruct(xs.shape, F32),
        grid_spec=pltpu.PrefetchScalarGridSpec(
            num_scalar_prefetch=2,
            grid=(nb,),
            in_specs=[pl.BlockSpec((rows, LANES), blk),
                      pl.BlockSpec((None, 1, D, F2), per_e),
                      pl.BlockSpec((None, 1, 1, F2), per_e),
                      pl.BlockSpec((None, 1, F, D), per_e),
                      pl.BlockSpec((None, 1, 1, D), per_e)],
            out_specs=pl.BlockSpec((rows, LANES), blk),
            scratch_shapes=[pltpu.VMEM((D, F2), BF16), pltpu.VMEM((F, D), BF16)]),
        compiler_params=_cparams(("arbitrary",)),
        name="moe_experts",
    )(blk_e, n_used, xs, w_gu, b_gu.reshape(L, E, 1, F2), w_down, b_down.reshape(L, E, 1, D))


def _combine_kernel(dest_ref, y_ref, x_ref, tw_ref, g2_ref, fg_ref, o_ref, rows_ref, sem, *, final):
    tm = x_ref.shape[1]

    def body(i, c):
        for u in range(DMA_UNROLL):
            t = i * DMA_UNROLL + u
            for k in range(TOP_K):
                pltpu.make_async_copy(_tile_at(y_ref, dest_ref[0, 0, t * TOP_K + k]),
                                      _tile_at(rows_ref.at[k], t), sem).start(priority=k % 2)
        return c

    lax.fori_loop(0, tm // DMA_UNROLL, body, 0)
    for k in range(TOP_K):
        pltpu.make_async_copy(y_ref.at[pl.ds(0, tm * ROW_TILES), :], rows_ref.at[k], sem).wait()
    tw = tw_ref[0]
    moe = tw[:, 0:1] * _tiles_to_rows(rows_ref.at[0], tm)
    for k in range(1, TOP_K):
        moe = moe + tw[:, k:k + 1] * _tiles_to_rows(rows_ref.at[k], tm)
    x = x_ref[0] + g2_ref[0] * moe
    if final:
        x = x * lax.rsqrt(jnp.mean(x * x, axis=-1, keepdims=True) + EPS) * fg_ref[...]
    o_ref[0] = x


def _combine(dest, y, x, tw, g2, final_g, final):
    B, S, D = x.shape
    tm = TOK_TILE
    nt = S // tm
    tok = lambda w: pl.BlockSpec((1, tm, w), lambda b, i: (b, i, 0))
    return pl.pallas_call(
        functools.partial(_combine_kernel, final=final),
        out_shape=jax.ShapeDtypeStruct((B, S, D), F32),
        grid=(B, nt),
        in_specs=[pl.BlockSpec((1, 1, tm * TOP_K), lambda b, i: (b * nt + i, 0, 0), memory_space=pltpu.SMEM),
                  pl.BlockSpec(memory_space=pl.ANY),
                  tok(D), tok(LANES),
                  pl.BlockSpec((1, 1, D), lambda b, i: (b, 0, 0)),
                  pl.BlockSpec((1, D), lambda b, i: (0, 0))],
        out_specs=tok(D),
        scratch_shapes=[pltpu.VMEM((TOP_K, tm * ROW_TILES, LANES), F32), pltpu.SemaphoreType.DMA(())],
        compiler_params=_cparams(("arbitrary", "arbitrary")),
        name="moe_combine",
    )(dest.reshape(B * nt, 1, tm * TOP_K), y, x, tw, g2, final_g.reshape(1, D))


def _moe(x, h2, ei, tw, counts, g2, layer, w_gu, b_gu, w_down, b_down, final_g, final):
    B, S, D = x.shape
    N = B * S
    n_slots = -(-(N * TOP_K + N_EXPERTS * EXPERT_BLOCK) // EXPERT_BLOCK) * EXPERT_BLOCK
    nb = n_slots // EXPERT_BLOCK
    cnt = counts[0, :N_EXPERTS].astype(I32)
    padded = (cnt + EXPERT_BLOCK - 1) // EXPERT_BLOCK * EXPERT_BLOCK
    pend = jnp.cumsum(padded)
    pstart = pend - padded
    ei2 = ei.reshape(N, LANES)
    dest = (pstart[ei2[:, 0:TOP_K]] + ei2[:, TOP_K:2 * TOP_K]).reshape(N * TOP_K)
    blk_start = jnp.arange(nb, dtype=I32) * EXPERT_BLOCK
    blk_e = jnp.minimum(jnp.sum((pend[None, :] <= blk_start[:, None]).astype(I32), axis=1), N_EXPERTS - 1)
    n_used = (pend[-1:] // EXPERT_BLOCK).astype(I32)
    xs = _dispatch(dest, h2, pstart + cnt, padded - cnt, n_slots)
    y = _experts(blk_e, n_used, xs, layer, w_gu, b_gu, w_down, b_down)
    return _combine(dest, y, x, tw, g2, final_g, final)


def kernel(x, c, w_mod, b_mod, norm1_g, norm2_g, w_in, w_out, pool_w, pool_scale, conv_w, conv_b, conv_ln_g,
           conv_ln_b, cmp_pe_k, cmp_pe_v, cmp_w1_k, cmp_w2_k, cmp_w1_v, cmp_w2_v, rel_bias, router_w, router_b,
           expert_w_gu, expert_b_gu, expert_w_down, expert_b_down, final_g):
    B, S, D = x.shape
    L = w_mod.shape[0]
    assert S % TOK_TILE == 0 and S // SEL_BLOCK <= LANES and D == ROW_TILES * LANES
    mod = _modulation(c, w_mod, b_mod)
    t_near, t_win, t_cmp, qc = _bias_tables(rel_bias, S)
    ovt = _overlap_t(S)
    w_big = _in_weight(w_in)
    cg = POOL_DIM // POOL_GROUPS
    for l in range(L):
        m6 = mod[l].reshape(B, 6, 1, D)
        sh1, sc1, g1, sh2, sc2, g2 = (m6[:, k] for k in range(6))
        upc, qt, kvc, ks, vs, kw, vw, gt = _in_proj(x, norm1_g[l], sc1, sh1, w_big[l])
        pw_bd = jnp.zeros((POOL_DIM, POOL_DIM), F32)
        for g in range(POOL_GROUPS):
            pw_bd = lax.dynamic_update_slice(pw_bd, pool_w[l, g], (g * cg, g * cg))
        cw = jnp.pad(conv_w[l], ((0, 1), (0, 0)))
        ypc = _pool_conv(upc, pw_bd.astype(BF16), pool_scale[l], cw, conv_b[l], conv_ln_g[l], conv_ln_b[l])
        kc, vct = _compress(kvc, _cmp_weights(cmp_w1_k[l], cmp_w1_v[l]), cmp_pe_k[l], cmp_pe_v[l],
                            cmp_w1_k[l], cmp_w1_v[l], cmp_w2_k[l], cmp_w2_v[l])
        ynsa = _attention(qt, qc, kc, vct, ks, vs, kw, vw, gt, t_near, t_win, t_cmp, ovt)
        rw = jnp.pad(router_w[l], ((0, 0), (0, LANES - N_EXPERTS)))
        rw_hi = rw.astype(BF16)
        rw_lo = (rw - rw_hi.astype(F32)).astype(BF16)
        rb = jnp.pad(router_b[l].reshape(1, -1), ((0, 0), (0, LANES - N_EXPERTS)), constant_values=NEG_INF)
        x, h2, ei, tw, counts = _post_attn(x, ypc, ynsa, w_out[l].astype(BF16), g1, norm2_g[l], sc2, sh2,
                                           rw_hi, rw_lo, rb)
        x = _moe(x, h2, ei, tw, counts, g2, l, expert_w_gu, expert_b_gu, expert_w_down, expert_b_down,
                 final_g, final=(l == L - 1))
    return x
```

```python
import functools
import math

import jax
import jax.numpy as jnp
import numpy as np
from jax import lax
from jax.experimental import pallas as pl
from jax.experimental.pallas import tpu as pltpu

F32 = jnp.float32
BF16 = jnp.bfloat16
I32 = jnp.int32

HEAD_DIM = 64
POOL_DIM = 256
POOL_GROUPS = 4
POOL_WINDOWS = (2, 4, 8, 16)
CONV_DIM = 256
CONV_WIDTH = 31
NSA_DIM = 512
NSA_HEADS = 8
KV_GROUPS = 2
HPG = 4
CMP_LEN = 32
CMP_STRIDE = 16
CMP_HIDDEN = 128
SEL_BLOCK = 64
SEL_TOPN = 16
N_FORCED = 3
WINDOW = 512
Q_TILE = 128
N_BUCKETS = 32
MAX_DISTANCE = 1024
N_EXPERTS = 32
TOP_K = 4
SWIGLU_ALPHA = 1.702
SWIGLU_LIMIT = 7.0
EPS = 1e-5
NEG_INF = -1e30
FORCE_SCORE = 1e4

LANES = 128
ROWS = HPG * Q_TILE
LOG2E = 1.4426950408889634
FAR_HEADROOM = 100.0
NEAR_KEYS = 1024
WIN_KEYS = WINDOW + Q_TILE
FAR_TILE = 512
FAR_GROUP = 4
KEY_PAD = FAR_GROUP * FAR_TILE
MASK_BIAS = -32768.0
TOK_TILE = 512
EXPERT_BLOCK = 512
VMEM_LIMIT = 56 * 1024 * 1024


def _cparams(sem, vmem=VMEM_LIMIT):
    return pltpu.CompilerParams(dimension_semantics=sem, vmem_limit_bytes=vmem)


def _t5_bucket(n):
    n = jnp.maximum(n, 0)
    max_exact = N_BUCKETS // 2
    nf = jnp.maximum(n, 1).astype(F32)
    large = max_exact + (jnp.log(nf / max_exact) / math.log(MAX_DISTANCE / max_exact)
                         * (N_BUCKETS - max_exact)).astype(I32)
    large = jnp.minimum(large, N_BUCKETS - 1)
    return jnp.where(n < max_exact, n, large)


def _mod_kernel(c_ref, w_ref, b_ref, o_ref):
    c = c_ref[...]
    cond = c * jax.nn.sigmoid(c)
    o_ref[0] = jnp.dot(cond.astype(BF16), w_ref[0].astype(BF16),
                       preferred_element_type=F32) + b_ref[0]


def _modulation(c, w_mod, b_mod):
    L, D, W = w_mod.shape
    B = c.shape[0]
    tn = 1536
    return pl.pallas_call(
        _mod_kernel,
        out_shape=jax.ShapeDtypeStruct((L, B, W), F32),
        grid=(L, W // tn),
        in_specs=[pl.BlockSpec((B, D), lambda l, j: (0, 0)),
                  pl.BlockSpec((1, D, tn), lambda l, j: (l, 0, j)),
                  pl.BlockSpec((1, 1, tn), lambda l, j: (l, 0, j))],
        out_specs=pl.BlockSpec((1, B, tn), lambda l, j: (l, 0, j)),
        compiler_params=_cparams(("arbitrary", "arbitrary")),
        name="modulation",
    )(c, w_mod, b_mod.reshape(L, 1, W))


C_UPC, C_Q, C_KVC, C_KS, C_VS, C_KW, C_VW, C_GT, C_END = 0, 768, 1280, 1536, 2048, 2304, 2560, 2816, 3072


def _in_weight(w_in):
    col = lambda a, n: w_in[:, :, a:a + n]
    zero = lambda n: jnp.zeros(w_in.shape[:2] + (n,), w_in.dtype)
    parts = [col(0, 768), col(768, 512) * (HEAD_DIM ** -0.5 * LOG2E), col(1280, 256)]
    for g in range(KV_GROUPS):
        parts += [col(1536 + g * HEAD_DIM, HEAD_DIM), zero(2 * LANES - HEAD_DIM)]
    for base in (1664, 1792, 1920):
        for g in range(KV_GROUPS):
            parts += [col(base + g * HEAD_DIM, HEAD_DIM), zero(LANES - HEAD_DIM)]
    for g in range(KV_GROUPS):
        parts += [col(2048 + br * NSA_HEADS + g * HPG, HPG) for br in range(3)] + [zero(LANES - 3 * HPG)]
    return jnp.concatenate(parts, axis=2).astype(BF16)


PAD_STEPS = KEY_PAD // TOK_TILE
TILES = TOK_TILE // Q_TILE


def _inproj_kernel(x_ref, g_ref, sc_ref, sh_ref, w_ref,
                   upc_ref, qt_ref, kvc_ref, ks_ref, vs_ref, kw_ref, vw_ref, gt_ref):
    step = pl.program_id(1)
    tm = x_ref.shape[1]

    @pl.when(step < PAD_STEPS)
    def _():
        lane = lax.broadcasted_iota(I32, (tm, 4 * LANES), 1)
        ks_ref[0] = jnp.where(lane % (2 * LANES) == HEAD_DIM, MASK_BIAS, 0.0).astype(BF16)
        lane = lax.broadcasted_iota(I32, (tm, 2 * LANES), 1)
        kw_ref[0] = jnp.where(lane % LANES == HEAD_DIM, MASK_BIAS, 0.0).astype(BF16)
        for g in range(KV_GROUPS):
            for t in range(TILES):
                vs_ref[0, g, t] = jnp.zeros((LANES, LANES), BF16)
                vw_ref[0, g, t] = jnp.zeros((LANES, LANES), BF16)

    @pl.when(step >= PAD_STEPS)
    def _():
        ti = step - PAD_STEPS
        x = x_ref[0]
        y = x * lax.rsqrt(jnp.mean(x * x, axis=-1, keepdims=True) + EPS) * g_ref[...]
        h = y * (1.0 + sc_ref[0]) + sh_ref[0]
        z = jnp.dot(h.astype(BF16), w_ref[...], preferred_element_type=F32)
        upc_ref[0] = z[:, C_UPC:C_Q]
        kvc_ref[0] = z[:, C_KVC:C_KS].astype(BF16)
        gt_ref[0] = jax.nn.sigmoid(z[:, C_GT:C_END])
        zq = z[:, C_Q:C_KVC].T
        for g in range(KV_GROUPS):
            for t in range(TILES):
                qt_ref[0, g, t] = jnp.concatenate(
                    [zq[(g * HPG + hh) * HEAD_DIM:(g * HPG + hh + 1) * HEAD_DIM, t * Q_TILE:(t + 1) * Q_TILE]
                     for hh in range(HPG)], axis=1).astype(BF16)
        lane = lax.broadcasted_iota(I32, (tm, 2 * LANES), 1)
        row = lax.broadcasted_iota(I32, (tm, 2 * LANES), 0)
        blk = (ti * tm + row) // SEL_BLOCK
        onehot = ((lane - LANES == blk) | (lane == HEAD_DIM + 1) | (lane == HEAD_DIM + 2)).astype(F32)
        for g in range(KV_GROUPS):
            ks_ref[0, :, g * 256:(g + 1) * 256] = (z[:, C_KS + g * 256:C_KS + (g + 1) * 256] + onehot).astype(BF16)
        kw_ref[0] = z[:, C_KW:C_VW].astype(BF16)
        ones_col = ((lane % LANES) == HEAD_DIM).astype(F32)
        for v_ref, c in ((vs_ref, C_VS), (vw_ref, C_VW)):
            zv = (z[:, c:c + 2 * LANES] + ones_col).T
            for g in range(KV_GROUPS):
                for t in range(TILES):
                    v_ref[0, g, t] = zv[g * LANES:(g + 1) * LANES, t * Q_TILE:(t + 1) * Q_TILE].astype(BF16)


def _in_proj(x, g1, sc, sh, w_big):
    B, S, D = x.shape
    tm = TOK_TILE
    sp = S + KEY_PAD
    nq = S // Q_TILE
    cur = lambda i: jnp.maximum(i - PAD_STEPS, 0)
    tok = lambda w: pl.BlockSpec((1, tm, w), lambda b, i: (b, cur(i), 0))
    key = lambda w: pl.BlockSpec((1, tm, w), lambda b, i: (b, i, 0))
    val = pl.BlockSpec((1, KV_GROUPS, TILES, LANES, LANES), lambda b, i: (b, 0, i, 0, 0))
    val_shape = jax.ShapeDtypeStruct((B, KV_GROUPS, sp // LANES, LANES, LANES), BF16)
    return pl.pallas_call(
        _inproj_kernel,
        out_shape=[jax.ShapeDtypeStruct((B, S, 768), F32),
                   jax.ShapeDtypeStruct((B, KV_GROUPS, nq, HEAD_DIM, ROWS), BF16),
                   jax.ShapeDtypeStruct((B, S, 256), BF16),
                   jax.ShapeDtypeStruct((B, sp, 4 * LANES), BF16), val_shape,
                   jax.ShapeDtypeStruct((B, sp, 2 * LANES), BF16), val_shape,
                   jax.ShapeDtypeStruct((B, S, 256), F32)],
        grid=(B, S // tm + PAD_STEPS),
        in_specs=[tok(D),
                  pl.BlockSpec((1, D), lambda b, i: (0, 0)),
                  pl.BlockSpec((1, 1, D), lambda b, i: (b, 0, 0)),
                  pl.BlockSpec((1, 1, D), lambda b, i: (b, 0, 0)),
                  pl.BlockSpec((D, C_END), lambda b, i: (0, 0))],
        out_specs=[tok(768),
                   pl.BlockSpec((1, KV_GROUPS, TILES, HEAD_DIM, ROWS), lambda b, i: (b, 0, cur(i), 0, 0)),
                   tok(256), key(4 * LANES), val, key(2 * LANES), val, tok(256)],
        compiler_params=_cparams(("arbitrary", "arbitrary")),
        name="in_proj",
    )(x, g1.reshape(1, D), sc, sh, w_big)


HALO = 32


def _poolconv_kernel(cur_ref, halo_ref, pw_ref, ps_ref, cw_ref, cb_ref, lg_ref, lb_ref, o_ref,
                     ext_ref, v_ref, pa_ref, pb_ref, vsh_ref):
    ti = pl.program_id(1)
    ts = cur_ref.shape[1]
    rows = HALO + ts
    halo = halo_ref[0] * (ti > 0).astype(F32)
    ext_ref[0:HALO, :] = halo
    ext_ref[HALO:rows, :] = cur_ref[0]
    u = ext_ref[HALO:rows, 0:POOL_DIM]
    lane = lax.broadcasted_iota(I32, (ts, POOL_DIM), 1)
    grp = lane // (POOL_DIM // POOL_GROUPS)
    pa_ref[8:rows, :] = ext_ref[8:rows, 0:POOL_DIM] + ext_ref[7:rows - 1, 0:POOL_DIM]
    pooled = pa_ref[HALO:rows, :]
    pb_ref[16:rows, :] = pa_ref[16:rows, :] + pa_ref[14:rows - 2, :]
    pooled = jnp.where(grp >= 1, pb_ref[HALO:rows, :], pooled)
    pa_ref[24:rows, :] = pb_ref[24:rows, :] + pb_ref[20:rows - 4, :]
    pooled = jnp.where(grp >= 2, pa_ref[HALO:rows, :], pooled)
    pooled = jnp.where(grp == 3, pa_ref[HALO:rows, :] + pa_ref[HALO - 8:rows - 8, :], pooled)
    wlane = jnp.where(grp == 0, 2.0, jnp.where(grp == 1, 4.0, jnp.where(grp == 2, 8.0, 16.0)))
    t1 = (ti * ts + lax.broadcasted_iota(I32, (ts, POOL_DIM), 0) + 1).astype(F32)
    cnt = jnp.minimum(t1, wlane)
    pooled = pooled / cnt - u
    y_pool = jnp.dot(pooled.astype(BF16), pw_ref[...], preferred_element_type=F32) * ps_ref[...]
    o_ref[0, :, 0:POOL_DIM] = y_pool.astype(BF16)
    uv = ext_ref[:, POOL_DIM:POOL_DIM + CONV_DIM]
    ug = ext_ref[:, POOL_DIM + CONV_DIM:POOL_DIM + 2 * CONV_DIM]
    v_ref[...] = uv * jax.nn.sigmoid(ug)
    for b in range(1, 8):
        vsh_ref[b - 1] = v_ref[b:rows - 8 + b, :]
    acc = jnp.zeros((ts, CONV_DIM), F32) + cb_ref[...]
    for k in range(CONV_WIDTH):
        a, b = divmod(HALO - (CONV_WIDTH - 1) + k, 8)
        tap = v_ref[8 * a:8 * a + ts, :] if b == 0 else vsh_ref[b - 1, 8 * a:8 * a + ts, :]
        acc = acc + tap * cw_ref[k:k + 1, :]
    mu = jnp.mean(acc, axis=-1, keepdims=True)
    d = acc - mu
    var = jnp.mean(d * d, axis=-1, keepdims=True)
    yn = d * lax.rsqrt(var + EPS) * lg_ref[...] + lb_ref[...]
    o_ref[0, :, POOL_DIM:POOL_DIM + CONV_DIM] = (yn * jax.nn.sigmoid(yn)).astype(BF16)


def _pool_conv(upc, pool_w_bd, pool_scale, conv_w, conv_b, ln_g, ln_b):
    B, S, W = upc.shape
    ts = TOK_TILE
    r = ts // HALO
    vec = lambda n: pl.BlockSpec((1, n), lambda b, i: (0, 0))
    return pl.pallas_call(
        _poolconv_kernel,
        out_shape=jax.ShapeDtypeStruct((B, S, POOL_DIM + CONV_DIM), BF16),
        grid=(B, S // ts),
        in_specs=[pl.BlockSpec((1, ts, W), lambda b, i: (b, i, 0)),
                  pl.BlockSpec((1, HALO, W), lambda b, i: (b, jnp.maximum(i * r - 1, 0), 0)),
                  pl.BlockSpec((POOL_DIM, POOL_DIM), lambda b, i: (0, 0)),
                  vec(POOL_DIM),
                  pl.BlockSpec((CONV_WIDTH + 1, CONV_DIM), lambda b, i: (0, 0)),
                  vec(CONV_DIM), vec(CONV_DIM), vec(CONV_DIM)],
        out_specs=pl.BlockSpec((1, ts, POOL_DIM + CONV_DIM), lambda b, i: (b, i, 0)),
        scratch_shapes=[pltpu.VMEM((HALO + ts, W), F32), pltpu.VMEM((HALO + ts, CONV_DIM), F32),
                        pltpu.VMEM((HALO + ts, POOL_DIM), F32), pltpu.VMEM((HALO + ts, POOL_DIM), F32),
                        pltpu.VMEM((7, HALO + ts - 8, CONV_DIM), F32)],
        compiler_params=_cparams(("arbitrary", "arbitrary")),
        name="pool_conv",
    )(upc, upc, pool_w_bd, pool_scale.reshape(1, -1), conv_w, conv_b.reshape(1, -1),
      ln_g.reshape(1, -1), ln_b.reshape(1, -1))


N_STREAM = 2 * KV_GROUPS
CHUNK_W = CMP_STRIDE * 2 * KV_GROUPS * HEAD_DIM


def _cmp_weights(w1_k, w1_v):
    half = CMP_STRIDE * HEAD_DIM
    cols = []
    for s in range(N_STREAM):
        w1 = w1_k if s < KV_GROUPS else w1_v
        for part in range(2):
            blk = w1[part * half:(part + 1) * half].reshape(CMP_STRIDE, 1, HEAD_DIM, CMP_HIDDEN)
            z = jnp.zeros((CMP_STRIDE, N_STREAM, HEAD_DIM, CMP_HIDDEN), w1.dtype)
            z = lax.dynamic_update_slice(z, blk, (0, s, 0, 0))
            cols.append(z.reshape(CHUNK_W, CMP_HIDDEN))
    return jnp.concatenate(cols, axis=1).astype(BF16)


def _gelu_tanh(x):
    return 0.5 * x * (1.0 + jnp.tanh(math.sqrt(2.0 / math.pi) * (x + 0.044715 * (x * x * x))))


def _compress_kernel(c_ref, w_ref, pek_ref, pev_ref, w1k_ref, w1v_ref, w2k_ref, w2vt_ref, kc_ref, vct_ref):
    r = jnp.dot(c_ref[0], w_ref[...], preferred_element_type=F32)
    ncp = r.shape[0]
    pe_k = jnp.dot(pek_ref[...], w1k_ref[...], preferred_element_type=F32)[0:1]
    pe_v = jnp.dot(pev_ref[...], w1v_ref[...], preferred_element_type=F32)[0:1]
    for s in range(N_STREAM):
        a = r[:, s * 256:s * 256 + CMP_HIDDEN]
        b = r[:, s * 256 + CMP_HIDDEN:(s + 1) * 256]
        hid = a + pltpu.roll(b, ncp - 1, 0) + (pe_k if s < KV_GROUPS else pe_v)
        act = _gelu_tanh(hid).astype(BF16)
        if s < KV_GROUPS:
            kc_ref[0, s] = jnp.dot(act, w2k_ref[...], preferred_element_type=F32).astype(BF16)
        else:
            vt = lax.dot_general(w2vt_ref[...], act, (((1,), (1,)), ((), ())), preferred_element_type=F32)
            ones_row = (lax.broadcasted_iota(I32, vt.shape, 0) == HEAD_DIM).astype(F32)
            vct_ref[0, s - KV_GROUPS] = (vt + ones_row).astype(BF16)


def _compress(kvc, wcmp, pe_k, pe_v, w1_k, w1_v, w2_k, w2_v):
    B, S, _ = kvc.shape
    ncp = S // CMP_STRIDE
    chunks = kvc.reshape(B, ncp, CHUNK_W)
    pe8 = lambda pe: jnp.broadcast_to(pe.reshape(1, -1), (8, CMP_LEN * HEAD_DIM)).astype(BF16)
    w2k = jnp.pad(w2_k, ((0, 0), (0, LANES - HEAD_DIM))).astype(BF16)
    w2vt = jnp.pad(w2_v.T, ((0, LANES - HEAD_DIM), (0, 0))).astype(BF16)
    full = lambda a: pl.BlockSpec(a.shape, lambda b: (0,) * a.ndim)
    args = (wcmp, pe8(pe_k), pe8(pe_v), w1_k.astype(BF16), w1_v.astype(BF16), w2k, w2vt)
    return pl.pallas_call(
        _compress_kernel,
        out_shape=[jax.ShapeDtypeStruct((B, KV_GROUPS, ncp, LANES), BF16),
                   jax.ShapeDtypeStruct((B, KV_GROUPS, LANES, ncp), BF16)],
        grid=(B,),
        in_specs=[pl.BlockSpec((1, ncp, CHUNK_W), lambda b: (b, 0, 0))] + [full(a) for a in args],
        out_specs=[pl.BlockSpec((1, KV_GROUPS, ncp, LANES), lambda b: (b, 0, 0, 0)),
                   pl.BlockSpec((1, KV_GROUPS, LANES, ncp), lambda b: (b, 0, 0, 0))],
        compiler_params=_cparams(("arbitrary",)),
        name="compress",
    )(chunks, *args)


def _bias_tables(rel_bias, S):
    nq = S // Q_TILE
    ncp = S // CMP_STRIDE
    rb = rel_bias.reshape(N_BUCKETS, KV_GROUPS, HPG).transpose(1, 2, 0) * LOG2E
    far = rb[:, :, N_BUCKETS - 1]
    far_hi = far.astype(BF16)
    far_lo = (far - far_hi.astype(F32)).astype(BF16)
    far_sum = far_hi.astype(F32) + far_lo.astype(F32)
    i = np.arange(Q_TILE)[None, :]

    def table(d, valid, sub):
        onehot = jax.nn.one_hot(_t5_bucket(jnp.asarray(d, I32)), N_BUCKETS, dtype=F32)
        t = jnp.einsum('rib,ghb->grhi', onehot, rb, precision=lax.Precision.HIGHEST)
        t = jnp.where(jnp.asarray(valid)[None, :, None, :], t - sub[:, None, :, None], NEG_INF)
        return t.reshape(KV_GROUPS, d.shape[0], ROWS)

    d = i - np.arange(NEAR_KEYS)[:, None] + (NEAR_KEYS - Q_TILE)
    t_near = table(d, d >= 0, far_sum)
    d = i - np.arange(WIN_KEYS)[:, None] + (WIN_KEYS - Q_TILE)
    t_win = table(d, (d >= 0) & (d < WINDOW), jnp.zeros_like(far_sum))
    c0 = (Q_TILE // CMP_STRIDE) * (nq - 1)
    d = i - CMP_STRIDE * (np.arange(c0 + ncp)[:, None] - c0) - (CMP_LEN - 1)
    t_cmp = table(d, d >= 0, jnp.zeros_like(far_sum))
    rows = jnp.zeros((KV_GROUPS, HEAD_DIM, HPG, Q_TILE), F32)
    rows = rows.at[:, 0].set(1.0)
    rows = rows.at[:, 1].set(jnp.broadcast_to(far_hi.astype(F32)[:, :, None], (KV_GROUPS, HPG, Q_TILE)))
    rows = rows.at[:, 2].set(jnp.broadcast_to(far_lo.astype(F32)[:, :, None], (KV_GROUPS, HPG, Q_TILE)))
    return t_near, t_win, t_cmp, rows.reshape(KV_GROUPS, HEAD_DIM, ROWS).astype(BF16)


def _overlap_t(S):
    ncp = S // CMP_STRIDE
    n = np.arange(ncp)[None, :]
    jb = np.arange(LANES)[:, None]
    end = n * CMP_STRIDE + CMP_LEN - 1
    start = n * CMP_STRIDE
    ov = (end >= jb * SEL_BLOCK) & (start < (jb + 1) * SEL_BLOCK) & (n < ncp - 1)
    return jnp.asarray(ov.astype(np.float32), BF16)


def _key_tiles(ref, k0, n):
    t0 = k0 // LANES
    return jnp.concatenate([ref[0, 0, t0 + u] for u in range(n)], axis=1)


def _attn_kernel(qt_ref, qc_ref, kc_ref, vct_ref, ks_ref, vs_ref, kw_ref, vw_ref, gt_ref, tn_ref, tw_ref, tct_ref,
                 ovt_ref, o_ref, qa_ref, acc_ref, *, c0):
    qi = pl.program_id(2)
    qt = qt_ref[0, 0, 0]
    ncp = kc_ref.shape[2]

    k1 = Q_TILE * qi + (KEY_PAD + Q_TILE)

    qa_ref[0:HEAD_DIM, :] = qt
    qa_ref[HEAD_DIM:LANES, :] = qc_ref[0]

    r0 = pl.multiple_of(c0 - (Q_TILE // CMP_STRIDE) * qi, 8)
    st = (jnp.dot(kc_ref[0, 0][:, 0:HEAD_DIM], qt, preferred_element_type=F32)
          + tct_ref[0, pl.ds(r0, ncp), :])
    w0 = pl.multiple_of(k1 - WIN_KEYS, LANES)
    sw = jnp.dot(kw_ref[0, pl.ds(w0, WIN_KEYS), :], qa_ref[0:LANES, :], preferred_element_type=F32) + tw_ref[0]

    mc = jnp.maximum(jnp.max(st, axis=0, keepdims=True), -1e20)
    pc = jnp.exp2(st - mc).astype(BF16)
    acc_c = jnp.dot(vct_ref[0, 0], pc, preferred_element_type=F32)
    inv_c = 1.0 / jnp.maximum(acc_c[HEAD_DIM:HEAD_DIM + 1], 1e-30)
    o_c = acc_c[0:HEAD_DIM] * inv_c

    imp = None
    for h in range(HPG):
        cols = slice(h * Q_TILE, (h + 1) * Q_TILE)
        part = jnp.dot(ovt_ref[...], pc[:, cols], preferred_element_type=F32) * inv_c[:, cols]
        imp = part if imp is None else imp + part

    def flash_step(carry, s, vt):
        m, acc = carry
        m_new = jnp.maximum(m, jnp.max(s, axis=0, keepdims=True))
        alpha = jnp.exp2(m - m_new)
        p = jnp.exp2(s - m_new)
        acc = acc * alpha + jnp.dot(vt, p.astype(BF16), preferred_element_type=F32)
        return m_new, acc

    def normalised(acc):
        return acc[0:HEAD_DIM] * (1.0 / acc[HEAD_DIM:HEAD_DIM + 1])

    p = jnp.exp2(sw - jnp.max(sw, axis=0, keepdims=True)).astype(BF16)
    o_w = normalised(jnp.dot(_key_tiles(vw_ref, w0, WIN_KEYS // LANES), p, preferred_element_type=F32))

    jb = lax.broadcasted_iota(I32, (LANES, Q_TILE), 0)
    ii = lax.broadcasted_iota(I32, (LANES, Q_TILE), 1)
    cur = (Q_TILE // SEL_BLOCK) * qi + (ii >= SEL_BLOCK).astype(I32)
    forced = (jb == 0) | (jb == cur) | (jb == cur - 1)
    score = jnp.where(jb <= cur, jnp.where(forced, -jnp.inf, imp), NEG_INF)
    for _ in range(SEL_TOPN - N_FORCED):
        mx = jnp.max(score, axis=0, keepdims=True)
        first = jnp.min(jnp.where(score == mx, jb, LANES), axis=0, keepdims=True)
        score = jnp.where(jb == first, -jnp.inf, score)
    mbt = jnp.where(score == -jnp.inf, 0.0, MASK_BIAS).astype(BF16)
    for h in range(HPG):
        qa_ref[LANES:2 * LANES, h * Q_TILE:(h + 1) * Q_TILE] = mbt

    n0 = pl.multiple_of(k1 - NEAR_KEYS, LANES)
    half = NEAR_KEYS // 2
    sa = (jnp.dot(ks_ref[0, pl.ds(n0, half), :], qa_ref[...], preferred_element_type=F32)
          + tn_ref[0, 0:half, :])
    sb = (jnp.dot(ks_ref[0, pl.ds(n0 + half, half), :], qa_ref[...], preferred_element_type=F32)
          + tn_ref[0, half:NEAR_KEYS, :])
    m_near = jnp.maximum(jnp.max(sa, axis=0, keepdims=True), jnp.max(sb, axis=0, keepdims=True))
    pa = jnp.exp2(sa - m_near).astype(BF16)
    da = jnp.dot(_key_tiles(vs_ref, n0, half // LANES), pa, preferred_element_type=F32)
    pb = jnp.exp2(sb - m_near).astype(BF16)
    acc_near = da + jnp.dot(_key_tiles(vs_ref, n0 + half, half // LANES), pb, preferred_element_type=F32)
    n_pairs = (jnp.maximum(n0 - KEY_PAD, 0) + KEY_PAD - 1) // KEY_PAD

    def far_scores(j):
        k0 = pl.multiple_of(n0 - FAR_TILE * (j + 1), LANES)
        return jnp.dot(ks_ref[0, pl.ds(k0, FAR_TILE), :], qa_ref[...], preferred_element_type=F32)

    def far_values(j):
        return _key_tiles(vs_ref, pl.multiple_of(n0 - FAR_TILE * (j + 1), LANES), FAR_TILE // LANES)

    def fast_body(jj, top):
        tiles = [FAR_GROUP * jj + u for u in range(FAR_GROUP)]
        scores = [far_scores(tiles[0]), far_scores(tiles[1])]
        acc = None
        for u, j in enumerate(tiles):
            if u + 2 < FAR_GROUP:
                scores.append(far_scores(tiles[u + 2]))
            p = jnp.exp2(scores[u] - m_near).astype(BF16)
            d = jnp.dot(far_values(j), p, preferred_element_type=F32)
            acc = d if acc is None else acc + d
            top = jnp.maximum(top, jnp.max(scores[u], axis=0, keepdims=True))
        acc_ref[...] += acc
        return top

    acc_ref[...] = acc_near
    top = lax.fori_loop(0, n_pairs, fast_body, m_near)

    @pl.when(jnp.max(top - m_near) > FAR_HEADROOM)
    def _():
        def safe_body(j, carry):
            return flash_step(carry, far_scores(j), far_values(j))

        acc_ref[...] = lax.fori_loop(0, FAR_GROUP * n_pairs, safe_body, (m_near, acc_near))[1]

    o_s = normalised(acc_ref[...])

    gtt = gt_ref[0].T
    gate = lambda br: jnp.concatenate([gtt[br * HPG + h:br * HPG + h + 1, :] for h in range(HPG)], axis=1)
    o_ref[0, 0, 0] = (gate(0) * o_c + gate(1) * o_s + gate(2) * o_w).astype(BF16)


def _attention(qt, qc, kc, vct, ks, vs, kw, vw, gt, tn, tw, tct, ovt):
    B, G, nq, _, _ = qt.shape
    S = nq * Q_TILE
    sp = S + KEY_PAD
    ncp = S // CMP_STRIDE
    c0 = (Q_TILE // CMP_STRIDE) * (nq - 1)
    per_g = lambda a: pl.BlockSpec((1,) + a.shape[1:], lambda b, g, i: (g,) + (0,) * (a.ndim - 1))
    val = pl.BlockSpec((1, 1, sp // LANES, LANES, LANES), lambda b, g, i: (b, g, 0, 0, 0))
    return pl.pallas_call(
        functools.partial(_attn_kernel, c0=c0),
        out_shape=jax.ShapeDtypeStruct((B, G, nq, HEAD_DIM, ROWS), BF16),
        grid=(B, G, nq),
        in_specs=[pl.BlockSpec((1, 1, 1, HEAD_DIM, ROWS), lambda b, g, i: (b, g, i, 0, 0)),
                  per_g(qc),
                  pl.BlockSpec((1, 1, ncp, LANES), lambda b, g, i: (b, g, 0, 0)),
                  pl.BlockSpec((1, 1, LANES, ncp), lambda b, g, i: (b, g, 0, 0)),
                  pl.BlockSpec((1, sp, 2 * LANES), lambda b, g, i: (b, 0, g)),
                  val,
                  pl.BlockSpec((1, sp, LANES), lambda b, g, i: (b, 0, g)),
                  val,
                  pl.BlockSpec((1, Q_TILE, LANES), lambda b, g, i: (b, i, g)),
                  per_g(tn), per_g(tw), per_g(tct),
                  pl.BlockSpec((LANES, ncp), lambda b, g, i: (0, 0))],
        out_specs=pl.BlockSpec((1, 1, 1, HEAD_DIM, ROWS), lambda b, g, i: (b, g, i, 0, 0)),
        scratch_shapes=[pltpu.VMEM((2 * LANES, ROWS), BF16), pltpu.VMEM((LANES, ROWS), F32)],
        compiler_params=_cparams(("arbitrary", "arbitrary", "arbitrary")),
        name="nsa_attention",
    )(qt, qc, kc, vct, ks, vs, kw, vw, gt, tn, tw, tct, ovt)


def _post_attn_kernel(x_ref, ypc_ref, yn_ref, wo_ref, g1_ref, n2_ref, sc_ref, sh_ref, rwh_ref, rwl_ref, rb_ref,
                      xo_ref, h2_ref, ei_ref, tw_ref, cnt_ref, run_ref, hd_ref):
    first = (pl.program_id(0) == 0) & (pl.program_id(1) == 0)

    @pl.when(first)
    def _():
        run_ref[...] = jnp.zeros_like(run_ref)

    tm = x_ref.shape[1]
    half = wo_ref.shape[0] // 2
    tiles = []
    for t in range(TILES):
        groups = []
        for g in range(KV_GROUPS):
            blk = yn_ref[0, g, t].astype(F32)
            for hh in range(HPG):
                hd_ref[hh * HEAD_DIM:(hh + 1) * HEAD_DIM, :] = blk[:, hh * Q_TILE:(hh + 1) * Q_TILE]
            groups.append(hd_ref[...].T)
        tiles.append(jnp.concatenate(groups, axis=1))
    ynsa = jnp.concatenate(tiles, axis=0).astype(BF16)
    mixed = (jnp.dot(ypc_ref[0], wo_ref[0:half, :], preferred_element_type=F32)
             + jnp.dot(ynsa, wo_ref[half:, :], preferred_element_type=F32))
    x = x_ref[0] + g1_ref[0] * mixed
    xo_ref[0] = x
    y = x * lax.rsqrt(jnp.mean(x * x, axis=-1, keepdims=True) + EPS) * n2_ref[...]
    h2 = y * (1.0 + sc_ref[0]) + sh_ref[0]
    _rows_to_tiles(h2_ref, h2)
    hh = h2.astype(BF16)
    hl = (h2 - hh.astype(F32)).astype(BF16)
    logit = (jnp.dot(hh, rwh_ref[...], preferred_element_type=F32)
             + jnp.dot(hl, rwh_ref[...], preferred_element_type=F32)
             + jnp.dot(hh, rwl_ref[...], preferred_element_type=F32)) + rb_ref[...]
    lane = lax.broadcasted_iota(I32, (tm, LANES), 1)
    vals, hots, idxs = [], [], []
    for _ in range(TOP_K):
        mx = jnp.max(logit, axis=1, keepdims=True)
        idx = jnp.min(jnp.where(logit == mx, lane, LANES), axis=1, keepdims=True)
        hot = lane == idx
        vals.append(mx)
        hots.append(hot)
        idxs.append(idx)
        logit = jnp.where(hot, -jnp.inf, logit)
    ex = [jnp.exp(v - vals[0]) for v in vals]
    inv = 1.0 / (ex[0] + ex[1] + ex[2] + ex[3])
    assign = (hots[0] | hots[1] | hots[2] | hots[3]).astype(BF16)
    r = lax.broadcasted_iota(I32, (tm, tm), 0)
    c = lax.broadcasted_iota(I32, (tm, tm), 1)
    before = jnp.dot((c < r).astype(BF16), assign, preferred_element_type=F32) + run_ref[...]
    ei = jnp.zeros((tm, LANES), I32)
    tw = jnp.zeros((tm, LANES), F32)
    for k in range(TOP_K):
        e_k = idxs[k]
        r_k = jnp.sum(jnp.where(hots[k], before, 0.0), axis=1, keepdims=True).astype(I32)
        ei = jnp.where(lane == k, e_k, jnp.where(lane == TOP_K + k, r_k, ei))
        tw = jnp.where(lane == k, ex[k] * inv, tw)
    ei_ref[0] = ei
    tw_ref[0] = tw
    run_ref[...] = run_ref[...] + jnp.sum(assign.astype(F32), axis=0, keepdims=True)
    cnt_ref[...] = run_ref[...]


def _post_attn(x, ypc, ynsa, w_out, g1, n2g, sc2, sh2, rw_hi, rw_lo, rb):
    B, S, D = x.shape
    tm = TOK_TILE
    tok = lambda w: pl.BlockSpec((1, tm, w), lambda b, i: (b, i, 0))
    per_b = pl.BlockSpec((1, 1, D), lambda b, i: (b, 0, 0))
    full = lambda a: pl.BlockSpec(a.shape, lambda b, i: (0,) * a.ndim)
    return pl.pallas_call(
        _post_attn_kernel,
        out_shape=[jax.ShapeDtypeStruct((B, S, D), F32), jax.ShapeDtypeStruct((B * S * ROW_TILES, LANES), F32),
                   jax.ShapeDtypeStruct((B, S, LANES), I32), jax.ShapeDtypeStruct((B, S, LANES), F32),
                   jax.ShapeDtypeStruct((1, LANES), F32)],
        grid=(B, S // tm),
        in_specs=[tok(D), tok(ypc.shape[-1]),
                  pl.BlockSpec((1, KV_GROUPS, TILES, HEAD_DIM, ROWS), lambda b, i: (b, 0, i, 0, 0)),
                  full(w_out), per_b,
                  pl.BlockSpec((1, D), lambda b, i: (0, 0)), per_b, per_b,
                  full(rw_hi), full(rw_lo), full(rb)],
        out_specs=[tok(D), pl.BlockSpec((tm * ROW_TILES, LANES), lambda b, i: (b * (S // tm) + i, 0)),
                   tok(LANES), tok(LANES), pl.BlockSpec((1, LANES), lambda b, i: (0, 0))],
        scratch_shapes=[pltpu.VMEM((1, LANES), F32), pltpu.VMEM((HPG * HEAD_DIM, Q_TILE), F32)],
        compiler_params=_cparams(("arbitrary", "arbitrary")),
        name="post_attn_router",
    )(x, ypc, ynsa, w_out, g1, n2g.reshape(1, D), sc2, sh2, rw_hi, rw_lo, rb)


ROW_TILES = 8
DMA_UNROLL = 2
ZERO_BITS = tuple(1 << b for b in reversed(range((EXPERT_BLOCK - 1).bit_length())))


def _rows_to_tiles(ref, val):
    n = val.shape[0]
    for s in range(ROW_TILES):
        ref[pl.ds(s, n, stride=ROW_TILES), :] = val[:, s * LANES:(s + 1) * LANES]


def _tiles_to_rows(ref, n):
    return jnp.concatenate([ref[pl.ds(s, n, stride=ROW_TILES), :] for s in range(ROW_TILES)], axis=1)


def _tile_at(ref, i):
    return ref.at[pl.ds(pl.multiple_of(i * ROW_TILES, ROW_TILES), ROW_TILES), :]


def _dispatch_kernel(fill0_ref, filln_ref, dest_ref, h_ref, xs_ref, zero_ref, sem, zsem):
    tm = h_ref.shape[0] // ROW_TILES

    @pl.when(pl.program_id(0) == 0)
    def _():
        zero_ref[...] = jnp.zeros_like(zero_ref)

        def expert(e, c):
            n = filln_ref[e]
            for wait in (False, True):
                for bit in ZERO_BITS:
                    @pl.when((n & bit) != 0)
                    def _():
                        first = fill0_ref[e] + (n & ~(2 * bit - 1))
                        dst = xs_ref.at[pl.ds(pl.multiple_of(first * ROW_TILES, ROW_TILES), bit * ROW_TILES), :]
                        cp = pltpu.make_async_copy(zero_ref.at[pl.ds(0, bit * ROW_TILES), :], dst, zsem)
                        cp.wait() if wait else cp.start()
            return c

        lax.fori_loop(0, N_EXPERTS, expert, 0)

    def body(i, c):
        for u in range(DMA_UNROLL):
            t = i * DMA_UNROLL + u
            src = _tile_at(h_ref, t)
            for k in range(TOP_K):
                pltpu.make_async_copy(src, _tile_at(xs_ref, dest_ref[0, 0, t * TOP_K + k]), sem).start(
                    priority=k % 2)
        return c

    lax.fori_loop(0, tm // DMA_UNROLL, body, 0)
    for _ in range(TOP_K):
        pltpu.make_async_copy(h_ref, xs_ref.at[pl.ds(0, tm * ROW_TILES), :], sem).wait()


def _dispatch(dest, h2t, fill0, filln, n_slots):
    tm = TOK_TILE
    nt = h2t.shape[0] // (tm * ROW_TILES)
    return pl.pallas_call(
        _dispatch_kernel,
        out_shape=jax.ShapeDtypeStruct((n_slots * ROW_TILES, LANES), F32),
        grid_spec=pltpu.PrefetchScalarGridSpec(
            num_scalar_prefetch=2,
            grid=(nt,),
            in_specs=[pl.BlockSpec((1, 1, tm * TOP_K), lambda i, f0, fn: (i, 0, 0), memory_space=pltpu.SMEM),
                      pl.BlockSpec((tm * ROW_TILES, LANES), lambda i, f0, fn: (i, 0))],
            out_specs=pl.BlockSpec(memory_space=pl.ANY),
            scratch_shapes=[pltpu.VMEM((ZERO_BITS[0] * ROW_TILES, LANES), F32), pltpu.SemaphoreType.DMA(()),
                            pltpu.SemaphoreType.DMA(())]),
        compiler_params=_cparams(("arbitrary",)),
        name="moe_dispatch",
    )(fill0, filln, dest.reshape(nt, 1, tm * TOP_K), h2t)


W_CHUNK = 512


def _expert_kernel(be_ref, nu_ref, x_ref, wgu_ref, bgu_ref, wd_ref, bd_ref, y_ref, wgu_s, wd_s):
    i = pl.program_id(0)

    @pl.when(i < nu_ref[0])
    def _():
        @pl.when((i == 0) | (be_ref[i] != be_ref[jnp.maximum(i - 1, 0)]))
        def _():
            for c in range(0, wgu_s.shape[1], W_CHUNK):
                wgu_s[:, c:c + W_CHUNK] = wgu_ref[0, :, c:c + W_CHUNK].astype(BF16)
            for c in range(0, wd_s.shape[1], W_CHUNK):
                wd_s[:, c:c + W_CHUNK] = wd_ref[0, :, c:c + W_CHUNK].astype(BF16)

        F = wd_s.shape[0]
        x = _tiles_to_rows(x_ref, EXPERT_BLOCK).astype(BF16)
        gu = jnp.dot(x, wgu_s[...], preferred_element_type=F32) + bgu_ref[0]
        gate = jnp.minimum(gu[:, :F], SWIGLU_LIMIT)
        up = jnp.clip(gu[:, F:], -SWIGLU_LIMIT, SWIGLU_LIMIT)
        act = (up + 1.0) * gate * jax.nn.sigmoid(SWIGLU_ALPHA * gate)
        _rows_to_tiles(y_ref, jnp.dot(act.astype(BF16), wd_s[...], preferred_element_type=F32) + bd_ref[0])


def _experts(blk_e, n_used, xs, layer, w_gu, b_gu, w_down, b_down):
    L, E, D, F2 = w_gu.shape
    F = F2 // 2
    rows = EXPERT_BLOCK * ROW_TILES
    nb = xs.shape[0] // rows
    blk = lambda i, be, nu: (jnp.minimum(i, nu[0] - 1), 0)
    per_e = lambda i, be, nu: (layer, be[i], 0, 0)
    return pl.pallas_call(
        _expert_kernel,
        out_shape=jax.ShapeDtypeStruct(xs.shape, F32),
        grid_spec=pltpu.PrefetchScalarGridSpec(
            num_scalar_prefetch=2,
            grid=(nb,),
            in_specs=[pl.BlockSpec((rows, LANES), blk),
                      pl.BlockSpec((None, 1, D, F2), per_e),
                      pl.BlockSpec((None, 1, 1, F2), per_e),
                      pl.BlockSpec((None, 1, F, D), per_e),
                      pl.BlockSpec((None, 1, 1, D), per_e)],
            out_specs=pl.BlockSpec((rows, LANES), blk),
            scratch_shapes=[pltpu.VMEM((D, F2), BF16), pltpu.VMEM((F, D), BF16)]),
        compiler_params=_cparams(("arbitrary",)),
        name="moe_experts",
    )(blk_e, n_used, xs, w_gu, b_gu.reshape(L, E, 1, F2), w_down, b_down.reshape(L, E, 1, D))


def _combine_kernel(dest_ref, y_ref, x_ref, tw_ref, g2_ref, fg_ref, o_ref, rows_ref, sem, *, final):
    tm = x_ref.shape[1]

    def body(i, c):
        for u in range(DMA_UNROLL):
            t = i * DMA_UNROLL + u
            for k in range(TOP_K):
                pltpu.make_async_copy(_tile_at(y_ref, dest_ref[0, 0, t * TOP_K + k]),
                                      _tile_at(rows_ref.at[k], t), sem).start(priority=k % 2)
        return c

    lax.fori_loop(0, tm // DMA_UNROLL, body, 0)
    for k in range(TOP_K):
        pltpu.make_async_copy(y_ref.at[pl.ds(0, tm * ROW_TILES), :], rows_ref.at[k], sem).wait()
    tw = tw_ref[0]
    moe = tw[:, 0:1] * _tiles_to_rows(rows_ref.at[0], tm)
    for k in range(1, TOP_K):
        moe = moe + tw[:, k:k + 1] * _tiles_to_rows(rows_ref.at[k], tm)
    x = x_ref[0] + g2_ref[0] * moe
    if final:
        x = x * lax.rsqrt(jnp.mean(x * x, axis=-1, keepdims=True) + EPS) * fg_ref[...]
    o_ref[0] = x


def _combine(dest, y, x, tw, g2, final_g, final):
    B, S, D = x.shape
    tm = TOK_TILE
    nt = S // tm
    tok = lambda w: pl.BlockSpec((1, tm, w), lambda b, i: (b, i, 0))
    return pl.pallas_call(
        functools.partial(_combine_kernel, final=final),
        out_shape=jax.ShapeDtypeStruct((B, S, D), F32),
        grid=(B, nt),
        in_specs=[pl.BlockSpec((1, 1, tm * TOP_K), lambda b, i: (b * nt + i, 0, 0), memory_space=pltpu.SMEM),
                  pl.BlockSpec(memory_space=pl.ANY),
                  tok(D), tok(LANES),
                  pl.BlockSpec((1, 1, D), lambda b, i: (b, 0, 0)),
                  pl.BlockSpec((1, D), lambda b, i: (0, 0))],
        out_specs=tok(D),
        scratch_shapes=[pltpu.VMEM((TOP_K, tm * ROW_TILES, LANES), F32), pltpu.SemaphoreType.DMA(())],
        compiler_params=_cparams(("arbitrary", "arbitrary")),
        name="moe_combine",
    )(dest.reshape(B * nt, 1, tm * TOP_K), y, x, tw, g2, final_g.reshape(1, D))


def _moe(x, h2, ei, tw, counts, g2, layer, w_gu, b_gu, w_down, b_down, final_g, final):
    B, S, D = x.shape
    N = B * S
    n_slots = -(-(N * TOP_K + N_EXPERTS * EXPERT_BLOCK) // EXPERT_BLOCK) * EXPERT_BLOCK
    nb = n_slots // EXPERT_BLOCK
    cnt = counts[0, :N_EXPERTS].astype(I32)
    padded = (cnt + EXPERT_BLOCK - 1) // EXPERT_BLOCK * EXPERT_BLOCK
    pend = jnp.cumsum(padded)
    pstart = pend - padded
    ei2 = ei.reshape(N, LANES)
    dest = (pstart[ei2[:, 0:TOP_K]] + ei2[:, TOP_K:2 * TOP_K]).reshape(N * TOP_K)
    blk_start = jnp.arange(nb, dtype=I32) * EXPERT_BLOCK
    blk_e = jnp.minimum(jnp.sum((pend[None, :] <= blk_start[:, None]).astype(I32), axis=1), N_EXPERTS - 1)
    n_used = (pend[-1:] // EXPERT_BLOCK).astype(I32)
    xs = _dispatch(dest, h2, pstart + cnt, padded - cnt, n_slots)
    y = _experts(blk_e, n_used, xs, layer, w_gu, b_gu, w_down, b_down)
    return _combine(dest, y, x, tw, g2, final_g, final)


def kernel(x, c, w_mod, b_mod, norm1_g, norm2_g, w_in, w_out, pool_w, pool_scale, conv_w, conv_b, conv_ln_g,
           conv_ln_b, cmp_pe_k, cmp_pe_v, cmp_w1_k, cmp_w2_k, cmp_w1_v, cmp_w2_v, rel_bias, router_w, router_b,
           expert_w_gu, expert_b_gu, expert_w_down, expert_b_down, final_g):
    B, S, D = x.shape
    L = w_mod.shape[0]
    assert S % TOK_TILE == 0 and S // SEL_BLOCK <= LANES and D == ROW_TILES * LANES
    mod = _modulation(c, w_mod, b_mod)
    t_near, t_win, t_cmp, qc = _bias_tables(rel_bias, S)
    ovt = _overlap_t(S)
    w_big = _in_weight(w_in)
    cg = POOL_DIM // POOL_GROUPS
    for l in range(L):
        m6 = mod[l].reshape(B, 6, 1, D)
        sh1, sc1, g1, sh2, sc2, g2 = (m6[:, k] for k in range(6))
        upc, qt, kvc, ks, vs, kw, vw, gt = _in_proj(x, norm1_g[l], sc1, sh1, w_big[l])
        pw_bd = jnp.zeros((POOL_DIM, POOL_DIM), F32)
        for g in range(POOL_GROUPS):
            pw_bd = lax.dynamic_update_slice(pw_bd, pool_w[l, g], (g * cg, g * cg))
        cw = jnp.pad(conv_w[l], ((0, 1), (0, 0)))
        ypc = _pool_conv(upc, pw_bd.astype(BF16), pool_scale[l], cw, conv_b[l], conv_ln_g[l], conv_ln_b[l])
        kc, vct = _compress(kvc, _cmp_weights(cmp_w1_k[l], cmp_w1_v[l]), cmp_pe_k[l], cmp_pe_v[l],
                            cmp_w1_k[l], cmp_w1_v[l], cmp_w2_k[l], cmp_w2_v[l])
        ynsa = _attention(qt, qc, kc, vct, ks, vs, kw, vw, gt, t_near, t_win, t_cmp, ovt)
        rw = jnp.pad(router_w[l], ((0, 0), (0, LANES - N_EXPERTS)))
        rw_hi = rw.astype(BF16)
        rw_lo = (rw - rw_hi.astype(F32)).astype(BF16)
        rb = jnp.pad(router_b[l].reshape(1, -1), ((0, 0), (0, LANES - N_EXPERTS)), constant_values=NEG_INF)
        x, h2, ei, tw, counts = _post_attn(x, ypc, ynsa, w_out[l].astype(BF16), g1, norm2_g[l], sc2, sh2,
                                           rw_hi, rw_lo, rb)
        x = _moe(x, h2, ei, tw, counts, g2, l, expert_w_gu, expert_b_gu, expert_w_down, expert_b_down,
                 final_g, final=(l == L - 1))
    return x
```

```python
import functools
import math

import jax
import jax.numpy as jnp
import numpy as np
from jax import lax
from jax.experimental import pallas as pl
from jax.experimental.pallas import tpu as pltpu

F32 = jnp.float32
BF16 = jnp.bfloat16
I32 = jnp.int32

HEAD_DIM = 64
POOL_DIM = 256
POOL_GROUPS = 4
POOL_WINDOWS = (2, 4, 8, 16)
CONV_DIM = 256
CONV_WIDTH = 31
NSA_DIM = 512
NSA_HEADS = 8
KV_GROUPS = 2
HPG = 4
CMP_LEN = 32
CMP_STRIDE = 16
CMP_HIDDEN = 128
SEL_BLOCK = 64
SEL_TOPN = 16
N_FORCED = 3
WINDOW = 512
Q_TILE = 128
N_BUCKETS = 32
MAX_DISTANCE = 1024
N_EXPERTS = 32
TOP_K = 4
SWIGLU_ALPHA = 1.702
SWIGLU_LIMIT = 7.0
EPS = 1e-5
NEG_INF = -1e30
FORCE_SCORE = 1e4

LANES = 128
ROWS = HPG * Q_TILE
LOG2E = 1.4426950408889634
FAR_HEADROOM = 100.0
NEAR_KEYS = 1024
WIN_KEYS = WINDOW + Q_TILE
FAR_TILE = 512
FAR_GROUP = 4
KEY_PAD = FAR_GROUP * FAR_TILE
MASK_BIAS = -32768.0
TOK_TILE = 512
EXPERT_BLOCK = 512
VMEM_LIMIT = 56 * 1024 * 1024


def _cparams(sem, vmem=VMEM_LIMIT):
    return pltpu.CompilerParams(dimension_semantics=sem, vmem_limit_bytes=vmem)


def _t5_bucket(n):
    n = jnp.maximum(n, 0)
    max_exact = N_BUCKETS // 2
    nf = jnp.maximum(n, 1).astype(F32)
    large = max_exact + (jnp.log(nf / max_exact) / math.log(MAX_DISTANCE / max_exact)
                         * (N_BUCKETS - max_exact)).astype(I32)
    large = jnp.minimum(large, N_BUCKETS - 1)
    return jnp.where(n < max_exact, n, large)


def _mod_kernel(c_ref, w_ref, b_ref, o_ref):
    c = c_ref[...]
    cond = c * jax.nn.sigmoid(c)
    o_ref[0] = jnp.dot(cond.astype(BF16), w_ref[0].astype(BF16),
                       preferred_element_type=F32) + b_ref[0]


def _modulation(c, w_mod, b_mod):
    L, D, W = w_mod.shape
    B = c.shape[0]
    tn = 1536
    return pl.pallas_call(
        _mod_kernel,
        out_shape=jax.ShapeDtypeStruct((L, B, W), F32),
        grid=(L, W // tn),
        in_specs=[pl.BlockSpec((B, D), lambda l, j: (0, 0)),
                  pl.BlockSpec((1, D, tn), lambda l, j: (l, 0, j)),
                  pl.BlockSpec((1, 1, tn), lambda l, j: (l, 0, j))],
        out_specs=pl.BlockSpec((1, B, tn), lambda l, j: (l, 0, j)),
        compiler_params=_cparams(("arbitrary", "arbitrary")),
        name="modulation",
    )(c, w_mod, b_mod.reshape(L, 1, W))


C_UPC, C_Q, C_KVC, C_KS, C_VS, C_KW, C_VW, C_GT, C_END = 0, 768, 1280, 1536, 2048, 2304, 2560, 2816, 3072


def _in_weight(w_in):
    col = lambda a, n: w_in[:, :, a:a + n]
    zero = lambda n: jnp.zeros(w_in.shape[:2] + (n,), w_in.dtype)
    parts = [col(0, 768), col(768, 512) * (HEAD_DIM ** -0.5 * LOG2E), col(1280, 256)]
    for g in range(KV_GROUPS):
        parts += [col(1536 + g * HEAD_DIM, HEAD_DIM), zero(2 * LANES - HEAD_DIM)]
    for base in (1664, 1792, 1920):
        for g in range(KV_GROUPS):
            parts += [col(base + g * HEAD_DIM, HEAD_DIM), zero(LANES - HEAD_DIM)]
    for g in range(KV_GROUPS):
        parts += [col(2048 + br * NSA_HEADS + g * HPG, HPG) for br in range(3)] + [zero(LANES - 3 * HPG)]
    return jnp.concatenate(parts, axis=2).astype(BF16)


PAD_STEPS = KEY_PAD // TOK_TILE
TILES = TOK_TILE // Q_TILE


def _inproj_kernel(x_ref, g_ref, sc_ref, sh_ref, w_ref,
                   upc_ref, qt_ref, kvc_ref, ks_ref, vs_ref, kw_ref, vw_ref, gt_ref):
    step = pl.program_id(1)
    tm = x_ref.shape[1]

    @pl.when(step < PAD_STEPS)
    def _():
        lane = lax.broadcasted_iota(I32, (tm, 4 * LANES), 1)
        ks_ref[0] = jnp.where(lane % (2 * LANES) == HEAD_DIM, MASK_BIAS, 0.0).astype(BF16)
        lane = lax.broadcasted_iota(I32, (tm, 2 * LANES), 1)
        kw_ref[0] = jnp.where(lane % LANES == HEAD_DIM, MASK_BIAS, 0.0).astype(BF16)
        for g in range(KV_GROUPS):
            for t in range(TILES):
                vs_ref[0, g, t] = jnp.zeros((LANES, LANES), BF16)
                vw_ref[0, g, t] = jnp.zeros((LANES, LANES), BF16)

    @pl.when(step >= PAD_STEPS)
    def _():
        ti = step - PAD_STEPS
        x = x_ref[0]
        y = x * lax.rsqrt(jnp.mean(x * x, axis=-1, keepdims=True) + EPS) * g_ref[...]
        h = y * (1.0 + sc_ref[0]) + sh_ref[0]
        z = jnp.dot(h.astype(BF16), w_ref[...], preferred_element_type=F32)
        upc_ref[0] = z[:, C_UPC:C_Q]
        kvc_ref[0] = z[:, C_KVC:C_KS].astype(BF16)
        gt_ref[0] = jax.nn.sigmoid(z[:, C_GT:C_END])
        zq = z[:, C_Q:C_KVC].T
        for g in range(KV_GROUPS):
            for t in range(TILES):
                qt_ref[0, g, t] = jnp.concatenate(
                    [zq[(g * HPG + hh) * HEAD_DIM:(g * HPG + hh + 1) * HEAD_DIM, t * Q_TILE:(t + 1) * Q_TILE]
                     for hh in range(HPG)], axis=1).astype(BF16)
        lane = lax.broadcasted_iota(I32, (tm, 2 * LANES), 1)
        row = lax.broadcasted_iota(I32, (tm, 2 * LANES), 0)
        blk = (ti * tm + row) // SEL_BLOCK
        onehot = ((lane - LANES == blk) | (lane == HEAD_DIM + 1) | (lane == HEAD_DIM + 2)).astype(F32)
        for g in range(KV_GROUPS):
            ks_ref[0, :, g * 256:(g + 1) * 256] = (z[:, C_KS + g * 256:C_KS + (g + 1) * 256] + onehot).astype(BF16)
        kw_ref[0] = z[:, C_KW:C_VW].astype(BF16)
        ones_col = ((lane % LANES) == HEAD_DIM).astype(F32)
        for v_ref, c in ((vs_ref, C_VS), (vw_ref, C_VW)):
            zv = (z[:, c:c + 2 * LANES] + ones_col).T
            for g in range(KV_GROUPS):
                for t in range(TILES):
                    v_ref[0, g, t] = zv[g * LANES:(g + 1) * LANES, t * Q_TILE:(t + 1) * Q_TILE].astype(BF16)


def _in_proj(x, g1, sc, sh, w_big):
    B, S, D = x.shape
    tm = TOK_TILE
    sp = S + KEY_PAD
    nq = S // Q_TILE
    cur = lambda i: jnp.maximum(i - PAD_STEPS, 0)
    tok = lambda w: pl.BlockSpec((1, tm, w), lambda b, i: (b, cur(i), 0))
    key = lambda w: pl.BlockSpec((1, tm, w), lambda b, i: (b, i, 0))
    val = pl.BlockSpec((1, KV_GROUPS, TILES, LANES, LANES), lambda b, i: (b, 0, i, 0, 0))
    val_shape = jax.ShapeDtypeStruct((B, KV_GROUPS, sp // LANES, LANES, LANES), BF16)
    return pl.pallas_call(
        _inproj_kernel,
        out_shape=[jax.ShapeDtypeStruct((B, S, 768), F32),
                   jax.ShapeDtypeStruct((B, KV_GROUPS, nq, HEAD_DIM, ROWS), BF16),
                   jax.ShapeDtypeStruct((B, S, 256), BF16),
                   jax.ShapeDtypeStruct((B, sp, 4 * LANES), BF16), val_shape,
                   jax.ShapeDtypeStruct((B, sp, 2 * LANES), BF16), val_shape,
                   jax.ShapeDtypeStruct((B, S, 256), F32)],
        grid=(B, S // tm + PAD_STEPS),
        in_specs=[tok(D),
                  pl.BlockSpec((1, D), lambda b, i: (0, 0)),
                  pl.BlockSpec((1, 1, D), lambda b, i: (b, 0, 0)),
                  pl.BlockSpec((1, 1, D), lambda b, i: (b, 0, 0)),
                  pl.BlockSpec((D, C_END), lambda b, i: (0, 0))],
        out_specs=[tok(768),
                   pl.BlockSpec((1, KV_GROUPS, TILES, HEAD_DIM, ROWS), lambda b, i: (b, 0, cur(i), 0, 0)),
                   tok(256), key(4 * LANES), val, key(2 * LANES), val, tok(256)],
        compiler_params=_cparams(("arbitrary", "arbitrary")),
        name="in_proj",
    )(x, g1.reshape(1, D), sc, sh, w_big)


HALO = 32


def _poolconv_kernel(cur_ref, halo_ref, pw_ref, ps_ref, cw_ref, cb_ref, lg_ref, lb_ref, o_ref,
                     ext_ref, v_ref, pa_ref, pb_ref, vsh_ref):
    ti = pl.program_id(1)
    ts = cur_ref.shape[1]
    rows = HALO + ts
    halo = halo_ref[0] * (ti > 0).astype(F32)
    ext_ref[0:HALO, :] = halo
    ext_ref[HALO:rows, :] = cur_ref[0]
    u = ext_ref[HALO:rows, 0:POOL_DIM]
    lane = lax.broadcasted_iota(I32, (ts, POOL_DIM), 1)
    grp = lane // (POOL_DIM // POOL_GROUPS)
    pa_ref[8:rows, :] = ext_ref[8:rows, 0:POOL_DIM] + ext_ref[7:rows - 1, 0:POOL_DIM]
    pooled = pa_ref[HALO:rows, :]
    pb_ref[16:rows, :] = pa_ref[16:rows, :] + pa_ref[14:rows - 2, :]
    pooled = jnp.where(grp >= 1, pb_ref[HALO:rows, :], pooled)
    pa_ref[24:rows, :] = pb_ref[24:rows, :] + pb_ref[20:rows - 4, :]
    pooled = jnp.where(grp >= 2, pa_ref[HALO:rows, :], pooled)
    pooled = jnp.where(grp == 3, pa_ref[HALO:rows, :] + pa_ref[HALO - 8:rows - 8, :], pooled)
    wlane = jnp.where(grp == 0, 2.0, jnp.where(grp == 1, 4.0, jnp.where(grp == 2, 8.0, 16.0)))
    t1 = (ti * ts + lax.broadcasted_iota(I32, (ts, POOL_DIM), 0) + 1).astype(F32)
    cnt = jnp.minimum(t1, wlane)
    pooled = pooled / cnt - u
    y_pool = jnp.dot(pooled.astype(BF16), pw_ref[...], preferred_element_type=F32) * ps_ref[...]
    o_ref[0, :, 0:POOL_DIM] = y_pool.astype(BF16)
    uv = ext_ref[:, POOL_DIM:POOL_DIM + CONV_DIM]
    ug = ext_ref[:, POOL_DIM + CONV_DIM:POOL_DIM + 2 * CONV_DIM]
    v_ref[...] = uv * jax.nn.sigmoid(ug)
    for b in range(1, 8):
        vsh_ref[b - 1] = v_ref[b:rows - 8 + b, :]
    acc = jnp.zeros((ts, CONV_DIM), F32) + cb_ref[...]
    for k in range(CONV_WIDTH):
        a, b = divmod(HALO - (CONV_WIDTH - 1) + k, 8)
        tap = v_ref[8 * a:8 * a + ts, :] if b == 0 else vsh_ref[b - 1, 8 * a:8 * a + ts, :]
        acc = acc + tap * cw_ref[k:k + 1, :]
    mu = jnp.mean(acc, axis=-1, keepdims=True)
    d = acc - mu
    var = jnp.mean(d * d, axis=-1, keepdims=True)
    yn = d * lax.rsqrt(var + EPS) * lg_ref[...] + lb_ref[...]
    o_ref[0, :, POOL_DIM:POOL_DIM + CONV_DIM] = (yn * jax.nn.sigmoid(yn)).astype(BF16)


def _pool_conv(upc, pool_w_bd, pool_scale, conv_w, conv_b, ln_g, ln_b):
    B, S, W = upc.shape
    ts = TOK_TILE
    r = ts // HALO
    vec = lambda n: pl.BlockSpec((1, n), lambda b, i: (0, 0))
    return pl.pallas_call(
        _poolconv_kernel,
        out_shape=jax.ShapeDtypeStruct((B, S, POOL_DIM + CONV_DIM), BF16),
        grid=(B, S // ts),
        in_specs=[pl.BlockSpec((1, ts, W), lambda b, i: (b, i, 0)),
                  pl.BlockSpec((1, HALO, W), lambda b, i: (b, jnp.maximum(i * r - 1, 0), 0)),
                  pl.BlockSpec((POOL_DIM, POOL_DIM), lambda b, i: (0, 0)),
                  vec(POOL_DIM),
                  pl.BlockSpec((CONV_WIDTH + 1, CONV_DIM), lambda b, i: (0, 0)),
                  vec(CONV_DIM), vec(CONV_DIM), vec(CONV_DIM)],
        out_specs=pl.BlockSpec((1, ts, POOL_DIM + CONV_DIM), lambda b, i: (b, i, 0)),
        scratch_shapes=[pltpu.VMEM((HALO + ts, W), F32), pltpu.VMEM((HALO + ts, CONV_DIM), F32),
                        pltpu.VMEM((HALO + ts, POOL_DIM), F32), pltpu.VMEM((HALO + ts, POOL_DIM), F32),
                        pltpu.VMEM((7, HALO + ts - 8, CONV_DIM), F32)],
        compiler_params=_cparams(("arbitrary", "arbitrary")),
        name="pool_conv",
    )(upc, upc, pool_w_bd, pool_scale.reshape(1, -1), conv_w, conv_b.reshape(1, -1),
      ln_g.reshape(1, -1), ln_b.reshape(1, -1))


N_STREAM = 2 * KV_GROUPS
CHUNK_W = CMP_STRIDE * 2 * KV_GROUPS * HEAD_DIM


def _cmp_weights(w1_k, w1_v):
    half = CMP_STRIDE * HEAD_DIM
    cols = []
    for s in range(N_STREAM):
        w1 = w1_k if s < KV_GROUPS else w1_v
        for part in range(2):
            blk = w1[part * half:(part + 1) * half].reshape(CMP_STRIDE, 1, HEAD_DIM, CMP_HIDDEN)
            z = jnp.zeros((CMP_STRIDE, N_STREAM, HEAD_DIM, CMP_HIDDEN), w1.dtype)
            z = lax.dynamic_update_slice(z, blk, (0, s, 0, 0))
            cols.append(z.reshape(CHUNK_W, CMP_HIDDEN))
    return jnp.concatenate(cols, axis=1).astype(BF16)


def _gelu_tanh(x):
    return 0.5 * x * (1.0 + jnp.tanh(math.sqrt(2.0 / math.pi) * (x + 0.044715 * (x * x * x))))


def _compress_kernel(c_ref, w_ref, pek_ref, pev_ref, w1k_ref, w1v_ref, w2k_ref, w2vt_ref, kc_ref, vct_ref):
    r = jnp.dot(c_ref[0], w_ref[...], preferred_element_type=F32)
    ncp = r.shape[0]
    pe_k = jnp.dot(pek_ref[...], w1k_ref[...], preferred_element_type=F32)[0:1]
    pe_v = jnp.dot(pev_ref[...], w1v_ref[...], preferred_element_type=F32)[0:1]
    for s in range(N_STREAM):
        a = r[:, s * 256:s * 256 + CMP_HIDDEN]
        b = r[:, s * 256 + CMP_HIDDEN:(s + 1) * 256]
        hid = a + pltpu.roll(b, ncp - 1, 0) + (pe_k if s < KV_GROUPS else pe_v)
        act = _gelu_tanh(hid).astype(BF16)
        if s < KV_GROUPS:
            kc_ref[0, s] = jnp.dot(act, w2k_ref[...], preferred_element_type=F32).astype(BF16)
        else:
            vt = lax.dot_general(w2vt_ref[...], act, (((1,), (1,)), ((), ())), preferred_element_type=F32)
            ones_row = (lax.broadcasted_iota(I32, vt.shape, 0) == HEAD_DIM).astype(F32)
            vct_ref[0, s - KV_GROUPS] = (vt + ones_row).astype(BF16)


def _compress(kvc, wcmp, pe_k, pe_v, w1_k, w1_v, w2_k, w2_v):
    B, S, _ = kvc.shape
    ncp = S // CMP_STRIDE
    chunks = kvc.reshape(B, ncp, CHUNK_W)
    pe8 = lambda pe: jnp.broadcast_to(pe.reshape(1, -1), (8, CMP_LEN * HEAD_DIM)).astype(BF16)
    w2k = jnp.pad(w2_k, ((0, 0), (0, LANES - HEAD_DIM))).astype(BF16)
    w2vt = jnp.pad(w2_v.T, ((0, LANES - HEAD_DIM), (0, 0))).astype(BF16)
    full = lambda a: pl.BlockSpec(a.shape, lambda b: (0,) * a.ndim)
    args = (wcmp, pe8(pe_k), pe8(pe_v), w1_k.astype(BF16), w1_v.astype(BF16), w2k, w2vt)
    return pl.pallas_call(
        _compress_kernel,
        out_shape=[jax.ShapeDtypeStruct((B, KV_GROUPS, ncp, LANES), BF16),
                   jax.ShapeDtypeStruct((B, KV_GROUPS, LANES, ncp), BF16)],
        grid=(B,),
        in_specs=[pl.BlockSpec((1, ncp, CHUNK_W), lambda b: (b, 0, 0))] + [full(a) for a in args],
        out_specs=[pl.BlockSpec((1, KV_GROUPS, ncp, LANES), lambda b: (b, 0, 0, 0)),
                   pl.BlockSpec((1, KV_GROUPS, LANES, ncp), lambda b: (b, 0, 0, 0))],
        compiler_params=_cparams(("arbitrary",)),
        name="compress",
    )(chunks, *args)


def _bias_tables(rel_bias, S):
    nq = S // Q_TILE
    ncp = S // CMP_STRIDE
    rb = rel_bias.reshape(N_BUCKETS, KV_GROUPS, HPG).transpose(1, 2, 0) * LOG2E
    far = rb[:, :, N_BUCKETS - 1]
    far_hi = far.astype(BF16)
    far_lo = (far - far_hi.astype(F32)).astype(BF16)
    far_sum = far_hi.astype(F32) + far_lo.astype(F32)
    i = np.arange(Q_TILE)[None, :]

    def table(d, valid, sub):
        onehot = jax.nn.one_hot(_t5_bucket(jnp.asarray(d, I32)), N_BUCKETS, dtype=F32)
        t = jnp.einsum('rib,ghb->grhi', onehot, rb, precision=lax.Precision.HIGHEST)
        t = jnp.where(jnp.asarray(valid)[None, :, None, :], t - sub[:, None, :, None], NEG_INF)
        return t.reshape(KV_GROUPS, d.shape[0], ROWS)

    d = i - np.arange(NEAR_KEYS)[:, None] + (NEAR_KEYS - Q_TILE)
    t_near = table(d, d >= 0, far_sum)
    d = i - np.arange(WIN_KEYS)[:, None] + (WIN_KEYS - Q_TILE)
    t_win = table(d, (d >= 0) & (d < WINDOW), jnp.zeros_like(far_sum))
    c0 = (Q_TILE // CMP_STRIDE) * (nq - 1)
    d = i - CMP_STRIDE * (np.arange(c0 + ncp)[:, None] - c0) - (CMP_LEN - 1)
    t_cmp = table(d, d >= 0, jnp.zeros_like(far_sum))
    rows = jnp.zeros((KV_GROUPS, HEAD_DIM, HPG, Q_TILE), F32)
    rows = rows.at[:, 0].set(1.0)
    rows = rows.at[:, 1].set(jnp.broadcast_to(far_hi.astype(F32)[:, :, None], (KV_GROUPS, HPG, Q_TILE)))
    rows = rows.at[:, 2].set(jnp.broadcast_to(far_lo.astype(F32)[:, :, None], (KV_GROUPS, HPG, Q_TILE)))
    return t_near, t_win, t_cmp, rows.reshape(KV_GROUPS, HEAD_DIM, ROWS).astype(BF16)


def _overlap_t(S):
    ncp = S // CMP_STRIDE
    n = np.arange(ncp)[None, :]
    jb = np.arange(LANES)[:, None]
    end = n * CMP_STRIDE + CMP_LEN - 1
    start = n * CMP_STRIDE
    ov = (end >= jb * SEL_BLOCK) & (start < (jb + 1) * SEL_BLOCK) & (n < ncp - 1)
    return jnp.asarray(ov.astype(np.float32), BF16)


def _key_tiles(ref, k0, n):
    t0 = k0 // LANES
    return jnp.concatenate([ref[0, 0, t0 + u] for u in range(n)], axis=1)


def _attn_kernel(qt_ref, qc_ref, kc_ref, vct_ref, ks_ref, vs_ref, kw_ref, vw_ref, gt_ref, tn_ref, tw_ref, tct_ref,
                 ovt_ref, o_ref, qa_ref, acc_ref, *, c0):
    qi = pl.program_id(2)
    qt = qt_ref[0, 0, 0]
    ncp = kc_ref.shape[2]

    k1 = Q_TILE * qi + (KEY_PAD + Q_TILE)

    qa_ref[0:HEAD_DIM, :] = qt
    qa_ref[HEAD_DIM:LANES, :] = qc_ref[0]

    r0 = pl.multiple_of(c0 - (Q_TILE // CMP_STRIDE) * qi, 8)
    st = (jnp.dot(kc_ref[0, 0][:, 0:HEAD_DIM], qt, preferred_element_type=F32)
          + tct_ref[0, pl.ds(r0, ncp), :])
    w0 = pl.multiple_of(k1 - WIN_KEYS, LANES)
    sw = jnp.dot(kw_ref[0, pl.ds(w0, WIN_KEYS), :], qa_ref[0:LANES, :], preferred_element_type=F32) + tw_ref[0]

    mc = jnp.maximum(jnp.max(st, axis=0, keepdims=True), -1e20)
    pc = jnp.exp2(st - mc).astype(BF16)
    acc_c = jnp.dot(vct_ref[0, 0], pc, preferred_element_type=F32)
    inv_c = 1.0 / jnp.maximum(acc_c[HEAD_DIM:HEAD_DIM + 1], 1e-30)
    o_c = acc_c[0:HEAD_DIM] * inv_c

    imp = None
    for h in range(HPG):
        cols = slice(h * Q_TILE, (h + 1) * Q_TILE)
        part = jnp.dot(ovt_ref[...], pc[:, cols], preferred_element_type=F32) * inv_c[:, cols]
        imp = part if imp is None else imp + part

    def flash_step(carry, s, vt):
        m, acc = carry
        m_new = jnp.maximum(m, jnp.max(s, axis=0, keepdims=True))
        alpha = jnp.exp2(m - m_new)
        p = jnp.exp2(s - m_new)
        acc = acc * alpha + jnp.dot(vt, p.astype(BF16), preferred_element_type=F32)
        return m_new, acc

    def normalised(acc):
        return acc[0:HEAD_DIM] * (1.0 / acc[HEAD_DIM:HEAD_DIM + 1])

    p = jnp.exp2(sw - jnp.max(sw, axis=0, keepdims=True)).astype(BF16)
    o_w = normalised(jnp.dot(_key_tiles(vw_ref, w0, WIN_KEYS // LANES), p, preferred_element_type=F32))

    jb = lax.broadcasted_iota(I32, (LANES, Q_TILE), 0)
    ii = lax.broadcasted_iota(I32, (LANES, Q_TILE), 1)
    cur = (Q_TILE // SEL_BLOCK) * qi + (ii >= SEL_BLOCK).astype(I32)
    forced = (jb == 0) | (jb == cur) | (jb == cur - 1)
    score = jnp.where(jb <= cur, jnp.where(forced, -jnp.inf, imp), NEG_INF)
    for _ in range(SEL_TOPN - N_FORCED):
        mx = jnp.max(score, axis=0, keepdims=True)
        first = jnp.min(jnp.where(score == mx, jb, LANES), axis=0, keepdims=True)
        score = jnp.where(jb == first, -jnp.inf, score)
    mbt = jnp.where(score == -jnp.inf, 0.0, MASK_BIAS).astype(BF16)
    for h in range(HPG):
        qa_ref[LANES:2 * LANES, h * Q_TILE:(h + 1) * Q_TILE] = mbt

    n0 = pl.multiple_of(k1 - NEAR_KEYS, LANES)
    half = NEAR_KEYS // 2
    sa = (jnp.dot(ks_ref[0, pl.ds(n0, half), :], qa_ref[...], preferred_element_type=F32)
          + tn_ref[0, 0:half, :])
    sb = (jnp.dot(ks_ref[0, pl.ds(n0 + half, half), :], qa_ref[...], preferred_element_type=F32)
          + tn_ref[0, half:NEAR_KEYS, :])
    m_near = jnp.maximum(jnp.max(sa, axis=0, keepdims=True), jnp.max(sb, axis=0, keepdims=True))
    pa = jnp.exp2(sa - m_near).astype(BF16)
    da = jnp.dot(_key_tiles(vs_ref, n0, half // LANES), pa, preferred_element_type=F32)
    pb = jnp.exp2(sb - m_near).astype(BF16)
    acc_near = da + jnp.dot(_key_tiles(vs_ref, n0 + half, half // LANES), pb, preferred_element_type=F32)
    n_pairs = (jnp.maximum(n0 - KEY_PAD, 0) + KEY_PAD - 1) // KEY_PAD

    def far_scores(j):
        k0 = pl.multiple_of(n0 - FAR_TILE * (j + 1), LANES)
        return jnp.dot(ks_ref[0, pl.ds(k0, FAR_TILE), :], qa_ref[...], preferred_element_type=F32)

    def far_values(j):
        return _key_tiles(vs_ref, pl.multiple_of(n0 - FAR_TILE * (j + 1), LANES), FAR_TILE // LANES)

    def fast_body(jj, top):
        tiles = [FAR_GROUP * jj + u for u in range(FAR_GROUP)]
        scores = [far_scores(tiles[0]), far_scores(tiles[1])]
        acc = None
        for u, j in enumerate(tiles):
            if u + 2 < FAR_GROUP:
                scores.append(far_scores(tiles[u + 2]))
            p = jnp.exp2(scores[u] - m_near).astype(BF16)
            d = jnp.dot(far_values(j), p, preferred_element_type=F32)
            acc = d if acc is None else acc + d
            top = jnp.maximum(top, jnp.max(scores[u], axis=0, keepdims=True))
        acc_ref[...] += acc
        return top

    acc_ref[...] = acc_near
    top = lax.fori_loop(0, n_pairs, fast_body, m_near)

    @pl.when(jnp.max(top - m_near) > FAR_HEADROOM)
    def _():
        def safe_body(j, carry):
            return flash_step(carry, far_scores(j), far_values(j))

        acc_ref[...] = lax.fori_loop(0, FAR_GROUP * n_pairs, safe_body, (m_near, acc_near))[1]

    o_s = normalised(acc_ref[...])

    gtt = gt_ref[0].T
    gate = lambda br: jnp.concatenate([gtt[br * HPG + h:br * HPG + h + 1, :] for h in range(HPG)], axis=1)
    o_ref[0, 0, 0] = (gate(0) * o_c + gate(1) * o_s + gate(2) * o_w).astype(BF16)


def _attention(qt, qc, kc, vct, ks, vs, kw, vw, gt, tn, tw, tct, ovt):
    B, G, nq, _, _ = qt.shape
    S = nq * Q_TILE
    sp = S + KEY_PAD
    ncp = S // CMP_STRIDE
    c0 = (Q_TILE // CMP_STRIDE) * (nq - 1)
    per_g = lambda a: pl.BlockSpec((1,) + a.shape[1:], lambda b, g, i: (g,) + (0,) * (a.ndim - 1))
    val = pl.BlockSpec((1, 1, sp // LANES, LANES, LANES), lambda b, g, i: (b, g, 0, 0, 0))
    return pl.pallas_call(
        functools.partial(_attn_kernel, c0=c0),
        out_shape=jax.ShapeDtypeStruct((B, G, nq, HEAD_DIM, ROWS), BF16),
        grid=(B, G, nq),
        in_specs=[pl.BlockSpec((1, 1, 1, HEAD_DIM, ROWS), lambda b, g, i: (b, g, i, 0, 0)),
                  per_g(qc),
                  pl.BlockSpec((1, 1, ncp, LANES), lambda b, g, i: (b, g, 0, 0)),
                  pl.BlockSpec((1, 1, LANES, ncp), lambda b, g, i: (b, g, 0, 0)),
                  pl.BlockSpec((1, sp, 2 * LANES), lambda b, g, i: (b, 0, g)),
                  val,
                  pl.BlockSpec((1, sp, LANES), lambda b, g, i: (b, 0, g)),
                  val,
                  pl.BlockSpec((1, Q_TILE, LANES), lambda b, g, i: (b, i, g)),
                  per_g(tn), per_g(tw), per_g(tct),
                  pl.BlockSpec((LANES, ncp), lambda b, g, i: (0, 0))],
        out_specs=pl.BlockSpec((1, 1, 1, HEAD_DIM, ROWS), lambda b, g, i: (b, g, i, 0, 0)),
        scratch_shapes=[pltpu.VMEM((2 * LANES, ROWS), BF16), pltpu.VMEM((LANES, ROWS), F32)],
        compiler_params=_cparams(("arbitrary", "arbitrary", "arbitrary")),
        name="nsa_attention",
    )(qt, qc, kc, vct, ks, vs, kw, vw, gt, tn, tw, tct, ovt)


def _post_attn_kernel(x_ref, ypc_ref, yn_ref, wo_ref, g1_ref, n2_ref, sc_ref, sh_ref, rwh_ref, rwl_ref, rb_ref,
                      xo_ref, h2_ref, ei_ref, tw_ref, cnt_ref, run_ref, hd_ref):
    first = (pl.program_id(0) == 0) & (pl.program_id(1) == 0)

    @pl.when(first)
    def _():
        run_ref[...] = jnp.zeros_like(run_ref)

    tm = x_ref.shape[1]
    half = wo_ref.shape[0] // 2
    tiles = []
    for t in range(TILES):
        groups = []
        for g in range(KV_GROUPS):
            blk = yn_ref[0, g, t].astype(F32)
            for hh in range(HPG):
                hd_ref[hh * HEAD_DIM:(hh + 1) * HEAD_DIM, :] = blk[:, hh * Q_TILE:(hh + 1) * Q_TILE]
            groups.append(hd_ref[...].T)
        tiles.append(jnp.concatenate(groups, axis=1))
    ynsa = jnp.concatenate(tiles, axis=0).astype(BF16)
    mixed = (jnp.dot(ypc_ref[0], wo_ref[0:half, :], preferred_element_type=F32)
             + jnp.dot(ynsa, wo_ref[half:, :], preferred_element_type=F32))
    x = x_ref[0] + g1_ref[0] * mixed
    xo_ref[0] = x
    y = x * lax.rsqrt(jnp.mean(x * x, axis=-1, keepdims=True) + EPS) * n2_ref[...]
    h2 = y * (1.0 + sc_ref[0]) + sh_ref[0]
    _rows_to_tiles(h2_ref, h2)
    hh = h2.astype(BF16)
    hl = (h2 - hh.astype(F32)).astype(BF16)
    logit = (jnp.dot(hh, rwh_ref[...], preferred_element_type=F32)
             + jnp.dot(hl, rwh_ref[...], preferred_element_type=F32)
             + jnp.dot(hh, rwl_ref[...], preferred_element_type=F32)) + rb_ref[...]
    lane = lax.broadcasted_iota(I32, (tm, LANES), 1)
    vals, hots, idxs = [], [], []
    for _ in range(TOP_K):
        mx = jnp.max(logit, axis=1, keepdims=True)
        idx = jnp.min(jnp.where(logit == mx, lane, LANES), axis=1, keepdims=True)
        hot = lane == idx
        vals.append(mx)
        hots.append(hot)
        idxs.append(idx)
        logit = jnp.where(hot, -jnp.inf, logit)
    ex = [jnp.exp(v - vals[0]) for v in vals]
    inv = 1.0 / (ex[0] + ex[1] + ex[2] + ex[3])
    assign = (hots[0] | hots[1] | hots[2] | hots[3]).astype(BF16)
    r = lax.broadcasted_iota(I32, (tm, tm), 0)
    c = lax.broadcasted_iota(I32, (tm, tm), 1)
    before = jnp.dot((c < r).astype(BF16), assign, preferred_element_type=F32) + run_ref[...]
    ei = jnp.zeros((tm, LANES), I32)
    tw = jnp.zeros((tm, LANES), F32)
    for k in range(TOP_K):
        e_k = idxs[k]
        r_k = jnp.sum(jnp.where(hots[k], before, 0.0), axis=1, keepdims=True).astype(I32)
        ei = jnp.where(lane == k, e_k, jnp.where(lane == TOP_K + k, r_k, ei))
        tw = jnp.where(lane == k, ex[k] * inv, tw)
    ei_ref[0] = ei
    tw_ref[0] = tw
    run_ref[...] = run_ref[...] + jnp.sum(assign.astype(F32), axis=0, keepdims=True)
    cnt_ref[...] = run_ref[...]


def _post_attn(x, ypc, ynsa, w_out, g1, n2g, sc2, sh2, rw_hi, rw_lo, rb):
    B, S, D = x.shape
    tm = TOK_TILE
    tok = lambda w: pl.BlockSpec((1, tm, w), lambda b, i: (b, i, 0))
    per_b = pl.BlockSpec((1, 1, D), lambda b, i: (b, 0, 0))
    full = lambda a: pl.BlockSpec(a.shape, lambda b, i: (0,) * a.ndim)
    return pl.pallas_call(
        _post_attn_kernel,
        out_shape=[jax.ShapeDtypeStruct((B, S, D), F32), jax.ShapeDtypeStruct((B * S * ROW_TILES, LANES), F32),
                   jax.ShapeDtypeStruct((B, S, LANES), I32), jax.ShapeDtypeStruct((B, S, LANES), F32),
                   jax.ShapeDtypeStruct((1, LANES), F32)],
        grid=(B, S // tm),
        in_specs=[tok(D), tok(ypc.shape[-1]),
                  pl.BlockSpec((1, KV_GROUPS, TILES, HEAD_DIM, ROWS), lambda b, i: (b, 0, i, 0, 0)),
                  full(w_out), per_b,
                  pl.BlockSpec((1, D), lambda b, i: (0, 0)), per_b, per_b,
                  full(rw_hi), full(rw_lo), full(rb)],
        out_specs=[tok(D), pl.BlockSpec((tm * ROW_TILES, LANES), lambda b, i: (b * (S // tm) + i, 0)),
                   tok(LANES), tok(LANES), pl.BlockSpec((1, LANES), lambda b, i: (0, 0))],
        scratch_shapes=[pltpu.VMEM((1, LANES), F32), pltpu.VMEM((HPG * HEAD_DIM, Q_TILE), F32)],
        compiler_params=_cparams(("arbitrary", "arbitrary")),
        name="post_attn_router",
    )(x, ypc, ynsa, w_out, g1, n2g.reshape(1, D), sc2, sh2, rw_hi, rw_lo, rb)


ROW_TILES = 8
DMA_UNROLL = 4
ZERO_BITS = tuple(1 << b for b in reversed(range((EXPERT_BLOCK - 1).bit_length())))


def _rows_to_tiles(ref, val):
    n = val.shape[0]
    for s in range(ROW_TILES):
        ref[pl.ds(s, n, stride=ROW_TILES), :] = val[:, s * LANES:(s + 1) * LANES]


def _tiles_to_rows(ref, n):
    return jnp.concatenate([ref[pl.ds(s, n, stride=ROW_TILES), :] for s in range(ROW_TILES)], axis=1)


def _tile_at(ref, i):
    return ref.at[pl.ds(pl.multiple_of(i * ROW_TILES, ROW_TILES), ROW_TILES), :]


def _dispatch_kernel(fill0_ref, filln_ref, dest_ref, h_ref, xs_ref, zero_ref, sem, zsem):
    tm = h_ref.shape[0] // ROW_TILES

    @pl.when(pl.program_id(0) == 0)
    def _():
        zero_ref[...] = jnp.zeros_like(zero_ref)

        def expert(e, c):
            n = filln_ref[e]
            for wait in (False, True):
                for bit in ZERO_BITS:
                    @pl.when((n & bit) != 0)
                    def _():
                        first = fill0_ref[e] + (n & ~(2 * bit - 1))
                        dst = xs_ref.at[pl.ds(pl.multiple_of(first * ROW_TILES, ROW_TILES), bit * ROW_TILES), :]
                        cp = pltpu.make_async_copy(zero_ref.at[pl.ds(0, bit * ROW_TILES), :], dst, zsem)
                        cp.wait() if wait else cp.start()
            return c

        lax.fori_loop(0, N_EXPERTS, expert, 0)

    def body(i, c):
        for u in range(DMA_UNROLL):
            t = i * DMA_UNROLL + u
            src = _tile_at(h_ref, t)
            for k in range(TOP_K):
                pltpu.make_async_copy(src, _tile_at(xs_ref, dest_ref[0, 0, t * TOP_K + k]), sem).start(
                    priority=k % 2)
        return c

    lax.fori_loop(0, tm // DMA_UNROLL, body, 0)
    for _ in range(TOP_K):
        pltpu.make_async_copy(h_ref, xs_ref.at[pl.ds(0, tm * ROW_TILES), :], sem).wait()


def _dispatch(dest, h2t, fill0, filln, n_slots):
    tm = TOK_TILE
    nt = h2t.shape[0] // (tm * ROW_TILES)
    return pl.pallas_call(
        _dispatch_kernel,
        out_shape=jax.ShapeDtypeStruct((n_slots * ROW_TILES, LANES), F32),
        grid_spec=pltpu.PrefetchScalarGridSpec(
            num_scalar_prefetch=2,
            grid=(nt,),
            in_specs=[pl.BlockSpec((1, 1, tm * TOP_K), lambda i, f0, fn: (i, 0, 0), memory_space=pltpu.SMEM),
                      pl.BlockSpec((tm * ROW_TILES, LANES), lambda i, f0, fn: (i, 0))],
            out_specs=pl.BlockSpec(memory_space=pl.ANY),
            scratch_shapes=[pltpu.VMEM((ZERO_BITS[0] * ROW_TILES, LANES), F32), pltpu.SemaphoreType.DMA(()),
                            pltpu.SemaphoreType.DMA(())]),
        compiler_params=_cparams(("arbitrary",)),
        name="moe_dispatch",
    )(fill0, filln, dest.reshape(nt, 1, tm * TOP_K), h2t)


W_CHUNK = 512


def _expert_kernel(be_ref, nu_ref, x_ref, wgu_ref, bgu_ref, wd_ref, bd_ref, y_ref, wgu_s, wd_s):
    i = pl.program_id(0)

    @pl.when(i < nu_ref[0])
    def _():
        @pl.when((i == 0) | (be_ref[i] != be_ref[jnp.maximum(i - 1, 0)]))
        def _():
            for c in range(0, wgu_s.shape[1], W_CHUNK):
                wgu_s[:, c:c + W_CHUNK] = wgu_ref[0, :, c:c + W_CHUNK].astype(BF16)
            for c in range(0, wd_s.shape[1], W_CHUNK):
                wd_s[:, c:c + W_CHUNK] = wd_ref[0, :, c:c + W_CHUNK].astype(BF16)

        F = wd_s.shape[0]
        x = _tiles_to_rows(x_ref, EXPERT_BLOCK).astype(BF16)
        gu = jnp.dot(x, wgu_s[...], preferred_element_type=F32) + bgu_ref[0]
        gate = jnp.minimum(gu[:, :F], SWIGLU_LIMIT)
        up = jnp.clip(gu[:, F:], -SWIGLU_LIMIT, SWIGLU_LIMIT)
        act = (up + 1.0) * gate * jax.nn.sigmoid(SWIGLU_ALPHA * gate)
        _rows_to_tiles(y_ref, jnp.dot(act.astype(BF16), wd_s[...], preferred_element_type=F32) + bd_ref[0])


def _experts(blk_e, n_used, xs, layer, w_gu, b_gu, w_down, b_down):
    L, E, D, F2 = w_gu.shape
    F = F2 // 2
    rows = EXPERT_BLOCK * ROW_TILES
    nb = xs.shape[0] // rows
    blk = lambda i, be, nu: (jnp.minimum(i, nu[0] - 1), 0)
    per_e = lambda i, be, nu: (layer, be[i], 0, 0)
    return pl.pallas_call(
        _expert_kernel,
        out_shape=jax.ShapeDtypeStruct(xs.shape, F32),
        grid_spec=pltpu.PrefetchScalarGridSpec(
            num_scalar_prefetch=2,
            grid=(nb,),
            in_specs=[pl.BlockSpec((rows, LANES), blk),
                      pl.BlockSpec((None, 1, D, F2), per_e),
                      pl.BlockSpec((None, 1, 1, F2), per_e),
                      pl.BlockSpec((None, 1, F, D), per_e),
                      pl.BlockSpec((None, 1, 1, D), per_e)],
            out_specs=pl.BlockSpec((rows, LANES), blk),
            scratch_shapes=[pltpu.VMEM((D, F2), BF16), pltpu.VMEM((F, D), BF16)]),
        compiler_params=_cparams(("arbitrary",)),
        name="moe_experts",
    )(blk_e, n_used, xs, w_gu, b_gu.reshape(L, E, 1, F2), w_down, b_down.reshape(L, E, 1, D))


def _combine_kernel(dest_ref, y_ref, x_ref, tw_ref, g2_ref, fg_ref, o_ref, rows_ref, sem, *, final, n_tiles):
    s = pl.program_id(0)
    tm = x_ref.shape[1]

    @pl.when(s < n_tiles)
    def _():
        slot = s % 2

        def body(i, c):
            for u in range(DMA_UNROLL):
                t = i * DMA_UNROLL + u
                for k in range(TOP_K):
                    pltpu.make_async_copy(_tile_at(y_ref, dest_ref[0, 0, t * TOP_K + k]),
                                          _tile_at(rows_ref.at[slot, k], t), sem.at[slot]).start(priority=k % 2)
            return c

        lax.fori_loop(0, tm // DMA_UNROLL, body, 0)

    @pl.when(s > 0)
    def _():
        slot = (s - 1) % 2
        for k in range(TOP_K):
            pltpu.make_async_copy(y_ref.at[pl.ds(0, tm * ROW_TILES), :], rows_ref.at[slot, k], sem.at[slot]).wait()
        tw = tw_ref[0]
        moe = tw[:, 0:1] * _tiles_to_rows(rows_ref.at[slot, 0], tm)
        for k in range(1, TOP_K):
            moe = moe + tw[:, k:k + 1] * _tiles_to_rows(rows_ref.at[slot, k], tm)
        x = x_ref[0] + g2_ref[0] * moe
        if final:
            x = x * lax.rsqrt(jnp.mean(x * x, axis=-1, keepdims=True) + EPS) * fg_ref[...]
        o_ref[0] = x


def _combine(dest, y, x, tw, g2, final_g, final):
    B, S, D = x.shape
    tm = TOK_TILE
    nt = S // tm
    n_tiles = B * nt
    done = lambda s: jnp.maximum(s - 1, 0)
    tok = lambda w: pl.BlockSpec((1, tm, w), lambda s: (done(s) // nt, done(s) % nt, 0))
    return pl.pallas_call(
        functools.partial(_combine_kernel, final=final, n_tiles=n_tiles),
        out_shape=jax.ShapeDtypeStruct((B, S, D), F32),
        grid=(n_tiles + 1,),
        in_specs=[pl.BlockSpec((1, 1, tm * TOP_K), lambda s: (jnp.minimum(s, n_tiles - 1), 0, 0),
                               memory_space=pltpu.SMEM),
                  pl.BlockSpec(memory_space=pl.ANY),
                  tok(D), tok(LANES),
                  pl.BlockSpec((1, 1, D), lambda s: (done(s) // nt, 0, 0)),
                  pl.BlockSpec((1, D), lambda s: (0, 0))],
        out_specs=tok(D),
        scratch_shapes=[pltpu.VMEM((2, TOP_K, tm * ROW_TILES, LANES), F32), pltpu.SemaphoreType.DMA((2,))],
        compiler_params=_cparams(("arbitrary",)),
        name="moe_combine",
    )(dest.reshape(n_tiles, 1, tm * TOP_K), y, x, tw, g2, final_g.reshape(1, D))


def _moe(x, h2, ei, tw, counts, g2, layer, w_gu, b_gu, w_down, b_down, final_g, final):
    B, S, D = x.shape
    N = B * S
    n_slots = -(-(N * TOP_K + N_EXPERTS * EXPERT_BLOCK) // EXPERT_BLOCK) * EXPERT_BLOCK
    nb = n_slots // EXPERT_BLOCK
    cnt = counts[0, :N_EXPERTS].astype(I32)
    padded = (cnt + EXPERT_BLOCK - 1) // EXPERT_BLOCK * EXPERT_BLOCK
    pend = jnp.cumsum(padded)
    pstart = pend - padded
    ei2 = ei.reshape(N, LANES)
    dest = (pstart[ei2[:, 0:TOP_K]] + ei2[:, TOP_K:2 * TOP_K]).reshape(N * TOP_K)
    blk_start = jnp.arange(nb, dtype=I32) * EXPERT_BLOCK
    blk_e = jnp.minimum(jnp.sum((pend[None, :] <= blk_start[:, None]).astype(I32), axis=1), N_EXPERTS - 1)
    n_used = (pend[-1:] // EXPERT_BLOCK).astype(I32)
    xs = _dispatch(dest, h2, pstart + cnt, padded - cnt, n_slots)
    y = _experts(blk_e, n_used, xs, layer, w_gu, b_gu, w_down, b_down)
    return _combine(dest, y, x, tw, g2, final_g, final)


def kernel(x, c, w_mod, b_mod, norm1_g, norm2_g, w_in, w_out, pool_w, pool_scale, conv_w, conv_b, conv_ln_g,
           conv_ln_b, cmp_pe_k, cmp_pe_v, cmp_w1_k, cmp_w2_k, cmp_w1_v, cmp_w2_v, rel_bias, router_w, router_b,
           expert_w_gu, expert_b_gu, expert_w_down, expert_b_down, final_g):
    B, S, D = x.shape
    L = w_mod.shape[0]
    assert S % TOK_TILE == 0 and S // SEL_BLOCK <= LANES and D == ROW_TILES * LANES
    mod = _modulation(c, w_mod, b_mod)
    t_near, t_win, t_cmp, qc = _bias_tables(rel_bias, S)
    ovt = _overlap_t(S)
    w_big = _in_weight(w_in)
    cg = POOL_DIM // POOL_GROUPS
    for l in range(L):
        m6 = mod[l].reshape(B, 6, 1, D)
        sh1, sc1, g1, sh2, sc2, g2 = (m6[:, k] for k in range(6))
        upc, qt, kvc, ks, vs, kw, vw, gt = _in_proj(x, norm1_g[l], sc1, sh1, w_big[l])
        pw_bd = jnp.zeros((POOL_DIM, POOL_DIM), F32)
        for g in range(POOL_GROUPS):
            pw_bd = lax.dynamic_update_slice(pw_bd, pool_w[l, g], (g * cg, g * cg))
        cw = jnp.pad(conv_w[l], ((0, 1), (0, 0)))
        ypc = _pool_conv(upc, pw_bd.astype(BF16), pool_scale[l], cw, conv_b[l], conv_ln_g[l], conv_ln_b[l])
        kc, vct = _compress(kvc, _cmp_weights(cmp_w1_k[l], cmp_w1_v[l]), cmp_pe_k[l], cmp_pe_v[l],
                            cmp_w1_k[l], cmp_w1_v[l], cmp_w2_k[l], cmp_w2_v[l])
        ynsa = _attention(qt, qc, kc, vct, ks, vs, kw, vw, gt, t_near, t_win, t_cmp, ovt)
        rw = jnp.pad(router_w[l], ((0, 0), (0, LANES - N_EXPERTS)))
        rw_hi = rw.astype(BF16)
        rw_lo = (rw - rw_hi.astype(F32)).astype(BF16)
        rb = jnp.pad(router_b[l].reshape(1, -1), ((0, 0), (0, LANES - N_EXPERTS)), constant_values=NEG_INF)
        x, h2, ei, tw, counts = _post_attn(x, ypc, ynsa, w_out[l].astype(BF16), g1, norm2_g[l], sc2, sh2,
                                           rw_hi, rw_lo, rb)
        x = _moe(x, h2, ei, tw, counts, g2, l, expert_w_gu, expert_b_gu, expert_w_down, expert_b_down,
                 final_g, final=(l == L - 1))
    return x
```

```python
import functools
import math

import jax
import jax.numpy as jnp
import numpy as np
from jax import lax
from jax.experimental import pallas as pl
from jax.experimental.pallas import tpu as pltpu

F32 = jnp.float32
BF16 = jnp.bfloat16
I32 = jnp.int32

HEAD_DIM = 64
POOL_DIM = 256
POOL_GROUPS = 4
POOL_WINDOWS = (2, 4, 8, 16)
CONV_DIM = 256
CONV_WIDTH = 31
NSA_DIM = 512
NSA_HEADS = 8
KV_GROUPS = 2
HPG = 4
CMP_LEN = 32
CMP_STRIDE = 16
CMP_HIDDEN = 128
SEL_BLOCK = 64
SEL_TOPN = 16
N_FORCED = 3
WINDOW = 512
Q_TILE = 128
N_BUCKETS = 32
MAX_DISTANCE = 1024
N_EXPERTS = 32
TOP_K = 4
SWIGLU_ALPHA = 1.702
SWIGLU_LIMIT = 7.0
EPS = 1e-5
NEG_INF = -1e30
FORCE_SCORE = 1e4

LANES = 128
ROWS = HPG * Q_TILE
LOG2E = 1.4426950408889634
FAR_HEADROOM = 100.0
NEAR_KEYS = 1024
WIN_KEYS = WINDOW + Q_TILE
FAR_TILE = 512
FAR_GROUP = 4
KEY_PAD = FAR_GROUP * FAR_TILE
MASK_BIAS = -32768.0
TOK_TILE = 512
EXPERT_BLOCK = 512
VMEM_LIMIT = 56 * 1024 * 1024


def _cparams(sem, vmem=VMEM_LIMIT):
    return pltpu.CompilerParams(dimension_semantics=sem, vmem_limit_bytes=vmem)


def _t5_bucket(n):
    n = jnp.maximum(n, 0)
    max_exact = N_BUCKETS // 2
    nf = jnp.maximum(n, 1).astype(F32)
    large = max_exact + (jnp.log(nf / max_exact) / math.log(MAX_DISTANCE / max_exact)
                         * (N_BUCKETS - max_exact)).astype(I32)
    large = jnp.minimum(large, N_BUCKETS - 1)
    return jnp.where(n < max_exact, n, large)


def _mod_kernel(c_ref, w_ref, b_ref, o_ref):
    c = c_ref[...]
    cond = c * jax.nn.sigmoid(c)
    o_ref[0] = jnp.dot(cond.astype(BF16), w_ref[0].astype(BF16),
                       preferred_element_type=F32) + b_ref[0]


def _modulation(c, w_mod, b_mod):
    L, D, W = w_mod.shape
    B = c.shape[0]
    tn = 1536
    return pl.pallas_call(
        _mod_kernel,
        out_shape=jax.ShapeDtypeStruct((L, B, W), F32),
        grid=(L, W // tn),
        in_specs=[pl.BlockSpec((B, D), lambda l, j: (0, 0)),
                  pl.BlockSpec((1, D, tn), lambda l, j: (l, 0, j)),
                  pl.BlockSpec((1, 1, tn), lambda l, j: (l, 0, j))],
        out_specs=pl.BlockSpec((1, B, tn), lambda l, j: (l, 0, j)),
        compiler_params=_cparams(("arbitrary", "arbitrary")),
        name="modulation",
    )(c, w_mod, b_mod.reshape(L, 1, W))


C_UPC, C_Q, C_KVC, C_K, C_V, C_GT, C_END = 0, 768, 1280, 1536, 1792, 2048, 2304


def _in_weight(w_in):
    col = lambda a, n: w_in[:, :, a:a + n]
    zero = lambda n: jnp.zeros(w_in.shape[:2] + (n,), w_in.dtype)
    parts = [col(0, 768), col(768, 512) * (HEAD_DIM ** -0.5 * LOG2E), col(1280, 256)]
    for sel, win in ((1536, 1792), (1664, 1920)):
        for g in range(KV_GROUPS):
            parts += [col(sel + g * HEAD_DIM, HEAD_DIM), col(win + g * HEAD_DIM, HEAD_DIM)]
    for g in range(KV_GROUPS):
        parts += [col(2048 + br * NSA_HEADS + g * HPG, HPG) for br in range(3)] + [zero(LANES - 3 * HPG)]
    return jnp.concatenate(parts, axis=2).astype(BF16)


PAD_STEPS = KEY_PAD // TOK_TILE
TILES = TOK_TILE // Q_TILE


def _inproj_kernel(x_ref, g_ref, sc_ref, sh_ref, w_ref,
                   upc_ref, qt_ref, kvc_ref, ks_ref, vs_ref, kw_ref, vw_ref, gt_ref):
    step = pl.program_id(1)
    tm = x_ref.shape[1]

    @pl.when(step < PAD_STEPS)
    def _():
        lane = lax.broadcasted_iota(I32, (tm, 4 * LANES), 1)
        ks_ref[0] = jnp.where(lane % (2 * LANES) == HEAD_DIM, MASK_BIAS, 0.0).astype(BF16)
        lane = lax.broadcasted_iota(I32, (tm, 2 * LANES), 1)
        kw_ref[0] = jnp.where(lane % LANES == HEAD_DIM, MASK_BIAS, 0.0).astype(BF16)
        for g in range(KV_GROUPS):
            for t in range(TILES):
                vs_ref[0, g, t] = jnp.zeros((LANES, LANES), BF16)
                vw_ref[0, g, t] = jnp.zeros((LANES, LANES), BF16)

    @pl.when(step >= PAD_STEPS)
    def _():
        ti = step - PAD_STEPS
        x = x_ref[0]
        y = x * lax.rsqrt(jnp.mean(x * x, axis=-1, keepdims=True) + EPS) * g_ref[...]
        h = y * (1.0 + sc_ref[0]) + sh_ref[0]
        z = jnp.dot(h.astype(BF16), w_ref[...], preferred_element_type=F32)
        upc_ref[0] = z[:, C_UPC:C_Q]
        kvc_ref[0] = z[:, C_KVC:C_K].astype(BF16)
        gt_ref[0] = jax.nn.sigmoid(z[:, C_GT:C_END])
        zq = z[:, C_Q:C_KVC].T
        for g in range(KV_GROUPS):
            for t in range(TILES):
                qt_ref[0, g, t] = jnp.concatenate(
                    [zq[(g * HPG + hh) * HEAD_DIM:(g * HPG + hh + 1) * HEAD_DIM, t * Q_TILE:(t + 1) * Q_TILE]
                     for hh in range(HPG)], axis=1).astype(BF16)
        lane = lax.broadcasted_iota(I32, (tm, LANES), 1)
        row = lax.broadcasted_iota(I32, (tm, LANES), 0)
        low = lane < HEAD_DIM
        consts = ((lane == HEAD_DIM + 1) | (lane == HEAD_DIM + 2)).astype(F32)
        onehot = (lane == (ti * tm + row) // SEL_BLOCK).astype(BF16)
        ones_col = (lane == HEAD_DIM).astype(F32)
        for g in range(KV_GROUPS):
            kk = z[:, C_K + g * LANES:C_K + (g + 1) * LANES]
            ks_ref[0, :, 2 * g * LANES:(2 * g + 1) * LANES] = jnp.where(low, kk, consts).astype(BF16)
            ks_ref[0, :, (2 * g + 1) * LANES:(2 * g + 2) * LANES] = onehot
            kw_ref[0, :, g * LANES:(g + 1) * LANES] = jnp.where(low, pltpu.roll(kk, HEAD_DIM, 1), 0.0).astype(BF16)
            vv = z[:, C_V + g * LANES:C_V + (g + 1) * LANES]
            for v_ref, half in ((vs_ref, vv), (vw_ref, pltpu.roll(vv, HEAD_DIM, 1))):
                zv = jnp.where(low, half, ones_col).T
                for t in range(TILES):
                    v_ref[0, g, t] = zv[:, t * Q_TILE:(t + 1) * Q_TILE].astype(BF16)


def _in_proj(x, g1, sc, sh, w_big):
    B, S, D = x.shape
    tm = TOK_TILE
    sp = S + KEY_PAD
    nq = S // Q_TILE
    cur = lambda i: jnp.maximum(i - PAD_STEPS, 0)
    tok = lambda w: pl.BlockSpec((1, tm, w), lambda b, i: (b, cur(i), 0))
    key = lambda w: pl.BlockSpec((1, tm, w), lambda b, i: (b, i, 0))
    val = pl.BlockSpec((1, KV_GROUPS, TILES, LANES, LANES), lambda b, i: (b, 0, i, 0, 0))
    val_shape = jax.ShapeDtypeStruct((B, KV_GROUPS, sp // LANES, LANES, LANES), BF16)
    return pl.pallas_call(
        _inproj_kernel,
        out_shape=[jax.ShapeDtypeStruct((B, S, 768), F32),
                   jax.ShapeDtypeStruct((B, KV_GROUPS, nq, HEAD_DIM, ROWS), BF16),
                   jax.ShapeDtypeStruct((B, S, 256), BF16),
                   jax.ShapeDtypeStruct((B, sp, 4 * LANES), BF16), val_shape,
                   jax.ShapeDtypeStruct((B, sp, 2 * LANES), BF16), val_shape,
                   jax.ShapeDtypeStruct((B, S, 256), F32)],
        grid=(B, S // tm + PAD_STEPS),
        in_specs=[tok(D),
                  pl.BlockSpec((1, D), lambda b, i: (0, 0)),
                  pl.BlockSpec((1, 1, D), lambda b, i: (b, 0, 0)),
                  pl.BlockSpec((1, 1, D), lambda b, i: (b, 0, 0)),
                  pl.BlockSpec((D, C_END), lambda b, i: (0, 0))],
        out_specs=[tok(768),
                   pl.BlockSpec((1, KV_GROUPS, TILES, HEAD_DIM, ROWS), lambda b, i: (b, 0, cur(i), 0, 0)),
                   tok(256), key(4 * LANES), val, key(2 * LANES), val, tok(256)],
        compiler_params=_cparams(("arbitrary", "arbitrary")),
        name="in_proj",
    )(x, g1.reshape(1, D), sc, sh, w_big)


HALO = 32


def _poolconv_kernel(cur_ref, halo_ref, pw_ref, ps_ref, cw_ref, cb_ref, lg_ref, lb_ref, o_ref,
                     ext_ref, v_ref, pa_ref, pb_ref, vsh_ref):
    ti = pl.program_id(1)
    ts = cur_ref.shape[1]
    rows = HALO + ts
    halo = halo_ref[0] * (ti > 0).astype(F32)
    ext_ref[0:HALO, :] = halo
    ext_ref[HALO:rows, :] = cur_ref[0]
    u = ext_ref[HALO:rows, 0:POOL_DIM]
    lane = lax.broadcasted_iota(I32, (ts, POOL_DIM), 1)
    grp = lane // (POOL_DIM // POOL_GROUPS)
    pa_ref[8:rows, :] = ext_ref[8:rows, 0:POOL_DIM] + ext_ref[7:rows - 1, 0:POOL_DIM]
    pooled = pa_ref[HALO:rows, :]
    pb_ref[16:rows, :] = pa_ref[16:rows, :] + pa_ref[14:rows - 2, :]
    pooled = jnp.where(grp >= 1, pb_ref[HALO:rows, :], pooled)
    pa_ref[24:rows, :] = pb_ref[24:rows, :] + pb_ref[20:rows - 4, :]
    pooled = jnp.where(grp >= 2, pa_ref[HALO:rows, :], pooled)
    pooled = jnp.where(grp == 3, pa_ref[HALO:rows, :] + pa_ref[HALO - 8:rows - 8, :], pooled)
    wlane = jnp.where(grp == 0, 2.0, jnp.where(grp == 1, 4.0, jnp.where(grp == 2, 8.0, 16.0)))
    t1 = (ti * ts + lax.broadcasted_iota(I32, (ts, POOL_DIM), 0) + 1).astype(F32)
    cnt = jnp.minimum(t1, wlane)
    pooled = pooled / cnt - u
    y_pool = jnp.dot(pooled.astype(BF16), pw_ref[...], preferred_element_type=F32) * ps_ref[...]
    o_ref[0, :, 0:POOL_DIM] = y_pool.astype(BF16)
    uv = ext_ref[:, POOL_DIM:POOL_DIM + CONV_DIM]
    ug = ext_ref[:, POOL_DIM + CONV_DIM:POOL_DIM + 2 * CONV_DIM]
    v_ref[...] = uv * jax.nn.sigmoid(ug)
    for b in range(1, 8):
        vsh_ref[b - 1] = v_ref[b:rows - 8 + b, :]
    acc = jnp.zeros((ts, CONV_DIM), F32) + cb_ref[...]
    for k in range(CONV_WIDTH):
        a, b = divmod(HALO - (CONV_WIDTH - 1) + k, 8)
        tap = v_ref[8 * a:8 * a + ts, :] if b == 0 else vsh_ref[b - 1, 8 * a:8 * a + ts, :]
        acc = acc + tap * cw_ref[k:k + 1, :]
    mu = jnp.mean(acc, axis=-1, keepdims=True)
    d = acc - mu
    var = jnp.mean(d * d, axis=-1, keepdims=True)
    yn = d * lax.rsqrt(var + EPS) * lg_ref[...] + lb_ref[...]
    o_ref[0, :, POOL_DIM:POOL_DIM + CONV_DIM] = (yn * jax.nn.sigmoid(yn)).astype(BF16)


def _pool_conv(upc, pool_w_bd, pool_scale, conv_w, conv_b, ln_g, ln_b):
    B, S, W = upc.shape
    ts = TOK_TILE
    r = ts // HALO
    vec = lambda n: pl.BlockSpec((1, n), lambda b, i: (0, 0))
    return pl.pallas_call(
        _poolconv_kernel,
        out_shape=jax.ShapeDtypeStruct((B, S, POOL_DIM + CONV_DIM), BF16),
        grid=(B, S // ts),
        in_specs=[pl.BlockSpec((1, ts, W), lambda b, i: (b, i, 0)),
                  pl.BlockSpec((1, HALO, W), lambda b, i: (b, jnp.maximum(i * r - 1, 0), 0)),
                  pl.BlockSpec((POOL_DIM, POOL_DIM), lambda b, i: (0, 0)),
                  vec(POOL_DIM),
                  pl.BlockSpec((CONV_WIDTH + 1, CONV_DIM), lambda b, i: (0, 0)),
                  vec(CONV_DIM), vec(CONV_DIM), vec(CONV_DIM)],
        out_specs=pl.BlockSpec((1, ts, POOL_DIM + CONV_DIM), lambda b, i: (b, i, 0)),
        scratch_shapes=[pltpu.VMEM((HALO + ts, W), F32), pltpu.VMEM((HALO + ts, CONV_DIM), F32),
                        pltpu.VMEM((HALO + ts, POOL_DIM), F32), pltpu.VMEM((HALO + ts, POOL_DIM), F32),
                        pltpu.VMEM((7, HALO + ts - 8, CONV_DIM), F32)],
        compiler_params=_cparams(("arbitrary", "arbitrary")),
        name="pool_conv",
    )(upc, upc, pool_w_bd, pool_scale.reshape(1, -1), conv_w, conv_b.reshape(1, -1),
      ln_g.reshape(1, -1), ln_b.reshape(1, -1))


N_STREAM = 2 * KV_GROUPS
CHUNK_W = CMP_STRIDE * 2 * KV_GROUPS * HEAD_DIM


def _cmp_weights(w1_k, w1_v):
    half = CMP_STRIDE * HEAD_DIM
    cols = []
    for s in range(N_STREAM):
        w1 = w1_k if s < KV_GROUPS else w1_v
        for part in range(2):
            blk = w1[part * half:(part + 1) * half].reshape(CMP_STRIDE, 1, HEAD_DIM, CMP_HIDDEN)
            z = jnp.zeros((CMP_STRIDE, N_STREAM, HEAD_DIM, CMP_HIDDEN), w1.dtype)
            z = lax.dynamic_update_slice(z, blk, (0, s, 0, 0))
            cols.append(z.reshape(CHUNK_W, CMP_HIDDEN))
    return jnp.concatenate(cols, axis=1).astype(BF16)


def _gelu_tanh(x):
    return 0.5 * x * (1.0 + jnp.tanh(math.sqrt(2.0 / math.pi) * (x + 0.044715 * (x * x * x))))


def _compress_kernel(c_ref, w_ref, pek_ref, pev_ref, w1k_ref, w1v_ref, w2k_ref, w2vt_ref, kc_ref, vct_ref):
    r = jnp.dot(c_ref[0], w_ref[...], preferred_element_type=F32)
    ncp = r.shape[0]
    pe_k = jnp.dot(pek_ref[...], w1k_ref[...], preferred_element_type=F32)[0:1]
    pe_v = jnp.dot(pev_ref[...], w1v_ref[...], preferred_element_type=F32)[0:1]
    for s in range(N_STREAM):
        a = r[:, s * 256:s * 256 + CMP_HIDDEN]
        b = r[:, s * 256 + CMP_HIDDEN:(s + 1) * 256]
        hid = a + pltpu.roll(b, ncp - 1, 0) + (pe_k if s < KV_GROUPS else pe_v)
        act = _gelu_tanh(hid).astype(BF16)
        if s < KV_GROUPS:
            kc_ref[0, s] = jnp.dot(act, w2k_ref[...], preferred_element_type=F32).astype(BF16)
        else:
            vt = lax.dot_general(w2vt_ref[...], act, (((1,), (1,)), ((), ())), preferred_element_type=F32)
            ones_row = (lax.broadcasted_iota(I32, vt.shape, 0) == HEAD_DIM).astype(F32)
            vct_ref[0, s - KV_GROUPS] = (vt + ones_row).astype(BF16)


def _compress(kvc, wcmp, pe_k, pe_v, w1_k, w1_v, w2_k, w2_v):
    B, S, _ = kvc.shape
    ncp = S // CMP_STRIDE
    chunks = kvc.reshape(B, ncp, CHUNK_W)
    pe8 = lambda pe: jnp.broadcast_to(pe.reshape(1, -1), (8, CMP_LEN * HEAD_DIM)).astype(BF16)
    w2k = jnp.pad(w2_k, ((0, 0), (0, LANES - HEAD_DIM))).astype(BF16)
    w2vt = jnp.pad(w2_v.T, ((0, LANES - HEAD_DIM), (0, 0))).astype(BF16)
    full = lambda a: pl.BlockSpec(a.shape, lambda b: (0,) * a.ndim)
    args = (wcmp, pe8(pe_k), pe8(pe_v), w1_k.astype(BF16), w1_v.astype(BF16), w2k, w2vt)
    return pl.pallas_call(
        _compress_kernel,
        out_shape=[jax.ShapeDtypeStruct((B, KV_GROUPS, ncp, LANES), BF16),
                   jax.ShapeDtypeStruct((B, KV_GROUPS, LANES, ncp), BF16)],
        grid=(B,),
        in_specs=[pl.BlockSpec((1, ncp, CHUNK_W), lambda b: (b, 0, 0))] + [full(a) for a in args],
        out_specs=[pl.BlockSpec((1, KV_GROUPS, ncp, LANES), lambda b: (b, 0, 0, 0)),
                   pl.BlockSpec((1, KV_GROUPS, LANES, ncp), lambda b: (b, 0, 0, 0))],
        compiler_params=_cparams(("arbitrary",)),
        name="compress",
    )(chunks, *args)


def _bias_tables(rel_bias, S):
    nq = S // Q_TILE
    ncp = S // CMP_STRIDE
    rb = rel_bias.reshape(N_BUCKETS, KV_GROUPS, HPG).transpose(1, 2, 0) * LOG2E
    far = rb[:, :, N_BUCKETS - 1]
    far_hi = far.astype(BF16)
    far_lo = (far - far_hi.astype(F32)).astype(BF16)
    far_sum = far_hi.astype(F32) + far_lo.astype(F32)
    i = np.arange(Q_TILE)[None, :]

    def table(d, valid, sub):
        onehot = jax.nn.one_hot(_t5_bucket(jnp.asarray(d, I32)), N_BUCKETS, dtype=F32)
        t = jnp.einsum('rib,ghb->grhi', onehot, rb, precision=lax.Precision.HIGHEST)
        t = jnp.where(jnp.asarray(valid)[None, :, None, :], t - sub[:, None, :, None], NEG_INF)
        return t.reshape(KV_GROUPS, d.shape[0], ROWS)

    d = i - np.arange(NEAR_KEYS)[:, None] + (NEAR_KEYS - Q_TILE)
    t_near = table(d, d >= 0, far_sum)
    d = i - np.arange(WIN_KEYS)[:, None] + (WIN_KEYS - Q_TILE)
    t_win = table(d, (d >= 0) & (d < WINDOW), jnp.zeros_like(far_sum))
    c0 = (Q_TILE // CMP_STRIDE) * (nq - 1)
    d = i - CMP_STRIDE * (np.arange(c0 + ncp)[:, None] - c0) - (CMP_LEN - 1)
    t_cmp = table(d, d >= 0, jnp.zeros_like(far_sum))
    rows = jnp.zeros((KV_GROUPS, HEAD_DIM, HPG, Q_TILE), F32)
    rows = rows.at[:, 0].set(1.0)
    rows = rows.at[:, 1].set(jnp.broadcast_to(far_hi.astype(F32)[:, :, None], (KV_GROUPS, HPG, Q_TILE)))
    rows = rows.at[:, 2].set(jnp.broadcast_to(far_lo.astype(F32)[:, :, None], (KV_GROUPS, HPG, Q_TILE)))
    return t_near, t_win, t_cmp, rows.reshape(KV_GROUPS, HEAD_DIM, ROWS).astype(BF16)


def _overlap_t(S):
    ncp = S // CMP_STRIDE
    n = np.arange(ncp)[None, :]
    jb = np.arange(LANES)[:, None]
    end = n * CMP_STRIDE + CMP_LEN - 1
    start = n * CMP_STRIDE
    ov = (end >= jb * SEL_BLOCK) & (start < (jb + 1) * SEL_BLOCK) & (n < ncp - 1)
    return jnp.asarray(ov.astype(np.float32), BF16)


def _key_tiles(ref, k0, n):
    t0 = k0 // LANES
    return jnp.concatenate([ref[0, 0, t0 + u] for u in range(n)], axis=1)


def _attn_kernel(qt_ref, qc_ref, kc_ref, vct_ref, ks_ref, vs_ref, kw_ref, vw_ref, gt_ref, tn_ref, tw_ref, tct_ref,
                 ovt_ref, o_ref, qa_ref, acc_ref, *, c0):
    qi = pl.program_id(2)
    qt = qt_ref[0, 0, 0]
    ncp = kc_ref.shape[2]

    k1 = Q_TILE * qi + (KEY_PAD + Q_TILE)

    qa_ref[0:HEAD_DIM, :] = qt
    qa_ref[HEAD_DIM:LANES, :] = qc_ref[0]

    r0 = pl.multiple_of(c0 - (Q_TILE // CMP_STRIDE) * qi, 8)
    st = (jnp.dot(kc_ref[0, 0][:, 0:HEAD_DIM], qt, preferred_element_type=F32)
          + tct_ref[0, pl.ds(r0, ncp), :])
    w0 = pl.multiple_of(k1 - WIN_KEYS, LANES)
    sw = jnp.dot(kw_ref[0, pl.ds(w0, WIN_KEYS), :], qa_ref[0:LANES, :], preferred_element_type=F32) + tw_ref[0]

    mc = jnp.maximum(jnp.max(st, axis=0, keepdims=True), -1e20)
    pc = jnp.exp2(st - mc).astype(BF16)
    acc_c = jnp.dot(vct_ref[0, 0], pc, preferred_element_type=F32)
    inv_c = 1.0 / jnp.maximum(acc_c[HEAD_DIM:HEAD_DIM + 1], 1e-30)
    o_c = acc_c[0:HEAD_DIM] * inv_c

    imp = None
    for h in range(HPG):
        cols = slice(h * Q_TILE, (h + 1) * Q_TILE)
        part = jnp.dot(ovt_ref[...], pc[:, cols], preferred_element_type=F32) * inv_c[:, cols]
        imp = part if imp is None else imp + part

    def flash_step(carry, s, vt):
        m, acc = carry
        m_new = jnp.maximum(m, jnp.max(s, axis=0, keepdims=True))
        alpha = jnp.exp2(m - m_new)
        p = jnp.exp2(s - m_new)
        acc = acc * alpha + jnp.dot(vt, p.astype(BF16), preferred_element_type=F32)
        return m_new, acc

    def normalised(acc):
        return acc[0:HEAD_DIM] * (1.0 / acc[HEAD_DIM:HEAD_DIM + 1])

    p = jnp.exp2(sw - jnp.max(sw, axis=0, keepdims=True)).astype(BF16)
    o_w = normalised(jnp.dot(_key_tiles(vw_ref, w0, WIN_KEYS // LANES), p, preferred_element_type=F32))

    jb = lax.broadcasted_iota(I32, (LANES, Q_TILE), 0)
    ii = lax.broadcasted_iota(I32, (LANES, Q_TILE), 1)
    cur = (Q_TILE // SEL_BLOCK) * qi + (ii >= SEL_BLOCK).astype(I32)
    forced = (jb == 0) | (jb == cur) | (jb == cur - 1)
    score = jnp.where(jb <= cur, jnp.where(forced, -jnp.inf, imp), NEG_INF)
    for _ in range(SEL_TOPN - N_FORCED):
        mx = jnp.max(score, axis=0, keepdims=True)
        first = jnp.min(jnp.where(score == mx, jb, LANES), axis=0, keepdims=True)
        score = jnp.where(jb == first, -jnp.inf, score)
    mbt = jnp.where(score == -jnp.inf, 0.0, MASK_BIAS).astype(BF16)
    for h in range(HPG):
        qa_ref[LANES:2 * LANES, h * Q_TILE:(h + 1) * Q_TILE] = mbt

    n0 = pl.multiple_of(k1 - NEAR_KEYS, LANES)
    half = NEAR_KEYS // 2
    sa = (jnp.dot(ks_ref[0, pl.ds(n0, half), :], qa_ref[...], preferred_element_type=F32)
          + tn_ref[0, 0:half, :])
    sb = (jnp.dot(ks_ref[0, pl.ds(n0 + half, half), :], qa_ref[...], preferred_element_type=F32)
          + tn_ref[0, half:NEAR_KEYS, :])
    m_near = jnp.maximum(jnp.max(sa, axis=0, keepdims=True), jnp.max(sb, axis=0, keepdims=True))
    pa = jnp.exp2(sa - m_near).astype(BF16)
    da = jnp.dot(_key_tiles(vs_ref, n0, half // LANES), pa, preferred_element_type=F32)
    pb = jnp.exp2(sb - m_near).astype(BF16)
    acc_near = da + jnp.dot(_key_tiles(vs_ref, n0 + half, half // LANES), pb, preferred_element_type=F32)
    n_pairs = (jnp.maximum(n0 - KEY_PAD, 0) + KEY_PAD - 1) // KEY_PAD

    def far_scores(j):
        k0 = pl.multiple_of(n0 - FAR_TILE * (j + 1), LANES)
        return jnp.dot(ks_ref[0, pl.ds(k0, FAR_TILE), :], qa_ref[...], preferred_element_type=F32)

    def far_values(j):
        return _key_tiles(vs_ref, pl.multiple_of(n0 - FAR_TILE * (j + 1), LANES), FAR_TILE // LANES)

    def fast_body(jj, top):
        tiles = [FAR_GROUP * jj + u for u in range(FAR_GROUP)]
        scores = [far_scores(tiles[0]), far_scores(tiles[1])]
        acc = None
        for u, j in enumerate(tiles):
            if u + 2 < FAR_GROUP:
                scores.append(far_scores(tiles[u + 2]))
            p = jnp.exp2(scores[u] - m_near).astype(BF16)
            d = jnp.dot(far_values(j), p, preferred_element_type=F32)
            acc = d if acc is None else acc + d
            top = jnp.maximum(top, jnp.max(scores[u], axis=0, keepdims=True))
        acc_ref[...] += acc
        return top

    acc_ref[...] = acc_near
    top = lax.fori_loop(0, n_pairs, fast_body, m_near)

    @pl.when(jnp.max(top - m_near) > FAR_HEADROOM)
    def _():
        def safe_body(j, carry):
            return flash_step(carry, far_scores(j), far_values(j))

        acc_ref[...] = lax.fori_loop(0, FAR_GROUP * n_pairs, safe_body, (m_near, acc_near))[1]

    o_s = normalised(acc_ref[...])

    gtt = gt_ref[0].T
    gate = lambda br: jnp.concatenate([gtt[br * HPG + h:br * HPG + h + 1, :] for h in range(HPG)], axis=1)
    o_ref[0, 0, 0] = (gate(0) * o_c + gate(1) * o_s + gate(2) * o_w).astype(BF16)


def _attention(qt, qc, kc, vct, ks, vs, kw, vw, gt, tn, tw, tct, ovt):
    B, G, nq, _, _ = qt.shape
    S = nq * Q_TILE
    sp = S + KEY_PAD
    ncp = S // CMP_STRIDE
    c0 = (Q_TILE // CMP_STRIDE) * (nq - 1)
    per_g = lambda a: pl.BlockSpec((1,) + a.shape[1:], lambda b, g, i: (g,) + (0,) * (a.ndim - 1))
    val = pl.BlockSpec((1, 1, sp // LANES, LANES, LANES), lambda b, g, i: (b, g, 0, 0, 0))
    return pl.pallas_call(
        functools.partial(_attn_kernel, c0=c0),
        out_shape=jax.ShapeDtypeStruct((B, G, nq, HEAD_DIM, ROWS), BF16),
        grid=(B, G, nq),
        in_specs=[pl.BlockSpec((1, 1, 1, HEAD_DIM, ROWS), lambda b, g, i: (b, g, i, 0, 0)),
                  per_g(qc),
                  pl.BlockSpec((1, 1, ncp, LANES), lambda b, g, i: (b, g, 0, 0)),
                  pl.BlockSpec((1, 1, LANES, ncp), lambda b, g, i: (b, g, 0, 0)),
                  pl.BlockSpec((1, sp, 2 * LANES), lambda b, g, i: (b, 0, g)),
                  val,
                  pl.BlockSpec((1, sp, LANES), lambda b, g, i: (b, 0, g)),
                  val,
                  pl.BlockSpec((1, Q_TILE, LANES), lambda b, g, i: (b, i, g)),
                  per_g(tn), per_g(tw), per_g(tct),
                  pl.BlockSpec((LANES, ncp), lambda b, g, i: (0, 0))],
        out_specs=pl.BlockSpec((1, 1, 1, HEAD_DIM, ROWS), lambda b, g, i: (b, g, i, 0, 0)),
        scratch_shapes=[pltpu.VMEM((2 * LANES, ROWS), BF16), pltpu.VMEM((LANES, ROWS), F32)],
        compiler_params=_cparams(("arbitrary", "arbitrary", "arbitrary")),
        name="nsa_attention",
    )(qt, qc, kc, vct, ks, vs, kw, vw, gt, tn, tw, tct, ovt)


def _post_attn_kernel(x_ref, ypc_ref, yn_ref, wo_ref, g1_ref, n2_ref, sc_ref, sh_ref, rwh_ref, rwl_ref, rb_ref,
                      xo_ref, h2_ref, ei_ref, tw_ref, cnt_ref, run_ref, hd_ref):
    first = (pl.program_id(0) == 0) & (pl.program_id(1) == 0)

    @pl.when(first)
    def _():
        run_ref[...] = jnp.zeros_like(run_ref)

    tm = x_ref.shape[1]
    half = wo_ref.shape[0] // 2
    tiles = []
    for t in range(TILES):
        groups = []
        for g in range(KV_GROUPS):
            blk = yn_ref[0, g, t].astype(F32)
            for hh in range(HPG):
                hd_ref[hh * HEAD_DIM:(hh + 1) * HEAD_DIM, :] = blk[:, hh * Q_TILE:(hh + 1) * Q_TILE]
            groups.append(hd_ref[...].T)
        tiles.append(jnp.concatenate(groups, axis=1))
    ynsa = jnp.concatenate(tiles, axis=0).astype(BF16)
    mixed = (jnp.dot(ypc_ref[0], wo_ref[0:half, :], preferred_element_type=F32)
             + jnp.dot(ynsa, wo_ref[half:, :], preferred_element_type=F32))
    x = x_ref[0] + g1_ref[0] * mixed
    xo_ref[0] = x
    y = x * lax.rsqrt(jnp.mean(x * x, axis=-1, keepdims=True) + EPS) * n2_ref[...]
    h2 = y * (1.0 + sc_ref[0]) + sh_ref[0]
    _rows_to_tiles(h2_ref, h2)
    hh = h2.astype(BF16)
    hl = (h2 - hh.astype(F32)).astype(BF16)
    logit = (jnp.dot(hh, rwh_ref[...], preferred_element_type=F32)
             + jnp.dot(hl, rwh_ref[...], preferred_element_type=F32)
             + jnp.dot(hh, rwl_ref[...], preferred_element_type=F32)) + rb_ref[...]
    lane = lax.broadcasted_iota(I32, (tm, LANES), 1)
    vals, hots, idxs = [], [], []
    for _ in range(TOP_K):
        mx = jnp.max(logit, axis=1, keepdims=True)
        idx = jnp.min(jnp.where(logit == mx, lane, LANES), axis=1, keepdims=True)
        hot = lane == idx
        vals.append(mx)
        hots.append(hot)
        idxs.append(idx)
        logit = jnp.where(hot, -jnp.inf, logit)
    ex = [jnp.exp(v - vals[0]) for v in vals]
    inv = 1.0 / (ex[0] + ex[1] + ex[2] + ex[3])
    assign = (hots[0] | hots[1] | hots[2] | hots[3]).astype(BF16)
    r = lax.broadcasted_iota(I32, (tm, tm), 0)
    c = lax.broadcasted_iota(I32, (tm, tm), 1)
    before = jnp.dot((c < r).astype(BF16), assign, preferred_element_type=F32) + run_ref[...]
    ei = jnp.zeros((tm, LANES), I32)
    tw = jnp.zeros((tm, LANES), F32)
    for k in range(TOP_K):
        e_k = idxs[k]
        r_k = jnp.sum(jnp.where(hots[k], before, 0.0), axis=1, keepdims=True).astype(I32)
        ei = jnp.where(lane == k, e_k, jnp.where(lane == TOP_K + k, r_k, ei))
        tw = jnp.where(lane == k, ex[k] * inv, tw)
    ei_ref[0] = ei
    tw_ref[0] = tw
    run_ref[...] = run_ref[...] + jnp.sum(assign.astype(F32), axis=0, keepdims=True)
    cnt_ref[...] = run_ref[...]


def _post_attn(x, ypc, ynsa, w_out, g1, n2g, sc2, sh2, rw_hi, rw_lo, rb):
    B, S, D = x.shape
    tm = TOK_TILE
    tok = lambda w: pl.BlockSpec((1, tm, w), lambda b, i: (b, i, 0))
    per_b = pl.BlockSpec((1, 1, D), lambda b, i: (b, 0, 0))
    full = lambda a: pl.BlockSpec(a.shape, lambda b, i: (0,) * a.ndim)
    return pl.pallas_call(
        _post_attn_kernel,
        out_shape=[jax.ShapeDtypeStruct((B, S, D), F32), jax.ShapeDtypeStruct((B * S * ROW_TILES, LANES), F32),
                   jax.ShapeDtypeStruct((B, S, LANES), I32), jax.ShapeDtypeStruct((B, S, LANES), F32),
                   jax.ShapeDtypeStruct((1, LANES), F32)],
        grid=(B, S // tm),
        in_specs=[tok(D), tok(ypc.shape[-1]),
                  pl.BlockSpec((1, KV_GROUPS, TILES, HEAD_DIM, ROWS), lambda b, i: (b, 0, i, 0, 0)),
                  full(w_out), per_b,
                  pl.BlockSpec((1, D), lambda b, i: (0, 0)), per_b, per_b,
                  full(rw_hi), full(rw_lo), full(rb)],
        out_specs=[tok(D), pl.BlockSpec((tm * ROW_TILES, LANES), lambda b, i: (b * (S // tm) + i, 0)),
                   tok(LANES), tok(LANES), pl.BlockSpec((1, LANES), lambda b, i: (0, 0))],
        scratch_shapes=[pltpu.VMEM((1, LANES), F32), pltpu.VMEM((HPG * HEAD_DIM, Q_TILE), F32)],
        compiler_params=_cparams(("arbitrary", "arbitrary")),
        name="post_attn_router",
    )(x, ypc, ynsa, w_out, g1, n2g.reshape(1, D), sc2, sh2, rw_hi, rw_lo, rb)


ROW_TILES = 8
DMA_UNROLL = 4
ZERO_BITS = tuple(1 << b for b in reversed(range((EXPERT_BLOCK - 1).bit_length())))


def _rows_to_tiles(ref, val):
    n = val.shape[0]
    for s in range(ROW_TILES):
        ref[pl.ds(s, n, stride=ROW_TILES), :] = val[:, s * LANES:(s + 1) * LANES]


def _tiles_to_rows(ref, n):
    return jnp.concatenate([ref[pl.ds(s, n, stride=ROW_TILES), :] for s in range(ROW_TILES)], axis=1)


def _tile_at(ref, i):
    return ref.at[pl.ds(pl.multiple_of(i * ROW_TILES, ROW_TILES), ROW_TILES), :]


def _dispatch_kernel(fill0_ref, filln_ref, dest_ref, h_ref, xs_ref, zero_ref, sem, zsem):
    tm = h_ref.shape[0] // ROW_TILES

    @pl.when(pl.program_id(0) == 0)
    def _():
        zero_ref[...] = jnp.zeros_like(zero_ref)

        def expert(e, c):
            n = filln_ref[e]
            for wait in (False, True):
                for bit in ZERO_BITS:
                    @pl.when((n & bit) != 0)
                    def _():
                        first = fill0_ref[e] + (n & ~(2 * bit - 1))
                        dst = xs_ref.at[pl.ds(pl.multiple_of(first * ROW_TILES, ROW_TILES), bit * ROW_TILES), :]
                        cp = pltpu.make_async_copy(zero_ref.at[pl.ds(0, bit * ROW_TILES), :], dst, zsem)
                        cp.wait() if wait else cp.start()
            return c

        lax.fori_loop(0, N_EXPERTS, expert, 0)

    def body(i, c):
        for u in range(DMA_UNROLL):
            t = i * DMA_UNROLL + u
            src = _tile_at(h_ref, t)
            for k in range(TOP_K):
                pltpu.make_async_copy(src, _tile_at(xs_ref, dest_ref[0, 0, t * TOP_K + k]), sem).start(
                    priority=k % 2)
        return c

    lax.fori_loop(0, tm // DMA_UNROLL, body, 0)
    for _ in range(TOP_K):
        pltpu.make_async_copy(h_ref, xs_ref.at[pl.ds(0, tm * ROW_TILES), :], sem).wait()


def _dispatch(dest, h2t, fill0, filln, n_slots):
    tm = TOK_TILE
    nt = h2t.shape[0] // (tm * ROW_TILES)
    return pl.pallas_call(
        _dispatch_kernel,
        out_shape=jax.ShapeDtypeStruct((n_slots * ROW_TILES, LANES), F32),
        grid_spec=pltpu.PrefetchScalarGridSpec(
            num_scalar_prefetch=2,
            grid=(nt,),
            in_specs=[pl.BlockSpec((1, 1, tm * TOP_K), lambda i, f0, fn: (i, 0, 0), memory_space=pltpu.SMEM),
                      pl.BlockSpec((tm * ROW_TILES, LANES), lambda i, f0, fn: (i, 0))],
            out_specs=pl.BlockSpec(memory_space=pl.ANY),
            scratch_shapes=[pltpu.VMEM((ZERO_BITS[0] * ROW_TILES, LANES), F32), pltpu.SemaphoreType.DMA(()),
                            pltpu.SemaphoreType.DMA(())]),
        compiler_params=_cparams(("arbitrary",)),
        name="moe_dispatch",
    )(fill0, filln, dest.reshape(nt, 1, tm * TOP_K), h2t)


W_CHUNK = 512


def _expert_kernel(be_ref, nu_ref, x_ref, wgu_ref, bgu_ref, wd_ref, bd_ref, y_ref, wgu_s, wd_s):
    i = pl.program_id(0)

    @pl.when(i < nu_ref[0])
    def _():
        @pl.when((i == 0) | (be_ref[i] != be_ref[jnp.maximum(i - 1, 0)]))
        def _():
            for c in range(0, wgu_s.shape[1], W_CHUNK):
                wgu_s[:, c:c + W_CHUNK] = wgu_ref[0, :, c:c + W_CHUNK].astype(BF16)
            for c in range(0, wd_s.shape[1], W_CHUNK):
                wd_s[:, c:c + W_CHUNK] = wd_ref[0, :, c:c + W_CHUNK].astype(BF16)

        F = wd_s.shape[0]
        x = _tiles_to_rows(x_ref, EXPERT_BLOCK).astype(BF16)
        gu = jnp.dot(x, wgu_s[...], preferred_element_type=F32) + bgu_ref[0]
        gate = jnp.minimum(gu[:, :F], SWIGLU_LIMIT)
        up = jnp.clip(gu[:, F:], -SWIGLU_LIMIT, SWIGLU_LIMIT)
        act = (up + 1.0) * gate * jax.nn.sigmoid(SWIGLU_ALPHA * gate)
        _rows_to_tiles(y_ref, jnp.dot(act.astype(BF16), wd_s[...], preferred_element_type=F32) + bd_ref[0])


def _experts(blk_e, n_used, xs, layer, w_gu, b_gu, w_down, b_down):
    L, E, D, F2 = w_gu.shape
    F = F2 // 2
    rows = EXPERT_BLOCK * ROW_TILES
    nb = xs.shape[0] // rows
    blk = lambda i, be, nu: (jnp.minimum(i, nu[0] - 1), 0)
    per_e = lambda i, be, nu: (layer, be[i], 0, 0)
    return pl.pallas_call(
        _expert_kernel,
        out_shape=jax.ShapeDtypeStruct(xs.shape, F32),
        grid_spec=pltpu.PrefetchScalarGridSpec(
            num_scalar_prefetch=2,
            grid=(nb,),
            in_specs=[pl.BlockSpec((rows, LANES), blk),
                      pl.BlockSpec((None, 1, D, F2), per_e),
                      pl.BlockSpec((None, 1, 1, F2), per_e),
                      pl.BlockSpec((None, 1, F, D), per_e),
                      pl.BlockSpec((None, 1, 1, D), per_e)],
            out_specs=pl.BlockSpec((rows, LANES), blk),
            scratch_shapes=[pltpu.VMEM((D, F2), BF16), pltpu.VMEM((F, D), BF16)]),
        compiler_params=_cparams(("arbitrary",)),
        name="moe_experts",
    )(blk_e, n_used, xs, w_gu, b_gu.reshape(L, E, 1, F2), w_down, b_down.reshape(L, E, 1, D))


def _combine_kernel(dest_ref, y_ref, x_ref, tw_ref, g2_ref, fg_ref, o_ref, rows_ref, sem, *, final, n_tiles):
    s = pl.program_id(0)
    tm = x_ref.shape[1]

    @pl.when(s < n_tiles)
    def _():
        slot = s % 2

        def body(i, c):
            for u in range(DMA_UNROLL):
                t = i * DMA_UNROLL + u
                for k in range(TOP_K):
                    pltpu.make_async_copy(_tile_at(y_ref, dest_ref[0, 0, t * TOP_K + k]),
                                          _tile_at(rows_ref.at[slot, k], t), sem.at[slot]).start(priority=k % 2)
            return c

        lax.fori_loop(0, tm // DMA_UNROLL, body, 0)

    @pl.when(s > 0)
    def _():
        slot = (s - 1) % 2
        for k in range(TOP_K):
            pltpu.make_async_copy(y_ref.at[pl.ds(0, tm * ROW_TILES), :], rows_ref.at[slot, k], sem.at[slot]).wait()
        tw = tw_ref[0]
        moe = tw[:, 0:1] * _tiles_to_rows(rows_ref.at[slot, 0], tm)
        for k in range(1, TOP_K):
            moe = moe + tw[:, k:k + 1] * _tiles_to_rows(rows_ref.at[slot, k], tm)
        x = x_ref[0] + g2_ref[0] * moe
        if final:
            x = x * lax.rsqrt(jnp.mean(x * x, axis=-1, keepdims=True) + EPS) * fg_ref[...]
        o_ref[0] = x


def _combine(dest, y, x, tw, g2, final_g, final):
    B, S, D = x.shape
    tm = TOK_TILE
    nt = S // tm
    n_tiles = B * nt
    done = lambda s: jnp.maximum(s - 1, 0)
    tok = lambda w: pl.BlockSpec((1, tm, w), lambda s: (done(s) // nt, done(s) % nt, 0))
    return pl.pallas_call(
        functools.partial(_combine_kernel, final=final, n_tiles=n_tiles),
        out_shape=jax.ShapeDtypeStruct((B, S, D), F32),
        grid=(n_tiles + 1,),
        in_specs=[pl.BlockSpec((1, 1, tm * TOP_K), lambda s: (jnp.minimum(s, n_tiles - 1), 0, 0),
                               memory_space=pltpu.SMEM),
                  pl.BlockSpec(memory_space=pl.ANY),
                  tok(D), tok(LANES),
                  pl.BlockSpec((1, 1, D), lambda s: (done(s) // nt, 0, 0)),
                  pl.BlockSpec((1, D), lambda s: (0, 0))],
        out_specs=tok(D),
        scratch_shapes=[pltpu.VMEM((2, TOP_K, tm * ROW_TILES, LANES), F32), pltpu.SemaphoreType.DMA((2,))],
        compiler_params=_cparams(("arbitrary",)),
        name="moe_combine",
    )(dest.reshape(n_tiles, 1, tm * TOP_K), y, x, tw, g2, final_g.reshape(1, D))


def _moe(x, h2, ei, tw, counts, g2, layer, w_gu, b_gu, w_down, b_down, final_g, final):
    B, S, D = x.shape
    N = B * S
    n_slots = -(-(N * TOP_K + N_EXPERTS * EXPERT_BLOCK) // EXPERT_BLOCK) * EXPERT_BLOCK
    nb = n_slots // EXPERT_BLOCK
    cnt = counts[0, :N_EXPERTS].astype(I32)
    padded = (cnt + EXPERT_BLOCK - 1) // EXPERT_BLOCK * EXPERT_BLOCK
    pend = jnp.cumsum(padded)
    pstart = pend - padded
    ei2 = ei.reshape(N, LANES)
    dest = (pstart[ei2[:, 0:TOP_K]] + ei2[:, TOP_K:2 * TOP_K]).reshape(N * TOP_K)
    blk_start = jnp.arange(nb, dtype=I32) * EXPERT_BLOCK
    blk_e = jnp.minimum(jnp.sum((pend[None, :] <= blk_start[:, None]).astype(I32), axis=1), N_EXPERTS - 1)
    n_used = (pend[-1:] // EXPERT_BLOCK).astype(I32)
    xs = _dispatch(dest, h2, pstart + cnt, padded - cnt, n_slots)
    y = _experts(blk_e, n_used, xs, layer, w_gu, b_gu, w_down, b_down)
    return _combine(dest, y, x, tw, g2, final_g, final)


def kernel(x, c, w_mod, b_mod, norm1_g, norm2_g, w_in, w_out, pool_w, pool_scale, conv_w, conv_b, conv_ln_g,
           conv_ln_b, cmp_pe_k, cmp_pe_v, cmp_w1_k, cmp_w2_k, cmp_w1_v, cmp_w2_v, rel_bias, router_w, router_b,
           expert_w_gu, expert_b_gu, expert_w_down, expert_b_down, final_g):
    B, S, D = x.shape
    L = w_mod.shape[0]
    assert S % TOK_TILE == 0 and S // SEL_BLOCK <= LANES and D == ROW_TILES * LANES
    mod = _modulation(c, w_mod, b_mod)
    t_near, t_win, t_cmp, qc = _bias_tables(rel_bias, S)
    ovt = _overlap_t(S)
    w_big = _in_weight(w_in)
    cg = POOL_DIM // POOL_GROUPS
    for l in range(L):
        m6 = mod[l].reshape(B, 6, 1, D)
        sh1, sc1, g1, sh2, sc2, g2 = (m6[:, k] for k in range(6))
        upc, qt, kvc, ks, vs, kw, vw, gt = _in_proj(x, norm1_g[l], sc1, sh1, w_big[l])
        pw_bd = jnp.zeros((POOL_DIM, POOL_DIM), F32)
        for g in range(POOL_GROUPS):
            pw_bd = lax.dynamic_update_slice(pw_bd, pool_w[l, g], (g * cg, g * cg))
        cw = jnp.pad(conv_w[l], ((0, 1), (0, 0)))
        ypc = _pool_conv(upc, pw_bd.astype(BF16), pool_scale[l], cw, conv_b[l], conv_ln_g[l], conv_ln_b[l])
        kc, vct = _compress(kvc, _cmp_weights(cmp_w1_k[l], cmp_w1_v[l]), cmp_pe_k[l], cmp_pe_v[l],
                            cmp_w1_k[l], cmp_w1_v[l], cmp_w2_k[l], cmp_w2_v[l])
        ynsa = _attention(qt, qc, kc, vct, ks, vs, kw, vw, gt, t_near, t_win, t_cmp, ovt)
        rw = jnp.pad(router_w[l], ((0, 0), (0, LANES - N_EXPERTS)))
        rw_hi = rw.astype(BF16)
        rw_lo = (rw - rw_hi.astype(F32)).astype(BF16)
        rb = jnp.pad(router_b[l].reshape(1, -1), ((0, 0), (0, LANES - N_EXPERTS)), constant_values=NEG_INF)
        x, h2, ei, tw, counts = _post_attn(x, ypc, ynsa, w_out[l].astype(BF16), g1, norm2_g[l], sc2, sh2,
                                           rw_hi, rw_lo, rb)
        x = _moe(x, h2, ei, tw, counts, g2, l, expert_w_gu, expert_b_gu, expert_w_down, expert_b_down,
                 final_g, final=(l == L - 1))
    return x
```

```python
import functools
import math

import jax
import jax.numpy as jnp
import numpy as np
from jax import lax
from jax.experimental import pallas as pl
from jax.experimental.pallas import tpu as pltpu

F32 = jnp.float32
BF16 = jnp.bfloat16
I32 = jnp.int32

HEAD_DIM = 64
POOL_DIM = 256
POOL_GROUPS = 4
POOL_WINDOWS = (2, 4, 8, 16)
CONV_DIM = 256
CONV_WIDTH = 31
NSA_HEADS = 8
KV_GROUPS = 2
HPG = 4
CMP_LEN = 32
CMP_STRIDE = 16
CMP_HIDDEN = 128
SEL_BLOCK = 64
SEL_TOPN = 16
N_FORCED = 3
WINDOW = 512
Q_TILE = 128
N_BUCKETS = 32
MAX_DISTANCE = 1024
N_EXPERTS = 32
TOP_K = 4
SWIGLU_ALPHA = 1.702
SWIGLU_LIMIT = 7.0
EPS = 1e-5
NEG_INF = -1e30

LANES = 128
ROWS = HPG * Q_TILE
LOG2E = 1.4426950408889634
FAR_HEADROOM = 100.0
NEAR_KEYS = 1024
WIN_KEYS = WINDOW + Q_TILE
FAR_TILE = 512
FAR_GROUP = 4
KEY_PAD = FAR_GROUP * FAR_TILE
MASK_BIAS = -32768.0
TOK_TILE = 512
EXPERT_BLOCK = 512
VMEM_LIMIT = 56 * 1024 * 1024


def _cparams(sem, vmem=VMEM_LIMIT):
    return pltpu.CompilerParams(dimension_semantics=sem, vmem_limit_bytes=vmem)


def _t5_bucket(n):
    n = jnp.maximum(n, 0)
    max_exact = N_BUCKETS // 2
    nf = jnp.maximum(n, 1).astype(F32)
    large = max_exact + (jnp.log(nf / max_exact) / math.log(MAX_DISTANCE / max_exact)
                         * (N_BUCKETS - max_exact)).astype(I32)
    large = jnp.minimum(large, N_BUCKETS - 1)
    return jnp.where(n < max_exact, n, large)


def _mod_kernel(c_ref, w_ref, b_ref, o_ref):
    c = c_ref[...]
    cond = c * jax.nn.sigmoid(c)
    o_ref[0] = jnp.dot(cond.astype(BF16), w_ref[0].astype(BF16),
                       preferred_element_type=F32) + b_ref[0]


def _modulation(c, w_mod, b_mod):
    L, D, W = w_mod.shape
    B = c.shape[0]
    tn = 1536
    return pl.pallas_call(
        _mod_kernel,
        out_shape=jax.ShapeDtypeStruct((L, B, W), F32),
        grid=(L, W // tn),
        in_specs=[pl.BlockSpec((B, D), lambda l, j: (0, 0)),
                  pl.BlockSpec((1, D, tn), lambda l, j: (l, 0, j)),
                  pl.BlockSpec((1, 1, tn), lambda l, j: (l, 0, j))],
        out_specs=pl.BlockSpec((1, B, tn), lambda l, j: (l, 0, j)),
        compiler_params=_cparams(("arbitrary", "arbitrary")),
        name="modulation",
    )(c, w_mod, b_mod.reshape(L, 1, W))


C_UPC, C_Q, C_KVC, C_K, C_V, C_GT, C_END = 0, 768, 1280, 1536, 1792, 2048, 2304


def _in_weight(w_in):
    col = lambda a, n: w_in[:, :, a:a + n]
    zero = lambda n: jnp.zeros(w_in.shape[:2] + (n,), w_in.dtype)
    parts = [col(0, 768), col(768, 512) * (HEAD_DIM ** -0.5 * LOG2E), col(1280, 256)]
    for sel, win in ((1536, 1792), (1664, 1920)):
        for g in range(KV_GROUPS):
            parts += [col(sel + g * HEAD_DIM, HEAD_DIM), col(win + g * HEAD_DIM, HEAD_DIM)]
    for g in range(KV_GROUPS):
        parts += [col(2048 + br * NSA_HEADS + g * HPG, HPG) for br in range(3)] + [zero(LANES - 3 * HPG)]
    return jnp.concatenate(parts, axis=2).astype(BF16)


PAD_STEPS = KEY_PAD // TOK_TILE
TILES = TOK_TILE // Q_TILE


def _inproj_kernel(x_ref, g_ref, sc_ref, sh_ref, w_ref,
                   upc_ref, qt_ref, kvc_ref, ks_ref, vs_ref, kw_ref, vw_ref, gt_ref):
    step = pl.program_id(1)
    tm = x_ref.shape[1]

    @pl.when(step < PAD_STEPS)
    def _():
        lane = lax.broadcasted_iota(I32, (tm, 4 * LANES), 1)
        ks_ref[0] = jnp.where(lane % (2 * LANES) == HEAD_DIM, MASK_BIAS, 0.0).astype(BF16)
        lane = lax.broadcasted_iota(I32, (tm, 2 * LANES), 1)
        kw_ref[0] = jnp.where(lane % LANES == HEAD_DIM, MASK_BIAS, 0.0).astype(BF16)
        for g in range(KV_GROUPS):
            for t in range(TILES):
                vs_ref[0, g, t] = jnp.zeros((LANES, LANES), BF16)
                vw_ref[0, g, t] = jnp.zeros((LANES, LANES), BF16)

    @pl.when(step >= PAD_STEPS)
    def _():
        ti = step - PAD_STEPS
        x = x_ref[0]
        y = x * lax.rsqrt(jnp.mean(x * x, axis=-1, keepdims=True) + EPS) * g_ref[...]
        h = y * (1.0 + sc_ref[0]) + sh_ref[0]
        z = jnp.dot(h.astype(BF16), w_ref[...], preferred_element_type=F32)
        upc_ref[0] = z[:, C_UPC:C_Q]
        kvc_ref[0] = z[:, C_KVC:C_K].astype(BF16)
        gt_ref[0] = jax.nn.sigmoid(z[:, C_GT:C_END])
        zq = z[:, C_Q:C_KVC].T
        for g in range(KV_GROUPS):
            for t in range(TILES):
                qt_ref[0, g, t] = jnp.concatenate(
                    [zq[(g * HPG + hh) * HEAD_DIM:(g * HPG + hh + 1) * HEAD_DIM, t * Q_TILE:(t + 1) * Q_TILE]
                     for hh in range(HPG)], axis=1).astype(BF16)
        lane = lax.broadcasted_iota(I32, (tm, LANES), 1)
        row = lax.broadcasted_iota(I32, (tm, LANES), 0)
        low = lane < HEAD_DIM
        consts = ((lane == HEAD_DIM + 1) | (lane == HEAD_DIM + 2)).astype(F32)
        onehot = (lane == (ti * tm + row) // SEL_BLOCK).astype(BF16)
        ones_col = (lane == HEAD_DIM).astype(F32)
        for g in range(KV_GROUPS):
            kk = z[:, C_K + g * LANES:C_K + (g + 1) * LANES]
            ks_ref[0, :, 2 * g * LANES:(2 * g + 1) * LANES] = jnp.where(low, kk, consts).astype(BF16)
            ks_ref[0, :, (2 * g + 1) * LANES:(2 * g + 2) * LANES] = onehot
            kw_ref[0, :, g * LANES:(g + 1) * LANES] = jnp.where(low, pltpu.roll(kk, HEAD_DIM, 1), 0.0).astype(BF16)
            vv = z[:, C_V + g * LANES:C_V + (g + 1) * LANES]
            for v_ref, half in ((vs_ref, vv), (vw_ref, pltpu.roll(vv, HEAD_DIM, 1))):
                zv = jnp.where(low, half, ones_col).T
                for t in range(TILES):
                    v_ref[0, g, t] = zv[:, t * Q_TILE:(t + 1) * Q_TILE].astype(BF16)


def _in_proj(x, g1, sc, sh, w_big):
    B, S, D = x.shape
    tm = TOK_TILE
    sp = S + KEY_PAD
    nq = S // Q_TILE
    cur = lambda i: jnp.maximum(i - PAD_STEPS, 0)
    tok = lambda w: pl.BlockSpec((1, tm, w), lambda b, i: (b, cur(i), 0))
    key = lambda w: pl.BlockSpec((1, tm, w), lambda b, i: (b, i, 0))
    val = pl.BlockSpec((1, KV_GROUPS, TILES, LANES, LANES), lambda b, i: (b, 0, i, 0, 0))
    val_shape = jax.ShapeDtypeStruct((B, KV_GROUPS, sp // LANES, LANES, LANES), BF16)
    return pl.pallas_call(
        _inproj_kernel,
        out_shape=[jax.ShapeDtypeStruct((B, S, 768), F32),
                   jax.ShapeDtypeStruct((B, KV_GROUPS, nq, HEAD_DIM, ROWS), BF16),
                   jax.ShapeDtypeStruct((B, S, 256), BF16),
                   jax.ShapeDtypeStruct((B, sp, 4 * LANES), BF16), val_shape,
                   jax.ShapeDtypeStruct((B, sp, 2 * LANES), BF16), val_shape,
                   jax.ShapeDtypeStruct((B, S, 256), F32)],
        grid=(B, S // tm + PAD_STEPS),
        in_specs=[tok(D),
                  pl.BlockSpec((1, D), lambda b, i: (0, 0)),
                  pl.BlockSpec((1, 1, D), lambda b, i: (b, 0, 0)),
                  pl.BlockSpec((1, 1, D), lambda b, i: (b, 0, 0)),
                  pl.BlockSpec((D, C_END), lambda b, i: (0, 0))],
        out_specs=[tok(768),
                   pl.BlockSpec((1, KV_GROUPS, TILES, HEAD_DIM, ROWS), lambda b, i: (b, 0, cur(i), 0, 0)),
                   tok(256), key(4 * LANES), val, key(2 * LANES), val, tok(256)],
        compiler_params=_cparams(("arbitrary", "arbitrary")),
        name="in_proj",
    )(x, g1.reshape(1, D), sc, sh, w_big)


HALO = 32


def _poolconv_kernel(cur_ref, halo_ref, pw_ref, ps_ref, cw_ref, cb_ref, lg_ref, lb_ref, o_ref,
                     ext_ref, v_ref, pa_ref, pb_ref, vsh_ref):
    ti = pl.program_id(1)
    ts = cur_ref.shape[1]
    rows = HALO + ts
    halo = halo_ref[0] * (ti > 0).astype(F32)
    ext_ref[0:HALO, :] = halo
    ext_ref[HALO:rows, :] = cur_ref[0]
    u = ext_ref[HALO:rows, 0:POOL_DIM]
    lane = lax.broadcasted_iota(I32, (ts, POOL_DIM), 1)
    grp = lane // (POOL_DIM // POOL_GROUPS)
    pa_ref[8:rows, :] = ext_ref[8:rows, 0:POOL_DIM] + ext_ref[7:rows - 1, 0:POOL_DIM]
    pooled = pa_ref[HALO:rows, :]
    pb_ref[16:rows, :] = pa_ref[16:rows, :] + pa_ref[14:rows - 2, :]
    pooled = jnp.where(grp >= 1, pb_ref[HALO:rows, :], pooled)
    pa_ref[24:rows, :] = pb_ref[24:rows, :] + pb_ref[20:rows - 4, :]
    pooled = jnp.where(grp >= 2, pa_ref[HALO:rows, :], pooled)
    pooled = jnp.where(grp == 3, pa_ref[HALO:rows, :] + pa_ref[HALO - 8:rows - 8, :], pooled)
    assert POOL_WINDOWS == (2, 4, 8, 16)
    wlane = jnp.full((ts, POOL_DIM), float(POOL_WINDOWS[-1]), F32)
    for g in range(POOL_GROUPS - 1):
        wlane = jnp.where(grp == g, float(POOL_WINDOWS[g]), wlane)
    t1 = (ti * ts + lax.broadcasted_iota(I32, (ts, POOL_DIM), 0) + 1).astype(F32)
    cnt = jnp.minimum(t1, wlane)
    pooled = pooled / cnt - u
    y_pool = jnp.dot(pooled.astype(BF16), pw_ref[...], preferred_element_type=F32) * ps_ref[...]
    o_ref[0, :, 0:POOL_DIM] = y_pool.astype(BF16)
    uv = ext_ref[:, POOL_DIM:POOL_DIM + CONV_DIM]
    ug = ext_ref[:, POOL_DIM + CONV_DIM:POOL_DIM + 2 * CONV_DIM]
    v_ref[...] = uv * jax.nn.sigmoid(ug)
    for b in range(1, 8):
        vsh_ref[b - 1] = v_ref[b:rows - 8 + b, :]
    acc = jnp.zeros((ts, CONV_DIM), F32) + cb_ref[...]
    for k in range(CONV_WIDTH):
        a, b = divmod(HALO - (CONV_WIDTH - 1) + k, 8)
        tap = v_ref[8 * a:8 * a + ts, :] if b == 0 else vsh_ref[b - 1, 8 * a:8 * a + ts, :]
        acc = acc + tap * cw_ref[k:k + 1, :]
    mu = jnp.mean(acc, axis=-1, keepdims=True)
    d = acc - mu
    var = jnp.mean(d * d, axis=-1, keepdims=True)
    yn = d * lax.rsqrt(var + EPS) * lg_ref[...] + lb_ref[...]
    o_ref[0, :, POOL_DIM:POOL_DIM + CONV_DIM] = (yn * jax.nn.sigmoid(yn)).astype(BF16)


def _pool_conv(upc, pool_w_bd, pool_scale, conv_w, conv_b, ln_g, ln_b):
    B, S, W = upc.shape
    ts = TOK_TILE
    r = ts // HALO
    vec = lambda n: pl.BlockSpec((1, n), lambda b, i: (0, 0))
    return pl.pallas_call(
        _poolconv_kernel,
        out_shape=jax.ShapeDtypeStruct((B, S, POOL_DIM + CONV_DIM), BF16),
        grid=(B, S // ts),
        in_specs=[pl.BlockSpec((1, ts, W), lambda b, i: (b, i, 0)),
                  pl.BlockSpec((1, HALO, W), lambda b, i: (b, jnp.maximum(i * r - 1, 0), 0)),
                  pl.BlockSpec((POOL_DIM, POOL_DIM), lambda b, i: (0, 0)),
                  vec(POOL_DIM),
                  pl.BlockSpec((CONV_WIDTH + 1, CONV_DIM), lambda b, i: (0, 0)),
                  vec(CONV_DIM), vec(CONV_DIM), vec(CONV_DIM)],
        out_specs=pl.BlockSpec((1, ts, POOL_DIM + CONV_DIM), lambda b, i: (b, i, 0)),
        scratch_shapes=[pltpu.VMEM((HALO + ts, W), F32), pltpu.VMEM((HALO + ts, CONV_DIM), F32),
                        pltpu.VMEM((HALO + ts, POOL_DIM), F32), pltpu.VMEM((HALO + ts, POOL_DIM), F32),
                        pltpu.VMEM((7, HALO + ts - 8, CONV_DIM), F32)],
        compiler_params=_cparams(("arbitrary", "arbitrary")),
        name="pool_conv",
    )(upc, upc, pool_w_bd, pool_scale.reshape(1, -1), conv_w, conv_b.reshape(1, -1),
      ln_g.reshape(1, -1), ln_b.reshape(1, -1))


N_STREAM = 2 * KV_GROUPS
CHUNK_W = CMP_STRIDE * 2 * KV_GROUPS * HEAD_DIM


def _cmp_weights(w1_k, w1_v):
    half = CMP_STRIDE * HEAD_DIM
    cols = []
    for s in range(N_STREAM):
        w1 = w1_k if s < KV_GROUPS else w1_v
        for part in range(2):
            blk = w1[part * half:(part + 1) * half].reshape(CMP_STRIDE, 1, HEAD_DIM, CMP_HIDDEN)
            z = jnp.zeros((CMP_STRIDE, N_STREAM, HEAD_DIM, CMP_HIDDEN), w1.dtype)
            z = lax.dynamic_update_slice(z, blk, (0, s, 0, 0))
            cols.append(z.reshape(CHUNK_W, CMP_HIDDEN))
    return jnp.concatenate(cols, axis=1).astype(BF16)


def _gelu_tanh(x):
    return 0.5 * x * (1.0 + jnp.tanh(math.sqrt(2.0 / math.pi) * (x + 0.044715 * (x * x * x))))


def _compress_kernel(c_ref, w_ref, pek_ref, pev_ref, w1k_ref, w1v_ref, w2k_ref, w2vt_ref, kc_ref, vct_ref):
    r = jnp.dot(c_ref[0], w_ref[...], preferred_element_type=F32)
    ncp = r.shape[0]
    pe_k = jnp.dot(pek_ref[...], w1k_ref[...], preferred_element_type=F32)[0:1]
    pe_v = jnp.dot(pev_ref[...], w1v_ref[...], preferred_element_type=F32)[0:1]
    for s in range(N_STREAM):
        a = r[:, s * 256:s * 256 + CMP_HIDDEN]
        b = r[:, s * 256 + CMP_HIDDEN:(s + 1) * 256]
        hid = a + pltpu.roll(b, ncp - 1, 0) + (pe_k if s < KV_GROUPS else pe_v)
        act = _gelu_tanh(hid).astype(BF16)
        if s < KV_GROUPS:
            kc_ref[0, s] = jnp.dot(act, w2k_ref[...], preferred_element_type=F32).astype(BF16)
        else:
            vt = lax.dot_general(w2vt_ref[...], act, (((1,), (1,)), ((), ())), preferred_element_type=F32)
            ones_row = (lax.broadcasted_iota(I32, vt.shape, 0) == HEAD_DIM).astype(F32)
            vct_ref[0, s - KV_GROUPS] = (vt + ones_row).astype(BF16)


def _compress(kvc, wcmp, pe_k, pe_v, w1_k, w1_v, w2_k, w2_v):
    B, S, _ = kvc.shape
    ncp = S // CMP_STRIDE
    chunks = kvc.reshape(B, ncp, CHUNK_W)
    pe8 = lambda pe: jnp.broadcast_to(pe.reshape(1, -1), (8, CMP_LEN * HEAD_DIM)).astype(BF16)
    w2k = jnp.pad(w2_k, ((0, 0), (0, LANES - HEAD_DIM))).astype(BF16)
    w2vt = jnp.pad(w2_v.T, ((0, LANES - HEAD_DIM), (0, 0))).astype(BF16)
    full = lambda a: pl.BlockSpec(a.shape, lambda b: (0,) * a.ndim)
    args = (wcmp, pe8(pe_k), pe8(pe_v), w1_k.astype(BF16), w1_v.astype(BF16), w2k, w2vt)
    return pl.pallas_call(
        _compress_kernel,
        out_shape=[jax.ShapeDtypeStruct((B, KV_GROUPS, ncp, LANES), BF16),
                   jax.ShapeDtypeStruct((B, KV_GROUPS, LANES, ncp), BF16)],
        grid=(B,),
        in_specs=[pl.BlockSpec((1, ncp, CHUNK_W), lambda b: (b, 0, 0))] + [full(a) for a in args],
        out_specs=[pl.BlockSpec((1, KV_GROUPS, ncp, LANES), lambda b: (b, 0, 0, 0)),
                   pl.BlockSpec((1, KV_GROUPS, LANES, ncp), lambda b: (b, 0, 0, 0))],
        compiler_params=_cparams(("arbitrary",)),
        name="compress",
    )(chunks, *args)


def _bias_tables(rel_bias, S):
    nq = S // Q_TILE
    ncp = S // CMP_STRIDE
    rb = rel_bias.reshape(N_BUCKETS, KV_GROUPS, HPG).transpose(1, 2, 0) * LOG2E
    far = rb[:, :, N_BUCKETS - 1]
    far_hi = far.astype(BF16)
    far_lo = (far - far_hi.astype(F32)).astype(BF16)
    far_sum = far_hi.astype(F32) + far_lo.astype(F32)
    i = np.arange(Q_TILE)[None, :]

    def table(d, valid, sub):
        onehot = jax.nn.one_hot(_t5_bucket(jnp.asarray(d, I32)), N_BUCKETS, dtype=F32)
        t = jnp.einsum('rib,ghb->grhi', onehot, rb, precision=lax.Precision.HIGHEST)
        t = jnp.where(jnp.asarray(valid)[None, :, None, :], t - sub[:, None, :, None], NEG_INF)
        return t.reshape(KV_GROUPS, d.shape[0], ROWS)

    d = i - np.arange(NEAR_KEYS)[:, None] + (NEAR_KEYS - Q_TILE)
    t_near = table(d, d >= 0, far_sum)
    d = i - np.arange(WIN_KEYS)[:, None] + (WIN_KEYS - Q_TILE)
    t_win = table(d, (d >= 0) & (d < WINDOW), jnp.zeros_like(far_sum))
    c0 = (Q_TILE // CMP_STRIDE) * (nq - 1)
    d = i - CMP_STRIDE * (np.arange(c0 + ncp)[:, None] - c0) - (CMP_LEN - 1)
    t_cmp = table(d, d >= 0, jnp.zeros_like(far_sum))
    rows = jnp.zeros((KV_GROUPS, HEAD_DIM, HPG, Q_TILE), F32)
    rows = rows.at[:, 0].set(1.0)
    rows = rows.at[:, 1].set(jnp.broadcast_to(far_hi.astype(F32)[:, :, None], (KV_GROUPS, HPG, Q_TILE)))
    rows = rows.at[:, 2].set(jnp.broadcast_to(far_lo.astype(F32)[:, :, None], (KV_GROUPS, HPG, Q_TILE)))
    return t_near, t_win, t_cmp, rows.reshape(KV_GROUPS, HEAD_DIM, ROWS).astype(BF16)


def _overlap_t(S):
    ncp = S // CMP_STRIDE
    n = np.arange(ncp)[None, :]
    jb = np.arange(LANES)[:, None]
    end = n * CMP_STRIDE + CMP_LEN - 1
    start = n * CMP_STRIDE
    ov = (end >= jb * SEL_BLOCK) & (start < (jb + 1) * SEL_BLOCK) & (n < ncp - 1)
    return jnp.asarray(ov.astype(np.float32), BF16)


def _key_tiles(ref, k0, n):
    t0 = k0 // LANES
    return jnp.concatenate([ref[0, 0, t0 + u] for u in range(n)], axis=1)


def _attn_kernel(qt_ref, qc_ref, kc_ref, vct_ref, ks_ref, vs_ref, kw_ref, vw_ref, gt_ref, tn_ref, tw_ref, tct_ref,
                 ovt_ref, o_ref, qa_ref, acc_ref, *, c0):
    qi = pl.program_id(2)
    qt = qt_ref[0, 0, 0]
    ncp = kc_ref.shape[2]

    k1 = Q_TILE * qi + (KEY_PAD + Q_TILE)

    qa_ref[0:HEAD_DIM, :] = qt
    qa_ref[HEAD_DIM:LANES, :] = qc_ref[0]

    r0 = pl.multiple_of(c0 - (Q_TILE // CMP_STRIDE) * qi, 8)
    st = (jnp.dot(kc_ref[0, 0][:, 0:HEAD_DIM], qt, preferred_element_type=F32)
          + tct_ref[0, pl.ds(r0, ncp), :])
    w0 = pl.multiple_of(k1 - WIN_KEYS, LANES)
    sw = jnp.dot(kw_ref[0, pl.ds(w0, WIN_KEYS), :], qa_ref[0:LANES, :], preferred_element_type=F32) + tw_ref[0]

    mc = jnp.maximum(jnp.max(st, axis=0, keepdims=True), -1e20)
    pc = jnp.exp2(st - mc).astype(BF16)
    acc_c = jnp.dot(vct_ref[0, 0], pc, preferred_element_type=F32)
    inv_c = 1.0 / jnp.maximum(acc_c[HEAD_DIM:HEAD_DIM + 1], 1e-30)
    o_c = acc_c[0:HEAD_DIM] * inv_c

    imp = None
    for h in range(HPG):
        cols = slice(h * Q_TILE, (h + 1) * Q_TILE)
        part = jnp.dot(ovt_ref[...], pc[:, cols], preferred_element_type=F32) * inv_c[:, cols]
        imp = part if imp is None else imp + part

    def flash_step(carry, s, vt):
        m, acc = carry
        m_new = jnp.maximum(m, jnp.max(s, axis=0, keepdims=True))
        alpha = jnp.exp2(m - m_new)
        p = jnp.exp2(s - m_new)
        acc = acc * alpha + jnp.dot(vt, p.astype(BF16), preferred_element_type=F32)
        return m_new, acc

    def normalised(acc):
        return acc[0:HEAD_DIM] * (1.0 / acc[HEAD_DIM:HEAD_DIM + 1])

    p = jnp.exp2(sw - jnp.max(sw, axis=0, keepdims=True)).astype(BF16)
    o_w = normalised(jnp.dot(_key_tiles(vw_ref, w0, WIN_KEYS // LANES), p, preferred_element_type=F32))

    jb = lax.broadcasted_iota(I32, (LANES, Q_TILE), 0)
    ii = lax.broadcasted_iota(I32, (LANES, Q_TILE), 1)
    cur = (Q_TILE // SEL_BLOCK) * qi + (ii >= SEL_BLOCK).astype(I32)
    forced = (jb == 0) | (jb == cur) | (jb == cur - 1)
    score = jnp.where(jb <= cur, jnp.where(forced, -jnp.inf, imp), NEG_INF)
    for _ in range(SEL_TOPN - N_FORCED):
        mx = jnp.max(score, axis=0, keepdims=True)
        first = jnp.min(jnp.where(score == mx, jb, LANES), axis=0, keepdims=True)
        score = jnp.where(jb == first, -jnp.inf, score)
    mbt = jnp.where(score == -jnp.inf, 0.0, MASK_BIAS).astype(BF16)
    for h in range(HPG):
        qa_ref[LANES:2 * LANES, h * Q_TILE:(h + 1) * Q_TILE] = mbt

    n0 = pl.multiple_of(k1 - NEAR_KEYS, LANES)
    half = NEAR_KEYS // 2
    sa = (jnp.dot(ks_ref[0, pl.ds(n0, half), :], qa_ref[...], preferred_element_type=F32)
          + tn_ref[0, 0:half, :])
    sb = (jnp.dot(ks_ref[0, pl.ds(n0 + half, half), :], qa_ref[...], preferred_element_type=F32)
          + tn_ref[0, half:NEAR_KEYS, :])
    m_near = jnp.maximum(jnp.max(sa, axis=0, keepdims=True), jnp.max(sb, axis=0, keepdims=True))
    pa = jnp.exp2(sa - m_near).astype(BF16)
    da = jnp.dot(_key_tiles(vs_ref, n0, half // LANES), pa, preferred_element_type=F32)
    pb = jnp.exp2(sb - m_near).astype(BF16)
    acc_near = da + jnp.dot(_key_tiles(vs_ref, n0 + half, half // LANES), pb, preferred_element_type=F32)
    n_pairs = (jnp.maximum(n0 - KEY_PAD, 0) + KEY_PAD - 1) // KEY_PAD

    def far_scores(j):
        k0 = pl.multiple_of(n0 - FAR_TILE * (j + 1), LANES)
        return jnp.dot(ks_ref[0, pl.ds(k0, FAR_TILE), :], qa_ref[...], preferred_element_type=F32)

    def far_values(j):
        return _key_tiles(vs_ref, pl.multiple_of(n0 - FAR_TILE * (j + 1), LANES), FAR_TILE // LANES)

    def fast_body(jj, top):
        tiles = [FAR_GROUP * jj + u for u in range(FAR_GROUP)]
        scores = [far_scores(tiles[0]), far_scores(tiles[1])]
        acc = None
        for u, j in enumerate(tiles):
            if u + 2 < FAR_GROUP:
                scores.append(far_scores(tiles[u + 2]))
            p = jnp.exp2(scores[u] - m_near).astype(BF16)
            d = jnp.dot(far_values(j), p, preferred_element_type=F32)
            acc = d if acc is None else acc + d
            top = jnp.maximum(top, jnp.max(scores[u], axis=0, keepdims=True))
        acc_ref[...] += acc
        return top

    acc_ref[...] = acc_near
    top = lax.fori_loop(0, n_pairs, fast_body, m_near)

    @pl.when(jnp.max(top - m_near) > FAR_HEADROOM)
    def _():
        def safe_body(j, carry):
            return flash_step(carry, far_scores(j), far_values(j))

        acc_ref[...] = lax.fori_loop(0, FAR_GROUP * n_pairs, safe_body, (m_near, acc_near))[1]

    o_s = normalised(acc_ref[...])

    gtt = gt_ref[0].T
    gate = lambda br: jnp.concatenate([gtt[br * HPG + h:br * HPG + h + 1, :] for h in range(HPG)], axis=1)
    o_ref[0, 0, 0] = (gate(0) * o_c + gate(1) * o_s + gate(2) * o_w).astype(BF16)


def _attention(qt, qc, kc, vct, ks, vs, kw, vw, gt, tn, tw, tct, ovt):
    B, G, nq, _, _ = qt.shape
    S = nq * Q_TILE
    sp = S + KEY_PAD
    ncp = S // CMP_STRIDE
    c0 = (Q_TILE // CMP_STRIDE) * (nq - 1)
    per_g = lambda a: pl.BlockSpec((1,) + a.shape[1:], lambda b, g, i: (g,) + (0,) * (a.ndim - 1))
    val = pl.BlockSpec((1, 1, sp // LANES, LANES, LANES), lambda b, g, i: (b, g, 0, 0, 0))
    return pl.pallas_call(
        functools.partial(_attn_kernel, c0=c0),
        out_shape=jax.ShapeDtypeStruct((B, G, nq, HEAD_DIM, ROWS), BF16),
        grid=(B, G, nq),
        in_specs=[pl.BlockSpec((1, 1, 1, HEAD_DIM, ROWS), lambda b, g, i: (b, g, i, 0, 0)),
                  per_g(qc),
                  pl.BlockSpec((1, 1, ncp, LANES), lambda b, g, i: (b, g, 0, 0)),
                  pl.BlockSpec((1, 1, LANES, ncp), lambda b, g, i: (b, g, 0, 0)),
                  pl.BlockSpec((1, sp, 2 * LANES), lambda b, g, i: (b, 0, g)),
                  val,
                  pl.BlockSpec((1, sp, LANES), lambda b, g, i: (b, 0, g)),
                  val,
                  pl.BlockSpec((1, Q_TILE, LANES), lambda b, g, i: (b, i, g)),
                  per_g(tn), per_g(tw), per_g(tct),
                  pl.BlockSpec((LANES, ncp), lambda b, g, i: (0, 0))],
        out_specs=pl.BlockSpec((1, 1, 1, HEAD_DIM, ROWS), lambda b, g, i: (b, g, i, 0, 0)),
        scratch_shapes=[pltpu.VMEM((2 * LANES, ROWS), BF16), pltpu.VMEM((LANES, ROWS), F32)],
        compiler_params=_cparams(("arbitrary", "arbitrary", "arbitrary")),
        name="nsa_attention",
    )(qt, qc, kc, vct, ks, vs, kw, vw, gt, tn, tw, tct, ovt)


def _post_attn_kernel(x_ref, ypc_ref, yn_ref, wo_ref, g1_ref, n2_ref, sc_ref, sh_ref, rwh_ref, rwl_ref, rb_ref,
                      xo_ref, h2_ref, ei_ref, tw_ref, cnt_ref, run_ref, hd_ref):
    first = (pl.program_id(0) == 0) & (pl.program_id(1) == 0)

    @pl.when(first)
    def _():
        run_ref[...] = jnp.zeros_like(run_ref)

    tm = x_ref.shape[1]
    half = wo_ref.shape[0] // 2
    tiles = []
    for t in range(TILES):
        groups = []
        for g in range(KV_GROUPS):
            blk = yn_ref[0, g, t].astype(F32)
            for hh in range(HPG):
                hd_ref[hh * HEAD_DIM:(hh + 1) * HEAD_DIM, :] = blk[:, hh * Q_TILE:(hh + 1) * Q_TILE]
            groups.append(hd_ref[...].T)
        tiles.append(jnp.concatenate(groups, axis=1))
    ynsa = jnp.concatenate(tiles, axis=0).astype(BF16)
    mixed = (jnp.dot(ypc_ref[0], wo_ref[0:half, :], preferred_element_type=F32)
             + jnp.dot(ynsa, wo_ref[half:, :], preferred_element_type=F32))
    x = x_ref[0] + g1_ref[0] * mixed
    xo_ref[0] = x
    y = x * lax.rsqrt(jnp.mean(x * x, axis=-1, keepdims=True) + EPS) * n2_ref[...]
    h2 = y * (1.0 + sc_ref[0]) + sh_ref[0]
    _rows_to_tiles(h2_ref, h2)
    hh = h2.astype(BF16)
    hl = (h2 - hh.astype(F32)).astype(BF16)
    logit = (jnp.dot(hh, rwh_ref[...], preferred_element_type=F32)
             + jnp.dot(hl, rwh_ref[...], preferred_element_type=F32)
             + jnp.dot(hh, rwl_ref[...], preferred_element_type=F32)) + rb_ref[...]
    lane = lax.broadcasted_iota(I32, (tm, LANES), 1)
    vals, hots, idxs = [], [], []
    for _ in range(TOP_K):
        mx = jnp.max(logit, axis=1, keepdims=True)
        idx = jnp.min(jnp.where(logit == mx, lane, LANES), axis=1, keepdims=True)
        hot = lane == idx
        vals.append(mx)
        hots.append(hot)
        idxs.append(idx)
        logit = jnp.where(hot, -jnp.inf, logit)
    ex = [jnp.exp(v - vals[0]) for v in vals]
    inv = 1.0 / (ex[0] + ex[1] + ex[2] + ex[3])
    assign = (hots[0] | hots[1] | hots[2] | hots[3]).astype(BF16)
    r = lax.broadcasted_iota(I32, (tm, tm), 0)
    c = lax.broadcasted_iota(I32, (tm, tm), 1)
    before = jnp.dot((c < r).astype(BF16), assign, preferred_element_type=F32) + run_ref[...]
    ei = jnp.zeros((tm, LANES), I32)
    tw = jnp.zeros((tm, LANES), F32)
    for k in range(TOP_K):
        e_k = idxs[k]
        r_k = jnp.sum(jnp.where(hots[k], before, 0.0), axis=1, keepdims=True).astype(I32)
        ei = jnp.where(lane == k, e_k, jnp.where(lane == TOP_K + k, r_k, ei))
        tw = jnp.where(lane == k, ex[k] * inv, tw)
    ei_ref[0] = ei
    tw_ref[0] = tw
    run_ref[...] = run_ref[...] + jnp.sum(assign.astype(F32), axis=0, keepdims=True)
    cnt_ref[...] = run_ref[...]


def _post_attn(x, ypc, ynsa, w_out, g1, n2g, sc2, sh2, rw_hi, rw_lo, rb):
    B, S, D = x.shape
    tm = TOK_TILE
    tok = lambda w: pl.BlockSpec((1, tm, w), lambda b, i: (b, i, 0))
    per_b = pl.BlockSpec((1, 1, D), lambda b, i: (b, 0, 0))
    full = lambda a: pl.BlockSpec(a.shape, lambda b, i: (0,) * a.ndim)
    return pl.pallas_call(
        _post_attn_kernel,
        out_shape=[jax.ShapeDtypeStruct((B, S, D), F32), jax.ShapeDtypeStruct((B * S * ROW_TILES, LANES), F32),
                   jax.ShapeDtypeStruct((B, S, LANES), I32), jax.ShapeDtypeStruct((B, S, LANES), F32),
                   jax.ShapeDtypeStruct((1, LANES), F32)],
        grid=(B, S // tm),
        in_specs=[tok(D), tok(ypc.shape[-1]),
                  pl.BlockSpec((1, KV_GROUPS, TILES, HEAD_DIM, ROWS), lambda b, i: (b, 0, i, 0, 0)),
                  full(w_out), per_b,
                  pl.BlockSpec((1, D), lambda b, i: (0, 0)), per_b, per_b,
                  full(rw_hi), full(rw_lo), full(rb)],
        out_specs=[tok(D), pl.BlockSpec((tm * ROW_TILES, LANES), lambda b, i: (b * (S // tm) + i, 0)),
                   tok(LANES), tok(LANES), pl.BlockSpec((1, LANES), lambda b, i: (0, 0))],
        scratch_shapes=[pltpu.VMEM((1, LANES), F32), pltpu.VMEM((HPG * HEAD_DIM, Q_TILE), F32)],
        compiler_params=_cparams(("arbitrary", "arbitrary")),
        name="post_attn_router",
    )(x, ypc, ynsa, w_out, g1, n2g.reshape(1, D), sc2, sh2, rw_hi, rw_lo, rb)


ROW_TILES = 8
DMA_UNROLL = 4
ZERO_BITS = tuple(1 << b for b in reversed(range((EXPERT_BLOCK - 1).bit_length())))


def _rows_to_tiles(ref, val):
    n = val.shape[0]
    for s in range(ROW_TILES):
        ref[pl.ds(s, n, stride=ROW_TILES), :] = val[:, s * LANES:(s + 1) * LANES]


def _tiles_to_rows(ref, n):
    return jnp.concatenate([ref[pl.ds(s, n, stride=ROW_TILES), :] for s in range(ROW_TILES)], axis=1)


def _tile_at(ref, i):
    return ref.at[pl.ds(pl.multiple_of(i * ROW_TILES, ROW_TILES), ROW_TILES), :]


def _dispatch_kernel(fill0_ref, filln_ref, dest_ref, h_ref, xs_ref, zero_ref, sem, zsem):
    tm = h_ref.shape[0] // ROW_TILES

    @pl.when(pl.program_id(0) == 0)
    def _():
        zero_ref[...] = jnp.zeros_like(zero_ref)

        def expert(e, c):
            n = filln_ref[e]
            for wait in (False, True):
                for bit in ZERO_BITS:
                    @pl.when((n & bit) != 0)
                    def _():
                        first = fill0_ref[e] + (n & ~(2 * bit - 1))
                        dst = xs_ref.at[pl.ds(pl.multiple_of(first * ROW_TILES, ROW_TILES), bit * ROW_TILES), :]
                        cp = pltpu.make_async_copy(zero_ref.at[pl.ds(0, bit * ROW_TILES), :], dst, zsem)
                        cp.wait() if wait else cp.start()
            return c

        lax.fori_loop(0, N_EXPERTS, expert, 0)

    def body(i, c):
        for u in range(DMA_UNROLL):
            t = i * DMA_UNROLL + u
            src = _tile_at(h_ref, t)
            for k in range(TOP_K):
                pltpu.make_async_copy(src, _tile_at(xs_ref, dest_ref[0, 0, t * TOP_K + k]), sem).start(
                    priority=k % 2)
        return c

    lax.fori_loop(0, tm // DMA_UNROLL, body, 0)
    for _ in range(TOP_K):
        pltpu.make_async_copy(h_ref, xs_ref.at[pl.ds(0, tm * ROW_TILES), :], sem).wait()


def _dispatch(dest, h2t, fill0, filln, n_slots):
    tm = TOK_TILE
    nt = h2t.shape[0] // (tm * ROW_TILES)
    return pl.pallas_call(
        _dispatch_kernel,
        out_shape=jax.ShapeDtypeStruct((n_slots * ROW_TILES, LANES), F32),
        grid_spec=pltpu.PrefetchScalarGridSpec(
            num_scalar_prefetch=2,
            grid=(nt,),
            in_specs=[pl.BlockSpec((1, 1, tm * TOP_K), lambda i, f0, fn: (i, 0, 0), memory_space=pltpu.SMEM),
                      pl.BlockSpec((tm * ROW_TILES, LANES), lambda i, f0, fn: (i, 0))],
            out_specs=pl.BlockSpec(memory_space=pl.ANY),
            scratch_shapes=[pltpu.VMEM((ZERO_BITS[0] * ROW_TILES, LANES), F32), pltpu.SemaphoreType.DMA(()),
                            pltpu.SemaphoreType.DMA(())]),
        compiler_params=_cparams(("arbitrary",)),
        name="moe_dispatch",
    )(fill0, filln, dest.reshape(nt, 1, tm * TOP_K), h2t)


W_CHUNK = 512


def _expert_kernel(be_ref, nu_ref, x_ref, wgu_ref, bgu_ref, wd_ref, bd_ref, y_ref, wgu_s, wd_s):
    i = pl.program_id(0)

    @pl.when(i < nu_ref[0])
    def _():
        @pl.when((i == 0) | (be_ref[i] != be_ref[jnp.maximum(i - 1, 0)]))
        def _():
            for c in range(0, wgu_s.shape[1], W_CHUNK):
                wgu_s[:, c:c + W_CHUNK] = wgu_ref[0, :, c:c + W_CHUNK].astype(BF16)
            for c in range(0, wd_s.shape[1], W_CHUNK):
                wd_s[:, c:c + W_CHUNK] = wd_ref[0, :, c:c + W_CHUNK].astype(BF16)

        F = wd_s.shape[0]
        x = _tiles_to_rows(x_ref, EXPERT_BLOCK).astype(BF16)
        gu = jnp.dot(x, wgu_s[...], preferred_element_type=F32) + bgu_ref[0]
        gate = jnp.minimum(gu[:, :F], SWIGLU_LIMIT)
        up = jnp.clip(gu[:, F:], -SWIGLU_LIMIT, SWIGLU_LIMIT)
        act = (up + 1.0) * gate * jax.nn.sigmoid(SWIGLU_ALPHA * gate)
        _rows_to_tiles(y_ref, jnp.dot(act.astype(BF16), wd_s[...], preferred_element_type=F32) + bd_ref[0])


def _experts(blk_e, n_used, xs, layer, w_gu, b_gu, w_down, b_down):
    L, E, D, F2 = w_gu.shape
    F = F2 // 2
    rows = EXPERT_BLOCK * ROW_TILES
    nb = xs.shape[0] // rows
    blk = lambda i, be, nu: (jnp.minimum(i, nu[0] - 1), 0)
    per_e = lambda i, be, nu: (layer, be[i], 0, 0)
    return pl.pallas_call(
        _expert_kernel,
        out_shape=jax.ShapeDtypeStruct(xs.shape, F32),
        grid_spec=pltpu.PrefetchScalarGridSpec(
            num_scalar_prefetch=2,
            grid=(nb,),
            in_specs=[pl.BlockSpec((rows, LANES), blk),
                      pl.BlockSpec((None, 1, D, F2), per_e),
                      pl.BlockSpec((None, 1, 1, F2), per_e),
                      pl.BlockSpec((None, 1, F, D), per_e),
                      pl.BlockSpec((None, 1, 1, D), per_e)],
            out_specs=pl.BlockSpec((rows, LANES), blk),
            scratch_shapes=[pltpu.VMEM((D, F2), BF16), pltpu.VMEM((F, D), BF16)]),
        compiler_params=_cparams(("arbitrary",)),
        name="moe_experts",
    )(blk_e, n_used, xs, w_gu, b_gu.reshape(L, E, 1, F2), w_down, b_down.reshape(L, E, 1, D))


def _combine_kernel(dest_ref, y_ref, x_ref, tw_ref, g2_ref, fg_ref, o_ref, rows_ref, sem, *, final, n_tiles):
    s = pl.program_id(0)
    tm = x_ref.shape[1]

    @pl.when(s < n_tiles)
    def _():
        slot = s % 2

        def body(i, c):
            for u in range(DMA_UNROLL):
                t = i * DMA_UNROLL + u
                for k in range(TOP_K):
                    pltpu.make_async_copy(_tile_at(y_ref, dest_ref[0, 0, t * TOP_K + k]),
                                          _tile_at(rows_ref.at[slot, k], t), sem.at[slot]).start(priority=k % 2)
            return c

        lax.fori_loop(0, tm // DMA_UNROLL, body, 0)

    @pl.when(s > 0)
    def _():
        slot = (s - 1) % 2
        for k in range(TOP_K):
            pltpu.make_async_copy(y_ref.at[pl.ds(0, tm * ROW_TILES), :], rows_ref.at[slot, k], sem.at[slot]).wait()
        tw = tw_ref[0]
        moe = tw[:, 0:1] * _tiles_to_rows(rows_ref.at[slot, 0], tm)
        for k in range(1, TOP_K):
            moe = moe + tw[:, k:k + 1] * _tiles_to_rows(rows_ref.at[slot, k], tm)
        x = x_ref[0] + g2_ref[0] * moe
        if final:
            x = x * lax.rsqrt(jnp.mean(x * x, axis=-1, keepdims=True) + EPS) * fg_ref[...]
        o_ref[0] = x


def _combine(dest, y, x, tw, g2, final_g, final):
    B, S, D = x.shape
    tm = TOK_TILE
    nt = S // tm
    n_tiles = B * nt
    done = lambda s: jnp.maximum(s - 1, 0)
    tok = lambda w: pl.BlockSpec((1, tm, w), lambda s: (done(s) // nt, done(s) % nt, 0))
    return pl.pallas_call(
        functools.partial(_combine_kernel, final=final, n_tiles=n_tiles),
        out_shape=jax.ShapeDtypeStruct((B, S, D), F32),
        grid=(n_tiles + 1,),
        in_specs=[pl.BlockSpec((1, 1, tm * TOP_K), lambda s: (jnp.minimum(s, n_tiles - 1), 0, 0),
                               memory_space=pltpu.SMEM),
                  pl.BlockSpec(memory_space=pl.ANY),
                  tok(D), tok(LANES),
                  pl.BlockSpec((1, 1, D), lambda s: (done(s) // nt, 0, 0)),
                  pl.BlockSpec((1, D), lambda s: (0, 0))],
        out_specs=tok(D),
        scratch_shapes=[pltpu.VMEM((2, TOP_K, tm * ROW_TILES, LANES), F32), pltpu.SemaphoreType.DMA((2,))],
        compiler_params=_cparams(("arbitrary",)),
        name="moe_combine",
    )(dest.reshape(n_tiles, 1, tm * TOP_K), y, x, tw, g2, final_g.reshape(1, D))


def _moe(x, h2, ei, tw, counts, g2, layer, w_gu, b_gu, w_down, b_down, final_g, final):
    B, S, D = x.shape
    N = B * S
    n_slots = -(-(N * TOP_K + N_EXPERTS * EXPERT_BLOCK) // EXPERT_BLOCK) * EXPERT_BLOCK
    nb = n_slots // EXPERT_BLOCK
    cnt = counts[0, :N_EXPERTS].astype(I32)
    padded = (cnt + EXPERT_BLOCK - 1) // EXPERT_BLOCK * EXPERT_BLOCK
    pend = jnp.cumsum(padded)
    pstart = pend - padded
    ei2 = ei.reshape(N, LANES)
    dest = (pstart[ei2[:, 0:TOP_K]] + ei2[:, TOP_K:2 * TOP_K]).reshape(N * TOP_K)
    blk_start = jnp.arange(nb, dtype=I32) * EXPERT_BLOCK
    blk_e = jnp.minimum(jnp.sum((pend[None, :] <= blk_start[:, None]).astype(I32), axis=1), N_EXPERTS - 1)
    n_used = (pend[-1:] // EXPERT_BLOCK).astype(I32)
    xs = _dispatch(dest, h2, pstart + cnt, padded - cnt, n_slots)
    y = _experts(blk_e, n_used, xs, layer, w_gu, b_gu, w_down, b_down)
    return _combine(dest, y, x, tw, g2, final_g, final)


def kernel(x, c, w_mod, b_mod, norm1_g, norm2_g, w_in, w_out, pool_w, pool_scale, conv_w, conv_b, conv_ln_g,
           conv_ln_b, cmp_pe_k, cmp_pe_v, cmp_w1_k, cmp_w2_k, cmp_w1_v, cmp_w2_v, rel_bias, router_w, router_b,
           expert_w_gu, expert_b_gu, expert_w_down, expert_b_down, final_g):
    B, S, D = x.shape
    L = w_mod.shape[0]
    assert S % TOK_TILE == 0 and S // SEL_BLOCK <= LANES and D == ROW_TILES * LANES
    mod = _modulation(c, w_mod, b_mod)
    t_near, t_win, t_cmp, qc = _bias_tables(rel_bias, S)
    ovt = _overlap_t(S)
    w_big = _in_weight(w_in)
    cg = POOL_DIM // POOL_GROUPS
    for l in range(L):
        m6 = mod[l].reshape(B, 6, 1, D)
        sh1, sc1, g1, sh2, sc2, g2 = (m6[:, k] for k in range(6))
        upc, qt, kvc, ks, vs, kw, vw, gt = _in_proj(x, norm1_g[l], sc1, sh1, w_big[l])
        pw_bd = jnp.zeros((POOL_DIM, POOL_DIM), F32)
        for g in range(POOL_GROUPS):
            pw_bd = lax.dynamic_update_slice(pw_bd, pool_w[l, g], (g * cg, g * cg))
        cw = jnp.pad(conv_w[l], ((0, 1), (0, 0)))
        ypc = _pool_conv(upc, pw_bd.astype(BF16), pool_scale[l], cw, conv_b[l], conv_ln_g[l], conv_ln_b[l])
        kc, vct = _compress(kvc, _cmp_weights(cmp_w1_k[l], cmp_w1_v[l]), cmp_pe_k[l], cmp_pe_v[l],
                            cmp_w1_k[l], cmp_w1_v[l], cmp_w2_k[l], cmp_w2_v[l])
        ynsa = _attention(qt, qc, kc, vct, ks, vs, kw, vw, gt, t_near, t_win, t_cmp, ovt)
        rw = jnp.pad(router_w[l], ((0, 0), (0, LANES - N_EXPERTS)))
        rw_hi = rw.astype(BF16)
        rw_lo = (rw - rw_hi.astype(F32)).astype(BF16)
        rb = jnp.pad(router_b[l].reshape(1, -1), ((0, 0), (0, LANES - N_EXPERTS)), constant_values=NEG_INF)
        x, h2, ei, tw, counts = _post_attn(x, ypc, ynsa, w_out[l].astype(BF16), g1, norm2_g[l], sc2, sh2,
                                           rw_hi, rw_lo, rb)
        x = _moe(x, h2, ei, tw, counts, g2, l, expert_w_gu, expert_b_gu, expert_w_down, expert_b_down,
                 final_g, final=(l == L - 1))
    return x
```

```python
import functools
import math

import jax
import jax.numpy as jnp
import numpy as np
from jax import lax
from jax.experimental import pallas as pl
from jax.experimental.pallas import tpu as pltpu

F32 = jnp.float32
BF16 = jnp.bfloat16
I32 = jnp.int32

HEAD_DIM = 64
POOL_DIM = 256
POOL_GROUPS = 4
POOL_WINDOWS = (2, 4, 8, 16)
CONV_DIM = 256
CONV_WIDTH = 31
NSA_HEADS = 8
KV_GROUPS = 2
HPG = 4
CMP_LEN = 32
CMP_STRIDE = 16
CMP_HIDDEN = 128
SEL_BLOCK = 64
SEL_TOPN = 16
N_FORCED = 3
WINDOW = 512
Q_TILE = 128
N_BUCKETS = 32
MAX_DISTANCE = 1024
N_EXPERTS = 32
TOP_K = 4
SWIGLU_ALPHA = 1.702
SWIGLU_LIMIT = 7.0
EPS = 1e-5
NEG_INF = -1e30

LANES = 128
ROWS = HPG * Q_TILE
LOG2E = 1.4426950408889634
FAR_HEADROOM = 100.0
NEAR_KEYS = 1024
WIN_KEYS = WINDOW + Q_TILE
FAR_TILE = 512
FAR_GROUP = 3
KEY_PAD = FAR_GROUP * FAR_TILE
MASK_BIAS = -32768.0
TOK_TILE = 512
EXPERT_BLOCK = 512
VMEM_LIMIT = 56 * 1024 * 1024


def _cparams(sem, vmem=VMEM_LIMIT):
    return pltpu.CompilerParams(dimension_semantics=sem, vmem_limit_bytes=vmem)


def _t5_bucket(n):
    n = jnp.maximum(n, 0)
    max_exact = N_BUCKETS // 2
    nf = jnp.maximum(n, 1).astype(F32)
    large = max_exact + (jnp.log(nf / max_exact) / math.log(MAX_DISTANCE / max_exact)
                         * (N_BUCKETS - max_exact)).astype(I32)
    large = jnp.minimum(large, N_BUCKETS - 1)
    return jnp.where(n < max_exact, n, large)


def _mod_kernel(c_ref, w_ref, b_ref, o_ref):
    c = c_ref[...]
    cond = c * jax.nn.sigmoid(c)
    o_ref[0] = jnp.dot(cond.astype(BF16), w_ref[0].astype(BF16),
                       preferred_element_type=F32) + b_ref[0]


def _modulation(c, w_mod, b_mod):
    L, D, W = w_mod.shape
    B = c.shape[0]
    tn = 1536
    return pl.pallas_call(
        _mod_kernel,
        out_shape=jax.ShapeDtypeStruct((L, B, W), F32),
        grid=(L, W // tn),
        in_specs=[pl.BlockSpec((B, D), lambda l, j: (0, 0)),
                  pl.BlockSpec((1, D, tn), lambda l, j: (l, 0, j)),
                  pl.BlockSpec((1, 1, tn), lambda l, j: (l, 0, j))],
        out_specs=pl.BlockSpec((1, B, tn), lambda l, j: (l, 0, j)),
        compiler_params=_cparams(("arbitrary", "arbitrary")),
        name="modulation",
    )(c, w_mod, b_mod.reshape(L, 1, W))


C_UPC, C_Q, C_KVC, C_K, C_V, C_GT, C_END = 0, 768, 1280, 1536, 1792, 2048, 2304


def _in_weight(w_in):
    col = lambda a, n: w_in[:, :, a:a + n]
    zero = lambda n: jnp.zeros(w_in.shape[:2] + (n,), w_in.dtype)
    parts = [col(0, 768), col(768, 512) * (HEAD_DIM ** -0.5 * LOG2E), col(1280, 256)]
    for sel, win in ((1536, 1792), (1664, 1920)):
        for g in range(KV_GROUPS):
            parts += [col(sel + g * HEAD_DIM, HEAD_DIM), col(win + g * HEAD_DIM, HEAD_DIM)]
    for g in range(KV_GROUPS):
        parts += [col(2048 + br * NSA_HEADS + g * HPG, HPG) for br in range(3)] + [zero(LANES - 3 * HPG)]
    return jnp.concatenate(parts, axis=2).astype(BF16)


PAD_STEPS = KEY_PAD // TOK_TILE
TILES = TOK_TILE // Q_TILE


def _inproj_kernel(x_ref, g_ref, sc_ref, sh_ref, w_ref,
                   upc_ref, qt_ref, kvc_ref, ks_ref, vs_ref, kw_ref, vw_ref, gt_ref):
    step = pl.program_id(1)
    tm = x_ref.shape[1]

    @pl.when(step < PAD_STEPS)
    def _():
        lane = lax.broadcasted_iota(I32, (tm, 4 * LANES), 1)
        ks_ref[0] = jnp.where(lane % (2 * LANES) == HEAD_DIM, MASK_BIAS, 0.0).astype(BF16)
        lane = lax.broadcasted_iota(I32, (tm, 2 * LANES), 1)
        kw_ref[0] = jnp.where(lane % LANES == HEAD_DIM, MASK_BIAS, 0.0).astype(BF16)
        for g in range(KV_GROUPS):
            for t in range(TILES):
                vs_ref[0, g, t] = jnp.zeros((LANES, LANES), BF16)
                vw_ref[0, g, t] = jnp.zeros((LANES, LANES), BF16)

    @pl.when(step >= PAD_STEPS)
    def _():
        ti = step - PAD_STEPS
        x = x_ref[0]
        y = x * lax.rsqrt(jnp.mean(x * x, axis=-1, keepdims=True) + EPS) * g_ref[...]
        h = y * (1.0 + sc_ref[0]) + sh_ref[0]
        z = jnp.dot(h.astype(BF16), w_ref[...], preferred_element_type=F32)
        upc_ref[0] = z[:, C_UPC:C_Q]
        kvc_ref[0] = z[:, C_KVC:C_K].astype(BF16)
        gt_ref[0] = jax.nn.sigmoid(z[:, C_GT:C_END])
        zq = z[:, C_Q:C_KVC].T
        for g in range(KV_GROUPS):
            for t in range(TILES):
                qt_ref[0, g, t] = jnp.concatenate(
                    [zq[(g * HPG + hh) * HEAD_DIM:(g * HPG + hh + 1) * HEAD_DIM, t * Q_TILE:(t + 1) * Q_TILE]
                     for hh in range(HPG)], axis=1).astype(BF16)
        lane = lax.broadcasted_iota(I32, (tm, LANES), 1)
        row = lax.broadcasted_iota(I32, (tm, LANES), 0)
        low = lane < HEAD_DIM
        consts = ((lane == HEAD_DIM + 1) | (lane == HEAD_DIM + 2)).astype(F32)
        onehot = (lane == (ti * tm + row) // SEL_BLOCK).astype(BF16)
        ones_col = (lane == HEAD_DIM).astype(F32)
        for g in range(KV_GROUPS):
            kk = z[:, C_K + g * LANES:C_K + (g + 1) * LANES]
            ks_ref[0, :, 2 * g * LANES:(2 * g + 1) * LANES] = jnp.where(low, kk, consts).astype(BF16)
            ks_ref[0, :, (2 * g + 1) * LANES:(2 * g + 2) * LANES] = onehot
            kw_ref[0, :, g * LANES:(g + 1) * LANES] = jnp.where(low, pltpu.roll(kk, HEAD_DIM, 1), 0.0).astype(BF16)
            vv = z[:, C_V + g * LANES:C_V + (g + 1) * LANES]
            for v_ref, half in ((vs_ref, vv), (vw_ref, pltpu.roll(vv, HEAD_DIM, 1))):
                zv = jnp.where(low, half, ones_col).T
                for t in range(TILES):
                    v_ref[0, g, t] = zv[:, t * Q_TILE:(t + 1) * Q_TILE].astype(BF16)


def _in_proj(x, g1, sc, sh, w_big):
    B, S, D = x.shape
    tm = TOK_TILE
    sp = S + KEY_PAD
    nq = S // Q_TILE
    cur = lambda i: jnp.maximum(i - PAD_STEPS, 0)
    tok = lambda w: pl.BlockSpec((1, tm, w), lambda b, i: (b, cur(i), 0))
    key = lambda w: pl.BlockSpec((1, tm, w), lambda b, i: (b, i, 0))
    val = pl.BlockSpec((1, KV_GROUPS, TILES, LANES, LANES), lambda b, i: (b, 0, i, 0, 0))
    val_shape = jax.ShapeDtypeStruct((B, KV_GROUPS, sp // LANES, LANES, LANES), BF16)
    return pl.pallas_call(
        _inproj_kernel,
        out_shape=[jax.ShapeDtypeStruct((B, S, 768), F32),
                   jax.ShapeDtypeStruct((B, KV_GROUPS, nq, HEAD_DIM, ROWS), BF16),
                   jax.ShapeDtypeStruct((B, S, 256), BF16),
                   jax.ShapeDtypeStruct((B, sp, 4 * LANES), BF16), val_shape,
                   jax.ShapeDtypeStruct((B, sp, 2 * LANES), BF16), val_shape,
                   jax.ShapeDtypeStruct((B, S, 256), F32)],
        grid=(B, S // tm + PAD_STEPS),
        in_specs=[tok(D),
                  pl.BlockSpec((1, D), lambda b, i: (0, 0)),
                  pl.BlockSpec((1, 1, D), lambda b, i: (b, 0, 0)),
                  pl.BlockSpec((1, 1, D), lambda b, i: (b, 0, 0)),
                  pl.BlockSpec((D, C_END), lambda b, i: (0, 0))],
        out_specs=[tok(768),
                   pl.BlockSpec((1, KV_GROUPS, TILES, HEAD_DIM, ROWS), lambda b, i: (b, 0, cur(i), 0, 0)),
                   tok(256), key(4 * LANES), val, key(2 * LANES), val, tok(256)],
        compiler_params=_cparams(("arbitrary", "arbitrary")),
        name="in_proj",
    )(x, g1.reshape(1, D), sc, sh, w_big)


HALO = 32


def _poolconv_kernel(cur_ref, halo_ref, pw_ref, ps_ref, cw_ref, cb_ref, lg_ref, lb_ref, o_ref,
                     ext_ref, v_ref, pa_ref, pb_ref, vsh_ref):
    ti = pl.program_id(1)
    ts = cur_ref.shape[1]
    rows = HALO + ts
    halo = halo_ref[0] * (ti > 0).astype(F32)
    ext_ref[0:HALO, :] = halo
    ext_ref[HALO:rows, :] = cur_ref[0]
    u = ext_ref[HALO:rows, 0:POOL_DIM]
    lane = lax.broadcasted_iota(I32, (ts, POOL_DIM), 1)
    grp = lane // (POOL_DIM // POOL_GROUPS)
    pa_ref[8:rows, :] = ext_ref[8:rows, 0:POOL_DIM] + ext_ref[7:rows - 1, 0:POOL_DIM]
    pooled = pa_ref[HALO:rows, :]
    pb_ref[16:rows, :] = pa_ref[16:rows, :] + pa_ref[14:rows - 2, :]
    pooled = jnp.where(grp >= 1, pb_ref[HALO:rows, :], pooled)
    pa_ref[24:rows, :] = pb_ref[24:rows, :] + pb_ref[20:rows - 4, :]
    pooled = jnp.where(grp >= 2, pa_ref[HALO:rows, :], pooled)
    pooled = jnp.where(grp == 3, pa_ref[HALO:rows, :] + pa_ref[HALO - 8:rows - 8, :], pooled)
    assert POOL_WINDOWS == (2, 4, 8, 16)
    wlane = jnp.full((ts, POOL_DIM), float(POOL_WINDOWS[-1]), F32)
    for g in range(POOL_GROUPS - 1):
        wlane = jnp.where(grp == g, float(POOL_WINDOWS[g]), wlane)
    t1 = (ti * ts + lax.broadcasted_iota(I32, (ts, POOL_DIM), 0) + 1).astype(F32)
    cnt = jnp.minimum(t1, wlane)
    pooled = pooled / cnt - u
    y_pool = jnp.dot(pooled.astype(BF16), pw_ref[...], preferred_element_type=F32) * ps_ref[...]
    o_ref[0, :, 0:POOL_DIM] = y_pool.astype(BF16)
    uv = ext_ref[:, POOL_DIM:POOL_DIM + CONV_DIM]
    ug = ext_ref[:, POOL_DIM + CONV_DIM:POOL_DIM + 2 * CONV_DIM]
    v_ref[...] = uv * jax.nn.sigmoid(ug)
    for b in range(1, 8):
        vsh_ref[b - 1] = v_ref[b:rows - 8 + b, :]
    acc = jnp.zeros((ts, CONV_DIM), F32) + cb_ref[...]
    for k in range(CONV_WIDTH):
        a, b = divmod(HALO - (CONV_WIDTH - 1) + k, 8)
        tap = v_ref[8 * a:8 * a + ts, :] if b == 0 else vsh_ref[b - 1, 8 * a:8 * a + ts, :]
        acc = acc + tap * cw_ref[k:k + 1, :]
    mu = jnp.mean(acc, axis=-1, keepdims=True)
    d = acc - mu
    var = jnp.mean(d * d, axis=-1, keepdims=True)
    yn = d * lax.rsqrt(var + EPS) * lg_ref[...] + lb_ref[...]
    o_ref[0, :, POOL_DIM:POOL_DIM + CONV_DIM] = (yn * jax.nn.sigmoid(yn)).astype(BF16)


def _pool_conv(upc, pool_w_bd, pool_scale, conv_w, conv_b, ln_g, ln_b):
    B, S, W = upc.shape
    ts = TOK_TILE
    r = ts // HALO
    vec = lambda n: pl.BlockSpec((1, n), lambda b, i: (0, 0))
    return pl.pallas_call(
        _poolconv_kernel,
        out_shape=jax.ShapeDtypeStruct((B, S, POOL_DIM + CONV_DIM), BF16),
        grid=(B, S // ts),
        in_specs=[pl.BlockSpec((1, ts, W), lambda b, i: (b, i, 0)),
                  pl.BlockSpec((1, HALO, W), lambda b, i: (b, jnp.maximum(i * r - 1, 0), 0)),
                  pl.BlockSpec((POOL_DIM, POOL_DIM), lambda b, i: (0, 0)),
                  vec(POOL_DIM),
                  pl.BlockSpec((CONV_WIDTH + 1, CONV_DIM), lambda b, i: (0, 0)),
                  vec(CONV_DIM), vec(CONV_DIM), vec(CONV_DIM)],
        out_specs=pl.BlockSpec((1, ts, POOL_DIM + CONV_DIM), lambda b, i: (b, i, 0)),
        scratch_shapes=[pltpu.VMEM((HALO + ts, W), F32), pltpu.VMEM((HALO + ts, CONV_DIM), F32),
                        pltpu.VMEM((HALO + ts, POOL_DIM), F32), pltpu.VMEM((HALO + ts, POOL_DIM), F32),
                        pltpu.VMEM((7, HALO + ts - 8, CONV_DIM), F32)],
        compiler_params=_cparams(("arbitrary", "arbitrary")),
        name="pool_conv",
    )(upc, upc, pool_w_bd, pool_scale.reshape(1, -1), conv_w, conv_b.reshape(1, -1),
      ln_g.reshape(1, -1), ln_b.reshape(1, -1))


N_STREAM = 2 * KV_GROUPS
CHUNK_W = CMP_STRIDE * 2 * KV_GROUPS * HEAD_DIM


def _cmp_weights(w1_k, w1_v):
    half = CMP_STRIDE * HEAD_DIM
    cols = []
    for s in range(N_STREAM):
        w1 = w1_k if s < KV_GROUPS else w1_v
        for part in range(2):
            blk = w1[part * half:(part + 1) * half].reshape(CMP_STRIDE, 1, HEAD_DIM, CMP_HIDDEN)
            z = jnp.zeros((CMP_STRIDE, N_STREAM, HEAD_DIM, CMP_HIDDEN), w1.dtype)
            z = lax.dynamic_update_slice(z, blk, (0, s, 0, 0))
            cols.append(z.reshape(CHUNK_W, CMP_HIDDEN))
    return jnp.concatenate(cols, axis=1).astype(BF16)


def _gelu_tanh(x):
    return 0.5 * x * (1.0 + jnp.tanh(math.sqrt(2.0 / math.pi) * (x + 0.044715 * (x * x * x))))


def _compress_kernel(c_ref, w_ref, pek_ref, pev_ref, w1k_ref, w1v_ref, w2k_ref, w2vt_ref, kc_ref, vct_ref):
    r = jnp.dot(c_ref[0], w_ref[...], preferred_element_type=F32)
    ncp = r.shape[0]
    pe_k = jnp.dot(pek_ref[...], w1k_ref[...], preferred_element_type=F32)[0:1]
    pe_v = jnp.dot(pev_ref[...], w1v_ref[...], preferred_element_type=F32)[0:1]
    for s in range(N_STREAM):
        a = r[:, s * 256:s * 256 + CMP_HIDDEN]
        b = r[:, s * 256 + CMP_HIDDEN:(s + 1) * 256]
        hid = a + pltpu.roll(b, ncp - 1, 0) + (pe_k if s < KV_GROUPS else pe_v)
        act = _gelu_tanh(hid).astype(BF16)
        if s < KV_GROUPS:
            kc_ref[0, s] = jnp.dot(act, w2k_ref[...], preferred_element_type=F32).astype(BF16)
        else:
            vt = lax.dot_general(w2vt_ref[...], act, (((1,), (1,)), ((), ())), preferred_element_type=F32)
            ones_row = (lax.broadcasted_iota(I32, vt.shape, 0) == HEAD_DIM).astype(F32)
            vct_ref[0, s - KV_GROUPS] = (vt + ones_row).astype(BF16)


def _compress(kvc, wcmp, pe_k, pe_v, w1_k, w1_v, w2_k, w2_v):
    B, S, _ = kvc.shape
    ncp = S // CMP_STRIDE
    chunks = kvc.reshape(B, ncp, CHUNK_W)
    pe8 = lambda pe: jnp.broadcast_to(pe.reshape(1, -1), (8, CMP_LEN * HEAD_DIM)).astype(BF16)
    w2k = jnp.pad(w2_k, ((0, 0), (0, LANES - HEAD_DIM))).astype(BF16)
    w2vt = jnp.pad(w2_v.T, ((0, LANES - HEAD_DIM), (0, 0))).astype(BF16)
    full = lambda a: pl.BlockSpec(a.shape, lambda b: (0,) * a.ndim)
    args = (wcmp, pe8(pe_k), pe8(pe_v), w1_k.astype(BF16), w1_v.astype(BF16), w2k, w2vt)
    return pl.pallas_call(
        _compress_kernel,
        out_shape=[jax.ShapeDtypeStruct((B, KV_GROUPS, ncp, LANES), BF16),
                   jax.ShapeDtypeStruct((B, KV_GROUPS, LANES, ncp), BF16)],
        grid=(B,),
        in_specs=[pl.BlockSpec((1, ncp, CHUNK_W), lambda b: (b, 0, 0))] + [full(a) for a in args],
        out_specs=[pl.BlockSpec((1, KV_GROUPS, ncp, LANES), lambda b: (b, 0, 0, 0)),
                   pl.BlockSpec((1, KV_GROUPS, LANES, ncp), lambda b: (b, 0, 0, 0))],
        compiler_params=_cparams(("arbitrary",)),
        name="compress",
    )(chunks, *args)


def _bias_tables(rel_bias, S):
    nq = S // Q_TILE
    ncp = S // CMP_STRIDE
    rb = rel_bias.reshape(N_BUCKETS, KV_GROUPS, HPG).transpose(1, 2, 0) * LOG2E
    far = rb[:, :, N_BUCKETS - 1]
    far_hi = far.astype(BF16)
    far_lo = (far - far_hi.astype(F32)).astype(BF16)
    far_sum = far_hi.astype(F32) + far_lo.astype(F32)
    i = np.arange(Q_TILE)[None, :]

    def table(d, valid, sub):
        onehot = jax.nn.one_hot(_t5_bucket(jnp.asarray(d, I32)), N_BUCKETS, dtype=F32)
        t = jnp.einsum('rib,ghb->grhi', onehot, rb, precision=lax.Precision.HIGHEST)
        t = jnp.where(jnp.asarray(valid)[None, :, None, :], t - sub[:, None, :, None], NEG_INF)
        return t.reshape(KV_GROUPS, d.shape[0], ROWS)

    d = i - np.arange(NEAR_KEYS)[:, None] + (NEAR_KEYS - Q_TILE)
    t_near = table(d, d >= 0, far_sum)
    d = i - np.arange(WIN_KEYS)[:, None] + (WIN_KEYS - Q_TILE)
    t_win = table(d, (d >= 0) & (d < WINDOW), jnp.zeros_like(far_sum))
    c0 = (Q_TILE // CMP_STRIDE) * (nq - 1)
    d = i - CMP_STRIDE * (np.arange(c0 + ncp)[:, None] - c0) - (CMP_LEN - 1)
    t_cmp = table(d, d >= 0, jnp.zeros_like(far_sum))
    rows = jnp.zeros((KV_GROUPS, HEAD_DIM, HPG, Q_TILE), F32)
    rows = rows.at[:, 0].set(1.0)
    rows = rows.at[:, 1].set(jnp.broadcast_to(far_hi.astype(F32)[:, :, None], (KV_GROUPS, HPG, Q_TILE)))
    rows = rows.at[:, 2].set(jnp.broadcast_to(far_lo.astype(F32)[:, :, None], (KV_GROUPS, HPG, Q_TILE)))
    return t_near, t_win, t_cmp, rows.reshape(KV_GROUPS, HEAD_DIM, ROWS).astype(BF16)


def _overlap_t(S):
    ncp = S // CMP_STRIDE
    n = np.arange(ncp)[None, :]
    jb = np.arange(LANES)[:, None]
    end = n * CMP_STRIDE + CMP_LEN - 1
    start = n * CMP_STRIDE
    ov = (end >= jb * SEL_BLOCK) & (start < (jb + 1) * SEL_BLOCK) & (n < ncp - 1)
    return jnp.asarray(ov.astype(np.float32), BF16)


def _key_tiles(ref, k0, n):
    t0 = k0 // LANES
    return jnp.concatenate([ref[0, 0, t0 + u] for u in range(n)], axis=1)


def _attn_kernel(qt_ref, qc_ref, kc_ref, vct_ref, ks_ref, vs_ref, kw_ref, vw_ref, gt_ref, tn_ref, tw_ref, tct_ref,
                 ovt_ref, o_ref, qa_ref, acc_ref, *, c0):
    qi = pl.program_id(2)
    qt = qt_ref[0, 0, 0]
    ncp = kc_ref.shape[2]

    k1 = Q_TILE * qi + (KEY_PAD + Q_TILE)

    qa_ref[0:HEAD_DIM, :] = qt
    qa_ref[HEAD_DIM:LANES, :] = qc_ref[0]

    r0 = pl.multiple_of(c0 - (Q_TILE // CMP_STRIDE) * qi, 8)
    st = (jnp.dot(kc_ref[0, 0][:, 0:HEAD_DIM], qt, preferred_element_type=F32)
          + tct_ref[0, pl.ds(r0, ncp), :])
    w0 = pl.multiple_of(k1 - WIN_KEYS, LANES)
    sw = jnp.dot(kw_ref[0, pl.ds(w0, WIN_KEYS), :], qa_ref[0:LANES, :], preferred_element_type=F32) + tw_ref[0]

    mc = jnp.maximum(jnp.max(st, axis=0, keepdims=True), -1e20)
    pc = jnp.exp2(st - mc).astype(BF16)
    acc_c = jnp.dot(vct_ref[0, 0], pc, preferred_element_type=F32)
    inv_c = 1.0 / jnp.maximum(acc_c[HEAD_DIM:HEAD_DIM + 1], 1e-30)
    o_c = acc_c[0:HEAD_DIM] * inv_c

    imp = None
    for h in range(HPG):
        cols = slice(h * Q_TILE, (h + 1) * Q_TILE)
        part = jnp.dot(ovt_ref[...], pc[:, cols], preferred_element_type=F32) * inv_c[:, cols]
        imp = part if imp is None else imp + part

    def flash_step(carry, s, vt):
        m, acc = carry
        m_new = jnp.maximum(m, jnp.max(s, axis=0, keepdims=True))
        alpha = jnp.exp2(m - m_new)
        p = jnp.exp2(s - m_new)
        acc = acc * alpha + jnp.dot(vt, p.astype(BF16), preferred_element_type=F32)
        return m_new, acc

    def normalised(acc):
        return acc[0:HEAD_DIM] * (1.0 / acc[HEAD_DIM:HEAD_DIM + 1])

    p = jnp.exp2(sw - jnp.max(sw, axis=0, keepdims=True)).astype(BF16)
    o_w = normalised(jnp.dot(_key_tiles(vw_ref, w0, WIN_KEYS // LANES), p, preferred_element_type=F32))

    jb = lax.broadcasted_iota(I32, (LANES, Q_TILE), 0)
    ii = lax.broadcasted_iota(I32, (LANES, Q_TILE), 1)
    cur = (Q_TILE // SEL_BLOCK) * qi + (ii >= SEL_BLOCK).astype(I32)
    forced = (jb == 0) | (jb == cur) | (jb == cur - 1)
    score = jnp.where(jb <= cur, jnp.where(forced, -jnp.inf, imp), NEG_INF)
    for _ in range(SEL_TOPN - N_FORCED):
        mx = jnp.max(score, axis=0, keepdims=True)
        first = jnp.min(jnp.where(score == mx, jb, LANES), axis=0, keepdims=True)
        score = jnp.where(jb == first, -jnp.inf, score)
    mbt = jnp.where(score == -jnp.inf, 0.0, MASK_BIAS).astype(BF16)
    for h in range(HPG):
        qa_ref[LANES:2 * LANES, h * Q_TILE:(h + 1) * Q_TILE] = mbt

    n0 = pl.multiple_of(k1 - NEAR_KEYS, LANES)
    half = NEAR_KEYS // 2
    sa = (jnp.dot(ks_ref[0, pl.ds(n0, half), :], qa_ref[...], preferred_element_type=F32)
          + tn_ref[0, 0:half, :])
    sb = (jnp.dot(ks_ref[0, pl.ds(n0 + half, half), :], qa_ref[...], preferred_element_type=F32)
          + tn_ref[0, half:NEAR_KEYS, :])
    m_near = jnp.maximum(jnp.max(sa, axis=0, keepdims=True), jnp.max(sb, axis=0, keepdims=True))
    pa = jnp.exp2(sa - m_near).astype(BF16)
    da = jnp.dot(_key_tiles(vs_ref, n0, half // LANES), pa, preferred_element_type=F32)
    pb = jnp.exp2(sb - m_near).astype(BF16)
    acc_near = da + jnp.dot(_key_tiles(vs_ref, n0 + half, half // LANES), pb, preferred_element_type=F32)
    n_pairs = (jnp.maximum(n0 - KEY_PAD, 0) + KEY_PAD - 1) // KEY_PAD

    def far_scores(j):
        k0 = pl.multiple_of(n0 - FAR_TILE * (j + 1), LANES)
        return jnp.dot(ks_ref[0, pl.ds(k0, FAR_TILE), :], qa_ref[...], preferred_element_type=F32)

    def far_values(j):
        return _key_tiles(vs_ref, pl.multiple_of(n0 - FAR_TILE * (j + 1), LANES), FAR_TILE // LANES)

    def fast_body(jj, top):
        tiles = [FAR_GROUP * jj + u for u in range(FAR_GROUP)]
        scores = [far_scores(tiles[0]), far_scores(tiles[1])]
        acc = None
        for u, j in enumerate(tiles):
            if u + 2 < FAR_GROUP:
                scores.append(far_scores(tiles[u + 2]))
            p = jnp.exp2(scores[u] - m_near).astype(BF16)
            d = jnp.dot(far_values(j), p, preferred_element_type=F32)
            acc = d if acc is None else acc + d
            top = jnp.maximum(top, jnp.max(scores[u], axis=0, keepdims=True))
        acc_ref[...] += acc
        return top

    acc_ref[...] = acc_near
    top = lax.fori_loop(0, n_pairs, fast_body, m_near)

    @pl.when(jnp.max(top - m_near) > FAR_HEADROOM)
    def _():
        def safe_body(j, carry):
            return flash_step(carry, far_scores(j), far_values(j))

        acc_ref[...] = lax.fori_loop(0, FAR_GROUP * n_pairs, safe_body, (m_near, acc_near))[1]

    o_s = normalised(acc_ref[...])

    gtt = gt_ref[0].T
    gate = lambda br: jnp.concatenate([gtt[br * HPG + h:br * HPG + h + 1, :] for h in range(HPG)], axis=1)
    o_ref[0, 0, 0] = (gate(0) * o_c + gate(1) * o_s + gate(2) * o_w).astype(BF16)


def _attention(qt, qc, kc, vct, ks, vs, kw, vw, gt, tn, tw, tct, ovt):
    B, G, nq, _, _ = qt.shape
    S = nq * Q_TILE
    sp = S + KEY_PAD
    ncp = S // CMP_STRIDE
    c0 = (Q_TILE // CMP_STRIDE) * (nq - 1)
    per_g = lambda a: pl.BlockSpec((1,) + a.shape[1:], lambda b, g, i: (g,) + (0,) * (a.ndim - 1))
    val = pl.BlockSpec((1, 1, sp // LANES, LANES, LANES), lambda b, g, i: (b, g, 0, 0, 0))
    return pl.pallas_call(
        functools.partial(_attn_kernel, c0=c0),
        out_shape=jax.ShapeDtypeStruct((B, G, nq, HEAD_DIM, ROWS), BF16),
        grid=(B, G, nq),
        in_specs=[pl.BlockSpec((1, 1, 1, HEAD_DIM, ROWS), lambda b, g, i: (b, g, i, 0, 0)),
                  per_g(qc),
                  pl.BlockSpec((1, 1, ncp, LANES), lambda b, g, i: (b, g, 0, 0)),
                  pl.BlockSpec((1, 1, LANES, ncp), lambda b, g, i: (b, g, 0, 0)),
                  pl.BlockSpec((1, sp, 2 * LANES), lambda b, g, i: (b, 0, g)),
                  val,
                  pl.BlockSpec((1, sp, LANES), lambda b, g, i: (b, 0, g)),
                  val,
                  pl.BlockSpec((1, Q_TILE, LANES), lambda b, g, i: (b, i, g)),
                  per_g(tn), per_g(tw), per_g(tct),
                  pl.BlockSpec((LANES, ncp), lambda b, g, i: (0, 0))],
        out_specs=pl.BlockSpec((1, 1, 1, HEAD_DIM, ROWS), lambda b, g, i: (b, g, i, 0, 0)),
        scratch_shapes=[pltpu.VMEM((2 * LANES, ROWS), BF16), pltpu.VMEM((LANES, ROWS), F32)],
        compiler_params=_cparams(("arbitrary", "arbitrary", "arbitrary")),
        name="nsa_attention",
    )(qt, qc, kc, vct, ks, vs, kw, vw, gt, tn, tw, tct, ovt)


def _post_attn_kernel(x_ref, ypc_ref, yn_ref, wo_ref, g1_ref, n2_ref, sc_ref, sh_ref, rwh_ref, rwl_ref, rb_ref,
                      xo_ref, h2_ref, ei_ref, tw_ref, cnt_ref, run_ref, hd_ref):
    first = (pl.program_id(0) == 0) & (pl.program_id(1) == 0)

    @pl.when(first)
    def _():
        run_ref[...] = jnp.zeros_like(run_ref)

    tm = x_ref.shape[1]
    half = wo_ref.shape[0] // 2
    tiles = []
    for t in range(TILES):
        groups = []
        for g in range(KV_GROUPS):
            blk = yn_ref[0, g, t].astype(F32)
            for hh in range(HPG):
                hd_ref[hh * HEAD_DIM:(hh + 1) * HEAD_DIM, :] = blk[:, hh * Q_TILE:(hh + 1) * Q_TILE]
            groups.append(hd_ref[...].T)
        tiles.append(jnp.concatenate(groups, axis=1))
    ynsa = jnp.concatenate(tiles, axis=0).astype(BF16)
    mixed = (jnp.dot(ypc_ref[0], wo_ref[0:half, :], preferred_element_type=F32)
             + jnp.dot(ynsa, wo_ref[half:, :], preferred_element_type=F32))
    x = x_ref[0] + g1_ref[0] * mixed
    xo_ref[0] = x
    y = x * lax.rsqrt(jnp.mean(x * x, axis=-1, keepdims=True) + EPS) * n2_ref[...]
    h2 = y * (1.0 + sc_ref[0]) + sh_ref[0]
    _rows_to_tiles(h2_ref, h2)
    hh = h2.astype(BF16)
    hl = (h2 - hh.astype(F32)).astype(BF16)
    logit = (jnp.dot(hh, rwh_ref[...], preferred_element_type=F32)
             + jnp.dot(hl, rwh_ref[...], preferred_element_type=F32)
             + jnp.dot(hh, rwl_ref[...], preferred_element_type=F32)) + rb_ref[...]
    lane = lax.broadcasted_iota(I32, (tm, LANES), 1)
    vals, hots, idxs = [], [], []
    for _ in range(TOP_K):
        mx = jnp.max(logit, axis=1, keepdims=True)
        idx = jnp.min(jnp.where(logit == mx, lane, LANES), axis=1, keepdims=True)
        hot = lane == idx
        vals.append(mx)
        hots.append(hot)
        idxs.append(idx)
        logit = jnp.where(hot, -jnp.inf, logit)
    ex = [jnp.exp(v - vals[0]) for v in vals]
    inv = 1.0 / (ex[0] + ex[1] + ex[2] + ex[3])
    assign = (hots[0] | hots[1] | hots[2] | hots[3]).astype(BF16)
    r = lax.broadcasted_iota(I32, (tm, tm), 0)
    c = lax.broadcasted_iota(I32, (tm, tm), 1)
    before = jnp.dot((c < r).astype(BF16), assign, preferred_element_type=F32) + run_ref[...]
    ei = jnp.zeros((tm, LANES), I32)
    tw = jnp.zeros((tm, LANES), F32)
    for k in range(TOP_K):
        e_k = idxs[k]
        r_k = jnp.sum(jnp.where(hots[k], before, 0.0), axis=1, keepdims=True).astype(I32)
        ei = jnp.where(lane == k, e_k, jnp.where(lane == TOP_K + k, r_k, ei))
        tw = jnp.where(lane == k, ex[k] * inv, tw)
    ei_ref[0] = ei
    tw_ref[0] = tw
    run_ref[...] = run_ref[...] + jnp.sum(assign.astype(F32), axis=0, keepdims=True)
    cnt_ref[...] = run_ref[...]


def _post_attn(x, ypc, ynsa, w_out, g1, n2g, sc2, sh2, rw_hi, rw_lo, rb):
    B, S, D = x.shape
    tm = TOK_TILE
    tok = lambda w: pl.BlockSpec((1, tm, w), lambda b, i: (b, i, 0))
    per_b = pl.BlockSpec((1, 1, D), lambda b, i: (b, 0, 0))
    full = lambda a: pl.BlockSpec(a.shape, lambda b, i: (0,) * a.ndim)
    return pl.pallas_call(
        _post_attn_kernel,
        out_shape=[jax.ShapeDtypeStruct((B, S, D), F32), jax.ShapeDtypeStruct((B * S * ROW_TILES, LANES), F32),
                   jax.ShapeDtypeStruct((B, S, LANES), I32), jax.ShapeDtypeStruct((B, S, LANES), F32),
                   jax.ShapeDtypeStruct((1, LANES), F32)],
        grid=(B, S // tm),
        in_specs=[tok(D), tok(ypc.shape[-1]),
                  pl.BlockSpec((1, KV_GROUPS, TILES, HEAD_DIM, ROWS), lambda b, i: (b, 0, i, 0, 0)),
                  full(w_out), per_b,
                  pl.BlockSpec((1, D), lambda b, i: (0, 0)), per_b, per_b,
                  full(rw_hi), full(rw_lo), full(rb)],
        out_specs=[tok(D), pl.BlockSpec((tm * ROW_TILES, LANES), lambda b, i: (b * (S // tm) + i, 0)),
                   tok(LANES), tok(LANES), pl.BlockSpec((1, LANES), lambda b, i: (0, 0))],
        scratch_shapes=[pltpu.VMEM((1, LANES), F32), pltpu.VMEM((HPG * HEAD_DIM, Q_TILE), F32)],
        compiler_params=_cparams(("arbitrary", "arbitrary")),
        name="post_attn_router",
    )(x, ypc, ynsa, w_out, g1, n2g.reshape(1, D), sc2, sh2, rw_hi, rw_lo, rb)


ROW_TILES = 8
DMA_UNROLL = 4
ZERO_BITS = tuple(1 << b for b in reversed(range((EXPERT_BLOCK - 1).bit_length())))


def _rows_to_tiles(ref, val):
    n = val.shape[0]
    for s in range(ROW_TILES):
        ref[pl.ds(s, n, stride=ROW_TILES), :] = val[:, s * LANES:(s + 1) * LANES]


def _tiles_to_rows(ref, n):
    return jnp.concatenate([ref[pl.ds(s, n, stride=ROW_TILES), :] for s in range(ROW_TILES)], axis=1)


def _tile_at(ref, i):
    return ref.at[pl.ds(pl.multiple_of(i * ROW_TILES, ROW_TILES), ROW_TILES), :]


def _dispatch_kernel(fill0_ref, filln_ref, dest_ref, h_ref, xs_ref, zero_ref, sem, zsem):
    tm = h_ref.shape[0] // ROW_TILES

    @pl.when(pl.program_id(0) == 0)
    def _():
        zero_ref[...] = jnp.zeros_like(zero_ref)

        def expert(e, c):
            n = filln_ref[e]
            for wait in (False, True):
                for bit in ZERO_BITS:
                    @pl.when((n & bit) != 0)
                    def _():
                        first = fill0_ref[e] + (n & ~(2 * bit - 1))
                        dst = xs_ref.at[pl.ds(pl.multiple_of(first * ROW_TILES, ROW_TILES), bit * ROW_TILES), :]
                        cp = pltpu.make_async_copy(zero_ref.at[pl.ds(0, bit * ROW_TILES), :], dst, zsem)
                        cp.wait() if wait else cp.start()
            return c

        lax.fori_loop(0, N_EXPERTS, expert, 0)

    def body(i, c):
        for u in range(DMA_UNROLL):
            t = i * DMA_UNROLL + u
            src = _tile_at(h_ref, t)
            for k in range(TOP_K):
                pltpu.make_async_copy(src, _tile_at(xs_ref, dest_ref[0, 0, t * TOP_K + k]), sem).start(
                    priority=k % 2)
        return c

    lax.fori_loop(0, tm // DMA_UNROLL, body, 0)
    for _ in range(TOP_K):
        pltpu.make_async_copy(h_ref, xs_ref.at[pl.ds(0, tm * ROW_TILES), :], sem).wait()


def _dispatch(dest, h2t, fill0, filln, n_slots):
    tm = TOK_TILE
    nt = h2t.shape[0] // (tm * ROW_TILES)
    return pl.pallas_call(
        _dispatch_kernel,
        out_shape=jax.ShapeDtypeStruct((n_slots * ROW_TILES, LANES), F32),
        grid_spec=pltpu.PrefetchScalarGridSpec(
            num_scalar_prefetch=2,
            grid=(nt,),
            in_specs=[pl.BlockSpec((1, 1, tm * TOP_K), lambda i, f0, fn: (i, 0, 0), memory_space=pltpu.SMEM),
                      pl.BlockSpec((tm * ROW_TILES, LANES), lambda i, f0, fn: (i, 0))],
            out_specs=pl.BlockSpec(memory_space=pl.ANY),
            scratch_shapes=[pltpu.VMEM((ZERO_BITS[0] * ROW_TILES, LANES), F32), pltpu.SemaphoreType.DMA(()),
                            pltpu.SemaphoreType.DMA(())]),
        compiler_params=_cparams(("arbitrary",)),
        name="moe_dispatch",
    )(fill0, filln, dest.reshape(nt, 1, tm * TOP_K), h2t)


W_CHUNK = 512


def _expert_kernel(be_ref, nu_ref, x_ref, wgu_ref, bgu_ref, wd_ref, bd_ref, y_ref, wgu_s, wd_s):
    i = pl.program_id(0)

    @pl.when(i < nu_ref[0])
    def _():
        @pl.when((i == 0) | (be_ref[i] != be_ref[jnp.maximum(i - 1, 0)]))
        def _():
            for c in range(0, wgu_s.shape[1], W_CHUNK):
                wgu_s[:, c:c + W_CHUNK] = wgu_ref[0, :, c:c + W_CHUNK].astype(BF16)
            for c in range(0, wd_s.shape[1], W_CHUNK):
                wd_s[:, c:c + W_CHUNK] = wd_ref[0, :, c:c + W_CHUNK].astype(BF16)

        F = wd_s.shape[0]
        x = _tiles_to_rows(x_ref, EXPERT_BLOCK).astype(BF16)
        gu = jnp.dot(x, wgu_s[...], preferred_element_type=F32) + bgu_ref[0]
        gate = jnp.minimum(gu[:, :F], SWIGLU_LIMIT)
        up = jnp.clip(gu[:, F:], -SWIGLU_LIMIT, SWIGLU_LIMIT)
        act = (up + 1.0) * gate * jax.nn.sigmoid(SWIGLU_ALPHA * gate)
        _rows_to_tiles(y_ref, jnp.dot(act.astype(BF16), wd_s[...], preferred_element_type=F32) + bd_ref[0])


def _experts(blk_e, n_used, xs, layer, w_gu, b_gu, w_down, b_down):
    L, E, D, F2 = w_gu.shape
    F = F2 // 2
    rows = EXPERT_BLOCK * ROW_TILES
    nb = xs.shape[0] // rows
    blk = lambda i, be, nu: (jnp.minimum(i, nu[0] - 1), 0)
    per_e = lambda i, be, nu: (layer, be[i], 0, 0)
    return pl.pallas_call(
        _expert_kernel,
        out_shape=jax.ShapeDtypeStruct(xs.shape, F32),
        grid_spec=pltpu.PrefetchScalarGridSpec(
            num_scalar_prefetch=2,
            grid=(nb,),
            in_specs=[pl.BlockSpec((rows, LANES), blk),
                      pl.BlockSpec((None, 1, D, F2), per_e),
                      pl.BlockSpec((None, 1, 1, F2), per_e),
                      pl.BlockSpec((None, 1, F, D), per_e),
                      pl.BlockSpec((None, 1, 1, D), per_e)],
            out_specs=pl.BlockSpec((rows, LANES), blk),
            scratch_shapes=[pltpu.VMEM((D, F2), BF16), pltpu.VMEM((F, D), BF16)]),
        compiler_params=_cparams(("arbitrary",)),
        name="moe_experts",
    )(blk_e, n_used, xs, w_gu, b_gu.reshape(L, E, 1, F2), w_down, b_down.reshape(L, E, 1, D))


def _combine_kernel(dest_ref, y_ref, x_ref, tw_ref, g2_ref, fg_ref, o_ref, rows_ref, sem, *, final, n_tiles):
    s = pl.program_id(0)
    tm = x_ref.shape[1]

    @pl.when(s < n_tiles)
    def _():
        slot = s % 2

        def body(i, c):
            for u in range(DMA_UNROLL):
                t = i * DMA_UNROLL + u
                for k in range(TOP_K):
                    pltpu.make_async_copy(_tile_at(y_ref, dest_ref[0, 0, t * TOP_K + k]),
                                          _tile_at(rows_ref.at[slot, k], t), sem.at[slot]).start(priority=k % 2)
            return c

        lax.fori_loop(0, tm // DMA_UNROLL, body, 0)

    @pl.when(s > 0)
    def _():
        slot = (s - 1) % 2
        for k in range(TOP_K):
            pltpu.make_async_copy(y_ref.at[pl.ds(0, tm * ROW_TILES), :], rows_ref.at[slot, k], sem.at[slot]).wait()
        tw = tw_ref[0]
        moe = tw[:, 0:1] * _tiles_to_rows(rows_ref.at[slot, 0], tm)
        for k in range(1, TOP_K):
            moe = moe + tw[:, k:k + 1] * _tiles_to_rows(rows_ref.at[slot, k], tm)
        x = x_ref[0] + g2_ref[0] * moe
        if final:
            x = x * lax.rsqrt(jnp.mean(x * x, axis=-1, keepdims=True) + EPS) * fg_ref[...]
        o_ref[0] = x


def _combine(dest, y, x, tw, g2, final_g, final):
    B, S, D = x.shape
    tm = TOK_TILE
    nt = S // tm
    n_tiles = B * nt
    done = lambda s: jnp.maximum(s - 1, 0)
    tok = lambda w: pl.BlockSpec((1, tm, w), lambda s: (done(s) // nt, done(s) % nt, 0))
    return pl.pallas_call(
        functools.partial(_combine_kernel, final=final, n_tiles=n_tiles),
        out_shape=jax.ShapeDtypeStruct((B, S, D), F32),
        grid=(n_tiles + 1,),
        in_specs=[pl.BlockSpec((1, 1, tm * TOP_K), lambda s: (jnp.minimum(s, n_tiles - 1), 0, 0),
                               memory_space=pltpu.SMEM),
                  pl.BlockSpec(memory_space=pl.ANY),
                  tok(D), tok(LANES),
                  pl.BlockSpec((1, 1, D), lambda s: (done(s) // nt, 0, 0)),
                  pl.BlockSpec((1, D), lambda s: (0, 0))],
        out_specs=tok(D),
        scratch_shapes=[pltpu.VMEM((2, TOP_K, tm * ROW_TILES, LANES), F32), pltpu.SemaphoreType.DMA((2,))],
        compiler_params=_cparams(("arbitrary",)),
        name="moe_combine",
    )(dest.reshape(n_tiles, 1, tm * TOP_K), y, x, tw, g2, final_g.reshape(1, D))


def _moe(x, h2, ei, tw, counts, g2, layer, w_gu, b_gu, w_down, b_down, final_g, final):
    B, S, D = x.shape
    N = B * S
    n_slots = -(-(N * TOP_K + N_EXPERTS * EXPERT_BLOCK) // EXPERT_BLOCK) * EXPERT_BLOCK
    nb = n_slots // EXPERT_BLOCK
    cnt = counts[0, :N_EXPERTS].astype(I32)
    padded = (cnt + EXPERT_BLOCK - 1) // EXPERT_BLOCK * EXPERT_BLOCK
    pend = jnp.cumsum(padded)
    pstart = pend - padded
    ei2 = ei.reshape(N, LANES)
    dest = (pstart[ei2[:, 0:TOP_K]] + ei2[:, TOP_K:2 * TOP_K]).reshape(N * TOP_K)
    blk_start = jnp.arange(nb, dtype=I32) * EXPERT_BLOCK
    blk_e = jnp.minimum(jnp.sum((pend[None, :] <= blk_start[:, None]).astype(I32), axis=1), N_EXPERTS - 1)
    n_used = (pend[-1:] // EXPERT_BLOCK).astype(I32)
    xs = _dispatch(dest, h2, pstart + cnt, padded - cnt, n_slots)
    y = _experts(blk_e, n_used, xs, layer, w_gu, b_gu, w_down, b_down)
    return _combine(dest, y, x, tw, g2, final_g, final)


def kernel(x, c, w_mod, b_mod, norm1_g, norm2_g, w_in, w_out, pool_w, pool_scale, conv_w, conv_b, conv_ln_g,
           conv_ln_b, cmp_pe_k, cmp_pe_v, cmp_w1_k, cmp_w2_k, cmp_w1_v, cmp_w2_v, rel_bias, router_w, router_b,
           expert_w_gu, expert_b_gu, expert_w_down, expert_b_down, final_g):
    B, S, D = x.shape
    L = w_mod.shape[0]
    assert S % TOK_TILE == 0 and S // SEL_BLOCK <= LANES and D == ROW_TILES * LANES
    mod = _modulation(c, w_mod, b_mod)
    t_near, t_win, t_cmp, qc = _bias_tables(rel_bias, S)
    ovt = _overlap_t(S)
    w_big = _in_weight(w_in)
    cg = POOL_DIM // POOL_GROUPS
    for l in range(L):
        m6 = mod[l].reshape(B, 6, 1, D)
        sh1, sc1, g1, sh2, sc2, g2 = (m6[:, k] for k in range(6))
        upc, qt, kvc, ks, vs, kw, vw, gt = _in_proj(x, norm1_g[l], sc1, sh1, w_big[l])
        pw_bd = jnp.zeros((POOL_DIM, POOL_DIM), F32)
        for g in range(POOL_GROUPS):
            pw_bd = lax.dynamic_update_slice(pw_bd, pool_w[l, g], (g * cg, g * cg))
        cw = jnp.pad(conv_w[l], ((0, 1), (0, 0)))
        ypc = _pool_conv(upc, pw_bd.astype(BF16), pool_scale[l], cw, conv_b[l], conv_ln_g[l], conv_ln_b[l])
        kc, vct = _compress(kvc, _cmp_weights(cmp_w1_k[l], cmp_w1_v[l]), cmp_pe_k[l], cmp_pe_v[l],
                            cmp_w1_k[l], cmp_w1_v[l], cmp_w2_k[l], cmp_w2_v[l])
        ynsa = _attention(qt, qc, kc, vct, ks, vs, kw, vw, gt, t_near, t_win, t_cmp, ovt)
        rw = jnp.pad(router_w[l], ((0, 0), (0, LANES - N_EXPERTS)))
        rw_hi = rw.astype(BF16)
        rw_lo = (rw - rw_hi.astype(F32)).astype(BF16)
        rb = jnp.pad(router_b[l].reshape(1, -1), ((0, 0), (0, LANES - N_EXPERTS)), constant_values=NEG_INF)
        x, h2, ei, tw, counts = _post_attn(x, ypc, ynsa, w_out[l].astype(BF16), g1, norm2_g[l], sc2, sh2,
                                           rw_hi, rw_lo, rb)
        x = _moe(x, h2, ei, tw, counts, g2, l, expert_w_gu, expert_b_gu, expert_w_down, expert_b_down,
                 final_g, final=(l == L - 1))
    return x
```
